```python
import jax, jax.numpy as jnp
from jax import lax
import numpy as np

D_MODEL = 1024
BATCH = 8
SEQ = 2048
DEPTH = 2
DEC_BATCH = 128
DEC_SEQ = 1
PAST_LEN = 16384
PAGE_SIZE = 128

CONV_W = D_MODEL // 2
CONV_K = 3
GLA_HEADS = 4
GLA_V = D_MODEL // 2
GLA_K = GLA_V // 2
GLA_DK = GLA_K // GLA_HEADS
GLA_DV = GLA_V // GLA_HEADS
GLA_RANK = 16
GLA_TAU = 16.0
GLA_CHUNK = 64
MIX_W = CONV_W + GLA_V
IN_COLS = 3 * CONV_W + 2 * GLA_K + 2 * GLA_V + GLA_RANK
IN_SPLITS = (CONV_W, 2 * CONV_W, 3 * CONV_W, 3 * CONV_W + GLA_K, 3 * CONV_W + 2 * GLA_K,
             3 * CONV_W + 2 * GLA_K + GLA_V, 3 * CONV_W + 2 * GLA_K + 2 * GLA_V)
N_MEM = 256
MEM_HEADS = 4
MEM_DH = D_MODEL // MEM_HEADS
N_GROUPS = 4
EXP_PER_GROUP = 4
N_EXPERTS = N_GROUPS * EXP_PER_GROUP
TOP_K_INNER = 2
EXPERT_FF = 256
EPS = 1e-6

kernel_name = "hymba_conv_gla_hiermoe_step"


def rmsnorm(x, g):
    xf = x.astype(jnp.float32)
    y = xf * lax.rsqrt(jnp.mean(xf * xf, axis=-1, keepdims=True) + EPS)
    return (y * g.astype(jnp.float32)).astype(x.dtype)


def short_conv(u, buf, w):
    full = jnp.concatenate([buf.astype(u.dtype), u], axis=1)
    T = u.shape[1]
    y = full[:, 0:T] * w[0]
    for j in range(1, CONV_K):
        y = y + full[:, j:j + T] * w[j]
    return y, full[:, -(CONV_K - 1):]


def gla_chunked(q, k, v, log_a, s0):
    Bn, T, H, dk = q.shape
    dv = v.shape[-1]
    c = min(GLA_CHUNK, T)
    pad = (-T) % c
    Tp = T + pad
    n = Tp // c

    def blocks(a):
        a = jnp.pad(a.astype(jnp.float32), ((0, 0), (0, pad), (0, 0), (0, 0)))
        return a.reshape(Bn, n, c, H, a.shape[-1]).transpose(0, 3, 1, 2, 4)

    qf = blocks(q) * (dk ** -0.5)
    kf, vf, la = blocks(k), blocks(v), blocks(log_a)
    b = jnp.cumsum(la, axis=3)
    b_mid = b[:, :, :, c // 2:c // 2 + 1, :]
    q_i = qf * jnp.exp(b - b_mid)
    k_i = kf * jnp.exp(b_mid - b)
    A = jnp.einsum('bhnid,bhnjd->bhnij', q_i, k_i)
    causal = jnp.tril(jnp.ones((c, c), dtype=bool))
    A = jnp.where(causal, A, 0.0)
    o_intra = jnp.einsum('bhnij,bhnjv->bhniv', A, vf)
    b_last = b[:, :, :, -1, :]
    k_dec = kf * jnp.exp(b_last[:, :, :, None, :] - b)
    dS = jnp.einsum('bhncd,bhncv->bhndv', k_dec, vf)
    decay = jnp.exp(b_last)

    def step(S, inp):
        dS_n, dec_n = inp
        return dec_n[..., None] * S + dS_n, S

    s_final, s_prev = lax.scan(step, s0.astype(jnp.float32),
                               (jnp.moveaxis(dS, 2, 0), jnp.moveaxis(decay, 2, 0)))
    s_prev = jnp.moveaxis(s_prev, 0, 2)
    o_inter = jnp.einsum('bhncd,bhndv->bhncv', qf * jnp.exp(b), s_prev)
    o = (o_intra + o_inter).transpose(0, 2, 3, 1, 4).reshape(Bn, Tp, H, dv)[:, :T]
    return o, s_final


def token_mixer(h, conv_buf, s0, w_in, conv_w, gate_up, gate_b, gla_g, w_out):
    Bn, T, _ = h.shape
    proj = h @ w_in
    cb, cc, ch, q, k, v, g, lr = jnp.split(proj, IN_SPLITS, axis=-1)
    uc, new_buf = short_conv(cc * ch, conv_buf, conv_w)
    y_conv = cb * uc
    log_a = jax.nn.log_sigmoid((lr @ gate_up + gate_b).astype(jnp.float32)) / GLA_TAU
    o, s_new = gla_chunked(q.reshape(Bn, T, GLA_HEADS, GLA_DK), k.reshape(Bn, T, GLA_HEADS, GLA_DK),
                           v.reshape(Bn, T, GLA_HEADS, GLA_DV), log_a.reshape(Bn, T, GLA_HEADS, GLA_DK), s0)
    o = rmsnorm(o, gla_g.reshape(GLA_HEADS, GLA_DV)).reshape(Bn, T, GLA_V)
    y_gla = (o * jax.nn.silu(g.astype(jnp.float32))).astype(h.dtype)
    y = jnp.concatenate([y_conv, y_gla], axis=-1) @ w_out
    return y, new_buf, s_new


def mem_kv(mem, w_k, w_v):
    Bn = mem.shape[0]
    mk = (mem @ w_k).reshape(Bn, N_MEM, MEM_HEADS, MEM_DH)
    mv = (mem @ w_v).reshape(Bn, N_MEM, MEM_HEADS, MEM_DH)
    return mk, mv


def mem_attend(h, mk, mv, w_q, w_o):
    Bn, T, _ = h.shape
    q = (h @ w_q).reshape(Bn, T, MEM_HEADS, MEM_DH)
    s = jnp.einsum('bthd,bmhd->bhtm', q, mk.astype(q.dtype)).astype(jnp.float32) * (MEM_DH ** -0.5)
    p = jax.nn.softmax(s, axis=-1).astype(h.dtype)
    o = jnp.einsum('bhtm,bmhd->bthd', p, mv.astype(h.dtype)).reshape(Bn, T, D_MODEL)
    return o @ w_o


def hier_moe(h, r_g, r_gb, r_e, r_eb, w_gate, w_up, w_down):
    shp = h.shape
    xt = h.reshape(-1, D_MODEL)
    g_logits = (xt @ r_g).astype(jnp.float32) + r_gb.astype(jnp.float32)
    g_prob = jax.nn.softmax(g_logits, axis=-1)
    g_idx = jnp.argmax(g_logits, axis=-1)
    g_w = jnp.take_along_axis(g_prob, g_idx[:, None], axis=1)[:, 0]
    e_logits = jnp.einsum('nd,gde->nge', xt, r_e).astype(jnp.float32) + r_eb.astype(jnp.float32)
    e_sel = jnp.take_along_axis(e_logits, g_idx[:, None, None], axis=1)[:, 0]
    top_v, top_i = lax.top_k(e_sel, TOP_K_INNER)
    top_w = jax.nn.softmax(top_v, axis=-1) * g_w[:, None]
    expert_id = g_idx[:, None] * EXP_PER_GROUP + top_i
    combine = jnp.sum(jax.nn.one_hot(expert_id, N_EXPERTS, dtype=jnp.float32) * top_w[..., None], axis=1)
    hid = jax.nn.silu(jnp.einsum('nd,edf->nef', xt, w_gate)) * jnp.einsum('nd,edf->nef', xt, w_up)
    hid = hid * combine.astype(hid.dtype)[..., None]
    y = jnp.einsum('nef,efd->nd', hid, w_down)
    return y.reshape(shp)


def layer(x, conv_buf, s0, mk, mv, nm, wi, cw, gu, gb, gn, wo, nmem, wq, wom, nf, rg, rgb, re, reb, wg, wu, wd):
    y, new_buf, s_new = token_mixer(rmsnorm(x, nm), conv_buf, s0, wi, cw, gu, gb, gn, wo)
    x = x + y
    x = x + mem_attend(rmsnorm(x, nmem), mk, mv, wq, wom)
    x = x + hier_moe(rmsnorm(x, nf), rg, rgb, re, reb, wg, wu, wd)
    return x, new_buf, s_new


def setup_inputs(seed: int = 0) -> dict:
    key = jax.random.key(seed)
    ks = jax.random.split(key, 32)
    f32 = jnp.float32

    def nrm(k, shape, scale):
        return jax.random.normal(k, shape, f32) * scale

    def gain(k, shape):
        return 1.0 + 0.02 * jax.random.normal(k, shape, f32)

    return {
        "x_prompt": nrm(ks[0], (BATCH, SEQ, D_MODEL), 1.0),
        "x_sample": nrm(ks[1], (DEC_BATCH, DEC_SEQ, D_MODEL), 1.0),
        "state_conv": nrm(ks[2], (DEPTH, DEC_BATCH, CONV_K - 1, CONV_W), 1.0),
        "state_gla": nrm(ks[3], (DEPTH, DEC_BATCH, GLA_HEADS, GLA_DK, GLA_DV), 1.0),
        "cache_mem_k": nrm(ks[4], (DEPTH, DEC_BATCH, N_MEM, MEM_HEADS, MEM_DH), 1.0),
        "cache_mem_v": nrm(ks[5], (DEPTH, DEC_BATCH, N_MEM, MEM_HEADS, MEM_DH), 1.0),
        "mem_prompt": nrm(ks[6], (BATCH, N_MEM, D_MODEL), 1.0),
        "norm_mix": gain(ks[7], (DEPTH, D_MODEL)),
        "w_in": nrm(ks[8], (DEPTH, D_MODEL, IN_COLS), D_MODEL ** -0.5),
        "conv_w": nrm(ks[9], (DEPTH, CONV_K, CONV_W), CONV_K ** -0.5),
        "gla_gate_up": nrm(ks[10], (DEPTH, GLA_RANK, GLA_K), GLA_RANK ** -0.5),
        "gla_gate_b": nrm(ks[11], (DEPTH, GLA_K), 0.1),
        "gla_out_norm": gain(ks[12], (DEPTH, GLA_V)),
        "w_out": nrm(ks[13], (DEPTH, MIX_W, D_MODEL), MIX_W ** -0.5),
        "norm_mem": gain(ks[14], (DEPTH, D_MODEL)),
        "w_q": nrm(ks[15], (DEPTH, D_MODEL, D_MODEL), D_MODEL ** -0.5),
        "w_k": nrm(ks[16], (DEPTH, D_MODEL, D_MODEL), D_MODEL ** -0.5),
        "w_v": nrm(ks[17], (DEPTH, D_MODEL, D_MODEL), D_MODEL ** -0.5),
        "w_o": nrm(ks[18], (DEPTH, D_MODEL, D_MODEL), D_MODEL ** -0.5),
        "norm_ffn": gain(ks[19], (DEPTH, D_MODEL)),
        "router_group": nrm(ks[20], (DEPTH, D_MODEL, N_GROUPS), D_MODEL ** -0.5),
        "router_group_b": nrm(ks[21], (DEPTH, N_GROUPS), 0.01),
        "router_expert": nrm(ks[22], (DEPTH, N_GROUPS, D_MODEL, EXP_PER_GROUP), D_MODEL ** -0.5),
        "router_expert_b": nrm(ks[23], (DEPTH, N_GROUPS, EXP_PER_GROUP), 0.01),
        "w_gate": nrm(ks[24], (DEPTH, N_EXPERTS, D_MODEL, EXPERT_FF), D_MODEL ** -0.5),
        "w_up": nrm(ks[25], (DEPTH, N_EXPERTS, D_MODEL, EXPERT_FF), D_MODEL ** -0.5),
        "w_down": nrm(ks[26], (DEPTH, N_EXPERTS, EXPERT_FF, D_MODEL), EXPERT_FF ** -0.5),
        "norm_final": gain(ks[27], (D_MODEL,)),
    }


def reference(x_prompt, x_sample, state_conv, state_gla, cache_mem_k, cache_mem_v, mem_prompt,
              norm_mix, w_in, conv_w, gla_gate_up, gla_gate_b, gla_out_norm, w_out,
              norm_mem, w_q, w_k, w_v, w_o, norm_ffn, router_group, router_group_b,
              router_expert, router_expert_b, w_gate, w_up, w_down, norm_final):
    xp, xs = x_prompt, x_sample
    conv_p, gla_p, mk_p, mv_p, conv_s, gla_s = [], [], [], [], [], []
    for l in range(DEPTH):
        wl = (norm_mix[l], w_in[l], conv_w[l], gla_gate_up[l], gla_gate_b[l], gla_out_norm[l], w_out[l],
              norm_mem[l], w_q[l], w_o[l], norm_ffn[l], router_group[l], router_group_b[l],
              router_expert[l], router_expert_b[l], w_gate[l], w_up[l], w_down[l])
        mk, mv = mem_kv(mem_prompt, w_k[l], w_v[l])
        buf0 = jnp.zeros((xp.shape[0], CONV_K - 1, CONV_W), xp.dtype)
        s00 = jnp.zeros((xp.shape[0], GLA_HEADS, GLA_DK, GLA_DV), jnp.float32)
        xp, nb, ns = layer(xp, buf0, s00, mk, mv, *wl)
        conv_p.append(nb)
        gla_p.append(ns)
        mk_p.append(mk)
        mv_p.append(mv)
        xs, nb, ns = layer(xs, state_conv[l], state_gla[l].astype(jnp.float32),
                           cache_mem_k[l], cache_mem_v[l], *wl)
        conv_s.append(nb)
        gla_s.append(ns)
    y_prompt = rmsnorm(xp, norm_final)
    y_sample = rmsnorm(xs, norm_final)
    return (y_prompt, y_sample, jnp.stack(conv_p), jnp.stack(gla_p), jnp.stack(mk_p), jnp.stack(mv_p),
            jnp.stack(conv_s), jnp.stack(gla_s))
```

```python
import functools

import jax
import jax.numpy as jnp
from jax import lax
from jax.experimental import pallas as pl
from jax.experimental.pallas import tpu as pltpu

F32 = jnp.float32
BF16 = jnp.bfloat16
HIGHEST = lax.Precision.HIGHEST

EPS = 1e-6
CONV_W = 512
GLA_HEADS = 4
GLA_DK = 64
GLA_DV = 128
GLA_K = GLA_HEADS * GLA_DK
GLA_V = GLA_HEADS * GLA_DV
GLA_RANK = 16
GLA_TAU = 16.0
GLA_CHUNK = 64
MEM_HEADS = 4
MEM_DH = 256
N_GROUPS = 4
EXP_PER_GROUP = 4
N_EXPERTS = 16
LANES = 128
C_CB, C_CC, C_CH, C_Q, C_K, C_V, C_G, C_LR = 0, 512, 1024, 1536, 1792, 2048, 2560, 3072
VMEM_LIMIT = 52 * 1024 * 1024


def _params(*sem):
    return pltpu.CompilerParams(dimension_semantics=sem, vmem_limit_bytes=VMEM_LIMIT)


def _rms(x, g):
    return x * lax.rsqrt(jnp.mean(x * x, axis=-1, keepdims=True) + EPS) * g


def _dot(a, b, precision=None):
    return jnp.dot(a, b, precision=precision, preferred_element_type=F32)


def _dot_nt(a, b):
    return lax.dot_general(a, b, (((1,), (1,)), ((), ())), preferred_element_type=F32)


def _dot_tn(a, b, precision=None):
    return lax.dot_general(a, b, (((0,), (0,)), ((), ())), precision=precision,
                           preferred_element_type=F32)


def _silu(x):
    return x / (1.0 + jnp.exp(-x))


def _log_sigmoid(x):
    return jnp.minimum(x, 0.0) - jnp.log1p(jnp.exp(-jnp.abs(x)))


def _head_norm_gate(o, g, gg):
    parts = []
    for h in range(GLA_HEADS):
        sl = slice(h * GLA_DV, (h + 1) * GLA_DV)
        parts.append(_rms(o[:, sl], gg[:, sl]))
    return jnp.concatenate(parts, axis=-1) * _silu(g)


def _full(shape):
    nd = len(shape)
    return pl.BlockSpec(shape, lambda *_: (0,) * nd)


def _mix_prompt_kernel(x_ref, g_ref, win_ref, wlr_ref, gup_ref, gb_ref, cw_ref, gg_ref, wout_ref,
                       x1_ref, conv_ref, s_ref, ubuf, s_scr, o_scr):
    t = pl.program_id(1)
    tt = x_ref.shape[1]

    @pl.when(t == 0)
    def _():
        ubuf[0:8, :] = jnp.zeros((8, CONV_W), F32)
        s_scr[...] = jnp.zeros(s_scr.shape, F32)

    x = x_ref[0]
    hb = _rms(x, g_ref[...]).astype(BF16)

    def proj(a, b):
        return _dot(hb, win_ref[:, a:b])

    u = proj(C_CC, C_CH) * proj(C_CH, C_Q)
    ubuf[8:8 + tt, :] = u
    cw = cw_ref[...]
    yc = proj(C_CB, C_CC) * (ubuf[6:6 + tt, :] * cw[0:1] + ubuf[7:7 + tt, :] * cw[1:2] + u * cw[2:3])
    ubuf[6:8, :] = u[tt - 2:tt, :]

    qs = proj(C_Q, C_K) * (GLA_DK ** -0.5)
    k = proj(C_K, C_V)
    vb = proj(C_V, C_G).astype(BF16)
    lr = _dot(hb, wlr_ref[...])
    gate = _dot(lr.astype(BF16), gup_ref[...]) + gb_ref[...]
    la = _log_sigmoid(gate) * (1.0 / GLA_TAU)

    c = GLA_CHUNK
    row = lax.broadcasted_iota(jnp.int32, (c, c), 0)
    col = lax.broadcasted_iota(jnp.int32, (c, c), 1)
    causal = row >= col
    ltri = causal.astype(F32)
    ones = jnp.ones((c, GLA_DV), F32)
    for j in range(tt // c):
        r = slice(j * c, (j + 1) * c)
        la_c, q_c, k_c, v_c = la[r], qs[r], k[r], vb[r]
        b = _dot(ltri, la_c, HIGHEST)
        b_mid = b[c // 2:c // 2 + 1, :]
        b_last = b[c - 1:c, :]
        q_i = (q_c * jnp.exp(b - b_mid)).astype(BF16)
        k_i = (k_c * jnp.exp(b_mid - b)).astype(BF16)
        k_dec = (k_c * jnp.exp(b_last - b)).astype(BF16)
        q_b = (q_c * jnp.exp(b)).astype(BF16)
        for h in range(GLA_HEADS):
            ks = slice(h * GLA_DK, (h + 1) * GLA_DK)
            vh = v_c[:, h * GLA_DV:(h + 1) * GLA_DV]
            a = jnp.where(causal, _dot_nt(q_i[:, ks], k_i[:, ks]), 0.0)
            s_prev = s_scr[h]
            o_scr[r, h * GLA_DV:(h + 1) * GLA_DV] = (
                _dot(a.astype(BF16), vh) + _dot(q_b[:, ks], s_prev.astype(BF16)))
            dec = jnp.exp(_dot_tn(la_c[:, ks], ones, HIGHEST))
            s_scr[h] = dec * s_prev + _dot_tn(k_dec[:, ks], vh)

    yg = _head_norm_gate(o_scr[...], proj(C_G, C_LR), gg_ref[...])
    y = _dot(yc.astype(BF16), wout_ref[0:CONV_W, :]) + _dot(yg.astype(BF16), wout_ref[CONV_W:, :])
    x1_ref[0] = x + y

    @pl.when(t == pl.num_programs(1) - 1)
    def _():
        conv_ref[0] = u[tt - 2:tt, :]
        s_ref[0] = s_scr[...]


def _mix_prompt(x, g, win, wlr, gup, gb, cw, gg, wout, tt):
    bsz, t, d = x.shape
    return pl.pallas_call(
        _mix_prompt_kernel,
        grid=(bsz, t // tt),
        in_specs=[
            pl.BlockSpec((1, tt, d), lambda b, i: (b, i, 0)),
            _full(g.shape), _full(win.shape), _full(wlr.shape), _full(gup.shape), _full(gb.shape),
            _full(cw.shape), _full(gg.shape), _full(wout.shape),
        ],
        out_specs=[
            pl.BlockSpec((1, tt, d), lambda b, i: (b, i, 0)),
            pl.BlockSpec((1, 2, CONV_W), lambda b, i: (b, 0, 0)),
            pl.BlockSpec((1, GLA_HEADS, GLA_DK, GLA_DV), lambda b, i: (b, 0, 0, 0)),
        ],
        out_shape=[
            jax.ShapeDtypeStruct((bsz, t, d), F32),
            jax.ShapeDtypeStruct((bsz, 2, CONV_W), F32),
            jax.ShapeDtypeStruct((bsz, GLA_HEADS, GLA_DK, GLA_DV), F32),
        ],
        scratch_shapes=[
            pltpu.VMEM((8 + tt, CONV_W), F32),
            pltpu.VMEM((GLA_HEADS, GLA_DK, GLA_DV), F32),
            pltpu.VMEM((tt, GLA_V), F32),
        ],
        compiler_params=_params("arbitrary", "arbitrary"),
        name="mix_prompt",
    )(x, g, win, wlr, gup, gb, cw, gg, wout)


def _kv_kernel(m_ref, wk_ref, wv_ref, k_ref, v_ref):
    mb = m_ref[...].astype(BF16)
    k_ref[...] = _dot(mb, wk_ref[...])
    v_ref[...] = _dot(mb, wv_ref[...])


def _mem_kv(mem, wk, wv, tm):
    n, d = mem.shape
    return pl.pallas_call(
        _kv_kernel,
        grid=(n // tm,),
        in_specs=[pl.BlockSpec((tm, d), lambda i: (i, 0)), _full(wk.shape), _full(wv.shape)],
        out_specs=[pl.BlockSpec((tm, d), lambda i: (i, 0))] * 2,
        out_shape=[jax.ShapeDtypeStruct((n, d), F32)] * 2,
        compiler_params=_params("arbitrary"),
        name="mem_kv",
    )(mem, wk, wv)


def _att_prompt_kernel(x_ref, g_ref, wq_ref, wo_ref, mk_ref, mv_ref, out_ref):
    x = x_ref[0]
    xb = _rms(x, g_ref[...]).astype(BF16)
    q = _dot(xb, wq_ref[...])
    outs = []
    for h in range(MEM_HEADS):
        sl = slice(h * MEM_DH, (h + 1) * MEM_DH)
        s = _dot_nt(q[:, sl].astype(BF16), mk_ref[0, :, sl].astype(BF16)) * (MEM_DH ** -0.5)
        e = jnp.exp(s - jnp.max(s, axis=-1, keepdims=True))
        p = e / jnp.sum(e, axis=-1, keepdims=True)
        outs.append(_dot(p.astype(BF16), mv_ref[0, :, sl].astype(BF16)))
    o = jnp.concatenate(outs, axis=-1)
    out_ref[0] = x + _dot(o.astype(BF16), wo_ref[...])


def _att_prompt(x, g, wq, wo, mk, mv, tq):
    bsz, t, d = x.shape
    nm = mk.shape[1]
    return pl.pallas_call(
        _att_prompt_kernel,
        grid=(bsz, t // tq),
        in_specs=[
            pl.BlockSpec((1, tq, d), lambda b, i: (b, i, 0)),
            _full(g.shape), _full(wq.shape), _full(wo.shape),
            pl.BlockSpec((1, nm, d), lambda b, i: (b, 0, 0)),
            pl.BlockSpec((1, nm, d), lambda b, i: (b, 0, 0)),
        ],
        out_specs=pl.BlockSpec((1, tq, d), lambda b, i: (b, i, 0)),
        out_shape=jax.ShapeDtypeStruct((bsz, t, d), F32),
        compiler_params=_params("arbitrary", "arbitrary"),
        name="att_prompt",
    )(x, g, wq, wo, mk, mv)


def _route(logits):
    lane = lax.broadcasted_iota(jnp.int32, logits.shape, 1)
    lanef = lane.astype(F32)
    ninf = -jnp.inf
    big = 1e9
    gl = jnp.where(lane < N_GROUPS, logits, ninf)
    gmax = jnp.max(gl, axis=-1, keepdims=True)
    g_idx = jnp.min(jnp.where(gl == gmax, lanef, big), axis=-1, keepdims=True)
    g_w = 1.0 / jnp.sum(jnp.exp(gl - gmax), axis=-1, keepdims=True)
    grp = ((lane - N_GROUPS) >> 2).astype(F32)
    emask = (lane >= N_GROUPS) & (lane < N_GROUPS + N_EXPERTS) & (grp == g_idx)
    el = jnp.where(emask, logits, ninf)
    m1 = jnp.max(el, axis=-1, keepdims=True)
    i1 = jnp.min(jnp.where(el == m1, lanef, big), axis=-1, keepdims=True)
    el2 = jnp.where(lanef == i1, ninf, el)
    m2 = jnp.max(el2, axis=-1, keepdims=True)
    i2 = jnp.min(jnp.where(el2 == m2, lanef, big), axis=-1, keepdims=True)
    tail = jnp.exp(m2 - m1)
    w1 = g_w / (1.0 + tail)
    w2 = g_w * tail / (1.0 + tail)
    return jnp.where(lanef == i1, w1, 0.0) + jnp.where(lanef == i2, w2, 0.0)


def _moe_kernel(x_ref, g_ref, rw_ref, rb_ref, wg_ref, wu_ref, wd_ref, out_ref, xn_scr, comb_scr, acc_scr):
    e = pl.program_id(1)

    @pl.when(e == 0)
    def _():
        x = x_ref[...]
        xn = _rms(x, g_ref[...])
        xn_scr[...] = xn.astype(BF16)
        comb_scr[...] = _route(_dot(xn, rw_ref[...], HIGHEST) + rb_ref[...])
        acc_scr[...] = x

    xb = xn_scr[...]
    lane = lax.broadcasted_iota(jnp.int32, comb_scr.shape, 1)
    cmb = jnp.sum(jnp.where(lane == e + N_GROUPS, comb_scr[...], 0.0), axis=-1, keepdims=True)
    hid = _silu(_dot(xb, wg_ref[0])) * _dot(xb, wu_ref[0]) * cmb
    acc_scr[...] += _dot(hid.astype(BF16), wd_ref[0])

    @pl.when(e == pl.num_programs(1) - 1)
    def _():
        out_ref[...] = acc_scr[...]


def _moe(x, g, rw, rb, wg, wu, wd, tm):
    n, d = x.shape
    ne, _, ff = wg.shape
    return pl.pallas_call(
        _moe_kernel,
        grid=(n // tm, ne),
        in_specs=[
            pl.BlockSpec((tm, d), lambda i, e: (i, 0)),
            _full(g.shape), _full(rw.shape), _full(rb.shape),
            pl.BlockSpec((1, d, ff), lambda i, e: (e, 0, 0)),
            pl.BlockSpec((1, d, ff), lambda i, e: (e, 0, 0)),
            pl.BlockSpec((1, ff, d), lambda i, e: (e, 0, 0)),
        ],
        out_specs=pl.BlockSpec((tm, d), lambda i, e: (i, 0)),
        out_shape=jax.ShapeDtypeStruct((n, d), F32),
        scratch_shapes=[pltpu.VMEM((tm, d), BF16), pltpu.VMEM((tm, LANES), F32), pltpu.VMEM((tm, d), F32)],
        compiler_params=_params("arbitrary", "arbitrary"),
        name="moe",
    )(x, g, rw, rb, wg, wu, wd)


def _norm_kernel(x_ref, g_ref, o_ref):
    o_ref[...] = _rms(x_ref[...], g_ref[...])


def _final_norm(x, g, tm):
    n, d = x.shape
    return pl.pallas_call(
        _norm_kernel,
        grid=(n // tm,),
        in_specs=[pl.BlockSpec((tm, d), lambda i: (i, 0)), _full(g.shape)],
        out_specs=pl.BlockSpec((tm, d), lambda i: (i, 0)),
        out_shape=jax.ShapeDtypeStruct((n, d), F32),
        compiler_params=_params("arbitrary"),
        name="final_norm",
    )(x, g)


def _mix_sample_in_kernel(x_ref, g_ref, win_ref, wlr_ref, gup_ref, gb_ref, cw_ref, b0_ref, b1_ref,
                          yc_ref, u_ref, q_ref, k_ref, v_ref, gate_ref, la_ref):
    hb = _rms(x_ref[...], g_ref[...]).astype(BF16)

    def proj(a, b):
        return _dot(hb, win_ref[:, a:b])

    u = proj(C_CC, C_CH) * proj(C_CH, C_Q)
    cw = cw_ref[...]
    yc_ref[...] = proj(C_CB, C_CC) * (b0_ref[...] * cw[0:1] + b1_ref[...] * cw[1:2] + u * cw[2:3])
    u_ref[...] = u
    q_ref[...] = proj(C_Q, C_K) * (GLA_DK ** -0.5)
    k_ref[...] = proj(C_K, C_V)
    v_ref[...] = proj(C_V, C_G)
    gate_ref[...] = proj(C_G, C_LR)
    lr = _dot(hb, wlr_ref[...])
    gate = _dot(lr.astype(BF16), gup_ref[...]) + gb_ref[...]
    la_ref[...] = _log_sigmoid(gate) * (1.0 / GLA_TAU)


def _mix_sample_in(x, g, win, wlr, gup, gb, cw, b0, b1):
    n = x.shape[0]
    args = (x, g, win, wlr, gup, gb, cw, b0, b1)
    widths = (CONV_W, CONV_W, GLA_K, GLA_K, GLA_V, GLA_V, GLA_K)
    return pl.pallas_call(
        _mix_sample_in_kernel,
        grid=(1,),
        in_specs=[_full(a.shape) for a in args],
        out_specs=[_full((n, w)) for w in widths],
        out_shape=[jax.ShapeDtypeStruct((n, w), F32) for w in widths],
        compiler_params=_params("arbitrary"),
        name="mix_sample_in",
    )(*args)


def _gla_step_kernel(q_ref, k_ref, v_ref, la_ref, s0_ref, s_ref, o_ref):
    nb = q_ref.shape[0]
    a = jnp.exp(la_ref[...])
    kb = k_ref[...]
    qb = q_ref[...].astype(BF16)
    vb = v_ref[...].astype(BF16)
    rows = lax.broadcasted_iota(jnp.int32, (nb, 1), 0)
    spread = (lax.broadcasted_iota(jnp.int32, (nb, nb * GLA_DV), 0)
              == lax.broadcasted_iota(jnp.int32, (nb, nb * GLA_DV), 1) // GLA_DV).astype(F32)
    for h in range(GLA_HEADS):
        ks = slice(h * GLA_DK, (h + 1) * GLA_DK)
        vh = vb[:, h * GLA_DV:(h + 1) * GLA_DV]
        a_cols = _dot_tn(a[:, ks], spread, HIGHEST)
        o_h = jnp.zeros((nb, GLA_DV), F32)
        for n in range(nb):
            k_n = jnp.where(rows == n, kb[:, ks], 0.0).astype(BF16)
            s_new = a_cols[:, n * GLA_DV:(n + 1) * GLA_DV] * s0_ref[0, n, h] + _dot_tn(k_n, vh)
            s_ref[n, h] = s_new
            o_h = o_h + jnp.where(rows == n, _dot(qb[:, ks], s_new.astype(BF16)), 0.0)
        o_ref[:, h * GLA_DV:(h + 1) * GLA_DV] = o_h


def _gla_step(q, k, v, la, state, layer, nb):
    n = q.shape[0]
    sshape = (GLA_HEADS, GLA_DK, GLA_DV)
    return pl.pallas_call(
        _gla_step_kernel,
        grid=(n // nb,),
        in_specs=[
            pl.BlockSpec((nb, GLA_K), lambda i: (i, 0)),
            pl.BlockSpec((nb, GLA_K), lambda i: (i, 0)),
            pl.BlockSpec((nb, GLA_V), lambda i: (i, 0)),
            pl.BlockSpec((nb, GLA_K), lambda i: (i, 0)),
            pl.BlockSpec((1, nb) + sshape, lambda i: (layer, i, 0, 0, 0)),
        ],
        out_specs=[
            pl.BlockSpec((nb,) + sshape, lambda i: (i, 0, 0, 0)),
            pl.BlockSpec((nb, GLA_V), lambda i: (i, 0)),
        ],
        out_shape=[jax.ShapeDtypeStruct((n,) + sshape, F32), jax.ShapeDtypeStruct((n, GLA_V), F32)],
        compiler_params=_params("arbitrary"),
        name="gla_step",
    )(q, k, v, la, state)


def _mix_sample_out_kernel(x_ref, yc_ref, o_ref, gate_ref, gg_ref, wout_ref, gm_ref, wq_ref, x1_ref, q_ref):
    yg = _head_norm_gate(o_ref[...], gate_ref[...], gg_ref[...])
    y = _dot(yc_ref[...].astype(BF16), wout_ref[0:CONV_W, :]) + _dot(yg.astype(BF16), wout_ref[CONV_W:, :])
    x1 = x_ref[...] + y
    x1_ref[...] = x1
    q_ref[...] = _dot(_rms(x1, gm_ref[...]).astype(BF16), wq_ref[...])


def _mix_sample_out(x, yc, o, gate, gg, wout, gm, wq):
    args = (x, yc, o, gate, gg, wout, gm, wq)
    return pl.pallas_call(
        _mix_sample_out_kernel,
        grid=(1,),
        in_specs=[_full(a.shape) for a in args],
        out_specs=[_full(x.shape)] * 2,
        out_shape=[jax.ShapeDtypeStruct(x.shape, F32)] * 2,
        compiler_params=_params("arbitrary"),
        name="mix_sample_out",
    )(*args)


def _att_sample_kernel(q_ref, k_ref, v_ref, o_ref):
    nb = q_ref.shape[0]
    rows = lax.broadcasted_iota(jnp.int32, (nb, 1), 0)
    qb = q_ref[...].astype(BF16)
    for h in range(MEM_HEADS):
        sl = slice(h * MEM_DH, (h + 1) * MEM_DH)
        s = jnp.zeros((nb, k_ref.shape[2]), F32)
        for n in range(nb):
            s = s + jnp.where(rows == n, _dot_nt(qb[:, sl], k_ref[0, n, :, sl].astype(BF16)), 0.0)
        s = s * (MEM_DH ** -0.5)
        e = jnp.exp(s - jnp.max(s, axis=-1, keepdims=True))
        p = (e / jnp.sum(e, axis=-1, keepdims=True)).astype(BF16)
        o = jnp.zeros((nb, MEM_DH), F32)
        for n in range(nb):
            o = o + jnp.where(rows == n, _dot(p, v_ref[0, n, :, sl].astype(BF16)), 0.0)
        o_ref[:, sl] = o


def _att_sample(q, ck, cv, layer, nb):
    n, d = q.shape
    nm = ck.shape[2]
    return pl.pallas_call(
        _att_sample_kernel,
        grid=(n // nb,),
        in_specs=[
            pl.BlockSpec((nb, d), lambda i: (i, 0)),
            pl.BlockSpec((1, nb, nm, d), lambda i: (layer, i, 0, 0)),
            pl.BlockSpec((1, nb, nm, d), lambda i: (layer, i, 0, 0)),
        ],
        out_specs=pl.BlockSpec((nb, d), lambda i: (i, 0)),
        out_shape=jax.ShapeDtypeStruct((n, d), F32),
        compiler_params=_params("arbitrary"),
        name="att_sample",
    )(q, ck, cv)


def _oproj_kernel(x_ref, o_ref, wo_ref, out_ref):
    out_ref[...] = x_ref[...] + _dot(o_ref[...].astype(BF16), wo_ref[...])


def _oproj(x, o, wo):
    return pl.pallas_call(
        _oproj_kernel,
        grid=(1,),
        in_specs=[_full(x.shape), _full(o.shape), _full(wo.shape)],
        out_specs=_full(x.shape),
        out_shape=jax.ShapeDtypeStruct(x.shape, F32),
        compiler_params=_params("arbitrary"),
        name="oproj",
    )(x, o, wo)


def kernel(x_prompt, x_sample, state_conv, state_gla, cache_mem_k, cache_mem_v, mem_prompt, norm_mix, w_in, conv_w, gla_gate_up, gla_gate_b, gla_out_norm, w_out, norm_mem, w_q, w_k, w_v, w_o, norm_ffn, router_group, router_group_b, router_expert, router_expert_b, w_gate, w_up, w_down, norm_final):
    depth = w_in.shape[0]
    bsz, t, d = x_prompt.shape
    ns = x_sample.shape[0]
    nm = mem_prompt.shape[1]
    n_tok = bsz * t

    tt = min(256, t)
    tq = min(512, t)
    tm_moe = min(1024, n_tok)
    tm_kv = min(512, bsz * nm)
    nb_gla = min(8, ns)
    nb_att = min(8, ns)

    row = lambda a: a.reshape(1, -1)
    ck = cache_mem_k.reshape(depth, ns, nm, d)
    cv = cache_mem_v.reshape(depth, ns, nm, d)
    mem2 = mem_prompt.reshape(bsz * nm, d)

    xp = x_prompt
    xs = x_sample.reshape(ns, d)
    conv_p, gla_p, mk_p, mv_p, conv_s, gla_s = [], [], [], [], [], []
    for l in range(depth):
        win = w_in[l, :, :C_LR].astype(BF16)
        wlr = jnp.pad(w_in[l, :, C_LR:], ((0, 0), (0, LANES - GLA_RANK))).astype(BF16)
        gup = jnp.pad(gla_gate_up[l], ((0, LANES - GLA_RANK), (0, 0))).astype(BF16)
        gb = row(gla_gate_b[l])
        gg = row(gla_out_norm[l])
        wout = w_out[l].astype(BF16)
        wq, wk, wv, wo = (w[l].astype(BF16) for w in (w_q, w_k, w_v, w_o))
        rw = jnp.concatenate([router_group[l], router_expert[l].transpose(1, 0, 2).reshape(d, N_EXPERTS)], axis=1)
        rw = jnp.pad(rw, ((0, 0), (0, LANES - rw.shape[1])))
        rb = jnp.concatenate([router_group_b[l], router_expert_b[l].reshape(-1)])
        rb = row(jnp.pad(rb, (0, LANES - rb.shape[0])))
        wg, wu, wd = w_gate[l].astype(BF16), w_up[l].astype(BF16), w_down[l].astype(BF16)
        moe_w = (row(norm_ffn[l]), rw, rb, wg, wu, wd)

        mk, mv = _mem_kv(mem2, wk, wv, tm_kv)
        mk_p.append(mk.reshape(bsz, nm, MEM_HEADS, MEM_DH))
        mv_p.append(mv.reshape(bsz, nm, MEM_HEADS, MEM_DH))
        xp, nbuf, ns_p = _mix_prompt(xp, row(norm_mix[l]), win, wlr, gup, gb, conv_w[l], gg, wout, tt)
        conv_p.append(nbuf)
        gla_p.append(ns_p)
        xp = _att_prompt(xp, row(norm_mem[l]), wq, wo, mk.reshape(bsz, nm, d), mv.reshape(bsz, nm, d), tq)
        xp = _moe(xp.reshape(n_tok, d), *moe_w, tm_moe).reshape(bsz, t, d)

        yc, u, q, k, v, gate, la = _mix_sample_in(
            xs, row(norm_mix[l]), win, wlr, gup, gb, conv_w[l], state_conv[l, :, 0], state_conv[l, :, 1])
        conv_s.append(jnp.stack([state_conv[l, :, 1], u], axis=1))
        s_new, o = _gla_step(q, k, v, la, state_gla, l, nb_gla)
        gla_s.append(s_new)
        xs, qa = _mix_sample_out(xs, yc, o, gate, gg, wout, row(norm_mem[l]), wq)
        xs = _oproj(xs, _att_sample(qa, ck, cv, l, nb_att), wo)
        xs = _moe(xs, *moe_w, min(tm_moe, ns))

    y_prompt = _final_norm(xp.reshape(n_tok, d), row(norm_final), tm_moe).reshape(bsz, t, d)
    y_sample = _final_norm(xs, row(norm_final), ns).reshape(ns, 1, d)
    return (y_prompt, y_sample, jnp.stack(conv_p), jnp.stack(gla_p), jnp.stack(mk_p), jnp.stack(mv_p),
            jnp.stack(conv_s), jnp.stack(gla_s))
```

```python
import functools

import jax
import jax.numpy as jnp
from jax import lax
from jax.experimental import pallas as pl
from jax.experimental.pallas import tpu as pltpu

F32 = jnp.float32
BF16 = jnp.bfloat16
HIGHEST = lax.Precision.HIGHEST

EPS = 1e-6
CONV_W = 512
GLA_HEADS = 4
GLA_DK = 64
GLA_DV = 128
GLA_K = GLA_HEADS * GLA_DK
GLA_V = GLA_HEADS * GLA_DV
GLA_RANK = 16
GLA_TAU = 16.0
GLA_CHUNK = 64
MEM_HEADS = 4
MEM_DH = 256
N_GROUPS = 4
EXP_PER_GROUP = 4
N_EXPERTS = 16
LANES = 128
C_CB, C_CC, C_CH, C_Q, C_K, C_V, C_G, C_LR = 0, 512, 1024, 1536, 1792, 2048, 2560, 3072
VMEM_LIMIT = 52 * 1024 * 1024


def _params(*sem):
    return pltpu.CompilerParams(dimension_semantics=sem, vmem_limit_bytes=VMEM_LIMIT)


def _rms(x, g):
    return x * lax.rsqrt(jnp.mean(x * x, axis=-1, keepdims=True) + EPS) * g


def _dot(a, b, precision=None):
    return jnp.dot(a, b, precision=precision, preferred_element_type=F32)


def _dot_nt(a, b):
    return lax.dot_general(a, b, (((1,), (1,)), ((), ())), preferred_element_type=F32)


def _dot_tn(a, b, precision=None):
    return lax.dot_general(a, b, (((0,), (0,)), ((), ())), precision=precision,
                           preferred_element_type=F32)


def _silu(x):
    return x / (1.0 + jnp.exp(-x))


def _log_sigmoid(x):
    return jnp.minimum(x, 0.0) - jnp.log1p(jnp.exp(-jnp.abs(x)))


def _head_norm_gate(o, g, gg):
    parts = []
    for h in range(GLA_HEADS):
        sl = slice(h * GLA_DV, (h + 1) * GLA_DV)
        parts.append(_rms(o[:, sl], gg[:, sl]))
    return jnp.concatenate(parts, axis=-1) * _silu(g)


def _full(shape):
    nd = len(shape)
    return pl.BlockSpec(shape, lambda *_: (0,) * nd)


def _mix_prompt_kernel(x_ref, g_ref, win_ref, wlr_ref, gup_ref, gb_ref, cw_ref, gg_ref, wout_ref,
                       x1_ref, conv_ref, s_ref, ubuf, s_scr, o_scr):
    t = pl.program_id(1)
    tt = x_ref.shape[1]

    @pl.when(t == 0)
    def _():
        ubuf[0:8, :] = jnp.zeros((8, CONV_W), F32)
        s_scr[...] = jnp.zeros(s_scr.shape, F32)

    x = x_ref[0]
    hb = _rms(x, g_ref[...]).astype(BF16)

    def proj(a, b):
        return _dot(hb, win_ref[:, a:b])

    u = proj(C_CC, C_CH) * proj(C_CH, C_Q)
    ubuf[8:8 + tt, :] = u
    cw = cw_ref[...]
    yc = proj(C_CB, C_CC) * (ubuf[6:6 + tt, :] * cw[0:1] + ubuf[7:7 + tt, :] * cw[1:2] + u * cw[2:3])
    ubuf[6:8, :] = u[tt - 2:tt, :]

    qs = proj(C_Q, C_K) * (GLA_DK ** -0.5)
    k = proj(C_K, C_V)
    vb = proj(C_V, C_G).astype(BF16)
    lr = _dot(hb, wlr_ref[...])
    gate = _dot(lr.astype(BF16), gup_ref[...]) + gb_ref[...]
    la = _log_sigmoid(gate) * (1.0 / GLA_TAU)

    c = GLA_CHUNK
    row = lax.broadcasted_iota(jnp.int32, (c, c), 0)
    col = lax.broadcasted_iota(jnp.int32, (c, c), 1)
    causal = row >= col
    ltri = causal.astype(F32)
    ones = jnp.ones((c, GLA_DV), F32)
    for j in range(tt // c):
        r = slice(j * c, (j + 1) * c)
        la_c, q_c, k_c, v_c = la[r], qs[r], k[r], vb[r]
        b = _dot(ltri, la_c, HIGHEST)
        b_mid = b[c // 2:c // 2 + 1, :]
        b_last = b[c - 1:c, :]
        q_i = (q_c * jnp.exp(b - b_mid)).astype(BF16)
        k_i = (k_c * jnp.exp(b_mid - b)).astype(BF16)
        k_dec = (k_c * jnp.exp(b_last - b)).astype(BF16)
        q_b = (q_c * jnp.exp(b)).astype(BF16)
        for h in range(GLA_HEADS):
            ks = slice(h * GLA_DK, (h + 1) * GLA_DK)
            vh = v_c[:, h * GLA_DV:(h + 1) * GLA_DV]
            a = jnp.where(causal, _dot_nt(q_i[:, ks], k_i[:, ks]), 0.0)
            s_prev = s_scr[h]
            o_scr[r, h * GLA_DV:(h + 1) * GLA_DV] = (
                _dot(a.astype(BF16), vh) + _dot(q_b[:, ks], s_prev.astype(BF16)))
            dec = jnp.exp(_dot_tn(la_c[:, ks], ones, HIGHEST))
            s_scr[h] = dec * s_prev + _dot_tn(k_dec[:, ks], vh)

    yg = _head_norm_gate(o_scr[...], proj(C_G, C_LR), gg_ref[...])
    y = _dot(yc.astype(BF16), wout_ref[0:CONV_W, :]) + _dot(yg.astype(BF16), wout_ref[CONV_W:, :])
    x1_ref[0] = x + y

    @pl.when(t == pl.num_programs(1) - 1)
    def _():
        conv_ref[0] = u[tt - 2:tt, :]
        s_ref[0] = s_scr[...]


def _mix_prompt(x, g, win, wlr, gup, gb, cw, gg, wout, tt):
    bsz, t, d = x.shape
    return pl.pallas_call(
        _mix_prompt_kernel,
        grid=(bsz, t // tt),
        in_specs=[
            pl.BlockSpec((1, tt, d), lambda b, i: (b, i, 0)),
            _full(g.shape), _full(win.shape), _full(wlr.shape), _full(gup.shape), _full(gb.shape),
            _full(cw.shape), _full(gg.shape), _full(wout.shape),
        ],
        out_specs=[
            pl.BlockSpec((1, tt, d), lambda b, i: (b, i, 0)),
            pl.BlockSpec((1, 2, CONV_W), lambda b, i: (b, 0, 0)),
            pl.BlockSpec((1, GLA_HEADS, GLA_DK, GLA_DV), lambda b, i: (b, 0, 0, 0)),
        ],
        out_shape=[
            jax.ShapeDtypeStruct((bsz, t, d), F32),
            jax.ShapeDtypeStruct((bsz, 2, CONV_W), F32),
            jax.ShapeDtypeStruct((bsz, GLA_HEADS, GLA_DK, GLA_DV), F32),
        ],
        scratch_shapes=[
            pltpu.VMEM((8 + tt, CONV_W), F32),
            pltpu.VMEM((GLA_HEADS, GLA_DK, GLA_DV), F32),
            pltpu.VMEM((tt, GLA_V), F32),
        ],
        compiler_params=_params("arbitrary", "arbitrary"),
        name="mix_prompt",
    )(x, g, win, wlr, gup, gb, cw, gg, wout)


def _kv_kernel(m_ref, wk_ref, wv_ref, k_ref, v_ref):
    mb = m_ref[...].astype(BF16)
    k_ref[...] = _dot(mb, wk_ref[...])
    v_ref[...] = _dot(mb, wv_ref[...])


def _mem_kv(mem, wk, wv, tm):
    n, d = mem.shape
    return pl.pallas_call(
        _kv_kernel,
        grid=(n // tm,),
        in_specs=[pl.BlockSpec((tm, d), lambda i: (i, 0)), _full(wk.shape), _full(wv.shape)],
        out_specs=[pl.BlockSpec((tm, d), lambda i: (i, 0))] * 2,
        out_shape=[jax.ShapeDtypeStruct((n, d), F32)] * 2,
        compiler_params=_params("arbitrary"),
        name="mem_kv",
    )(mem, wk, wv)


def _att_prompt_kernel(x_ref, g_ref, wq_ref, wo_ref, mk_ref, mv_ref, out_ref):
    x = x_ref[0]
    xb = _rms(x, g_ref[...]).astype(BF16)
    q = _dot(xb, wq_ref[...])
    outs = []
    for h in range(MEM_HEADS):
        sl = slice(h * MEM_DH, (h + 1) * MEM_DH)
        s = _dot_nt(q[:, sl].astype(BF16), mk_ref[0, :, sl].astype(BF16)) * (MEM_DH ** -0.5)
        e = jnp.exp(s - jnp.max(s, axis=-1, keepdims=True))
        p = e / jnp.sum(e, axis=-1, keepdims=True)
        outs.append(_dot(p.astype(BF16), mv_ref[0, :, sl].astype(BF16)))
    o = jnp.concatenate(outs, axis=-1)
    out_ref[0] = x + _dot(o.astype(BF16), wo_ref[...])


def _att_prompt(x, g, wq, wo, mk, mv, tq):
    bsz, t, d = x.shape
    nm = mk.shape[1]
    return pl.pallas_call(
        _att_prompt_kernel,
        grid=(bsz, t // tq),
        in_specs=[
            pl.BlockSpec((1, tq, d), lambda b, i: (b, i, 0)),
            _full(g.shape), _full(wq.shape), _full(wo.shape),
            pl.BlockSpec((1, nm, d), lambda b, i: (b, 0, 0)),
            pl.BlockSpec((1, nm, d), lambda b, i: (b, 0, 0)),
        ],
        out_specs=pl.BlockSpec((1, tq, d), lambda b, i: (b, i, 0)),
        out_shape=jax.ShapeDtypeStruct((bsz, t, d), F32),
        compiler_params=_params("arbitrary", "arbitrary"),
        name="att_prompt",
    )(x, g, wq, wo, mk, mv)


def _route(logits):
    lane = lax.broadcasted_iota(jnp.int32, logits.shape, 1)
    lanef = lane.astype(F32)
    ninf = -jnp.inf
    big = 1e9
    gl = jnp.where(lane < N_GROUPS, logits, ninf)
    gmax = jnp.max(gl, axis=-1, keepdims=True)
    g_idx = jnp.min(jnp.where(gl == gmax, lanef, big), axis=-1, keepdims=True)
    g_w = 1.0 / jnp.sum(jnp.exp(gl - gmax), axis=-1, keepdims=True)
    grp = ((lane - N_GROUPS) >> 2).astype(F32)
    emask = (lane >= N_GROUPS) & (lane < N_GROUPS + N_EXPERTS) & (grp == g_idx)
    el = jnp.where(emask, logits, ninf)
    m1 = jnp.max(el, axis=-1, keepdims=True)
    i1 = jnp.min(jnp.where(el == m1, lanef, big), axis=-1, keepdims=True)
    el2 = jnp.where(lanef == i1, ninf, el)
    m2 = jnp.max(el2, axis=-1, keepdims=True)
    i2 = jnp.min(jnp.where(el2 == m2, lanef, big), axis=-1, keepdims=True)
    tail = jnp.exp(m2 - m1)
    w1 = g_w / (1.0 + tail)
    w2 = g_w * tail / (1.0 + tail)
    return jnp.where(lanef == i1, w1, 0.0) + jnp.where(lanef == i2, w2, 0.0)


def _moe_kernel(x_ref, g_ref, rw_ref, rb_ref, wg_ref, wu_ref, wd_ref, out_ref, xn_scr, comb_scr, acc_scr):
    e = pl.program_id(1)

    @pl.when(e == 0)
    def _():
        x = x_ref[...]
        xn = _rms(x, g_ref[...])
        xn_scr[...] = xn.astype(BF16)
        comb_scr[...] = _route(_dot(xn, rw_ref[...], HIGHEST) + rb_ref[...])
        acc_scr[...] = x

    xb = xn_scr[...]
    lane = lax.broadcasted_iota(jnp.int32, comb_scr.shape, 1)
    cmb = jnp.sum(jnp.where(lane == e + N_GROUPS, comb_scr[...], 0.0), axis=-1, keepdims=True)
    hid = _silu(_dot(xb, wg_ref[0])) * _dot(xb, wu_ref[0]) * cmb
    acc_scr[...] += _dot(hid.astype(BF16), wd_ref[0])

    @pl.when(e == pl.num_programs(1) - 1)
    def _():
        out_ref[...] = acc_scr[...]


def _moe(x, g, rw, rb, wg, wu, wd, tm):
    n, d = x.shape
    ne, _, ff = wg.shape
    return pl.pallas_call(
        _moe_kernel,
        grid=(n // tm, ne),
        in_specs=[
            pl.BlockSpec((tm, d), lambda i, e: (i, 0)),
            _full(g.shape), _full(rw.shape), _full(rb.shape),
            pl.BlockSpec((1, d, ff), lambda i, e: (e, 0, 0)),
            pl.BlockSpec((1, d, ff), lambda i, e: (e, 0, 0)),
            pl.BlockSpec((1, ff, d), lambda i, e: (e, 0, 0)),
        ],
        out_specs=pl.BlockSpec((tm, d), lambda i, e: (i, 0)),
        out_shape=jax.ShapeDtypeStruct((n, d), F32),
        scratch_shapes=[pltpu.VMEM((tm, d), BF16), pltpu.VMEM((tm, LANES), F32), pltpu.VMEM((tm, d), F32)],
        compiler_params=_params("arbitrary", "arbitrary"),
        name="moe",
    )(x, g, rw, rb, wg, wu, wd)


def _norm_kernel(x_ref, g_ref, o_ref):
    o_ref[...] = _rms(x_ref[...], g_ref[...])


def _final_norm(x, g, tm):
    n, d = x.shape
    return pl.pallas_call(
        _norm_kernel,
        grid=(n // tm,),
        in_specs=[pl.BlockSpec((tm, d), lambda i: (i, 0)), _full(g.shape)],
        out_specs=pl.BlockSpec((tm, d), lambda i: (i, 0)),
        out_shape=jax.ShapeDtypeStruct((n, d), F32),
        compiler_params=_params("arbitrary"),
        name="final_norm",
    )(x, g)


def _mix_sample_in_kernel(x_ref, g_ref, win_ref, wlr_ref, gup_ref, gb_ref, cw_ref, b0_ref, b1_ref,
                          yc_ref, u_ref, q_ref, k_ref, v_ref, gate_ref, la_ref):
    hb = _rms(x_ref[...], g_ref[...]).astype(BF16)

    def proj(a, b):
        return _dot(hb, win_ref[:, a:b])

    u = proj(C_CC, C_CH) * proj(C_CH, C_Q)
    cw = cw_ref[...]
    yc_ref[...] = proj(C_CB, C_CC) * (b0_ref[...] * cw[0:1] + b1_ref[...] * cw[1:2] + u * cw[2:3])
    u_ref[...] = u
    q_ref[...] = proj(C_Q, C_K) * (GLA_DK ** -0.5)
    k_ref[...] = proj(C_K, C_V)
    v_ref[...] = proj(C_V, C_G)
    gate_ref[...] = proj(C_G, C_LR)
    lr = _dot(hb, wlr_ref[...])
    gate = _dot(lr.astype(BF16), gup_ref[...]) + gb_ref[...]
    la_ref[...] = _log_sigmoid(gate) * (1.0 / GLA_TAU)


def _mix_sample_in(x, g, win, wlr, gup, gb, cw, b0, b1):
    n = x.shape[0]
    args = (x, g, win, wlr, gup, gb, cw, b0, b1)
    widths = (CONV_W, CONV_W, GLA_K, GLA_K, GLA_V, GLA_V, GLA_K)
    return pl.pallas_call(
        _mix_sample_in_kernel,
        grid=(1,),
        in_specs=[_full(a.shape) for a in args],
        out_specs=[_full((n, w)) for w in widths],
        out_shape=[jax.ShapeDtypeStruct((n, w), F32) for w in widths],
        compiler_params=_params("arbitrary"),
        name="mix_sample_in",
    )(*args)


def _gla_step_kernel(q_ref, k_ref, v_ref, la_ref, s0_ref, s_ref, o_ref):
    nb = q_ref.shape[0]
    a = jnp.exp(la_ref[...])
    kb = k_ref[...]
    qb = q_ref[...].astype(BF16)
    vb = v_ref[...].astype(BF16)
    rows = lax.broadcasted_iota(jnp.int32, (nb, 1), 0)
    spread = (lax.broadcasted_iota(jnp.int32, (nb, nb * GLA_DV), 0)
              == lax.broadcasted_iota(jnp.int32, (nb, nb * GLA_DV), 1) // GLA_DV).astype(F32)
    for h in range(GLA_HEADS):
        ks = slice(h * GLA_DK, (h + 1) * GLA_DK)
        vh = vb[:, h * GLA_DV:(h + 1) * GLA_DV]
        a_cols = _dot_tn(a[:, ks], spread, HIGHEST)
        o_h = jnp.zeros((nb, GLA_DV), F32)
        for n in range(nb):
            k_n = jnp.where(rows == n, kb[:, ks], 0.0).astype(BF16)
            s_new = a_cols[:, n * GLA_DV:(n + 1) * GLA_DV] * s0_ref[0, n, h] + _dot_tn(k_n, vh)
            s_ref[n, h] = s_new
            o_h = o_h + jnp.where(rows == n, _dot(qb[:, ks], s_new.astype(BF16)), 0.0)
        o_ref[:, h * GLA_DV:(h + 1) * GLA_DV] = o_h


def _gla_step(q, k, v, la, state, layer, nb):
    n = q.shape[0]
    sshape = (GLA_HEADS, GLA_DK, GLA_DV)
    return pl.pallas_call(
        _gla_step_kernel,
        grid=(n // nb,),
        in_specs=[
            pl.BlockSpec((nb, GLA_K), lambda i: (i, 0)),
            pl.BlockSpec((nb, GLA_K), lambda i: (i, 0)),
            pl.BlockSpec((nb, GLA_V), lambda i: (i, 0)),
            pl.BlockSpec((nb, GLA_K), lambda i: (i, 0)),
            pl.BlockSpec((1, nb) + sshape, lambda i: (layer, i, 0, 0, 0)),
        ],
        out_specs=[
            pl.BlockSpec((nb,) + sshape, lambda i: (i, 0, 0, 0)),
            pl.BlockSpec((nb, GLA_V), lambda i: (i, 0)),
        ],
        out_shape=[jax.ShapeDtypeStruct((n,) + sshape, F32), jax.ShapeDtypeStruct((n, GLA_V), F32)],
        compiler_params=_params("arbitrary"),
        name="gla_step",
    )(q, k, v, la, state)


def _mix_sample_out_kernel(x_ref, yc_ref, o_ref, gate_ref, gg_ref, wout_ref, gm_ref, wq_ref, x1_ref, q_ref):
    yg = _head_norm_gate(o_ref[...], gate_ref[...], gg_ref[...])
    y = _dot(yc_ref[...].astype(BF16), wout_ref[0:CONV_W, :]) + _dot(yg.astype(BF16), wout_ref[CONV_W:, :])
    x1 = x_ref[...] + y
    x1_ref[...] = x1
    q_ref[...] = _dot(_rms(x1, gm_ref[...]).astype(BF16), wq_ref[...])


def _mix_sample_out(x, yc, o, gate, gg, wout, gm, wq):
    args = (x, yc, o, gate, gg, wout, gm, wq)
    return pl.pallas_call(
        _mix_sample_out_kernel,
        grid=(1,),
        in_specs=[_full(a.shape) for a in args],
        out_specs=[_full(x.shape)] * 2,
        out_shape=[jax.ShapeDtypeStruct(x.shape, F32)] * 2,
        compiler_params=_params("arbitrary"),
        name="mix_sample_out",
    )(*args)


ATT_ROWS = 2 * MEM_HEADS


def _class_allreduce(x, op):
    n = x.shape[-1]
    shift = ATT_ROWS
    while shift < n:
        x = op(x, pltpu.roll(x, shift, axis=1))
        shift *= 2
    return x


def _att_sample_kernel(q_ref, k_ref, v_ref, o_ref):
    nb = q_ref.shape[0]
    ncol = k_ref.shape[2]
    diag = (lax.broadcasted_iota(jnp.int32, (ATT_ROWS, ncol), 0)
            == (lax.broadcasted_iota(jnp.int32, (ATT_ROWS, ncol), 1) & (ATT_ROWS - 1)))
    rows = lax.broadcasted_iota(jnp.int32, (nb, 1), 0)
    t = jnp.zeros((nb, ncol), F32)
    for n in range(nb):
        sc = _dot_nt(q_ref[n].astype(BF16), k_ref[0, n].astype(BF16))
        t = t + jnp.where(rows == n, jnp.sum(jnp.where(diag, sc, 0.0), axis=0, keepdims=True), 0.0)
    valid = (lax.broadcasted_iota(jnp.int32, (nb, ncol), 1) & (ATT_ROWS - 1)) < MEM_HEADS
    s = jnp.where(valid, (t + pltpu.roll(t, ncol - MEM_HEADS, axis=1)) * (MEM_DH ** -0.5), 0.0)
    e = jnp.where(valid, jnp.exp(s - _class_allreduce(s, jnp.maximum)), 0.0)
    den = jnp.where(valid, _class_allreduce(e, jnp.add), 1.0)
    p = e / den
    p = p + pltpu.roll(p, MEM_HEADS, axis=1)
    for n in range(nb):
        p_n = jnp.where(diag, jnp.broadcast_to(p[n:n + 1, :], (ATT_ROWS, ncol)), 0.0)
        o_ref[n] = _dot(p_n.astype(BF16), v_ref[0, n].astype(BF16))


def _att_sample(q, ck, cv, layer, nb):
    n = q.shape[0]
    ncol = ck.shape[2]
    return pl.pallas_call(
        _att_sample_kernel,
        grid=(n // nb,),
        in_specs=[
            pl.BlockSpec((nb, ATT_ROWS, LANES), lambda i: (i, 0, 0)),
            pl.BlockSpec((1, nb, ncol, LANES), lambda i: (layer, i, 0, 0)),
            pl.BlockSpec((1, nb, ncol, LANES), lambda i: (layer, i, 0, 0)),
        ],
        out_specs=pl.BlockSpec((nb, ATT_ROWS, LANES), lambda i: (i, 0, 0)),
        out_shape=jax.ShapeDtypeStruct((n, ATT_ROWS, LANES), F32),
        compiler_params=_params("arbitrary"),
        name="att_sample",
    )(q, ck, cv)


def _oproj_kernel(x_ref, o_ref, wo_ref, out_ref):
    out_ref[...] = x_ref[...] + _dot(o_ref[...].astype(BF16), wo_ref[...])


def _oproj(x, o, wo):
    return pl.pallas_call(
        _oproj_kernel,
        grid=(1,),
        in_specs=[_full(x.shape), _full(o.shape), _full(wo.shape)],
        out_specs=_full(x.shape),
        out_shape=jax.ShapeDtypeStruct(x.shape, F32),
        compiler_params=_params("arbitrary"),
        name="oproj",
    )(x, o, wo)


def kernel(x_prompt, x_sample, state_conv, state_gla, cache_mem_k, cache_mem_v, mem_prompt, norm_mix, w_in, conv_w, gla_gate_up, gla_gate_b, gla_out_norm, w_out, norm_mem, w_q, w_k, w_v, w_o, norm_ffn, router_group, router_group_b, router_expert, router_expert_b, w_gate, w_up, w_down, norm_final):
    depth = w_in.shape[0]
    bsz, t, d = x_prompt.shape
    ns = x_sample.shape[0]
    nm = mem_prompt.shape[1]
    n_tok = bsz * t

    tt = min(256, t)
    tq = min(512, t)
    tm_moe = min(1024, n_tok)
    tm_kv = min(512, bsz * nm)
    nb_gla = min(8, ns)
    nb_att = min(8, ns)

    row = lambda a: a.reshape(1, -1)
    mem2 = mem_prompt.reshape(bsz * nm, d)

    def tile_rows(c):
        c = c.reshape(depth, ns, nm, MEM_HEADS, 2, LANES).transpose(0, 1, 2, 4, 3, 5)
        return c.reshape(depth, ns, nm * ATT_ROWS, LANES)

    ck, cv = tile_rows(cache_mem_k), tile_rows(cache_mem_v)

    xp = x_prompt
    xs = x_sample.reshape(ns, d)
    conv_p, gla_p, mk_p, mv_p, conv_s, gla_s = [], [], [], [], [], []
    for l in range(depth):
        win = w_in[l, :, :C_LR].astype(BF16)
        wlr = jnp.pad(w_in[l, :, C_LR:], ((0, 0), (0, LANES - GLA_RANK))).astype(BF16)
        gup = jnp.pad(gla_gate_up[l], ((0, LANES - GLA_RANK), (0, 0))).astype(BF16)
        gb = row(gla_gate_b[l])
        gg = row(gla_out_norm[l])
        wout = w_out[l].astype(BF16)
        wq, wk, wv, wo = (w[l].astype(BF16) for w in (w_q, w_k, w_v, w_o))
        rw = jnp.concatenate([router_group[l], router_expert[l].transpose(1, 0, 2).reshape(d, N_EXPERTS)], axis=1)
        rw = jnp.pad(rw, ((0, 0), (0, LANES - rw.shape[1])))
        rb = jnp.concatenate([router_group_b[l], router_expert_b[l].reshape(-1)])
        rb = row(jnp.pad(rb, (0, LANES - rb.shape[0])))
        wg, wu, wd = w_gate[l].astype(BF16), w_up[l].astype(BF16), w_down[l].astype(BF16)
        moe_w = (row(norm_ffn[l]), rw, rb, wg, wu, wd)

        mk, mv = _mem_kv(mem2, wk, wv, tm_kv)
        mk_p.append(mk.reshape(bsz, nm, MEM_HEADS, MEM_DH))
        mv_p.append(mv.reshape(bsz, nm, MEM_HEADS, MEM_DH))
        xp, nbuf, ns_p = _mix_prompt(xp, row(norm_mix[l]), win, wlr, gup, gb, conv_w[l], gg, wout, tt)
        conv_p.append(nbuf)
        gla_p.append(ns_p)
        xp = _att_prompt(xp, row(norm_mem[l]), wq, wo, mk.reshape(bsz, nm, d), mv.reshape(bsz, nm, d), tq)
        xp = _moe(xp.reshape(n_tok, d), *moe_w, tm_moe).reshape(bsz, t, d)

        yc, u, q, k, v, gate, la = _mix_sample_in(
            xs, row(norm_mix[l]), win, wlr, gup, gb, conv_w[l], state_conv[l, :, 0], state_conv[l, :, 1])
        conv_s.append(jnp.stack([state_conv[l, :, 1], u], axis=1))
        s_new, o = _gla_step(q, k, v, la, state_gla, l, nb_gla)
        gla_s.append(s_new)
        wq_s = wq.reshape(d, MEM_HEADS, 2, LANES).transpose(0, 2, 1, 3).reshape(d, d)
        wo_s = wo.reshape(MEM_HEADS, 2, LANES, d).transpose(1, 0, 2, 3).reshape(d, d)
        xs, qa = _mix_sample_out(xs, yc, o, gate, gg, wout, row(norm_mem[l]), wq_s)
        oa = _att_sample(qa.reshape(ns, ATT_ROWS, LANES), ck, cv, l, nb_att)
        xs = _oproj(xs, oa.reshape(ns, d), wo_s)
        xs = _moe(xs, *moe_w, min(tm_moe, ns))

    y_prompt = _final_norm(xp.reshape(n_tok, d), row(norm_final), tm_moe).reshape(bsz, t, d)
    y_sample = _final_norm(xs, row(norm_final), ns).reshape(ns, 1, d)
    return (y_prompt, y_sample, jnp.stack(conv_p), jnp.stack(gla_p), jnp.stack(mk_p), jnp.stack(mv_p),
            jnp.stack(conv_s), jnp.stack(gla_s))
```

```python
import functools

import jax
import jax.numpy as jnp
from jax import lax
from jax.experimental import pallas as pl
from jax.experimental.pallas import tpu as pltpu

F32 = jnp.float32
BF16 = jnp.bfloat16
HIGHEST = lax.Precision.HIGHEST

EPS = 1e-6
CONV_W = 512
GLA_HEADS = 4
GLA_DK = 64
GLA_DV = 128
GLA_K = GLA_HEADS * GLA_DK
GLA_V = GLA_HEADS * GLA_DV
GLA_RANK = 16
GLA_TAU = 16.0
GLA_CHUNK = 64
MEM_HEADS = 4
MEM_DH = 256
N_GROUPS = 4
EXP_PER_GROUP = 4
N_EXPERTS = 16
LANES = 128
C_CB, C_CC, C_CH, C_Q, C_K, C_V, C_G, C_LR = 0, 512, 1024, 1536, 1792, 2048, 2560, 3072
VMEM_LIMIT = 52 * 1024 * 1024


def _params(*sem):
    return pltpu.CompilerParams(dimension_semantics=sem, vmem_limit_bytes=VMEM_LIMIT)


def _rms(x, g):
    return x * lax.rsqrt(jnp.mean(x * x, axis=-1, keepdims=True) + EPS) * g


def _dot(a, b, precision=None):
    return jnp.dot(a, b, precision=precision, preferred_element_type=F32)


def _dot_nt(a, b):
    return lax.dot_general(a, b, (((1,), (1,)), ((), ())), preferred_element_type=F32)


def _dot_tn(a, b, precision=None):
    return lax.dot_general(a, b, (((0,), (0,)), ((), ())), precision=precision,
                           preferred_element_type=F32)


def _silu(x):
    return x / (1.0 + jnp.exp(-x))


def _log_sigmoid(x):
    return jnp.minimum(x, 0.0) - jnp.log1p(jnp.exp(-jnp.abs(x)))


def _head_norm_gate(o, g, gg):
    parts = []
    for h in range(GLA_HEADS):
        sl = slice(h * GLA_DV, (h + 1) * GLA_DV)
        parts.append(_rms(o[:, sl], gg[:, sl]))
    return jnp.concatenate(parts, axis=-1) * _silu(g)


def _full(shape):
    nd = len(shape)
    return pl.BlockSpec(shape, lambda *_: (0,) * nd)


def _mix_prompt_kernel(x_ref, g_ref, win_ref, wlr_ref, gup_ref, gb_ref, cw_ref, gg_ref, wout_ref,
                       x1_ref, conv_ref, s_ref, ubuf, s_scr, o_scr):
    t = pl.program_id(1)
    tt = x_ref.shape[1]

    @pl.when(t == 0)
    def _():
        ubuf[0:8, :] = jnp.zeros((8, CONV_W), F32)
        s_scr[...] = jnp.zeros(s_scr.shape, F32)

    x = x_ref[0]
    hb = _rms(x, g_ref[...]).astype(BF16)

    def proj(a, b):
        return _dot(hb, win_ref[:, a:b])

    u = proj(C_CC, C_CH) * proj(C_CH, C_Q)
    ubuf[8:8 + tt, :] = u
    cw = cw_ref[...]
    yc = proj(C_CB, C_CC) * (ubuf[6:6 + tt, :] * cw[0:1] + ubuf[7:7 + tt, :] * cw[1:2] + u * cw[2:3])
    ubuf[6:8, :] = u[tt - 2:tt, :]

    qs = proj(C_Q, C_K) * (GLA_DK ** -0.5)
    k = proj(C_K, C_V)
    vb = proj(C_V, C_G).astype(BF16)
    lr = _dot(hb, wlr_ref[...])
    gate = _dot(lr.astype(BF16), gup_ref[...]) + gb_ref[...]
    la = _log_sigmoid(gate) * (1.0 / GLA_TAU)

    c = GLA_CHUNK
    row = lax.broadcasted_iota(jnp.int32, (c, c), 0)
    col = lax.broadcasted_iota(jnp.int32, (c, c), 1)
    causal = row >= col
    ltri = causal.astype(F32)
    ones = jnp.ones((c, GLA_DV), F32)
    for j in range(tt // c):
        r = slice(j * c, (j + 1) * c)
        la_c, q_c, k_c, v_c = la[r], qs[r], k[r], vb[r]
        b = _dot(ltri, la_c, HIGHEST)
        b_mid = b[c // 2:c // 2 + 1, :]
        b_last = b[c - 1:c, :]
        q_i = (q_c * jnp.exp(b - b_mid)).astype(BF16)
        k_i = (k_c * jnp.exp(b_mid - b)).astype(BF16)
        k_dec = (k_c * jnp.exp(b_last - b)).astype(BF16)
        q_b = (q_c * jnp.exp(b)).astype(BF16)
        for h in range(GLA_HEADS):
            ks = slice(h * GLA_DK, (h + 1) * GLA_DK)
            vh = v_c[:, h * GLA_DV:(h + 1) * GLA_DV]
            a = jnp.where(causal, _dot_nt(q_i[:, ks], k_i[:, ks]), 0.0)
            s_prev = s_scr[h]
            o_scr[r, h * GLA_DV:(h + 1) * GLA_DV] = (
                _dot(a.astype(BF16), vh) + _dot(q_b[:, ks], s_prev.astype(BF16)))
            dec = jnp.exp(_dot_tn(la_c[:, ks], ones, HIGHEST))
            s_scr[h] = dec * s_prev + _dot_tn(k_dec[:, ks], vh)

    yg = _head_norm_gate(o_scr[...], proj(C_G, C_LR), gg_ref[...])
    y = _dot(yc.astype(BF16), wout_ref[0:CONV_W, :]) + _dot(yg.astype(BF16), wout_ref[CONV_W:, :])
    x1_ref[0] = x + y

    @pl.when(t == pl.num_programs(1) - 1)
    def _():
        conv_ref[0] = u[tt - 2:tt, :]
        s_ref[0] = s_scr[...]


def _mix_prompt(x, g, win, wlr, gup, gb, cw, gg, wout, tt):
    bsz, t, d = x.shape
    return pl.pallas_call(
        _mix_prompt_kernel,
        grid=(bsz, t // tt),
        in_specs=[
            pl.BlockSpec((1, tt, d), lambda b, i: (b, i, 0)),
            _full(g.shape), _full(win.shape), _full(wlr.shape), _full(gup.shape), _full(gb.shape),
            _full(cw.shape), _full(gg.shape), _full(wout.shape),
        ],
        out_specs=[
            pl.BlockSpec((1, tt, d), lambda b, i: (b, i, 0)),
            pl.BlockSpec((1, 2, CONV_W), lambda b, i: (b, 0, 0)),
            pl.BlockSpec((1, GLA_HEADS, GLA_DK, GLA_DV), lambda b, i: (b, 0, 0, 0)),
        ],
        out_shape=[
            jax.ShapeDtypeStruct((bsz, t, d), F32),
            jax.ShapeDtypeStruct((bsz, 2, CONV_W), F32),
            jax.ShapeDtypeStruct((bsz, GLA_HEADS, GLA_DK, GLA_DV), F32),
        ],
        scratch_shapes=[
            pltpu.VMEM((8 + tt, CONV_W), F32),
            pltpu.VMEM((GLA_HEADS, GLA_DK, GLA_DV), F32),
            pltpu.VMEM((tt, GLA_V), F32),
        ],
        compiler_params=_params("arbitrary", "arbitrary"),
        name="mix_prompt",
    )(x, g, win, wlr, gup, gb, cw, gg, wout)


def _kv_kernel(m_ref, wk_ref, wv_ref, k_ref, v_ref):
    mb = m_ref[...].astype(BF16)
    k_ref[...] = _dot(mb, wk_ref[...])
    v_ref[...] = _dot(mb, wv_ref[...])


def _mem_kv(mem, wk, wv, tm):
    n, d = mem.shape
    return pl.pallas_call(
        _kv_kernel,
        grid=(n // tm,),
        in_specs=[pl.BlockSpec((tm, d), lambda i: (i, 0)), _full(wk.shape), _full(wv.shape)],
        out_specs=[pl.BlockSpec((tm, d), lambda i: (i, 0))] * 2,
        out_shape=[jax.ShapeDtypeStruct((n, d), F32)] * 2,
        compiler_params=_params("arbitrary"),
        name="mem_kv",
    )(mem, wk, wv)


def _att_prompt_kernel(x_ref, g_ref, wq_ref, wo_ref, mk_ref, mv_ref, out_ref):
    x = x_ref[0]
    xb = _rms(x, g_ref[...]).astype(BF16)
    q = _dot(xb, wq_ref[...])
    outs = []
    for h in range(MEM_HEADS):
        sl = slice(h * MEM_DH, (h + 1) * MEM_DH)
        s = _dot_nt(q[:, sl].astype(BF16), mk_ref[0, :, sl].astype(BF16)) * (MEM_DH ** -0.5)
        e = jnp.exp(s - jnp.max(s, axis=-1, keepdims=True))
        p = e / jnp.sum(e, axis=-1, keepdims=True)
        outs.append(_dot(p.astype(BF16), mv_ref[0, :, sl].astype(BF16)))
    o = jnp.concatenate(outs, axis=-1)
    out_ref[0] = x + _dot(o.astype(BF16), wo_ref[...])


def _att_prompt(x, g, wq, wo, mk, mv, tq):
    bsz, t, d = x.shape
    nm = mk.shape[1]
    return pl.pallas_call(
        _att_prompt_kernel,
        grid=(bsz, t // tq),
        in_specs=[
            pl.BlockSpec((1, tq, d), lambda b, i: (b, i, 0)),
            _full(g.shape), _full(wq.shape), _full(wo.shape),
            pl.BlockSpec((1, nm, d), lambda b, i: (b, 0, 0)),
            pl.BlockSpec((1, nm, d), lambda b, i: (b, 0, 0)),
        ],
        out_specs=pl.BlockSpec((1, tq, d), lambda b, i: (b, i, 0)),
        out_shape=jax.ShapeDtypeStruct((bsz, t, d), F32),
        compiler_params=_params("arbitrary", "arbitrary"),
        name="att_prompt",
    )(x, g, wq, wo, mk, mv)


def _route_top2(logits):
    lane = lax.broadcasted_iota(jnp.int32, logits.shape, 1)
    lanef = lane.astype(F32)
    ninf = -jnp.inf
    big = 1e9
    gl = jnp.where(lane < N_GROUPS, logits, ninf)
    gmax = jnp.max(gl, axis=-1, keepdims=True)
    g_idx = jnp.min(jnp.where(gl == gmax, lanef, big), axis=-1, keepdims=True)
    g_w = 1.0 / jnp.sum(jnp.exp(gl - gmax), axis=-1, keepdims=True)
    grp = ((lane - N_GROUPS) >> 2).astype(F32)
    emask = (lane >= N_GROUPS) & (lane < N_GROUPS + N_EXPERTS) & (grp == g_idx)
    el = jnp.where(emask, logits, ninf)
    m1 = jnp.max(el, axis=-1, keepdims=True)
    i1 = jnp.min(jnp.where(el == m1, lanef, big), axis=-1, keepdims=True)
    el2 = jnp.where(lanef == i1, ninf, el)
    m2 = jnp.max(el2, axis=-1, keepdims=True)
    i2 = jnp.min(jnp.where(el2 == m2, lanef, big), axis=-1, keepdims=True)
    tail = jnp.exp(m2 - m1)
    w1 = g_w / (1.0 + tail)
    w2 = g_w * tail / (1.0 + tail)
    return g_idx, i1, i2, w1, w2


def _route(logits):
    _, i1, i2, w1, w2 = _route_top2(logits)
    lanef = lax.broadcasted_iota(jnp.int32, logits.shape, 1).astype(F32)
    return jnp.where(lanef == i1, w1, 0.0) + jnp.where(lanef == i2, w2, 0.0)


N_PAIRS = 6
N_CLASSES = N_GROUPS * N_PAIRS
INFO_W = LANES


def _moe_route_kernel(x_ref, g_ref, rw_ref, rb_ref, xe_ref, meta_ref, cnt_ref, carry):
    i = pl.program_id(0)
    tr = x_ref.shape[0]

    @pl.when(i == 0)
    def _():
        carry[...] = jnp.zeros(carry.shape, F32)

    x = x_ref[...]
    xn = _rms(x, g_ref[...])
    g_idx, i1, i2, w1, w2 = _route_top2(_dot(xn, rw_ref[...], HIGHEST) + rb_ref[...])
    first_lo = i1 < i2
    lo = jnp.where(first_lo, i1, i2) - N_GROUPS - EXP_PER_GROUP * g_idx
    hi = jnp.where(first_lo, i2, i1) - N_GROUPS - EXP_PER_GROUP * g_idx
    cls = g_idx * N_PAIRS + lo * (7.0 - lo) * 0.5 + hi - lo - 1.0
    w_lo = jnp.where(first_lo, w1, w2)
    w_hi = jnp.where(first_lo, w2, w1)

    lane = lax.broadcasted_iota(jnp.int32, (tr, LANES), 1)
    xe_ref[:, 0:x.shape[1]] = x
    xe_ref[:, x.shape[1]:] = jnp.where(lane == 0, w_lo, 0.0) + jnp.where(lane == 1, w_hi, 0.0)

    onehot = (lane.astype(F32) == cls).astype(BF16)
    ltri = (lax.broadcasted_iota(jnp.int32, (tr, tr), 0)
            >= lax.broadcasted_iota(jnp.int32, (tr, tr), 1)).astype(BF16)
    prefix = _dot(ltri, onehot) + carry[...]
    rank = jnp.sum(jnp.where(onehot > 0, prefix, 0.0), axis=-1, keepdims=True) - 1.0
    carry[...] = prefix[tr - 1:tr, :]
    meta_ref[...] = (jnp.where(lane == 0, cls, 0.0) + jnp.where(lane == 1, rank, 0.0)).astype(jnp.int32)
    cnt_ref[...] = prefix[tr - 1:tr, :].astype(jnp.int32)


def _moe_route(x, g, rw, rb, tr):
    n, d = x.shape
    return pl.pallas_call(
        _moe_route_kernel,
        grid=(n // tr,),
        in_specs=[pl.BlockSpec((tr, d), lambda i: (i, 0)), _full(g.shape), _full(rw.shape), _full(rb.shape)],
        out_specs=[
            pl.BlockSpec((tr, d + INFO_W), lambda i: (i, 0)),
            pl.BlockSpec((tr, LANES), lambda i: (i, 0)),
            _full((1, LANES)),
        ],
        out_shape=[
            jax.ShapeDtypeStruct((n, d + INFO_W), F32),
            jax.ShapeDtypeStruct((n, LANES), jnp.int32),
            jax.ShapeDtypeStruct((1, LANES), jnp.int32),
        ],
        scratch_shapes=[pltpu.VMEM((1, LANES), F32)],
        compiler_params=_params("arbitrary"),
        name="moe_route",
    )(x, g, rw, rb)


def _row_permute_kernel(src_idx, dst_idx, src_ref, dst_in_ref, dst_ref, sem, *, rows):
    del dst_in_ref
    base = pl.program_id(0) * rows

    def row_copy(s, t):
        return pltpu.make_async_copy(src_ref.at[pl.ds(s, 1)], dst_ref.at[pl.ds(t, 1)], sem)

    def start(r, c):
        row_copy(src_idx[base + r], dst_idx[base + r]).start()
        return c

    lax.fori_loop(0, rows, start, 0, unroll=8)

    def wait(r, c):
        row_copy(0, 0).wait()
        return c

    lax.fori_loop(0, rows, wait, 0, unroll=8)


def _row_permute(src, dst_init, src_idx, dst_idx, rows):
    n = src_idx.shape[0]
    return pl.pallas_call(
        functools.partial(_row_permute_kernel, rows=rows),
        grid_spec=pltpu.PrefetchScalarGridSpec(
            num_scalar_prefetch=2,
            grid=(n // rows,),
            in_specs=[pl.BlockSpec(memory_space=pl.ANY), pl.BlockSpec(memory_space=pl.ANY)],
            out_specs=pl.BlockSpec(memory_space=pl.ANY),
            scratch_shapes=[pltpu.SemaphoreType.DMA(())],
        ),
        out_shape=jax.ShapeDtypeStruct(dst_init.shape, dst_init.dtype),
        input_output_aliases={3: 0},
        compiler_params=_params("arbitrary"),
        name="row_permute",
    )(src_idx, dst_idx, src, dst_init)


def _moe_sorted_kernel(e_lo, e_hi, valid, xs_ref, g_ref, wgu_lo, wgu_hi, wd_lo, wd_hi, out_ref):
    t = pl.program_id(0)
    d = out_ref.shape[1]
    ff = wd_lo.shape[1]

    @pl.when(valid[t] > 0)
    def _():
        x = xs_ref[:, 0:d]
        xb = _rms(x, g_ref[...]).astype(BF16)
        y = x
        for wgu, wd, col in ((wgu_lo, wd_lo, d), (wgu_hi, wd_hi, d + 1)):
            gu = _dot(xb, wgu[0])
            hid = _silu(gu[:, 0:ff]) * gu[:, ff:] * xs_ref[:, col:col + 1]
            y = y + _dot(hid.astype(BF16), wd[0])
        out_ref[...] = y

    @pl.when(valid[t] == 0)
    def _():
        out_ref[...] = jnp.zeros(out_ref.shape, F32)


def _moe_sorted(xs, g, wgu, wd, e_lo, e_hi, valid, tm):
    npad, de = xs.shape
    d = de - INFO_W
    ff2 = wgu.shape[2]
    return pl.pallas_call(
        _moe_sorted_kernel,
        grid_spec=pltpu.PrefetchScalarGridSpec(
            num_scalar_prefetch=3,
            grid=(npad // tm,),
            in_specs=[
                pl.BlockSpec((tm, de), lambda t, lo, hi, v: (t, 0)),
                pl.BlockSpec(g.shape, lambda t, lo, hi, v: (0, 0)),
                pl.BlockSpec((1, d, ff2), lambda t, lo, hi, v: (lo[t], 0, 0)),
                pl.BlockSpec((1, d, ff2), lambda t, lo, hi, v: (hi[t], 0, 0)),
                pl.BlockSpec((1, ff2 // 2, d), lambda t, lo, hi, v: (lo[t], 0, 0)),
                pl.BlockSpec((1, ff2 // 2, d), lambda t, lo, hi, v: (hi[t], 0, 0)),
            ],
            out_specs=pl.BlockSpec((tm, d), lambda t, lo, hi, v: (t, 0)),
        ),
        out_shape=jax.ShapeDtypeStruct((npad, d), F32),
        compiler_params=_params("arbitrary"),
        name="moe_sorted",
    )(e_lo, e_hi, valid, xs, g, wgu, wgu, wd, wd)


def _moe_sparse(x, g, rw, rb, wgu, wd, tm, tr, rows):
    n, d = x.shape
    xe, meta, counts = _moe_route(x, g, rw, rb, tr)
    cls, rank = meta[:, 0], meta[:, 1]
    cnt = counts[0, :N_CLASSES]
    padded = ((cnt + tm - 1) // tm) * tm
    ends = jnp.cumsum(padded)
    pos = (ends - padded)[cls] + rank
    n_tiles = n // tm + N_CLASSES
    tile_start = jnp.arange(n_tiles, dtype=jnp.int32) * tm
    tile_cls = jnp.minimum(jnp.searchsorted(ends, tile_start, side="right"), N_CLASSES - 1).astype(jnp.int32)
    valid = (tile_start < ends[-1]).astype(jnp.int32)
    tile_cls = jnp.where(valid > 0, tile_cls, tile_cls[jnp.maximum(ends[-1] // tm - 1, 0)])
    pair_lo = jnp.array([0, 0, 0, 1, 1, 2], jnp.int32)
    pair_hi = jnp.array([1, 2, 3, 2, 3, 3], jnp.int32)
    e_lo = (tile_cls // N_PAIRS) * EXP_PER_GROUP + pair_lo[tile_cls % N_PAIRS]
    e_hi = (tile_cls // N_PAIRS) * EXP_PER_GROUP + pair_hi[tile_cls % N_PAIRS]

    ident = jnp.arange(n, dtype=jnp.int32)
    xs = _row_permute(xe, jnp.zeros((n_tiles * tm, d + INFO_W), F32), ident, pos, rows)
    ys = _moe_sorted(xs, g, wgu, wd, e_lo, e_hi, valid, tm)
    return _row_permute(ys, x, pos, ident, rows)


def _moe_kernel(x_ref, g_ref, rw_ref, rb_ref, wg_ref, wu_ref, wd_ref, out_ref, xn_scr, comb_scr, acc_scr):
    e = pl.program_id(1)

    @pl.when(e == 0)
    def _():
        x = x_ref[...]
        xn = _rms(x, g_ref[...])
        xn_scr[...] = xn.astype(BF16)
        comb_scr[...] = _route(_dot(xn, rw_ref[...], HIGHEST) + rb_ref[...])
        acc_scr[...] = x

    xb = xn_scr[...]
    lane = lax.broadcasted_iota(jnp.int32, comb_scr.shape, 1)
    cmb = jnp.sum(jnp.where(lane == e + N_GROUPS, comb_scr[...], 0.0), axis=-1, keepdims=True)
    hid = _silu(_dot(xb, wg_ref[0])) * _dot(xb, wu_ref[0]) * cmb
    acc_scr[...] += _dot(hid.astype(BF16), wd_ref[0])

    @pl.when(e == pl.num_programs(1) - 1)
    def _():
        out_ref[...] = acc_scr[...]


def _moe(x, g, rw, rb, wg, wu, wd, tm):
    n, d = x.shape
    ne, _, ff = wg.shape
    return pl.pallas_call(
        _moe_kernel,
        grid=(n // tm, ne),
        in_specs=[
            pl.BlockSpec((tm, d), lambda i, e: (i, 0)),
            _full(g.shape), _full(rw.shape), _full(rb.shape),
            pl.BlockSpec((1, d, ff), lambda i, e: (e, 0, 0)),
            pl.BlockSpec((1, d, ff), lambda i, e: (e, 0, 0)),
            pl.BlockSpec((1, ff, d), lambda i, e: (e, 0, 0)),
        ],
        out_specs=pl.BlockSpec((tm, d), lambda i, e: (i, 0)),
        out_shape=jax.ShapeDtypeStruct((n, d), F32),
        scratch_shapes=[pltpu.VMEM((tm, d), BF16), pltpu.VMEM((tm, LANES), F32), pltpu.VMEM((tm, d), F32)],
        compiler_params=_params("arbitrary", "arbitrary"),
        name="moe",
    )(x, g, rw, rb, wg, wu, wd)


def _norm_kernel(x_ref, g_ref, o_ref):
    o_ref[...] = _rms(x_ref[...], g_ref[...])


def _final_norm(x, g, tm):
    n, d = x.shape
    return pl.pallas_call(
        _norm_kernel,
        grid=(n // tm,),
        in_specs=[pl.BlockSpec((tm, d), lambda i: (i, 0)), _full(g.shape)],
        out_specs=pl.BlockSpec((tm, d), lambda i: (i, 0)),
        out_shape=jax.ShapeDtypeStruct((n, d), F32),
        compiler_params=_params("arbitrary"),
        name="final_norm",
    )(x, g)


def _mix_sample_in_kernel(x_ref, g_ref, win_ref, wlr_ref, gup_ref, gb_ref, cw_ref, b0_ref, b1_ref,
                          yc_ref, u_ref, q_ref, k_ref, v_ref, gate_ref, la_ref):
    hb = _rms(x_ref[...], g_ref[...]).astype(BF16)

    def proj(a, b):
        return _dot(hb, win_ref[:, a:b])

    u = proj(C_CC, C_CH) * proj(C_CH, C_Q)
    cw = cw_ref[...]
    yc_ref[...] = proj(C_CB, C_CC) * (b0_ref[...] * cw[0:1] + b1_ref[...] * cw[1:2] + u * cw[2:3])
    u_ref[...] = u
    q_ref[...] = proj(C_Q, C_K) * (GLA_DK ** -0.5)
    k_ref[...] = proj(C_K, C_V)
    v_ref[...] = proj(C_V, C_G)
    gate_ref[...] = proj(C_G, C_LR)
    lr = _dot(hb, wlr_ref[...])
    gate = _dot(lr.astype(BF16), gup_ref[...]) + gb_ref[...]
    la_ref[...] = _log_sigmoid(gate) * (1.0 / GLA_TAU)


def _mix_sample_in(x, g, win, wlr, gup, gb, cw, b0, b1):
    n = x.shape[0]
    args = (x, g, win, wlr, gup, gb, cw, b0, b1)
    widths = (CONV_W, CONV_W, GLA_K, GLA_K, GLA_V, GLA_V, GLA_K)
    return pl.pallas_call(
        _mix_sample_in_kernel,
        grid=(1,),
        in_specs=[_full(a.shape) for a in args],
        out_specs=[_full((n, w)) for w in widths],
        out_shape=[jax.ShapeDtypeStruct((n, w), F32) for w in widths],
        compiler_params=_params("arbitrary"),
        name="mix_sample_in",
    )(*args)


def _gla_step_kernel(q_ref, k_ref, v_ref, la_ref, s0_ref, s_ref, o_ref):
    nb = q_ref.shape[0]
    a = jnp.exp(la_ref[...])
    kb = k_ref[...]
    qb = q_ref[...].astype(BF16)
    vb = v_ref[...].astype(BF16)
    rows = lax.broadcasted_iota(jnp.int32, (nb, 1), 0)
    spread = (lax.broadcasted_iota(jnp.int32, (nb, nb * GLA_DV), 0)
              == lax.broadcasted_iota(jnp.int32, (nb, nb * GLA_DV), 1) // GLA_DV).astype(F32)
    for h in range(GLA_HEADS):
        ks = slice(h * GLA_DK, (h + 1) * GLA_DK)
        vh = vb[:, h * GLA_DV:(h + 1) * GLA_DV]
        a_cols = _dot_tn(a[:, ks], spread, HIGHEST)
        o_h = jnp.zeros((nb, GLA_DV), F32)
        for n in range(nb):
            k_n = jnp.where(rows == n, kb[:, ks], 0.0).astype(BF16)
            s_new = a_cols[:, n * GLA_DV:(n + 1) * GLA_DV] * s0_ref[0, n, h] + _dot_tn(k_n, vh)
            s_ref[n, h] = s_new
            o_h = o_h + jnp.where(rows == n, _dot(qb[:, ks], s_new.astype(BF16)), 0.0)
        o_ref[:, h * GLA_DV:(h + 1) * GLA_DV] = o_h


def _gla_step(q, k, v, la, state, layer, nb):
    n = q.shape[0]
    sshape = (GLA_HEADS, GLA_DK, GLA_DV)
    return pl.pallas_call(
        _gla_step_kernel,
        grid=(n // nb,),
        in_specs=[
            pl.BlockSpec((nb, GLA_K), lambda i: (i, 0)),
            pl.BlockSpec((nb, GLA_K), lambda i: (i, 0)),
            pl.BlockSpec((nb, GLA_V), lambda i: (i, 0)),
            pl.BlockSpec((nb, GLA_K), lambda i: (i, 0)),
            pl.BlockSpec((1, nb) + sshape, lambda i: (layer, i, 0, 0, 0)),
        ],
        out_specs=[
            pl.BlockSpec((nb,) + sshape, lambda i: (i, 0, 0, 0)),
            pl.BlockSpec((nb, GLA_V), lambda i: (i, 0)),
        ],
        out_shape=[jax.ShapeDtypeStruct((n,) + sshape, F32), jax.ShapeDtypeStruct((n, GLA_V), F32)],
        compiler_params=_params("arbitrary"),
        name="gla_step",
    )(q, k, v, la, state)


def _mix_sample_out_kernel(x_ref, yc_ref, o_ref, gate_ref, gg_ref, wout_ref, gm_ref, wq_ref, x1_ref, q_ref):
    yg = _head_norm_gate(o_ref[...], gate_ref[...], gg_ref[...])
    y = _dot(yc_ref[...].astype(BF16), wout_ref[0:CONV_W, :]) + _dot(yg.astype(BF16), wout_ref[CONV_W:, :])
    x1 = x_ref[...] + y
    x1_ref[...] = x1
    q_ref[...] = _dot(_rms(x1, gm_ref[...]).astype(BF16), wq_ref[...])


def _mix_sample_out(x, yc, o, gate, gg, wout, gm, wq):
    args = (x, yc, o, gate, gg, wout, gm, wq)
    return pl.pallas_call(
        _mix_sample_out_kernel,
        grid=(1,),
        in_specs=[_full(a.shape) for a in args],
        out_specs=[_full(x.shape)] * 2,
        out_shape=[jax.ShapeDtypeStruct(x.shape, F32)] * 2,
        compiler_params=_params("arbitrary"),
        name="mix_sample_out",
    )(*args)


ATT_ROWS = 2 * MEM_HEADS


def _class_allreduce(x, op):
    n = x.shape[-1]
    shift = ATT_ROWS
    while shift < n:
        x = op(x, pltpu.roll(x, shift, axis=1))
        shift *= 2
    return x


def _att_sample_kernel(q_ref, k_ref, v_ref, o_ref):
    nb = q_ref.shape[0]
    ncol = k_ref.shape[2]
    diag = (lax.broadcasted_iota(jnp.int32, (ATT_ROWS, ncol), 0)
            == (lax.broadcasted_iota(jnp.int32, (ATT_ROWS, ncol), 1) & (ATT_ROWS - 1)))
    rows = lax.broadcasted_iota(jnp.int32, (nb, 1), 0)
    t = jnp.zeros((nb, ncol), F32)
    for n in range(nb):
        sc = _dot_nt(q_ref[n].astype(BF16), k_ref[0, n].astype(BF16))
        t = t + jnp.where(rows == n, jnp.sum(jnp.where(diag, sc, 0.0), axis=0, keepdims=True), 0.0)
    valid = (lax.broadcasted_iota(jnp.int32, (nb, ncol), 1) & (ATT_ROWS - 1)) < MEM_HEADS
    s = jnp.where(valid, (t + pltpu.roll(t, ncol - MEM_HEADS, axis=1)) * (MEM_DH ** -0.5), 0.0)
    e = jnp.where(valid, jnp.exp(s - _class_allreduce(s, jnp.maximum)), 0.0)
    den = jnp.where(valid, _class_allreduce(e, jnp.add), 1.0)
    p = e / den
    p = p + pltpu.roll(p, MEM_HEADS, axis=1)
    for n in range(nb):
        p_n = jnp.where(diag, jnp.broadcast_to(p[n:n + 1, :], (ATT_ROWS, ncol)), 0.0)
        o_ref[n] = _dot(p_n.astype(BF16), v_ref[0, n].astype(BF16))


def _att_sample(q, ck, cv, layer, nb):
    n = q.shape[0]
    ncol = ck.shape[2]
    return pl.pallas_call(
        _att_sample_kernel,
        grid=(n // nb,),
        in_specs=[
            pl.BlockSpec((nb, ATT_ROWS, LANES), lambda i: (i, 0, 0)),
            pl.BlockSpec((1, nb, ncol, LANES), lambda i: (layer, i, 0, 0)),
            pl.BlockSpec((1, nb, ncol, LANES), lambda i: (layer, i, 0, 0)),
        ],
        out_specs=pl.BlockSpec((nb, ATT_ROWS, LANES), lambda i: (i, 0, 0)),
        out_shape=jax.ShapeDtypeStruct((n, ATT_ROWS, LANES), F32),
        compiler_params=_params("arbitrary"),
        name="att_sample",
    )(q, ck, cv)


def _oproj_kernel(x_ref, o_ref, wo_ref, out_ref):
    out_ref[...] = x_ref[...] + _dot(o_ref[...].astype(BF16), wo_ref[...])


def _oproj(x, o, wo):
    return pl.pallas_call(
        _oproj_kernel,
        grid=(1,),
        in_specs=[_full(x.shape), _full(o.shape), _full(wo.shape)],
        out_specs=_full(x.shape),
        out_shape=jax.ShapeDtypeStruct(x.shape, F32),
        compiler_params=_params("arbitrary"),
        name="oproj",
    )(x, o, wo)


def kernel(x_prompt, x_sample, state_conv, state_gla, cache_mem_k, cache_mem_v, mem_prompt, norm_mix, w_in, conv_w, gla_gate_up, gla_gate_b, gla_out_norm, w_out, norm_mem, w_q, w_k, w_v, w_o, norm_ffn, router_group, router_group_b, router_expert, router_expert_b, w_gate, w_up, w_down, norm_final):
    depth = w_in.shape[0]
    bsz, t, d = x_prompt.shape
    ns = x_sample.shape[0]
    nm = mem_prompt.shape[1]
    n_tok = bsz * t

    tt = min(256, t)
    tq = min(512, t)
    tm_moe = min(1024, n_tok)
    tm_kv = min(512, bsz * nm)
    tm_sorted = 256
    tr_route = min(512, n_tok)
    rows_perm = min(512, n_tok)
    nb_gla = min(8, ns)
    nb_att = min(8, ns)

    row = lambda a: a.reshape(1, -1)
    mem2 = mem_prompt.reshape(bsz * nm, d)

    def tile_rows(c):
        c = c.reshape(depth, ns, nm, MEM_HEADS, 2, LANES).transpose(0, 1, 2, 4, 3, 5)
        return c.reshape(depth, ns, nm * ATT_ROWS, LANES)

    ck, cv = tile_rows(cache_mem_k), tile_rows(cache_mem_v)

    xp = x_prompt
    xs = x_sample.reshape(ns, d)
    conv_p, gla_p, mk_p, mv_p, conv_s, gla_s = [], [], [], [], [], []
    for l in range(depth):
        win = w_in[l, :, :C_LR].astype(BF16)
        wlr = jnp.pad(w_in[l, :, C_LR:], ((0, 0), (0, LANES - GLA_RANK))).astype(BF16)
        gup = jnp.pad(gla_gate_up[l], ((0, LANES - GLA_RANK), (0, 0))).astype(BF16)
        gb = row(gla_gate_b[l])
        gg = row(gla_out_norm[l])
        wout = w_out[l].astype(BF16)
        wq, wk, wv, wo = (w[l].astype(BF16) for w in (w_q, w_k, w_v, w_o))
        rw = jnp.concatenate([router_group[l], router_expert[l].transpose(1, 0, 2).reshape(d, N_EXPERTS)], axis=1)
        rw = jnp.pad(rw, ((0, 0), (0, LANES - rw.shape[1])))
        rb = jnp.concatenate([router_group_b[l], router_expert_b[l].reshape(-1)])
        rb = row(jnp.pad(rb, (0, LANES - rb.shape[0])))
        wg, wu, wd = w_gate[l].astype(BF16), w_up[l].astype(BF16), w_down[l].astype(BF16)
        moe_w = (row(norm_ffn[l]), rw, rb, wg, wu, wd)

        mk, mv = _mem_kv(mem2, wk, wv, tm_kv)
        mk_p.append(mk.reshape(bsz, nm, MEM_HEADS, MEM_DH))
        mv_p.append(mv.reshape(bsz, nm, MEM_HEADS, MEM_DH))
        xp, nbuf, ns_p = _mix_prompt(xp, row(norm_mix[l]), win, wlr, gup, gb, conv_w[l], gg, wout, tt)
        conv_p.append(nbuf)
        gla_p.append(ns_p)
        xp = _att_prompt(xp, row(norm_mem[l]), wq, wo, mk.reshape(bsz, nm, d), mv.reshape(bsz, nm, d), tq)
        wgu = jnp.concatenate([wg, wu], axis=-1)
        xp = _moe_sparse(xp.reshape(n_tok, d), row(norm_ffn[l]), rw, rb, wgu, wd,
                         tm_sorted, tr_route, rows_perm).reshape(bsz, t, d)

        yc, u, q, k, v, gate, la = _mix_sample_in(
            xs, row(norm_mix[l]), win, wlr, gup, gb, conv_w[l], state_conv[l, :, 0], state_conv[l, :, 1])
        conv_s.append(jnp.stack([state_conv[l, :, 1], u], axis=1))
        s_new, o = _gla_step(q, k, v, la, state_gla, l, nb_gla)
        gla_s.append(s_new)
        wq_s = wq.reshape(d, MEM_HEADS, 2, LANES).transpose(0, 2, 1, 3).reshape(d, d)
        wo_s = wo.reshape(MEM_HEADS, 2, LANES, d).transpose(1, 0, 2, 3).reshape(d, d)
        xs, qa = _mix_sample_out(xs, yc, o, gate, gg, wout, row(norm_mem[l]), wq_s)
        oa = _att_sample(qa.reshape(ns, ATT_ROWS, LANES), ck, cv, l, nb_att)
        xs = _oproj(xs, oa.reshape(ns, d), wo_s)
        xs = _moe(xs, *moe_w, min(tm_moe, ns))

    y_prompt = _final_norm(xp.reshape(n_tok, d), row(norm_final), tm_moe).reshape(bsz, t, d)
    y_sample = _final_norm(xs, row(norm_final), ns).reshape(ns, 1, d)
    return (y_prompt, y_sample, jnp.stack(conv_p), jnp.stack(gla_p), jnp.stack(mk_p), jnp.stack(mv_p),
            jnp.stack(conv_s), jnp.stack(gla_s))
```

```python
import functools

import jax
import jax.numpy as jnp
from jax import lax
from jax.experimental import pallas as pl
from jax.experimental.pallas import tpu as pltpu

F32 = jnp.float32
BF16 = jnp.bfloat16
HIGHEST = lax.Precision.HIGHEST

EPS = 1e-6
CONV_W = 512
GLA_HEADS = 4
GLA_DK = 64
GLA_DV = 128
GLA_K = GLA_HEADS * GLA_DK
GLA_V = GLA_HEADS * GLA_DV
GLA_RANK = 16
GLA_TAU = 16.0
GLA_CHUNK = 64
MEM_HEADS = 4
MEM_DH = 256
N_GROUPS = 4
EXP_PER_GROUP = 4
N_EXPERTS = 16
LANES = 128
C_CB, C_CC, C_CH, C_Q, C_K, C_V, C_G, C_LR = 0, 512, 1024, 1536, 1792, 2048, 2560, 3072
VMEM_LIMIT = 52 * 1024 * 1024


def _params(*sem):
    return pltpu.CompilerParams(dimension_semantics=sem, vmem_limit_bytes=VMEM_LIMIT)


def _rms(x, g):
    return x * lax.rsqrt(jnp.mean(x * x, axis=-1, keepdims=True) + EPS) * g


def _dot(a, b, precision=None):
    return jnp.dot(a, b, precision=precision, preferred_element_type=F32)


def _dot_nt(a, b):
    return lax.dot_general(a, b, (((1,), (1,)), ((), ())), preferred_element_type=F32)


def _dot_tn(a, b, precision=None):
    return lax.dot_general(a, b, (((0,), (0,)), ((), ())), precision=precision,
                           preferred_element_type=F32)


def _silu(x):
    return x / (1.0 + jnp.exp(-x))


def _log_sigmoid(x):
    return jnp.minimum(x, 0.0) - jnp.log1p(jnp.exp(-jnp.abs(x)))


def _head_norm_gate(o, g, gg):
    parts = []
    for h in range(GLA_HEADS):
        sl = slice(h * GLA_DV, (h + 1) * GLA_DV)
        parts.append(_rms(o[:, sl], gg[:, sl]))
    return jnp.concatenate(parts, axis=-1) * _silu(g)


def _full(shape):
    nd = len(shape)
    return pl.BlockSpec(shape, lambda *_: (0,) * nd)


def _mix_prompt_kernel(x_ref, g_ref, win_ref, wlr_ref, gup_ref, gb_ref, cw_ref, gg_ref, wout_ref,
                       x1_ref, conv_ref, s_ref, ubuf, s_scr, o_scr, *, tok_in):
    t = pl.program_id(1)
    tt = x1_ref.shape[1]

    @pl.when(t == 0)
    def _():
        ubuf[0:8, :] = jnp.zeros((8, CONV_W), F32)
        s_scr[...] = jnp.zeros(s_scr.shape, F32)

    x = _load_tok(x_ref, tt) if tok_in else x_ref[0]
    hb = _rms(x, g_ref[...]).astype(BF16)

    def proj(a, b):
        return _dot(hb, win_ref[:, a:b])

    u = proj(C_CC, C_CH) * proj(C_CH, C_Q)
    ubuf[8:8 + tt, :] = u
    cw = cw_ref[...]
    yc = proj(C_CB, C_CC) * (ubuf[6:6 + tt, :] * cw[0:1] + ubuf[7:7 + tt, :] * cw[1:2] + u * cw[2:3])
    ubuf[6:8, :] = u[tt - 2:tt, :]

    qs = proj(C_Q, C_K) * (GLA_DK ** -0.5)
    k = proj(C_K, C_V)
    vb = proj(C_V, C_G).astype(BF16)
    lr = _dot(hb, wlr_ref[...])
    gate = _dot(lr.astype(BF16), gup_ref[...]) + gb_ref[...]
    la = _log_sigmoid(gate) * (1.0 / GLA_TAU)

    c = GLA_CHUNK
    row = lax.broadcasted_iota(jnp.int32, (c, c), 0)
    col = lax.broadcasted_iota(jnp.int32, (c, c), 1)
    causal = row >= col
    ltri = causal.astype(F32)
    ones = jnp.ones((c, GLA_DV), F32)
    for j in range(tt // c):
        r = slice(j * c, (j + 1) * c)
        la_c, q_c, k_c, v_c = la[r], qs[r], k[r], vb[r]
        b = _dot(ltri, la_c, HIGHEST)
        b_mid = b[c // 2:c // 2 + 1, :]
        b_last = b[c - 1:c, :]
        q_i = (q_c * jnp.exp(b - b_mid)).astype(BF16)
        k_i = (k_c * jnp.exp(b_mid - b)).astype(BF16)
        k_dec = (k_c * jnp.exp(b_last - b)).astype(BF16)
        q_b = (q_c * jnp.exp(b)).astype(BF16)
        for h in range(GLA_HEADS):
            ks = slice(h * GLA_DK, (h + 1) * GLA_DK)
            vh = v_c[:, h * GLA_DV:(h + 1) * GLA_DV]
            a = jnp.where(causal, _dot_nt(q_i[:, ks], k_i[:, ks]), 0.0)
            s_prev = s_scr[h]
            o_scr[r, h * GLA_DV:(h + 1) * GLA_DV] = (
                _dot(a.astype(BF16), vh) + _dot(q_b[:, ks], s_prev.astype(BF16)))
            dec = jnp.exp(_dot_tn(la_c[:, ks], ones, HIGHEST))
            s_scr[h] = dec * s_prev + _dot_tn(k_dec[:, ks], vh)

    yg = _head_norm_gate(o_scr[...], proj(C_G, C_LR), gg_ref[...])
    y = _dot(yc.astype(BF16), wout_ref[0:CONV_W, :]) + _dot(yg.astype(BF16), wout_ref[CONV_W:, :])
    x1_ref[0] = x + y

    @pl.when(t == pl.num_programs(1) - 1)
    def _():
        conv_ref[0] = u[tt - 2:tt, :]
        s_ref[0] = s_scr[...]


def _mix_prompt(x, shape, g, win, wlr, gup, gb, cw, gg, wout, tt):
    bsz, t, d = shape
    tok_in = x.ndim == 2
    nt = t // tt
    x_spec = (pl.BlockSpec((tt * TOK_ROWS, LANES), lambda b, i: (b * nt + i, 0)) if tok_in
              else pl.BlockSpec((1, tt, d), lambda b, i: (b, i, 0)))
    return pl.pallas_call(
        functools.partial(_mix_prompt_kernel, tok_in=tok_in),
        grid=(bsz, nt),
        in_specs=[
            x_spec,
            _full(g.shape), _full(win.shape), _full(wlr.shape), _full(gup.shape), _full(gb.shape),
            _full(cw.shape), _full(gg.shape), _full(wout.shape),
        ],
        out_specs=[
            pl.BlockSpec((1, tt, d), lambda b, i: (b, i, 0)),
            pl.BlockSpec((1, 2, CONV_W), lambda b, i: (b, 0, 0)),
            pl.BlockSpec((1, GLA_HEADS, GLA_DK, GLA_DV), lambda b, i: (b, 0, 0, 0)),
        ],
        out_shape=[
            jax.ShapeDtypeStruct((bsz, t, d), F32),
            jax.ShapeDtypeStruct((bsz, 2, CONV_W), F32),
            jax.ShapeDtypeStruct((bsz, GLA_HEADS, GLA_DK, GLA_DV), F32),
        ],
        scratch_shapes=[
            pltpu.VMEM((8 + tt, CONV_W), F32),
            pltpu.VMEM((GLA_HEADS, GLA_DK, GLA_DV), F32),
            pltpu.VMEM((tt, GLA_V), F32),
        ],
        compiler_params=_params("arbitrary", "arbitrary"),
        name="mix_prompt",
    )(x, g, win, wlr, gup, gb, cw, gg, wout)


def _kv_kernel(m_ref, wk_ref, wv_ref, k_ref, v_ref):
    mb = m_ref[...].astype(BF16)
    k_ref[...] = _dot(mb, wk_ref[...])
    v_ref[...] = _dot(mb, wv_ref[...])


def _mem_kv(mem, wk, wv, tm):
    n, d = mem.shape
    return pl.pallas_call(
        _kv_kernel,
        grid=(n // tm,),
        in_specs=[pl.BlockSpec((tm, d), lambda i: (i, 0)), _full(wk.shape), _full(wv.shape)],
        out_specs=[pl.BlockSpec((tm, d), lambda i: (i, 0))] * 2,
        out_shape=[jax.ShapeDtypeStruct((n, d), F32)] * 2,
        compiler_params=_params("arbitrary"),
        name="mem_kv",
    )(mem, wk, wv)


def _att_prompt_kernel(x_ref, g_ref, wq_ref, wo_ref, mk_ref, mv_ref, out_ref):
    x = x_ref[0]
    xb = _rms(x, g_ref[...]).astype(BF16)
    q = _dot(xb, wq_ref[...])
    outs = []
    for h in range(MEM_HEADS):
        sl = slice(h * MEM_DH, (h + 1) * MEM_DH)
        s = _dot_nt(q[:, sl].astype(BF16), mk_ref[0, :, sl].astype(BF16)) * (MEM_DH ** -0.5)
        e = jnp.exp(s - jnp.max(s, axis=-1, keepdims=True))
        p = e / jnp.sum(e, axis=-1, keepdims=True)
        outs.append(_dot(p.astype(BF16), mv_ref[0, :, sl].astype(BF16)))
    o = jnp.concatenate(outs, axis=-1)
    out_ref[0] = x + _dot(o.astype(BF16), wo_ref[...])


def _att_prompt(x, g, wq, wo, mk, mv, tq):
    bsz, t, d = x.shape
    nm = mk.shape[1]
    return pl.pallas_call(
        _att_prompt_kernel,
        grid=(bsz, t // tq),
        in_specs=[
            pl.BlockSpec((1, tq, d), lambda b, i: (b, i, 0)),
            _full(g.shape), _full(wq.shape), _full(wo.shape),
            pl.BlockSpec((1, nm, d), lambda b, i: (b, 0, 0)),
            pl.BlockSpec((1, nm, d), lambda b, i: (b, 0, 0)),
        ],
        out_specs=pl.BlockSpec((1, tq, d), lambda b, i: (b, i, 0)),
        out_shape=jax.ShapeDtypeStruct((bsz, t, d), F32),
        compiler_params=_params("arbitrary", "arbitrary"),
        name="att_prompt",
    )(x, g, wq, wo, mk, mv)


def _route_top2(logits):
    lane = lax.broadcasted_iota(jnp.int32, logits.shape, 1)
    lanef = lane.astype(F32)
    ninf = -jnp.inf
    big = 1e9
    gl = jnp.where(lane < N_GROUPS, logits, ninf)
    gmax = jnp.max(gl, axis=-1, keepdims=True)
    g_idx = jnp.min(jnp.where(gl == gmax, lanef, big), axis=-1, keepdims=True)
    g_w = 1.0 / jnp.sum(jnp.exp(gl - gmax), axis=-1, keepdims=True)
    grp = ((lane - N_GROUPS) >> 2).astype(F32)
    emask = (lane >= N_GROUPS) & (lane < N_GROUPS + N_EXPERTS) & (grp == g_idx)
    el = jnp.where(emask, logits, ninf)
    m1 = jnp.max(el, axis=-1, keepdims=True)
    i1 = jnp.min(jnp.where(el == m1, lanef, big), axis=-1, keepdims=True)
    el2 = jnp.where(lanef == i1, ninf, el)
    m2 = jnp.max(el2, axis=-1, keepdims=True)
    i2 = jnp.min(jnp.where(el2 == m2, lanef, big), axis=-1, keepdims=True)
    tail = jnp.exp(m2 - m1)
    w1 = g_w / (1.0 + tail)
    w2 = g_w * tail / (1.0 + tail)
    return g_idx, i1, i2, w1, w2


def _route(logits):
    _, i1, i2, w1, w2 = _route_top2(logits)
    lanef = lax.broadcasted_iota(jnp.int32, logits.shape, 1).astype(F32)
    return jnp.where(lanef == i1, w1, 0.0) + jnp.where(lanef == i2, w2, 0.0)


N_PAIRS = 6
N_CLASSES = N_GROUPS * N_PAIRS
TOK_ROWS = 8


def _load_tok(ref, n):
    return jnp.concatenate([ref[pl.ds(j, n, stride=TOK_ROWS), :] for j in range(TOK_ROWS)], axis=-1)


def _store_tok(ref, val):
    n = val.shape[0]
    for j in range(TOK_ROWS):
        ref[pl.ds(j, n, stride=TOK_ROWS), :] = val[:, j * LANES:(j + 1) * LANES]


def _moe_route_kernel(x_ref, g_ref, rw_ref, rb_ref, xt_ref, meta_ref, cnt_ref, carry):
    i = pl.program_id(0)
    tr = x_ref.shape[0]

    @pl.when(i == 0)
    def _():
        carry[...] = jnp.zeros(carry.shape, F32)

    x = x_ref[...]
    _store_tok(xt_ref, x)
    xn = _rms(x, g_ref[...])
    g_idx, i1, i2, _, _ = _route_top2(_dot(xn, rw_ref[...], HIGHEST) + rb_ref[...])
    lo = jnp.minimum(i1, i2) - N_GROUPS - EXP_PER_GROUP * g_idx
    hi = jnp.maximum(i1, i2) - N_GROUPS - EXP_PER_GROUP * g_idx
    cls = g_idx * N_PAIRS + lo * (7.0 - lo) * 0.5 + hi - lo - 1.0

    lane = lax.broadcasted_iota(jnp.int32, (tr, LANES), 1)
    onehot = (lane.astype(F32) == cls).astype(BF16)
    ltri = (lax.broadcasted_iota(jnp.int32, (tr, tr), 0)
            >= lax.broadcasted_iota(jnp.int32, (tr, tr), 1)).astype(BF16)
    prefix = _dot(ltri, onehot) + carry[...]
    rank = jnp.sum(jnp.where(onehot > 0, prefix, 0.0), axis=-1, keepdims=True) - 1.0
    carry[...] = prefix[tr - 1:tr, :]
    meta_ref[...] = (jnp.where(lane == 0, cls, 0.0) + jnp.where(lane == 1, rank, 0.0)).astype(jnp.int32)
    cnt_ref[...] = prefix[tr - 1:tr, :].astype(jnp.int32)


def _moe_route(x, g, rw, rb, tr):
    n, d = x.shape
    assert d == TOK_ROWS * LANES
    return pl.pallas_call(
        _moe_route_kernel,
        grid=(n // tr,),
        in_specs=[pl.BlockSpec((tr, d), lambda i: (i, 0)), _full(g.shape), _full(rw.shape), _full(rb.shape)],
        out_specs=[
            pl.BlockSpec((tr * TOK_ROWS, LANES), lambda i: (i, 0)),
            pl.BlockSpec((tr, LANES), lambda i: (i, 0)),
            _full((1, LANES)),
        ],
        out_shape=[
            jax.ShapeDtypeStruct((n * TOK_ROWS, LANES), F32),
            jax.ShapeDtypeStruct((n, LANES), jnp.int32),
            jax.ShapeDtypeStruct((1, LANES), jnp.int32),
        ],
        scratch_shapes=[pltpu.VMEM((1, LANES), F32)],
        compiler_params=_params("arbitrary"),
        name="moe_route",
    )(x, g, rw, rb)


def _tok_permute_kernel(src_idx, dst_idx, src_ref, dst_in_ref, dst_ref, sem, *, toks):
    del dst_in_ref
    base = pl.program_id(0) * toks

    def tok_copy(s, t):
        return pltpu.make_async_copy(src_ref.at[pl.ds(pl.multiple_of(s * TOK_ROWS, TOK_ROWS), TOK_ROWS)],
                                     dst_ref.at[pl.ds(pl.multiple_of(t * TOK_ROWS, TOK_ROWS), TOK_ROWS)], sem)

    def start(r, c):
        tok_copy(src_idx[base + r], dst_idx[base + r]).start()
        return c

    lax.fori_loop(0, toks, start, 0, unroll=8)

    def wait(r, c):
        tok_copy(0, 0).wait()
        return c

    lax.fori_loop(0, toks, wait, 0, unroll=8)


def _tok_permute(src, dst_init, src_idx, dst_idx, toks):
    n = src_idx.shape[0]
    return pl.pallas_call(
        functools.partial(_tok_permute_kernel, toks=toks),
        grid_spec=pltpu.PrefetchScalarGridSpec(
            num_scalar_prefetch=2,
            grid=(n // toks,),
            in_specs=[pl.BlockSpec(memory_space=pl.ANY), pl.BlockSpec(memory_space=pl.ANY)],
            out_specs=pl.BlockSpec(memory_space=pl.ANY),
            scratch_shapes=[pltpu.SemaphoreType.DMA(())],
        ),
        out_shape=jax.ShapeDtypeStruct(dst_init.shape, dst_init.dtype),
        input_output_aliases={3: 0},
        compiler_params=_params("arbitrary"),
        name="tok_permute",
    )(src_idx, dst_idx, src, dst_init)


def _moe_sorted_kernel(e_lo, e_hi, valid, xs_ref, g_ref, rw_ref, rb_ref, wgu_lo, wgu_hi, wd_lo, wd_hi, out_ref):
    t = pl.program_id(0)
    tm = xs_ref.shape[0] // TOK_ROWS
    ff = wd_lo.shape[1]

    @pl.when(valid[t] > 0)
    def _():
        x = _load_tok(xs_ref, tm)
        xn = _rms(x, g_ref[...])
        logits = _dot(xn, rw_ref[...], HIGHEST) + rb_ref[...]
        lane = lax.broadcasted_iota(jnp.int32, logits.shape, 1)
        gl = jnp.where(lane < N_GROUPS, logits, -jnp.inf)
        g_w = 1.0 / jnp.sum(jnp.exp(gl - jnp.max(gl, axis=-1, keepdims=True)), axis=-1, keepdims=True)
        l_lo = jnp.sum(jnp.where(lane == e_lo[t] + N_GROUPS, logits, 0.0), axis=-1, keepdims=True)
        l_hi = jnp.sum(jnp.where(lane == e_hi[t] + N_GROUPS, logits, 0.0), axis=-1, keepdims=True)
        tail = jnp.exp(-jnp.abs(l_lo - l_hi))
        w_top = g_w / (1.0 + tail)
        w_oth = g_w * tail / (1.0 + tail)
        lo_top = l_lo >= l_hi
        xb = xn.astype(BF16)
        y = x
        for wgu, wd, w in ((wgu_lo, wd_lo, jnp.where(lo_top, w_top, w_oth)),
                           (wgu_hi, wd_hi, jnp.where(lo_top, w_oth, w_top))):
            gu = _dot(xb, wgu[0])
            hid = _silu(gu[:, 0:ff]) * gu[:, ff:] * w
            y = y + _dot(hid.astype(BF16), wd[0])
        _store_tok(out_ref, y)

    @pl.when(valid[t] == 0)
    def _():
        out_ref[...] = jnp.zeros(out_ref.shape, F32)


def _moe_sorted(xs, g, rw, rb, wgu, wd, e_lo, e_hi, valid, tm):
    npad = xs.shape[0] // TOK_ROWS
    _, d, ff2 = wgu.shape
    const = lambda t, lo, hi, v: (0, 0)
    return pl.pallas_call(
        _moe_sorted_kernel,
        grid_spec=pltpu.PrefetchScalarGridSpec(
            num_scalar_prefetch=3,
            grid=(npad // tm,),
            in_specs=[
                pl.BlockSpec((tm * TOK_ROWS, LANES), lambda t, lo, hi, v: (t, 0)),
                pl.BlockSpec(g.shape, const), pl.BlockSpec(rw.shape, const), pl.BlockSpec(rb.shape, const),
                pl.BlockSpec((1, d, ff2), lambda t, lo, hi, v: (lo[t], 0, 0)),
                pl.BlockSpec((1, d, ff2), lambda t, lo, hi, v: (hi[t], 0, 0)),
                pl.BlockSpec((1, ff2 // 2, d), lambda t, lo, hi, v: (lo[t], 0, 0)),
                pl.BlockSpec((1, ff2 // 2, d), lambda t, lo, hi, v: (hi[t], 0, 0)),
            ],
            out_specs=pl.BlockSpec((tm * TOK_ROWS, LANES), lambda t, lo, hi, v: (t, 0)),
        ),
        out_shape=jax.ShapeDtypeStruct(xs.shape, F32),
        compiler_params=_params("arbitrary"),
        name="moe_sorted",
    )(e_lo, e_hi, valid, xs, g, rw, rb, wgu, wgu, wd, wd)


def _moe_sparse(x, g, rw, rb, wgu, wd, tm, tr, toks):
    n, d = x.shape
    xt, meta, counts = _moe_route(x, g, rw, rb, tr)
    cls, rank = meta[:, 0], meta[:, 1]
    cnt = counts[0, :N_CLASSES]
    padded = ((cnt + tm - 1) // tm) * tm
    ends = jnp.cumsum(padded)
    pos = (ends - padded)[cls] + rank
    n_tiles = n // tm + N_CLASSES
    tile_start = jnp.arange(n_tiles, dtype=jnp.int32) * tm
    tile_cls = jnp.minimum(jnp.searchsorted(ends, tile_start, side="right"), N_CLASSES - 1).astype(jnp.int32)
    valid = (tile_start < ends[-1]).astype(jnp.int32)
    tile_cls = jnp.where(valid > 0, tile_cls, tile_cls[jnp.maximum(ends[-1] // tm - 1, 0)])
    pair_lo = jnp.array([0, 0, 0, 1, 1, 2], jnp.int32)
    pair_hi = jnp.array([1, 2, 3, 2, 3, 3], jnp.int32)
    e_lo = (tile_cls // N_PAIRS) * EXP_PER_GROUP + pair_lo[tile_cls % N_PAIRS]
    e_hi = (tile_cls // N_PAIRS) * EXP_PER_GROUP + pair_hi[tile_cls % N_PAIRS]

    ident = jnp.arange(n, dtype=jnp.int32)
    xs = _tok_permute(xt, jnp.zeros((n_tiles * tm * TOK_ROWS, LANES), F32), ident, pos, toks)
    ys = _moe_sorted(xs, g, rw, rb, wgu, wd, e_lo, e_hi, valid, tm)
    return _tok_permute(ys, xt, pos, ident, toks)


def _moe_kernel(x_ref, g_ref, rw_ref, rb_ref, wg_ref, wu_ref, wd_ref, out_ref, xn_scr, comb_scr, acc_scr):
    e = pl.program_id(1)

    @pl.when(e == 0)
    def _():
        x = x_ref[...]
        xn = _rms(x, g_ref[...])
        xn_scr[...] = xn.astype(BF16)
        comb_scr[...] = _route(_dot(xn, rw_ref[...], HIGHEST) + rb_ref[...])
        acc_scr[...] = x

    xb = xn_scr[...]
    lane = lax.broadcasted_iota(jnp.int32, comb_scr.shape, 1)
    cmb = jnp.sum(jnp.where(lane == e + N_GROUPS, comb_scr[...], 0.0), axis=-1, keepdims=True)
    hid = _silu(_dot(xb, wg_ref[0])) * _dot(xb, wu_ref[0]) * cmb
    acc_scr[...] += _dot(hid.astype(BF16), wd_ref[0])

    @pl.when(e == pl.num_programs(1) - 1)
    def _():
        out_ref[...] = acc_scr[...]


def _moe(x, g, rw, rb, wg, wu, wd, tm):
    n, d = x.shape
    ne, _, ff = wg.shape
    return pl.pallas_call(
        _moe_kernel,
        grid=(n // tm, ne),
        in_specs=[
            pl.BlockSpec((tm, d), lambda i, e: (i, 0)),
            _full(g.shape), _full(rw.shape), _full(rb.shape),
            pl.BlockSpec((1, d, ff), lambda i, e: (e, 0, 0)),
            pl.BlockSpec((1, d, ff), lambda i, e: (e, 0, 0)),
            pl.BlockSpec((1, ff, d), lambda i, e: (e, 0, 0)),
        ],
        out_specs=pl.BlockSpec((tm, d), lambda i, e: (i, 0)),
        out_shape=jax.ShapeDtypeStruct((n, d), F32),
        scratch_shapes=[pltpu.VMEM((tm, d), BF16), pltpu.VMEM((tm, LANES), F32), pltpu.VMEM((tm, d), F32)],
        compiler_params=_params("arbitrary", "arbitrary"),
        name="moe",
    )(x, g, rw, rb, wg, wu, wd)


def _norm_kernel(x_ref, g_ref, o_ref, *, tok_in):
    x = _load_tok(x_ref, o_ref.shape[0]) if tok_in else x_ref[...]
    o_ref[...] = _rms(x, g_ref[...])


def _final_norm(x, g, tm, tok_in=False):
    d = g.shape[1]
    n = x.shape[0] // TOK_ROWS if tok_in else x.shape[0]
    x_spec = (pl.BlockSpec((tm * TOK_ROWS, LANES), lambda i: (i, 0)) if tok_in
              else pl.BlockSpec((tm, d), lambda i: (i, 0)))
    return pl.pallas_call(
        functools.partial(_norm_kernel, tok_in=tok_in),
        grid=(n // tm,),
        in_specs=[x_spec, _full(g.shape)],
        out_specs=pl.BlockSpec((tm, d), lambda i: (i, 0)),
        out_shape=jax.ShapeDtypeStruct((n, d), F32),
        compiler_params=_params("arbitrary"),
        name="final_norm",
    )(x, g)


def _mix_sample_in_kernel(x_ref, g_ref, win_ref, wlr_ref, gup_ref, gb_ref, cw_ref, b0_ref, b1_ref,
                          yc_ref, u_ref, q_ref, k_ref, v_ref, gate_ref, la_ref):
    hb = _rms(x_ref[...], g_ref[...]).astype(BF16)

    def proj(a, b):
        return _dot(hb, win_ref[:, a:b])

    u = proj(C_CC, C_CH) * proj(C_CH, C_Q)
    cw = cw_ref[...]
    yc_ref[...] = proj(C_CB, C_CC) * (b0_ref[...] * cw[0:1] + b1_ref[...] * cw[1:2] + u * cw[2:3])
    u_ref[...] = u
    q_ref[...] = proj(C_Q, C_K) * (GLA_DK ** -0.5)
    k_ref[...] = proj(C_K, C_V)
    v_ref[...] = proj(C_V, C_G)
    gate_ref[...] = proj(C_G, C_LR)
    lr = _dot(hb, wlr_ref[...])
    gate = _dot(lr.astype(BF16), gup_ref[...]) + gb_ref[...]
    la_ref[...] = _log_sigmoid(gate) * (1.0 / GLA_TAU)


def _mix_sample_in(x, g, win, wlr, gup, gb, cw, b0, b1):
    n = x.shape[0]
    args = (x, g, win, wlr, gup, gb, cw, b0, b1)
    widths = (CONV_W, CONV_W, GLA_K, GLA_K, GLA_V, GLA_V, GLA_K)
    return pl.pallas_call(
        _mix_sample_in_kernel,
        grid=(1,),
        in_specs=[_full(a.shape) for a in args],
        out_specs=[_full((n, w)) for w in widths],
        out_shape=[jax.ShapeDtypeStruct((n, w), F32) for w in widths],
        compiler_params=_params("arbitrary"),
        name="mix_sample_in",
    )(*args)


def _gla_step_kernel(q_ref, k_ref, v_ref, la_ref, s0_ref, s_ref, o_ref):
    nb = q_ref.shape[0]
    a = jnp.exp(la_ref[...])
    kb = k_ref[...]
    qb = q_ref[...].astype(BF16)
    vb = v_ref[...].astype(BF16)
    rows = lax.broadcasted_iota(jnp.int32, (nb, 1), 0)
    spread = (lax.broadcasted_iota(jnp.int32, (nb, nb * GLA_DV), 0)
              == lax.broadcasted_iota(jnp.int32, (nb, nb * GLA_DV), 1) // GLA_DV).astype(F32)
    for h in range(GLA_HEADS):
        ks = slice(h * GLA_DK, (h + 1) * GLA_DK)
        vh = vb[:, h * GLA_DV:(h + 1) * GLA_DV]
        a_cols = _dot_tn(a[:, ks], spread, HIGHEST)
        o_h = jnp.zeros((nb, GLA_DV), F32)
        for n in range(nb):
            k_n = jnp.where(rows == n, kb[:, ks], 0.0).astype(BF16)
            s_new = a_cols[:, n * GLA_DV:(n + 1) * GLA_DV] * s0_ref[0, n, h] + _dot_tn(k_n, vh)
            s_ref[n, h] = s_new
            o_h = o_h + jnp.where(rows == n, _dot(qb[:, ks], s_new.astype(BF16)), 0.0)
        o_ref[:, h * GLA_DV:(h + 1) * GLA_DV] = o_h


def _gla_step(q, k, v, la, state, layer, nb):
    n = q.shape[0]
    sshape = (GLA_HEADS, GLA_DK, GLA_DV)
    return pl.pallas_call(
        _gla_step_kernel,
        grid=(n // nb,),
        in_specs=[
            pl.BlockSpec((nb, GLA_K), lambda i: (i, 0)),
            pl.BlockSpec((nb, GLA_K), lambda i: (i, 0)),
            pl.BlockSpec((nb, GLA_V), lambda i: (i, 0)),
            pl.BlockSpec((nb, GLA_K), lambda i: (i, 0)),
            pl.BlockSpec((1, nb) + sshape, lambda i: (layer, i, 0, 0, 0)),
        ],
        out_specs=[
            pl.BlockSpec((nb,) + sshape, lambda i: (i, 0, 0, 0)),
            pl.BlockSpec((nb, GLA_V), lambda i: (i, 0)),
        ],
        out_shape=[jax.ShapeDtypeStruct((n,) + sshape, F32), jax.ShapeDtypeStruct((n, GLA_V), F32)],
        compiler_params=_params("arbitrary"),
        name="gla_step",
    )(q, k, v, la, state)


def _mix_sample_out_kernel(x_ref, yc_ref, o_ref, gate_ref, gg_ref, wout_ref, gm_ref, wq_ref, x1_ref, q_ref):
    yg = _head_norm_gate(o_ref[...], gate_ref[...], gg_ref[...])
    y = _dot(yc_ref[...].astype(BF16), wout_ref[0:CONV_W, :]) + _dot(yg.astype(BF16), wout_ref[CONV_W:, :])
    x1 = x_ref[...] + y
    x1_ref[...] = x1
    q_ref[...] = _dot(_rms(x1, gm_ref[...]).astype(BF16), wq_ref[...])


def _mix_sample_out(x, yc, o, gate, gg, wout, gm, wq):
    args = (x, yc, o, gate, gg, wout, gm, wq)
    return pl.pallas_call(
        _mix_sample_out_kernel,
        grid=(1,),
        in_specs=[_full(a.shape) for a in args],
        out_specs=[_full(x.shape)] * 2,
        out_shape=[jax.ShapeDtypeStruct(x.shape, F32)] * 2,
        compiler_params=_params("arbitrary"),
        name="mix_sample_out",
    )(*args)


ATT_ROWS = 2 * MEM_HEADS


def _class_allreduce(x, op):
    n = x.shape[-1]
    shift = ATT_ROWS
    while shift < n:
        x = op(x, pltpu.roll(x, shift, axis=1))
        shift *= 2
    return x


def _att_sample_kernel(q_ref, k_ref, v_ref, o_ref):
    nb = q_ref.shape[0]
    ncol = k_ref.shape[2]
    diag = (lax.broadcasted_iota(jnp.int32, (ATT_ROWS, ncol), 0)
            == (lax.broadcasted_iota(jnp.int32, (ATT_ROWS, ncol), 1) & (ATT_ROWS - 1)))
    rows = lax.broadcasted_iota(jnp.int32, (nb, 1), 0)
    t = jnp.zeros((nb, ncol), F32)
    for n in range(nb):
        sc = _dot_nt(q_ref[n].astype(BF16), k_ref[0, n].astype(BF16))
        t = t + jnp.where(rows == n, jnp.sum(jnp.where(diag, sc, 0.0), axis=0, keepdims=True), 0.0)
    valid = (lax.broadcasted_iota(jnp.int32, (nb, ncol), 1) & (ATT_ROWS - 1)) < MEM_HEADS
    s = jnp.where(valid, (t + pltpu.roll(t, ncol - MEM_HEADS, axis=1)) * (MEM_DH ** -0.5), 0.0)
    e = jnp.where(valid, jnp.exp(s - _class_allreduce(s, jnp.maximum)), 0.0)
    den = jnp.where(valid, _class_allreduce(e, jnp.add), 1.0)
    p = e / den
    p = p + pltpu.roll(p, MEM_HEADS, axis=1)
    for n in range(nb):
        p_n = jnp.where(diag, jnp.broadcast_to(p[n:n + 1, :], (ATT_ROWS, ncol)), 0.0)
        o_ref[n] = _dot(p_n.astype(BF16), v_ref[0, n].astype(BF16))


def _att_sample(q, ck, cv, layer, nb):
    n = q.shape[0]
    ncol = ck.shape[2]
    return pl.pallas_call(
        _att_sample_kernel,
        grid=(n // nb,),
        in_specs=[
            pl.BlockSpec((nb, ATT_ROWS, LANES), lambda i: (i, 0, 0)),
            pl.BlockSpec((1, nb, ncol, LANES), lambda i: (layer, i, 0, 0)),
            pl.BlockSpec((1, nb, ncol, LANES), lambda i: (layer, i, 0, 0)),
        ],
        out_specs=pl.BlockSpec((nb, ATT_ROWS, LANES), lambda i: (i, 0, 0)),
        out_shape=jax.ShapeDtypeStruct((n, ATT_ROWS, LANES), F32),
        compiler_params=_params("arbitrary"),
        name="att_sample",
    )(q, ck, cv)


def _oproj_kernel(x_ref, o_ref, wo_ref, out_ref):
    out_ref[...] = x_ref[...] + _dot(o_ref[...].astype(BF16), wo_ref[...])


def _oproj(x, o, wo):
    return pl.pallas_call(
        _oproj_kernel,
        grid=(1,),
        in_specs=[_full(x.shape), _full(o.shape), _full(wo.shape)],
        out_specs=_full(x.shape),
        out_shape=jax.ShapeDtypeStruct(x.shape, F32),
        compiler_params=_params("arbitrary"),
        name="oproj",
    )(x, o, wo)


def kernel(x_prompt, x_sample, state_conv, state_gla, cache_mem_k, cache_mem_v, mem_prompt, norm_mix, w_in, conv_w, gla_gate_up, gla_gate_b, gla_out_norm, w_out, norm_mem, w_q, w_k, w_v, w_o, norm_ffn, router_group, router_group_b, router_expert, router_expert_b, w_gate, w_up, w_down, norm_final):
    depth = w_in.shape[0]
    bsz, t, d = x_prompt.shape
    ns = x_sample.shape[0]
    nm = mem_prompt.shape[1]
    n_tok = bsz * t

    tt = min(256, t)
    tq = min(512, t)
    tm_moe = min(1024, n_tok)
    tm_kv = min(512, bsz * nm)
    tm_sorted = 256
    tr_route = min(512, n_tok)
    rows_perm = min(512, n_tok)
    nb_gla = min(8, ns)
    nb_att = min(8, ns)

    row = lambda a: a.reshape(1, -1)
    mem2 = mem_prompt.reshape(bsz * nm, d)

    def tile_rows(c):
        c = c.reshape(depth, ns, nm, MEM_HEADS, 2, LANES).transpose(0, 1, 2, 4, 3, 5)
        return c.reshape(depth, ns, nm * ATT_ROWS, LANES)

    ck, cv = tile_rows(cache_mem_k), tile_rows(cache_mem_v)

    xp = x_prompt
    xs = x_sample.reshape(ns, d)
    conv_p, gla_p, mk_p, mv_p, conv_s, gla_s = [], [], [], [], [], []
    for l in range(depth):
        win = w_in[l, :, :C_LR].astype(BF16)
        wlr = jnp.pad(w_in[l, :, C_LR:], ((0, 0), (0, LANES - GLA_RANK))).astype(BF16)
        gup = jnp.pad(gla_gate_up[l], ((0, LANES - GLA_RANK), (0, 0))).astype(BF16)
        gb = row(gla_gate_b[l])
        gg = row(gla_out_norm[l])
        wout = w_out[l].astype(BF16)
        wq, wk, wv, wo = (w[l].astype(BF16) for w in (w_q, w_k, w_v, w_o))
        rw = jnp.concatenate([router_group[l], router_expert[l].transpose(1, 0, 2).reshape(d, N_EXPERTS)], axis=1)
        rw = jnp.pad(rw, ((0, 0), (0, LANES - rw.shape[1])))
        rb = jnp.concatenate([router_group_b[l], router_expert_b[l].reshape(-1)])
        rb = row(jnp.pad(rb, (0, LANES - rb.shape[0])))
        wg, wu, wd = w_gate[l].astype(BF16), w_up[l].astype(BF16), w_down[l].astype(BF16)
        moe_w = (row(norm_ffn[l]), rw, rb, wg, wu, wd)

        mk, mv = _mem_kv(mem2, wk, wv, tm_kv)
        mk_p.append(mk.reshape(bsz, nm, MEM_HEADS, MEM_DH))
        mv_p.append(mv.reshape(bsz, nm, MEM_HEADS, MEM_DH))
        xp, nbuf, ns_p = _mix_prompt(xp, (bsz, t, d), row(norm_mix[l]), win, wlr, gup, gb, conv_w[l], gg, wout, tt)
        conv_p.append(nbuf)
        gla_p.append(ns_p)
        xp = _att_prompt(xp, row(norm_mem[l]), wq, wo, mk.reshape(bsz, nm, d), mv.reshape(bsz, nm, d), tq)
        wgu = jnp.concatenate([wg, wu], axis=-1)
        xp = _moe_sparse(xp.reshape(n_tok, d), row(norm_ffn[l]), rw, rb, wgu, wd, tm_sorted, tr_route, rows_perm)

        yc, u, q, k, v, gate, la = _mix_sample_in(
            xs, row(norm_mix[l]), win, wlr, gup, gb, conv_w[l], state_conv[l, :, 0], state_conv[l, :, 1])
        conv_s.append(jnp.stack([state_conv[l, :, 1], u], axis=1))
        s_new, o = _gla_step(q, k, v, la, state_gla, l, nb_gla)
        gla_s.append(s_new)
        wq_s = wq.reshape(d, MEM_HEADS, 2, LANES).transpose(0, 2, 1, 3).reshape(d, d)
        wo_s = wo.reshape(MEM_HEADS, 2, LANES, d).transpose(1, 0, 2, 3).reshape(d, d)
        xs, qa = _mix_sample_out(xs, yc, o, gate, gg, wout, row(norm_mem[l]), wq_s)
        oa = _att_sample(qa.reshape(ns, ATT_ROWS, LANES), ck, cv, l, nb_att)
        xs = _oproj(xs, oa.reshape(ns, d), wo_s)
        xs = _moe(xs, *moe_w, min(tm_moe, ns))

    y_prompt = _final_norm(xp, row(norm_final), tm_moe, tok_in=True).reshape(bsz, t, d)
    y_sample = _final_norm(xs, row(norm_final), ns).reshape(ns, 1, d)
    return (y_prompt, y_sample, jnp.stack(conv_p), jnp.stack(gla_p), jnp.stack(mk_p), jnp.stack(mv_p),
            jnp.stack(conv_s), jnp.stack(gla_s))
```

```python
import functools

import jax
import jax.numpy as jnp
from jax import lax
from jax.experimental import pallas as pl
from jax.experimental.pallas import tpu as pltpu

F32 = jnp.float32
BF16 = jnp.bfloat16
HIGHEST = lax.Precision.HIGHEST

EPS = 1e-6
CONV_W = 512
GLA_HEADS = 4
GLA_DK = 64
GLA_DV = 128
GLA_K = GLA_HEADS * GLA_DK
GLA_V = GLA_HEADS * GLA_DV
GLA_RANK = 16
GLA_TAU = 16.0
GLA_CHUNK = 64
MEM_HEADS = 4
MEM_DH = 256
N_GROUPS = 4
EXP_PER_GROUP = 4
N_EXPERTS = 16
LANES = 128
C_CB, C_CC, C_CH, C_Q, C_K, C_V, C_G, C_LR = 0, 512, 1024, 1536, 1792, 2048, 2560, 3072
VMEM_LIMIT = 52 * 1024 * 1024


def _params(*sem):
    return pltpu.CompilerParams(dimension_semantics=sem, vmem_limit_bytes=VMEM_LIMIT)


def _rms(x, g):
    return x * lax.rsqrt(jnp.mean(x * x, axis=-1, keepdims=True) + EPS) * g


def _dot(a, b, precision=None):
    return jnp.dot(a, b, precision=precision, preferred_element_type=F32)


def _dot_nt(a, b):
    return lax.dot_general(a, b, (((1,), (1,)), ((), ())), preferred_element_type=F32)


def _dot_tn(a, b, precision=None):
    return lax.dot_general(a, b, (((0,), (0,)), ((), ())), precision=precision,
                           preferred_element_type=F32)


def _silu(x):
    return x / (1.0 + jnp.exp(-x))


def _log_sigmoid(x):
    return jnp.minimum(x, 0.0) - jnp.log1p(jnp.exp(-jnp.abs(x)))


def _head_norm_gate(o, g, gg):
    parts = []
    for h in range(GLA_HEADS):
        sl = slice(h * GLA_DV, (h + 1) * GLA_DV)
        parts.append(_rms(o[:, sl], gg[:, sl]))
    return jnp.concatenate(parts, axis=-1) * _silu(g)


def _full(shape):
    nd = len(shape)
    return pl.BlockSpec(shape, lambda *_: (0,) * nd)


def _mix_prompt_kernel(x_ref, g_ref, win_ref, wlr_ref, gup_ref, gb_ref, cw_ref, gg_ref, wout_ref,
                       x1_ref, conv_ref, s_ref, ubuf, s_scr, o_scr, *, tok_in):
    t = pl.program_id(1)
    tt = x1_ref.shape[1]

    @pl.when(t == 0)
    def _():
        ubuf[0:8, :] = jnp.zeros((8, CONV_W), F32)
        s_scr[...] = jnp.zeros(s_scr.shape, F32)

    x = _load_tok(x_ref, tt) if tok_in else x_ref[0]
    hb = _rms(x, g_ref[...]).astype(BF16)

    def proj(a, b):
        return _dot(hb, win_ref[:, a:b])

    u = proj(C_CC, C_CH) * proj(C_CH, C_Q)
    ubuf[8:8 + tt, :] = u
    cw = cw_ref[...]
    yc = proj(C_CB, C_CC) * (ubuf[6:6 + tt, :] * cw[0:1] + ubuf[7:7 + tt, :] * cw[1:2] + u * cw[2:3])
    ubuf[6:8, :] = u[tt - 2:tt, :]

    qs = proj(C_Q, C_K) * (GLA_DK ** -0.5)
    k = proj(C_K, C_V)
    vb = proj(C_V, C_G).astype(BF16)
    lr = _dot(hb, wlr_ref[...])
    gate = _dot(lr.astype(BF16), gup_ref[...]) + gb_ref[...]
    la = _log_sigmoid(gate) * (1.0 / GLA_TAU)

    c = GLA_CHUNK
    row = lax.broadcasted_iota(jnp.int32, (c, c), 0)
    col = lax.broadcasted_iota(jnp.int32, (c, c), 1)
    causal = row >= col
    ltri = causal.astype(F32)
    ones = jnp.ones((c, GLA_DV), F32)
    for j in range(tt // c):
        r = slice(j * c, (j + 1) * c)
        la_c, q_c, k_c, v_c = la[r], qs[r], k[r], vb[r]
        b = _dot(ltri, la_c, HIGHEST)
        b_mid = b[c // 2:c // 2 + 1, :]
        b_last = b[c - 1:c, :]
        q_i = (q_c * jnp.exp(b - b_mid)).astype(BF16)
        k_i = (k_c * jnp.exp(b_mid - b)).astype(BF16)
        k_dec = (k_c * jnp.exp(b_last - b)).astype(BF16)
        q_b = (q_c * jnp.exp(b)).astype(BF16)
        for h in range(GLA_HEADS):
            ks = slice(h * GLA_DK, (h + 1) * GLA_DK)
            vh = v_c[:, h * GLA_DV:(h + 1) * GLA_DV]
            a = jnp.where(causal, _dot_nt(q_i[:, ks], k_i[:, ks]), 0.0)
            s_prev = s_scr[h]
            o_scr[r, h * GLA_DV:(h + 1) * GLA_DV] = (
                _dot(a.astype(BF16), vh) + _dot(q_b[:, ks], s_prev.astype(BF16)))
            dec = jnp.exp(_dot_tn(la_c[:, ks], ones, HIGHEST))
            s_scr[h] = dec * s_prev + _dot_tn(k_dec[:, ks], vh)

    yg = _head_norm_gate(o_scr[...], proj(C_G, C_LR), gg_ref[...])
    y = _dot(yc.astype(BF16), wout_ref[0:CONV_W, :]) + _dot(yg.astype(BF16), wout_ref[CONV_W:, :])
    x1_ref[0] = x + y

    @pl.when(t == pl.num_programs(1) - 1)
    def _():
        conv_ref[0] = u[tt - 2:tt, :]
        s_ref[0] = s_scr[...]


def _mix_prompt(x, shape, g, win, wlr, gup, gb, cw, gg, wout, tt):
    bsz, t, d = shape
    tok_in = x.ndim == 2
    nt = t // tt
    x_spec = (pl.BlockSpec((tt * TOK_ROWS, LANES), lambda b, i: (b * nt + i, 0)) if tok_in
              else pl.BlockSpec((1, tt, d), lambda b, i: (b, i, 0)))
    return pl.pallas_call(
        functools.partial(_mix_prompt_kernel, tok_in=tok_in),
        grid=(bsz, nt),
        in_specs=[
            x_spec,
            _full(g.shape), _full(win.shape), _full(wlr.shape), _full(gup.shape), _full(gb.shape),
            _full(cw.shape), _full(gg.shape), _full(wout.shape),
        ],
        out_specs=[
            pl.BlockSpec((1, tt, d), lambda b, i: (b, i, 0)),
            pl.BlockSpec((1, 2, CONV_W), lambda b, i: (b, 0, 0)),
            pl.BlockSpec((1, GLA_HEADS, GLA_DK, GLA_DV), lambda b, i: (b, 0, 0, 0)),
        ],
        out_shape=[
            jax.ShapeDtypeStruct((bsz, t, d), F32),
            jax.ShapeDtypeStruct((bsz, 2, CONV_W), F32),
            jax.ShapeDtypeStruct((bsz, GLA_HEADS, GLA_DK, GLA_DV), F32),
        ],
        scratch_shapes=[
            pltpu.VMEM((8 + tt, CONV_W), F32),
            pltpu.VMEM((GLA_HEADS, GLA_DK, GLA_DV), F32),
            pltpu.VMEM((tt, GLA_V), F32),
        ],
        compiler_params=_params("arbitrary", "arbitrary"),
        name="mix_prompt",
    )(x, g, win, wlr, gup, gb, cw, gg, wout)


def _kv_kernel(m_ref, wk_ref, wv_ref, k_ref, v_ref):
    mb = m_ref[...].astype(BF16)
    k_ref[...] = _dot(mb, wk_ref[...])
    v_ref[...] = _dot(mb, wv_ref[...])


def _mem_kv(mem, wk, wv, tm):
    n, d = mem.shape
    return pl.pallas_call(
        _kv_kernel,
        grid=(n // tm,),
        in_specs=[pl.BlockSpec((tm, d), lambda i: (i, 0)), _full(wk.shape), _full(wv.shape)],
        out_specs=[pl.BlockSpec((tm, d), lambda i: (i, 0))] * 2,
        out_shape=[jax.ShapeDtypeStruct((n, d), F32)] * 2,
        compiler_params=_params("arbitrary"),
        name="mem_kv",
    )(mem, wk, wv)


def _att_prompt_kernel(x_ref, g_ref, wq_ref, wo_ref, mk_ref, mv_ref, out_ref):
    x = x_ref[0]
    xb = _rms(x, g_ref[...]).astype(BF16)
    q = _dot(xb, wq_ref[...])
    outs = []
    for h in range(MEM_HEADS):
        sl = slice(h * MEM_DH, (h + 1) * MEM_DH)
        s = _dot_nt(q[:, sl].astype(BF16), mk_ref[0, :, sl].astype(BF16)) * (MEM_DH ** -0.5)
        e = jnp.exp(s - jnp.max(s, axis=-1, keepdims=True))
        p = e / jnp.sum(e, axis=-1, keepdims=True)
        outs.append(_dot(p.astype(BF16), mv_ref[0, :, sl].astype(BF16)))
    o = jnp.concatenate(outs, axis=-1)
    out_ref[0] = x + _dot(o.astype(BF16), wo_ref[...])


def _att_prompt(x, g, wq, wo, mk, mv, tq):
    bsz, t, d = x.shape
    nm = mk.shape[1]
    return pl.pallas_call(
        _att_prompt_kernel,
        grid=(bsz, t // tq),
        in_specs=[
            pl.BlockSpec((1, tq, d), lambda b, i: (b, i, 0)),
            _full(g.shape), _full(wq.shape), _full(wo.shape),
            pl.BlockSpec((1, nm, d), lambda b, i: (b, 0, 0)),
            pl.BlockSpec((1, nm, d), lambda b, i: (b, 0, 0)),
        ],
        out_specs=pl.BlockSpec((1, tq, d), lambda b, i: (b, i, 0)),
        out_shape=jax.ShapeDtypeStruct((bsz, t, d), F32),
        compiler_params=_params("arbitrary", "arbitrary"),
        name="att_prompt",
    )(x, g, wq, wo, mk, mv)


def _route_top2(logits):
    lane = lax.broadcasted_iota(jnp.int32, logits.shape, 1)
    lanef = lane.astype(F32)
    ninf = -jnp.inf
    big = 1e9
    gl = jnp.where(lane < N_GROUPS, logits, ninf)
    gmax = jnp.max(gl, axis=-1, keepdims=True)
    g_idx = jnp.min(jnp.where(gl == gmax, lanef, big), axis=-1, keepdims=True)
    g_w = 1.0 / jnp.sum(jnp.exp(gl - gmax), axis=-1, keepdims=True)
    grp = ((lane - N_GROUPS) >> 2).astype(F32)
    emask = (lane >= N_GROUPS) & (lane < N_GROUPS + N_EXPERTS) & (grp == g_idx)
    el = jnp.where(emask, logits, ninf)
    m1 = jnp.max(el, axis=-1, keepdims=True)
    i1 = jnp.min(jnp.where(el == m1, lanef, big), axis=-1, keepdims=True)
    el2 = jnp.where(lanef == i1, ninf, el)
    m2 = jnp.max(el2, axis=-1, keepdims=True)
    i2 = jnp.min(jnp.where(el2 == m2, lanef, big), axis=-1, keepdims=True)
    tail = jnp.exp(m2 - m1)
    w1 = g_w / (1.0 + tail)
    w2 = g_w * tail / (1.0 + tail)
    return g_idx, i1, i2, w1, w2


def _route(logits):
    _, i1, i2, w1, w2 = _route_top2(logits)
    lanef = lax.broadcasted_iota(jnp.int32, logits.shape, 1).astype(F32)
    return jnp.where(lanef == i1, w1, 0.0) + jnp.where(lanef == i2, w2, 0.0)


N_PAIRS = 6
N_CLASSES = N_GROUPS * N_PAIRS
TOK_ROWS = 8


def _load_tok(ref, n):
    return jnp.concatenate([ref[pl.ds(j, n, stride=TOK_ROWS), :] for j in range(TOK_ROWS)], axis=-1)


def _store_tok(ref, val):
    n = val.shape[0]
    for j in range(TOK_ROWS):
        ref[pl.ds(j, n, stride=TOK_ROWS), :] = val[:, j * LANES:(j + 1) * LANES]


def _moe_route_kernel(x_ref, g_ref, rw_ref, rb_ref, meta_ref, cnt_ref, carry):
    i = pl.program_id(0)
    tr = x_ref.shape[0]

    @pl.when(i == 0)
    def _():
        carry[...] = jnp.zeros(carry.shape, F32)

    xn = _rms(x_ref[...], g_ref[...])
    g_idx, i1, i2, _, _ = _route_top2(_dot(xn, rw_ref[...], HIGHEST) + rb_ref[...])
    lo = jnp.minimum(i1, i2) - N_GROUPS - EXP_PER_GROUP * g_idx
    hi = jnp.maximum(i1, i2) - N_GROUPS - EXP_PER_GROUP * g_idx
    cls = g_idx * N_PAIRS + lo * (7.0 - lo) * 0.5 + hi - lo - 1.0

    lane = lax.broadcasted_iota(jnp.int32, (tr, LANES), 1)
    onehot = (lane.astype(F32) == cls).astype(BF16)
    ltri = (lax.broadcasted_iota(jnp.int32, (tr, tr), 0)
            >= lax.broadcasted_iota(jnp.int32, (tr, tr), 1)).astype(BF16)
    prefix = _dot(ltri, onehot) + carry[...]
    rank = jnp.sum(jnp.where(onehot > 0, prefix, 0.0), axis=-1, keepdims=True) - 1.0
    carry[...] = prefix[tr - 1:tr, :]
    meta_ref[...] = (jnp.where(lane == 0, cls, 0.0) + jnp.where(lane == 1, rank, 0.0)).astype(jnp.int32)
    cnt_ref[...] = prefix[tr - 1:tr, :].astype(jnp.int32)


def _moe_route(x, g, rw, rb, tr):
    n, d = x.shape
    assert d == TOK_ROWS * LANES
    return pl.pallas_call(
        _moe_route_kernel,
        grid=(n // tr,),
        in_specs=[pl.BlockSpec((tr, d), lambda i: (i, 0)), _full(g.shape), _full(rw.shape), _full(rb.shape)],
        out_specs=[
            pl.BlockSpec((tr, LANES), lambda i: (i, 0)),
            _full((1, LANES)),
        ],
        out_shape=[
            jax.ShapeDtypeStruct((n, LANES), jnp.int32),
            jax.ShapeDtypeStruct((1, LANES), jnp.int32),
        ],
        scratch_shapes=[pltpu.VMEM((1, LANES), F32)],
        compiler_params=_params("arbitrary"),
        name="moe_route",
    )(x, g, rw, rb)


def _tok_rows(t):
    return pl.ds(pl.multiple_of(t * TOK_ROWS, TOK_ROWS), TOK_ROWS)


def _tok_scatter_kernel(pos, x_ref, dst_in_ref, dst_ref, stage, sems, *, toks, steps):
    del dst_in_ref
    i = pl.program_id(0)
    slot = i % 2
    base = i * toks

    def wait_slot(s):
        pltpu.make_async_copy(stage.at[s], dst_ref.at[pl.ds(0, toks * TOK_ROWS)], sems.at[s]).wait()

    @pl.when(i >= 2)
    def _():
        wait_slot(slot)

    _store_tok(stage.at[slot], x_ref[...])

    def start(r, c):
        pltpu.make_async_copy(stage.at[slot, _tok_rows(r)], dst_ref.at[_tok_rows(pos[base + r])], sems.at[slot]).start()
        return c

    lax.fori_loop(0, toks, start, 0, unroll=8)

    @pl.when(i == steps - 1)
    def _():
        wait_slot(slot)
        if steps >= 2:
            wait_slot(1 - slot)


def _tok_scatter(x, dst_init, pos, toks):
    n, d = x.shape
    steps = n // toks
    return pl.pallas_call(
        functools.partial(_tok_scatter_kernel, toks=toks, steps=steps),
        grid_spec=pltpu.PrefetchScalarGridSpec(
            num_scalar_prefetch=1,
            grid=(steps,),
            in_specs=[pl.BlockSpec((toks, d), lambda i, p: (i, 0)), pl.BlockSpec(memory_space=pl.ANY)],
            out_specs=pl.BlockSpec(memory_space=pl.ANY),
            scratch_shapes=[pltpu.VMEM((2, toks * TOK_ROWS, LANES), F32), pltpu.SemaphoreType.DMA((2,))],
        ),
        out_shape=jax.ShapeDtypeStruct(dst_init.shape, dst_init.dtype),
        input_output_aliases={2: 0},
        compiler_params=_params("arbitrary"),
        name="tok_scatter",
    )(pos, x, dst_init)


def _tok_gather_kernel(pos, src_ref, out_ref, sem, *, toks):
    base = pl.program_id(0) * toks

    def start(r, c):
        pltpu.make_async_copy(src_ref.at[_tok_rows(pos[base + r])], out_ref.at[_tok_rows(r)], sem).start()
        return c

    lax.fori_loop(0, toks, start, 0, unroll=8)
    pltpu.make_async_copy(src_ref.at[pl.ds(0, toks * TOK_ROWS)], out_ref, sem).wait()


def _tok_gather(src, pos, toks):
    n = pos.shape[0]
    return pl.pallas_call(
        functools.partial(_tok_gather_kernel, toks=toks),
        grid_spec=pltpu.PrefetchScalarGridSpec(
            num_scalar_prefetch=1,
            grid=(n // toks,),
            in_specs=[pl.BlockSpec(memory_space=pl.ANY)],
            out_specs=pl.BlockSpec((toks * TOK_ROWS, LANES), lambda i, p: (i, 0)),
            scratch_shapes=[pltpu.SemaphoreType.DMA(())],
        ),
        out_shape=jax.ShapeDtypeStruct((n * TOK_ROWS, LANES), F32),
        compiler_params=_params("arbitrary"),
        name="tok_gather",
    )(pos, src)


def _moe_sorted_kernel(e_lo, e_hi, valid, xs_ref, g_ref, rw_ref, rb_ref, wgu_lo, wgu_hi, wd_lo, wd_hi, out_ref):
    t = pl.program_id(0)
    tm = xs_ref.shape[0] // TOK_ROWS
    ff = wd_lo.shape[1]

    @pl.when(valid[t] > 0)
    def _():
        x = _load_tok(xs_ref, tm)
        xb = _rms(x, g_ref[...]).astype(BF16)
        logits = _dot(xb, rw_ref[...]) + rb_ref[...]
        lane = lax.broadcasted_iota(jnp.int32, logits.shape, 1)
        gl = jnp.where(lane < N_GROUPS, logits, -jnp.inf)
        g_w = 1.0 / jnp.sum(jnp.exp(gl - jnp.max(gl, axis=-1, keepdims=True)), axis=-1, keepdims=True)
        l_lo = jnp.sum(jnp.where(lane == e_lo[t] + N_GROUPS, logits, 0.0), axis=-1, keepdims=True)
        l_hi = jnp.sum(jnp.where(lane == e_hi[t] + N_GROUPS, logits, 0.0), axis=-1, keepdims=True)
        tail = jnp.exp(-jnp.abs(l_lo - l_hi))
        w_top = g_w / (1.0 + tail)
        w_oth = g_w * tail / (1.0 + tail)
        lo_top = l_lo >= l_hi
        y = x
        for wgu, wd, w in ((wgu_lo, wd_lo, jnp.where(lo_top, w_top, w_oth)),
                           (wgu_hi, wd_hi, jnp.where(lo_top, w_oth, w_top))):
            gu = _dot(xb, wgu[0])
            hid = _silu(gu[:, 0:ff]) * gu[:, ff:] * w
            y = y + _dot(hid.astype(BF16), wd[0])
        _store_tok(out_ref, y)

    @pl.when(valid[t] == 0)
    def _():
        out_ref[...] = jnp.zeros(out_ref.shape, F32)


def _moe_sorted(xs, g, rw, rb, wgu, wd, e_lo, e_hi, valid, tm):
    npad = xs.shape[0] // TOK_ROWS
    _, d, ff2 = wgu.shape
    const = lambda t, lo, hi, v: (0, 0)
    return pl.pallas_call(
        _moe_sorted_kernel,
        grid_spec=pltpu.PrefetchScalarGridSpec(
            num_scalar_prefetch=3,
            grid=(npad // tm,),
            in_specs=[
                pl.BlockSpec((tm * TOK_ROWS, LANES), lambda t, lo, hi, v: (t, 0)),
                pl.BlockSpec(g.shape, const), pl.BlockSpec(rw.shape, const), pl.BlockSpec(rb.shape, const),
                pl.BlockSpec((1, d, ff2), lambda t, lo, hi, v: (lo[t], 0, 0)),
                pl.BlockSpec((1, d, ff2), lambda t, lo, hi, v: (hi[t], 0, 0)),
                pl.BlockSpec((1, ff2 // 2, d), lambda t, lo, hi, v: (lo[t], 0, 0)),
                pl.BlockSpec((1, ff2 // 2, d), lambda t, lo, hi, v: (hi[t], 0, 0)),
            ],
            out_specs=pl.BlockSpec((tm * TOK_ROWS, LANES), lambda t, lo, hi, v: (t, 0)),
        ),
        out_shape=jax.ShapeDtypeStruct(xs.shape, F32),
        compiler_params=_params("arbitrary"),
        name="moe_sorted",
    )(e_lo, e_hi, valid, xs, g, rw, rb, wgu, wgu, wd, wd)


def _moe_sparse(x, g, rw, rb, wgu, wd, tm, tr, toks):
    n, d = x.shape
    meta, counts = _moe_route(x, g, rw, rb, tr)
    cls, rank = meta[:, 0], meta[:, 1]
    cnt = counts[0, :N_CLASSES]
    padded = ((cnt + tm - 1) // tm) * tm
    ends = jnp.cumsum(padded)
    pos = (ends - padded)[cls] + rank
    n_tiles = n // tm + N_CLASSES
    tile_start = jnp.arange(n_tiles, dtype=jnp.int32) * tm
    tile_cls = jnp.minimum(jnp.searchsorted(ends, tile_start, side="right"), N_CLASSES - 1).astype(jnp.int32)
    valid = (tile_start < ends[-1]).astype(jnp.int32)
    tile_cls = jnp.where(valid > 0, tile_cls, tile_cls[jnp.maximum(ends[-1] // tm - 1, 0)])
    pair_lo = jnp.array([0, 0, 0, 1, 1, 2], jnp.int32)
    pair_hi = jnp.array([1, 2, 3, 2, 3, 3], jnp.int32)
    e_lo = (tile_cls // N_PAIRS) * EXP_PER_GROUP + pair_lo[tile_cls % N_PAIRS]
    e_hi = (tile_cls // N_PAIRS) * EXP_PER_GROUP + pair_hi[tile_cls % N_PAIRS]

    xs = _tok_scatter(x, jnp.zeros((n_tiles * tm * TOK_ROWS, LANES), F32), pos, toks)
    ys = _moe_sorted(xs, g, rw.astype(BF16), rb, wgu, wd, e_lo, e_hi, valid, tm)
    return _tok_gather(ys, pos, toks)


def _moe_kernel(x_ref, g_ref, rw_ref, rb_ref, wg_ref, wu_ref, wd_ref, out_ref, xn_scr, comb_scr, acc_scr):
    e = pl.program_id(1)

    @pl.when(e == 0)
    def _():
        x = x_ref[...]
        xn = _rms(x, g_ref[...])
        xn_scr[...] = xn.astype(BF16)
        comb_scr[...] = _route(_dot(xn, rw_ref[...], HIGHEST) + rb_ref[...])
        acc_scr[...] = x

    xb = xn_scr[...]
    lane = lax.broadcasted_iota(jnp.int32, comb_scr.shape, 1)
    cmb = jnp.sum(jnp.where(lane == e + N_GROUPS, comb_scr[...], 0.0), axis=-1, keepdims=True)
    hid = _silu(_dot(xb, wg_ref[0])) * _dot(xb, wu_ref[0]) * cmb
    acc_scr[...] += _dot(hid.astype(BF16), wd_ref[0])

    @pl.when(e == pl.num_programs(1) - 1)
    def _():
        out_ref[...] = acc_scr[...]


def _moe(x, g, rw, rb, wg, wu, wd, tm):
    n, d = x.shape
    ne, _, ff = wg.shape
    return pl.pallas_call(
        _moe_kernel,
        grid=(n // tm, ne),
        in_specs=[
            pl.BlockSpec((tm, d), lambda i, e: (i, 0)),
            _full(g.shape), _full(rw.shape), _full(rb.shape),
            pl.BlockSpec((1, d, ff), lambda i, e: (e, 0, 0)),
            pl.BlockSpec((1, d, ff), lambda i, e: (e, 0, 0)),
            pl.BlockSpec((1, ff, d), lambda i, e: (e, 0, 0)),
        ],
        out_specs=pl.BlockSpec((tm, d), lambda i, e: (i, 0)),
        out_shape=jax.ShapeDtypeStruct((n, d), F32),
        scratch_shapes=[pltpu.VMEM((tm, d), BF16), pltpu.VMEM((tm, LANES), F32), pltpu.VMEM((tm, d), F32)],
        compiler_params=_params("arbitrary", "arbitrary"),
        name="moe",
    )(x, g, rw, rb, wg, wu, wd)


def _norm_kernel(x_ref, g_ref, o_ref, *, tok_in):
    x = _load_tok(x_ref, o_ref.shape[0]) if tok_in else x_ref[...]
    o_ref[...] = _rms(x, g_ref[...])


def _final_norm(x, g, tm, tok_in=False):
    d = g.shape[1]
    n = x.shape[0] // TOK_ROWS if tok_in else x.shape[0]
    x_spec = (pl.BlockSpec((tm * TOK_ROWS, LANES), lambda i: (i, 0)) if tok_in
              else pl.BlockSpec((tm, d), lambda i: (i, 0)))
    return pl.pallas_call(
        functools.partial(_norm_kernel, tok_in=tok_in),
        grid=(n // tm,),
        in_specs=[x_spec, _full(g.shape)],
        out_specs=pl.BlockSpec((tm, d), lambda i: (i, 0)),
        out_shape=jax.ShapeDtypeStruct((n, d), F32),
        compiler_params=_params("arbitrary"),
        name="final_norm",
    )(x, g)


def _mix_sample_in_kernel(x_ref, g_ref, win_ref, wlr_ref, gup_ref, gb_ref, cw_ref, b0_ref, b1_ref,
                          yc_ref, u_ref, q_ref, k_ref, v_ref, gate_ref, la_ref):
    hb = _rms(x_ref[...], g_ref[...]).astype(BF16)

    def proj(a, b):
        return _dot(hb, win_ref[:, a:b])

    u = proj(C_CC, C_CH) * proj(C_CH, C_Q)
    cw = cw_ref[...]
    yc_ref[...] = proj(C_CB, C_CC) * (b0_ref[...] * cw[0:1] + b1_ref[...] * cw[1:2] + u * cw[2:3])
    u_ref[...] = u
    q_ref[...] = proj(C_Q, C_K) * (GLA_DK ** -0.5)
    k_ref[...] = proj(C_K, C_V)
    v_ref[...] = proj(C_V, C_G)
    gate_ref[...] = proj(C_G, C_LR)
    lr = _dot(hb, wlr_ref[...])
    gate = _dot(lr.astype(BF16), gup_ref[...]) + gb_ref[...]
    la_ref[...] = _log_sigmoid(gate) * (1.0 / GLA_TAU)


def _mix_sample_in(x, g, win, wlr, gup, gb, cw, b0, b1):
    n = x.shape[0]
    args = (x, g, win, wlr, gup, gb, cw, b0, b1)
    widths = (CONV_W, CONV_W, GLA_K, GLA_K, GLA_V, GLA_V, GLA_K)
    return pl.pallas_call(
        _mix_sample_in_kernel,
        grid=(1,),
        in_specs=[_full(a.shape) for a in args],
        out_specs=[_full((n, w)) for w in widths],
        out_shape=[jax.ShapeDtypeStruct((n, w), F32) for w in widths],
        compiler_params=_params("arbitrary"),
        name="mix_sample_in",
    )(*args)


def _gla_step_kernel(q_ref, k_ref, v_ref, la_ref, s0_ref, s_ref, o_ref):
    nb = q_ref.shape[0]
    a = jnp.exp(la_ref[...])
    kb = k_ref[...]
    qb = q_ref[...].astype(BF16)
    vb = v_ref[...].astype(BF16)
    rows = lax.broadcasted_iota(jnp.int32, (nb, 1), 0)
    spread = (lax.broadcasted_iota(jnp.int32, (nb, nb * GLA_DV), 0)
              == lax.broadcasted_iota(jnp.int32, (nb, nb * GLA_DV), 1) // GLA_DV).astype(F32)
    for h in range(GLA_HEADS):
        ks = slice(h * GLA_DK, (h + 1) * GLA_DK)
        vh = vb[:, h * GLA_DV:(h + 1) * GLA_DV]
        a_cols = _dot_tn(a[:, ks], spread, HIGHEST)
        o_h = jnp.zeros((nb, GLA_DV), F32)
        for n in range(nb):
            k_n = jnp.where(rows == n, kb[:, ks], 0.0).astype(BF16)
            s_new = a_cols[:, n * GLA_DV:(n + 1) * GLA_DV] * s0_ref[0, n, h] + _dot_tn(k_n, vh)
            s_ref[n, h] = s_new
            o_h = o_h + jnp.where(rows == n, _dot(qb[:, ks], s_new.astype(BF16)), 0.0)
        o_ref[:, h * GLA_DV:(h + 1) * GLA_DV] = o_h


def _gla_step(q, k, v, la, state, layer, nb):
    n = q.shape[0]
    sshape = (GLA_HEADS, GLA_DK, GLA_DV)
    return pl.pallas_call(
        _gla_step_kernel,
        grid=(n // nb,),
        in_specs=[
            pl.BlockSpec((nb, GLA_K), lambda i: (i, 0)),
            pl.BlockSpec((nb, GLA_K), lambda i: (i, 0)),
            pl.BlockSpec((nb, GLA_V), lambda i: (i, 0)),
            pl.BlockSpec((nb, GLA_K), lambda i: (i, 0)),
            pl.BlockSpec((1, nb) + sshape, lambda i: (layer, i, 0, 0, 0)),
        ],
        out_specs=[
            pl.BlockSpec((nb,) + sshape, lambda i: (i, 0, 0, 0)),
            pl.BlockSpec((nb, GLA_V), lambda i: (i, 0)),
        ],
        out_shape=[jax.ShapeDtypeStruct((n,) + sshape, F32), jax.ShapeDtypeStruct((n, GLA_V), F32)],
        compiler_params=_params("arbitrary"),
        name="gla_step",
    )(q, k, v, la, state)


def _mix_sample_out_kernel(x_ref, yc_ref, o_ref, gate_ref, gg_ref, wout_ref, gm_ref, wq_ref, x1_ref, q_ref):
    yg = _head_norm_gate(o_ref[...], gate_ref[...], gg_ref[...])
    y = _dot(yc_ref[...].astype(BF16), wout_ref[0:CONV_W, :]) + _dot(yg.astype(BF16), wout_ref[CONV_W:, :])
    x1 = x_ref[...] + y
    x1_ref[...] = x1
    q_ref[...] = _dot(_rms(x1, gm_ref[...]).astype(BF16), wq_ref[...])


def _mix_sample_out(x, yc, o, gate, gg, wout, gm, wq):
    args = (x, yc, o, gate, gg, wout, gm, wq)
    return pl.pallas_call(
        _mix_sample_out_kernel,
        grid=(1,),
        in_specs=[_full(a.shape) for a in args],
        out_specs=[_full(x.shape)] * 2,
        out_shape=[jax.ShapeDtypeStruct(x.shape, F32)] * 2,
        compiler_params=_params("arbitrary"),
        name="mix_sample_out",
    )(*args)


ATT_ROWS = 2 * MEM_HEADS


def _class_allreduce(x, op):
    n = x.shape[-1]
    shift = ATT_ROWS
    while shift < n:
        x = op(x, pltpu.roll(x, shift, axis=1))
        shift *= 2
    return x


def _att_sample_kernel(q_ref, k_ref, v_ref, o_ref):
    nb = q_ref.shape[0]
    ncol = k_ref.shape[2]
    diag = (lax.broadcasted_iota(jnp.int32, (ATT_ROWS, ncol), 0)
            == (lax.broadcasted_iota(jnp.int32, (ATT_ROWS, ncol), 1) & (ATT_ROWS - 1)))
    rows = lax.broadcasted_iota(jnp.int32, (nb, 1), 0)
    t = jnp.zeros((nb, ncol), F32)
    for n in range(nb):
        sc = _dot_nt(q_ref[n].astype(BF16), k_ref[0, n].astype(BF16))
        t = t + jnp.where(rows == n, jnp.sum(jnp.where(diag, sc, 0.0), axis=0, keepdims=True), 0.0)
    valid = (lax.broadcasted_iota(jnp.int32, (nb, ncol), 1) & (ATT_ROWS - 1)) < MEM_HEADS
    s = jnp.where(valid, (t + pltpu.roll(t, ncol - MEM_HEADS, axis=1)) * (MEM_DH ** -0.5), 0.0)
    e = jnp.where(valid, jnp.exp(s - _class_allreduce(s, jnp.maximum)), 0.0)
    den = jnp.where(valid, _class_allreduce(e, jnp.add), 1.0)
    p = e / den
    p = p + pltpu.roll(p, MEM_HEADS, axis=1)
    for n in range(nb):
        p_n = jnp.where(diag, jnp.broadcast_to(p[n:n + 1, :], (ATT_ROWS, ncol)), 0.0)
        o_ref[n] = _dot(p_n.astype(BF16), v_ref[0, n].astype(BF16))


def _att_sample(q, ck, cv, layer, nb):
    n = q.shape[0]
    ncol = ck.shape[2]
    return pl.pallas_call(
        _att_sample_kernel,
        grid=(n // nb,),
        in_specs=[
            pl.BlockSpec((nb, ATT_ROWS, LANES), lambda i: (i, 0, 0)),
            pl.BlockSpec((1, nb, ncol, LANES), lambda i: (layer, i, 0, 0)),
            pl.BlockSpec((1, nb, ncol, LANES), lambda i: (layer, i, 0, 0)),
        ],
        out_specs=pl.BlockSpec((nb, ATT_ROWS, LANES), lambda i: (i, 0, 0)),
        out_shape=jax.ShapeDtypeStruct((n, ATT_ROWS, LANES), F32),
        compiler_params=_params("arbitrary"),
        name="att_sample",
    )(q, ck, cv)


def _oproj_kernel(x_ref, o_ref, wo_ref, out_ref):
    out_ref[...] = x_ref[...] + _dot(o_ref[...].astype(BF16), wo_ref[...])


def _oproj(x, o, wo):
    return pl.pallas_call(
        _oproj_kernel,
        grid=(1,),
        in_specs=[_full(x.shape), _full(o.shape), _full(wo.shape)],
        out_specs=_full(x.shape),
        out_shape=jax.ShapeDtypeStruct(x.shape, F32),
        compiler_params=_params("arbitrary"),
        name="oproj",
    )(x, o, wo)


def kernel(x_prompt, x_sample, state_conv, state_gla, cache_mem_k, cache_mem_v, mem_prompt, norm_mix, w_in, conv_w, gla_gate_up, gla_gate_b, gla_out_norm, w_out, norm_mem, w_q, w_k, w_v, w_o, norm_ffn, router_group, router_group_b, router_expert, router_expert_b, w_gate, w_up, w_down, norm_final):
    depth = w_in.shape[0]
    bsz, t, d = x_prompt.shape
    ns = x_sample.shape[0]
    nm = mem_prompt.shape[1]
    n_tok = bsz * t

    tt = min(256, t)
    tq = min(512, t)
    tm_moe = min(1024, n_tok)
    tm_kv = min(512, bsz * nm)
    tm_sorted = 256
    tr_route = min(512, n_tok)
    rows_perm = min(512, n_tok)
    nb_gla = min(8, ns)
    nb_att = min(8, ns)

    row = lambda a: a.reshape(1, -1)
    mem2 = mem_prompt.reshape(bsz * nm, d)

    def tile_rows(c):
        c = c.reshape(depth, ns, nm, MEM_HEADS, 2, LANES).transpose(0, 1, 2, 4, 3, 5)
        return c.reshape(depth, ns, nm * ATT_ROWS, LANES)

    ck, cv = tile_rows(cache_mem_k), tile_rows(cache_mem_v)

    xp = x_prompt
    xs = x_sample.reshape(ns, d)
    conv_p, gla_p, mk_p, mv_p, conv_s, gla_s = [], [], [], [], [], []
    for l in range(depth):
        win = w_in[l, :, :C_LR].astype(BF16)
        wlr = jnp.pad(w_in[l, :, C_LR:], ((0, 0), (0, LANES - GLA_RANK))).astype(BF16)
        gup = jnp.pad(gla_gate_up[l], ((0, LANES - GLA_RANK), (0, 0))).astype(BF16)
        gb = row(gla_gate_b[l])
        gg = row(gla_out_norm[l])
        wout = w_out[l].astype(BF16)
        wq, wk, wv, wo = (w[l].astype(BF16) for w in (w_q, w_k, w_v, w_o))
        rw = jnp.concatenate([router_group[l], router_expert[l].transpose(1, 0, 2).reshape(d, N_EXPERTS)], axis=1)
        rw = jnp.pad(rw, ((0, 0), (0, LANES - rw.shape[1])))
        rb = jnp.concatenate([router_group_b[l], router_expert_b[l].reshape(-1)])
        rb = row(jnp.pad(rb, (0, LANES - rb.shape[0])))
        wg, wu, wd = w_gate[l].astype(BF16), w_up[l].astype(BF16), w_down[l].astype(BF16)
        moe_w = (row(norm_ffn[l]), rw, rb, wg, wu, wd)

        mk, mv = _mem_kv(mem2, wk, wv, tm_kv)
        mk_p.append(mk.reshape(bsz, nm, MEM_HEADS, MEM_DH))
        mv_p.append(mv.reshape(bsz, nm, MEM_HEADS, MEM_DH))
        xp, nbuf, ns_p = _mix_prompt(xp, (bsz, t, d), row(norm_mix[l]), win, wlr, gup, gb, conv_w[l], gg, wout, tt)
        conv_p.append(nbuf)
        gla_p.append(ns_p)
        xp = _att_prompt(xp, row(norm_mem[l]), wq, wo, mk.reshape(bsz, nm, d), mv.reshape(bsz, nm, d), tq)
        wgu = jnp.concatenate([wg, wu], axis=-1)
        xp = _moe_sparse(xp.reshape(n_tok, d), row(norm_ffn[l]), rw, rb, wgu, wd, tm_sorted, tr_route, rows_perm)

        yc, u, q, k, v, gate, la = _mix_sample_in(
            xs, row(norm_mix[l]), win, wlr, gup, gb, conv_w[l], state_conv[l, :, 0], state_conv[l, :, 1])
        conv_s.append(jnp.stack([state_conv[l, :, 1], u], axis=1))
        s_new, o = _gla_step(q, k, v, la, state_gla, l, nb_gla)
        gla_s.append(s_new)
        wq_s = wq.reshape(d, MEM_HEADS, 2, LANES).transpose(0, 2, 1, 3).reshape(d, d)
        wo_s = wo.reshape(MEM_HEADS, 2, LANES, d).transpose(1, 0, 2, 3).reshape(d, d)
        xs, qa = _mix_sample_out(xs, yc, o, gate, gg, wout, row(norm_mem[l]), wq_s)
        oa = _att_sample(qa.reshape(ns, ATT_ROWS, LANES), ck, cv, l, nb_att)
        xs = _oproj(xs, oa.reshape(ns, d), wo_s)
        xs = _moe(xs, *moe_w, min(tm_moe, ns))

    y_prompt = _final_norm(xp, row(norm_final), tm_moe, tok_in=True).reshape(bsz, t, d)
    y_sample = _final_norm(xs, row(norm_final), ns).reshape(ns, 1, d)
    return (y_prompt, y_sample, jnp.stack(conv_p), jnp.stack(gla_p), jnp.stack(mk_p), jnp.stack(mv_p),
            jnp.stack(conv_s), jnp.stack(gla_s))
```

```python
import functools

import jax
import jax.numpy as jnp
from jax import lax
from jax.experimental import pallas as pl
from jax.experimental.pallas import tpu as pltpu

F32 = jnp.float32
BF16 = jnp.bfloat16
HIGHEST = lax.Precision.HIGHEST

EPS = 1e-6
CONV_W = 512
GLA_HEADS = 4
GLA_DK = 64
GLA_DV = 128
GLA_K = GLA_HEADS * GLA_DK
GLA_V = GLA_HEADS * GLA_DV
GLA_RANK = 16
GLA_TAU = 16.0
GLA_CHUNK = 64
MEM_HEADS = 4
MEM_DH = 256
N_GROUPS = 4
EXP_PER_GROUP = 4
N_EXPERTS = 16
LANES = 128
C_CB, C_CC, C_CH, C_Q, C_K, C_V, C_G, C_LR = 0, 512, 1024, 1536, 1792, 2048, 2560, 3072
VMEM_LIMIT = 52 * 1024 * 1024


def _params(*sem):
    return pltpu.CompilerParams(dimension_semantics=sem, vmem_limit_bytes=VMEM_LIMIT)


def _rms(x, g):
    return x * lax.rsqrt(jnp.mean(x * x, axis=-1, keepdims=True) + EPS) * g


def _dot(a, b, precision=None):
    return jnp.dot(a, b, precision=precision, preferred_element_type=F32)


def _dot_nt(a, b):
    return lax.dot_general(a, b, (((1,), (1,)), ((), ())), preferred_element_type=F32)


def _dot_tn(a, b, precision=None):
    return lax.dot_general(a, b, (((0,), (0,)), ((), ())), precision=precision,
                           preferred_element_type=F32)


def _dot_split(dot, mask, x):
    hi = x.astype(BF16)
    r1 = x - hi.astype(F32)
    mid = r1.astype(BF16)
    lo = (r1 - mid.astype(F32)).astype(BF16)
    return dot(mask, hi) + dot(mask, mid) + dot(mask, lo)


def _silu(x):
    return x / (1.0 + jnp.exp(-x))


def _log_sigmoid(x):
    return jnp.minimum(x, 0.0) - jnp.log1p(jnp.exp(-jnp.abs(x)))


def _head_norm_gate(o, g, gg):
    parts = []
    for h in range(GLA_HEADS):
        sl = slice(h * GLA_DV, (h + 1) * GLA_DV)
        parts.append(_rms(o[:, sl], gg[:, sl]))
    return jnp.concatenate(parts, axis=-1) * _silu(g)


def _full(shape):
    nd = len(shape)
    return pl.BlockSpec(shape, lambda *_: (0,) * nd)


def _mix_prompt_kernel(x_ref, g_ref, win_ref, wlr_ref, gup_ref, gb_ref, cw_ref, gg_ref, wout_ref,
                       x1_ref, conv_ref, s_ref, ubuf, s_scr, o_scr, *, tok_in):
    t = pl.program_id(1)
    tt = x1_ref.shape[1]

    @pl.when(t == 0)
    def _():
        ubuf[0:8, :] = jnp.zeros((8, CONV_W), F32)
        s_scr[...] = jnp.zeros(s_scr.shape, F32)

    x = _load_tok(x_ref, tt) if tok_in else x_ref[0]
    hb = _rms(x, g_ref[...]).astype(BF16)

    def proj(a, b):
        return _dot(hb, win_ref[:, a:b])

    u = proj(C_CC, C_CH) * proj(C_CH, C_Q)
    ubuf[8:8 + tt, :] = u
    cw = cw_ref[...]
    yc = proj(C_CB, C_CC) * (ubuf[6:6 + tt, :] * cw[0:1] + ubuf[7:7 + tt, :] * cw[1:2] + u * cw[2:3])
    ubuf[6:8, :] = u[tt - 2:tt, :]

    qs = proj(C_Q, C_K) * (GLA_DK ** -0.5)
    k = proj(C_K, C_V)
    v = proj(C_V, C_G)
    lr = _dot(hb, wlr_ref[...])
    gate = _dot(lr.astype(BF16), gup_ref[...]) + gb_ref[...]
    la = _log_sigmoid(gate) * (1.0 / GLA_TAU)

    c = GLA_CHUNK
    nc = tt // c
    iota = lambda shape, dim: lax.broadcasted_iota(jnp.int32, shape, dim)
    ltri = (iota((c, c), 0) >= iota((c, c), 1)).astype(BF16)
    b_wide = _dot_split(_dot, ltri, jnp.concatenate([la[j * c:(j + 1) * c] for j in range(nc)], axis=1))
    b_ends = jnp.concatenate([b_wide[c - 1:c, j * GLA_K:(j + 1) * GLA_K] for j in range(nc)], axis=0)
    spread = (iota((nc, nc * GLA_DV), 0) == iota((nc, nc * GLA_DV), 1) // GLA_DV).astype(BF16)
    dec_all = jnp.exp(_dot_split(lambda m, x: _dot_tn(x, m), spread, b_ends))
    head_feat = iota((GLA_K, GLA_K), 0) // c == iota((GLA_K, GLA_K), 1) // GLA_DK
    head_blk = iota((GLA_K, GLA_V), 0) // GLA_DK == iota((GLA_K, GLA_V), 1) // GLA_DV
    causal = iota((c, GLA_K), 0) >= iota((c, GLA_K), 1) % c
    for j in range(nc):
        r = slice(j * c, (j + 1) * c)
        b_c, q_c, k_c, v_c = b_wide[:, j * GLA_K:(j + 1) * GLA_K], qs[r], k[r], v[r]
        b_mid = b_c[c // 2:c // 2 + 1, :]
        b_last = b_c[c - 1:c, :]
        q_i = (q_c * jnp.exp(b_c - b_mid)).astype(BF16)
        k_i = k_c * jnp.exp(b_mid - b_c)
        k_dec = (k_c * jnp.exp(b_last - b_c)).astype(BF16)
        q_b = (q_c * jnp.exp(b_c)).astype(BF16)
        k_rows = jnp.where(head_feat, jnp.concatenate([k_i] * GLA_HEADS, axis=0), 0.0).astype(BF16)
        a = jnp.where(causal, _dot_nt(q_i, k_rows), 0.0).astype(BF16)
        v_blk = jnp.where(head_blk, jnp.concatenate([v_c] * GLA_HEADS, axis=0), 0.0).astype(BF16)
        s_prev = s_scr[...]
        o_scr[r, :] = _dot(jnp.concatenate([a, q_b], axis=1),
                           jnp.concatenate([v_blk, s_prev.astype(BF16)], axis=0))
        dec = dec_all[:, j * GLA_DV:(j + 1) * GLA_DV]
        s_scr[...] = (jnp.concatenate([dec] * GLA_HEADS, axis=1) * s_prev
                      + jnp.where(head_blk, _dot_tn(k_dec, v_c.astype(BF16)), 0.0))

    yg = _head_norm_gate(o_scr[...], proj(C_G, C_LR), gg_ref[...])
    y = _dot(yc.astype(BF16), wout_ref[0:CONV_W, :]) + _dot(yg.astype(BF16), wout_ref[CONV_W:, :])
    x1_ref[0] = x + y

    @pl.when(t == pl.num_programs(1) - 1)
    def _():
        conv_ref[0] = u[tt - 2:tt, :]
        for h in range(GLA_HEADS):
            s_ref[0, h] = s_scr[h * GLA_DK:(h + 1) * GLA_DK, h * GLA_DV:(h + 1) * GLA_DV]


def _mix_prompt(x, shape, g, win, wlr, gup, gb, cw, gg, wout, tt):
    bsz, t, d = shape
    tok_in = x.ndim == 2
    nt = t // tt
    x_spec = (pl.BlockSpec((tt * TOK_ROWS, LANES), lambda b, i: (b * nt + i, 0)) if tok_in
              else pl.BlockSpec((1, tt, d), lambda b, i: (b, i, 0)))
    return pl.pallas_call(
        functools.partial(_mix_prompt_kernel, tok_in=tok_in),
        grid=(bsz, nt),
        in_specs=[
            x_spec,
            _full(g.shape), _full(win.shape), _full(wlr.shape), _full(gup.shape), _full(gb.shape),
            _full(cw.shape), _full(gg.shape), _full(wout.shape),
        ],
        out_specs=[
            pl.BlockSpec((1, tt, d), lambda b, i: (b, i, 0)),
            pl.BlockSpec((1, 2, CONV_W), lambda b, i: (b, 0, 0)),
            pl.BlockSpec((1, GLA_HEADS, GLA_DK, GLA_DV), lambda b, i: (b, 0, 0, 0)),
        ],
        out_shape=[
            jax.ShapeDtypeStruct((bsz, t, d), F32),
            jax.ShapeDtypeStruct((bsz, 2, CONV_W), F32),
            jax.ShapeDtypeStruct((bsz, GLA_HEADS, GLA_DK, GLA_DV), F32),
        ],
        scratch_shapes=[
            pltpu.VMEM((8 + tt, CONV_W), F32),
            pltpu.VMEM((GLA_K, GLA_V), F32),
            pltpu.VMEM((tt, GLA_V), F32),
        ],
        compiler_params=_params("arbitrary", "arbitrary"),
        name="mix_prompt",
    )(x, g, win, wlr, gup, gb, cw, gg, wout)


def _kv_kernel(m_ref, wk_ref, wv_ref, k_ref, v_ref):
    mb = m_ref[...].astype(BF16)
    k_ref[...] = _dot(mb, wk_ref[...])
    v_ref[...] = _dot(mb, wv_ref[...])


def _mem_kv(mem, wk, wv, tm):
    n, d = mem.shape
    return pl.pallas_call(
        _kv_kernel,
        grid=(n // tm,),
        in_specs=[pl.BlockSpec((tm, d), lambda i: (i, 0)), _full(wk.shape), _full(wv.shape)],
        out_specs=[pl.BlockSpec((tm, d), lambda i: (i, 0))] * 2,
        out_shape=[jax.ShapeDtypeStruct((n, d), F32)] * 2,
        compiler_params=_params("arbitrary"),
        name="mem_kv",
    )(mem, wk, wv)


def _att_prompt_kernel(x_ref, g_ref, wq_ref, wo_ref, mk_ref, mv_ref, out_ref):
    x = x_ref[0]
    xb = _rms(x, g_ref[...]).astype(BF16)
    q = _dot(xb, wq_ref[...])
    outs = []
    for h in range(MEM_HEADS):
        sl = slice(h * MEM_DH, (h + 1) * MEM_DH)
        s = _dot_nt(q[:, sl].astype(BF16), mk_ref[0, :, sl].astype(BF16)) * (MEM_DH ** -0.5)
        e = jnp.exp(s - jnp.max(s, axis=-1, keepdims=True))
        p = e / jnp.sum(e, axis=-1, keepdims=True)
        outs.append(_dot(p.astype(BF16), mv_ref[0, :, sl].astype(BF16)))
    o = jnp.concatenate(outs, axis=-1)
    out_ref[0] = x + _dot(o.astype(BF16), wo_ref[...])


def _att_prompt(x, g, wq, wo, mk, mv, tq):
    bsz, t, d = x.shape
    nm = mk.shape[1]
    return pl.pallas_call(
        _att_prompt_kernel,
        grid=(bsz, t // tq),
        in_specs=[
            pl.BlockSpec((1, tq, d), lambda b, i: (b, i, 0)),
            _full(g.shape), _full(wq.shape), _full(wo.shape),
            pl.BlockSpec((1, nm, d), lambda b, i: (b, 0, 0)),
            pl.BlockSpec((1, nm, d), lambda b, i: (b, 0, 0)),
        ],
        out_specs=pl.BlockSpec((1, tq, d), lambda b, i: (b, i, 0)),
        out_shape=jax.ShapeDtypeStruct((bsz, t, d), F32),
        compiler_params=_params("arbitrary", "arbitrary"),
        name="att_prompt",
    )(x, g, wq, wo, mk, mv)


def _route_top2(logits):
    lane = lax.broadcasted_iota(jnp.int32, logits.shape, 1)
    lanef = lane.astype(F32)
    ninf = -jnp.inf
    big = 1e9
    gl = jnp.where(lane < N_GROUPS, logits, ninf)
    gmax = jnp.max(gl, axis=-1, keepdims=True)
    g_idx = jnp.min(jnp.where(gl == gmax, lanef, big), axis=-1, keepdims=True)
    g_w = 1.0 / jnp.sum(jnp.exp(gl - gmax), axis=-1, keepdims=True)
    grp = ((lane - N_GROUPS) >> 2).astype(F32)
    emask = (lane >= N_GROUPS) & (lane < N_GROUPS + N_EXPERTS) & (grp == g_idx)
    el = jnp.where(emask, logits, ninf)
    m1 = jnp.max(el, axis=-1, keepdims=True)
    i1 = jnp.min(jnp.where(el == m1, lanef, big), axis=-1, keepdims=True)
    el2 = jnp.where(lanef == i1, ninf, el)
    m2 = jnp.max(el2, axis=-1, keepdims=True)
    i2 = jnp.min(jnp.where(el2 == m2, lanef, big), axis=-1, keepdims=True)
    tail = jnp.exp(m2 - m1)
    w1 = g_w / (1.0 + tail)
    w2 = g_w * tail / (1.0 + tail)
    return g_idx, i1, i2, w1, w2


def _route(logits):
    _, i1, i2, w1, w2 = _route_top2(logits)
    lanef = lax.broadcasted_iota(jnp.int32, logits.shape, 1).astype(F32)
    return jnp.where(lanef == i1, w1, 0.0) + jnp.where(lanef == i2, w2, 0.0)


N_PAIRS = 6
N_CLASSES = N_GROUPS * N_PAIRS
TOK_ROWS = 8


def _load_tok(ref, n):
    return jnp.concatenate([ref[pl.ds(j, n, stride=TOK_ROWS), :] for j in range(TOK_ROWS)], axis=-1)


def _store_tok(ref, val):
    n = val.shape[0]
    for j in range(TOK_ROWS):
        ref[pl.ds(j, n, stride=TOK_ROWS), :] = val[:, j * LANES:(j + 1) * LANES]


def _moe_route_kernel(x_ref, g_ref, rw_ref, rb_ref, meta_ref, cnt_ref, carry):
    i = pl.program_id(0)
    tr = x_ref.shape[0]

    @pl.when(i == 0)
    def _():
        carry[...] = jnp.zeros(carry.shape, F32)

    xn = _rms(x_ref[...], g_ref[...])
    g_idx, i1, i2, _, _ = _route_top2(_dot(xn, rw_ref[...], HIGHEST) + rb_ref[...])
    lo = jnp.minimum(i1, i2) - N_GROUPS - EXP_PER_GROUP * g_idx
    hi = jnp.maximum(i1, i2) - N_GROUPS - EXP_PER_GROUP * g_idx
    cls = g_idx * N_PAIRS + lo * (7.0 - lo) * 0.5 + hi - lo - 1.0

    lane = lax.broadcasted_iota(jnp.int32, (tr, LANES), 1)
    onehot = (lane.astype(F32) == cls).astype(BF16)
    ltri = (lax.broadcasted_iota(jnp.int32, (tr, tr), 0)
            >= lax.broadcasted_iota(jnp.int32, (tr, tr), 1)).astype(BF16)
    prefix = _dot(ltri, onehot) + carry[...]
    rank = jnp.sum(jnp.where(onehot > 0, prefix, 0.0), axis=-1, keepdims=True) - 1.0
    carry[...] = prefix[tr - 1:tr, :]
    meta_ref[...] = (jnp.where(lane == 0, cls, 0.0) + jnp.where(lane == 1, rank, 0.0)).astype(jnp.int32)
    cnt_ref[...] = prefix[tr - 1:tr, :].astype(jnp.int32)


def _moe_route(x, g, rw, rb, tr):
    n, d = x.shape
    assert d == TOK_ROWS * LANES
    return pl.pallas_call(
        _moe_route_kernel,
        grid=(n // tr,),
        in_specs=[pl.BlockSpec((tr, d), lambda i: (i, 0)), _full(g.shape), _full(rw.shape), _full(rb.shape)],
        out_specs=[
            pl.BlockSpec((tr, LANES), lambda i: (i, 0)),
            _full((1, LANES)),
        ],
        out_shape=[
            jax.ShapeDtypeStruct((n, LANES), jnp.int32),
            jax.ShapeDtypeStruct((1, LANES), jnp.int32),
        ],
        scratch_shapes=[pltpu.VMEM((1, LANES), F32)],
        compiler_params=_params("arbitrary"),
        name="moe_route",
    )(x, g, rw, rb)


def _tok_rows(t):
    return pl.ds(pl.multiple_of(t * TOK_ROWS, TOK_ROWS), TOK_ROWS)


def _tok_scatter_kernel(pos, x_ref, dst_in_ref, dst_ref, stage, sems, *, toks, steps):
    del dst_in_ref
    i = pl.program_id(0)
    slot = i % 2
    base = i * toks

    def wait_slot(s):
        pltpu.make_async_copy(stage.at[s], dst_ref.at[pl.ds(0, toks * TOK_ROWS)], sems.at[s]).wait()

    @pl.when(i >= 2)
    def _():
        wait_slot(slot)

    _store_tok(stage.at[slot], x_ref[...])

    def start(r, c):
        pltpu.make_async_copy(stage.at[slot, _tok_rows(r)], dst_ref.at[_tok_rows(pos[base + r])], sems.at[slot]).start()
        return c

    lax.fori_loop(0, toks, start, 0, unroll=8)

    @pl.when(i == steps - 1)
    def _():
        wait_slot(slot)
        if steps >= 2:
            wait_slot(1 - slot)


def _tok_scatter(x, dst_init, pos, toks):
    n, d = x.shape
    steps = n // toks
    return pl.pallas_call(
        functools.partial(_tok_scatter_kernel, toks=toks, steps=steps),
        grid_spec=pltpu.PrefetchScalarGridSpec(
            num_scalar_prefetch=1,
            grid=(steps,),
            in_specs=[pl.BlockSpec((toks, d), lambda i, p: (i, 0)), pl.BlockSpec(memory_space=pl.ANY)],
            out_specs=pl.BlockSpec(memory_space=pl.ANY),
            scratch_shapes=[pltpu.VMEM((2, toks * TOK_ROWS, LANES), F32), pltpu.SemaphoreType.DMA((2,))],
        ),
        out_shape=jax.ShapeDtypeStruct(dst_init.shape, dst_init.dtype),
        input_output_aliases={2: 0},
        compiler_params=_params("arbitrary"),
        name="tok_scatter",
    )(pos, x, dst_init)


def _tok_gather_kernel(pos, src_ref, out_ref, sem, *, toks):
    base = pl.program_id(0) * toks

    def start(r, c):
        pltpu.make_async_copy(src_ref.at[_tok_rows(pos[base + r])], out_ref.at[_tok_rows(r)], sem).start()
        return c

    lax.fori_loop(0, toks, start, 0, unroll=8)
    pltpu.make_async_copy(src_ref.at[pl.ds(0, toks * TOK_ROWS)], out_ref, sem).wait()


def _tok_gather(src, pos, toks):
    n = pos.shape[0]
    return pl.pallas_call(
        functools.partial(_tok_gather_kernel, toks=toks),
        grid_spec=pltpu.PrefetchScalarGridSpec(
            num_scalar_prefetch=1,
            grid=(n // toks,),
            in_specs=[pl.BlockSpec(memory_space=pl.ANY)],
            out_specs=pl.BlockSpec((toks * TOK_ROWS, LANES), lambda i, p: (i, 0)),
            scratch_shapes=[pltpu.SemaphoreType.DMA(())],
        ),
        out_shape=jax.ShapeDtypeStruct((n * TOK_ROWS, LANES), F32),
        compiler_params=_params("arbitrary"),
        name="tok_gather",
    )(pos, src)


def _moe_sorted_kernel(e_lo, e_hi, valid, xs_ref, g_ref, rw_ref, rb_ref, wgu_lo, wgu_hi, wd_lo, wd_hi, out_ref):
    t = pl.program_id(0)
    tm = xs_ref.shape[0] // TOK_ROWS
    ff = wd_lo.shape[1]

    @pl.when(valid[t] > 0)
    def _():
        x = _load_tok(xs_ref, tm)
        xb = _rms(x, g_ref[...]).astype(BF16)
        logits = _dot(xb, rw_ref[...]) + rb_ref[...]
        lane = lax.broadcasted_iota(jnp.int32, logits.shape, 1)
        gl = jnp.where(lane < N_GROUPS, logits, -jnp.inf)
        g_w = 1.0 / jnp.sum(jnp.exp(gl - jnp.max(gl, axis=-1, keepdims=True)), axis=-1, keepdims=True)
        l_lo = jnp.sum(jnp.where(lane == e_lo[t] + N_GROUPS, logits, 0.0), axis=-1, keepdims=True)
        l_hi = jnp.sum(jnp.where(lane == e_hi[t] + N_GROUPS, logits, 0.0), axis=-1, keepdims=True)
        tail = jnp.exp(-jnp.abs(l_lo - l_hi))
        w_top = g_w / (1.0 + tail)
        w_oth = g_w * tail / (1.0 + tail)
        lo_top = l_lo >= l_hi
        y = x
        for wgu, wd, w in ((wgu_lo, wd_lo, jnp.where(lo_top, w_top, w_oth)),
                           (wgu_hi, wd_hi, jnp.where(lo_top, w_oth, w_top))):
            gu = _dot(xb, wgu[0])
            hid = _silu(gu[:, 0:ff]) * gu[:, ff:] * w
            y = y + _dot(hid.astype(BF16), wd[0])
        _store_tok(out_ref, y)

    @pl.when(valid[t] == 0)
    def _():
        out_ref[...] = jnp.zeros(out_ref.shape, F32)


def _moe_sorted(xs, g, rw, rb, wgu, wd, e_lo, e_hi, valid, tm):
    npad = xs.shape[0] // TOK_ROWS
    _, d, ff2 = wgu.shape
    const = lambda t, lo, hi, v: (0, 0)
    return pl.pallas_call(
        _moe_sorted_kernel,
        grid_spec=pltpu.PrefetchScalarGridSpec(
            num_scalar_prefetch=3,
            grid=(npad // tm,),
            in_specs=[
                pl.BlockSpec((tm * TOK_ROWS, LANES), lambda t, lo, hi, v: (t, 0)),
                pl.BlockSpec(g.shape, const), pl.BlockSpec(rw.shape, const), pl.BlockSpec(rb.shape, const),
                pl.BlockSpec((1, d, ff2), lambda t, lo, hi, v: (lo[t], 0, 0)),
                pl.BlockSpec((1, d, ff2), lambda t, lo, hi, v: (hi[t], 0, 0)),
                pl.BlockSpec((1, ff2 // 2, d), lambda t, lo, hi, v: (lo[t], 0, 0)),
                pl.BlockSpec((1, ff2 // 2, d), lambda t, lo, hi, v: (hi[t], 0, 0)),
            ],
            out_specs=pl.BlockSpec((tm * TOK_ROWS, LANES), lambda t, lo, hi, v: (t, 0)),
        ),
        out_shape=jax.ShapeDtypeStruct(xs.shape, F32),
        compiler_params=_params("arbitrary"),
        name="moe_sorted",
    )(e_lo, e_hi, valid, xs, g, rw, rb, wgu, wgu, wd, wd)


def _moe_sparse(x, g, rw, rb, wgu, wd, tm, tr, toks):
    n, d = x.shape
    meta, counts = _moe_route(x, g, rw, rb, tr)
    cls, rank = meta[:, 0], meta[:, 1]
    cnt = counts[0, :N_CLASSES]
    padded = ((cnt + tm - 1) // tm) * tm
    ends = jnp.cumsum(padded)
    pos = (ends - padded)[cls] + rank
    n_tiles = n // tm + N_CLASSES
    tile_start = jnp.arange(n_tiles, dtype=jnp.int32) * tm
    tile_cls = jnp.minimum(jnp.searchsorted(ends, tile_start, side="right"), N_CLASSES - 1).astype(jnp.int32)
    valid = (tile_start < ends[-1]).astype(jnp.int32)
    tile_cls = jnp.where(valid > 0, tile_cls, tile_cls[jnp.maximum(ends[-1] // tm - 1, 0)])
    pair_lo = jnp.array([0, 0, 0, 1, 1, 2], jnp.int32)
    pair_hi = jnp.array([1, 2, 3, 2, 3, 3], jnp.int32)
    e_lo = (tile_cls // N_PAIRS) * EXP_PER_GROUP + pair_lo[tile_cls % N_PAIRS]
    e_hi = (tile_cls // N_PAIRS) * EXP_PER_GROUP + pair_hi[tile_cls % N_PAIRS]

    xs = _tok_scatter(x, jnp.zeros((n_tiles * tm * TOK_ROWS, LANES), F32), pos, toks)
    ys = _moe_sorted(xs, g, rw.astype(BF16), rb, wgu, wd, e_lo, e_hi, valid, tm)
    return _tok_gather(ys, pos, toks)


def _moe_kernel(x_ref, g_ref, rw_ref, rb_ref, wg_ref, wu_ref, wd_ref, out_ref, xn_scr, comb_scr, acc_scr):
    e = pl.program_id(1)

    @pl.when(e == 0)
    def _():
        x = x_ref[...]
        xn = _rms(x, g_ref[...])
        xn_scr[...] = xn.astype(BF16)
        comb_scr[...] = _route(_dot(xn, rw_ref[...], HIGHEST) + rb_ref[...])
        acc_scr[...] = x

    xb = xn_scr[...]
    lane = lax.broadcasted_iota(jnp.int32, comb_scr.shape, 1)
    cmb = jnp.sum(jnp.where(lane == e + N_GROUPS, comb_scr[...], 0.0), axis=-1, keepdims=True)
    hid = _silu(_dot(xb, wg_ref[0])) * _dot(xb, wu_ref[0]) * cmb
    acc_scr[...] += _dot(hid.astype(BF16), wd_ref[0])

    @pl.when(e == pl.num_programs(1) - 1)
    def _():
        out_ref[...] = acc_scr[...]


def _moe(x, g, rw, rb, wg, wu, wd, tm):
    n, d = x.shape
    ne, _, ff = wg.shape
    return pl.pallas_call(
        _moe_kernel,
        grid=(n // tm, ne),
        in_specs=[
            pl.BlockSpec((tm, d), lambda i, e: (i, 0)),
            _full(g.shape), _full(rw.shape), _full(rb.shape),
            pl.BlockSpec((1, d, ff), lambda i, e: (e, 0, 0)),
            pl.BlockSpec((1, d, ff), lambda i, e: (e, 0, 0)),
            pl.BlockSpec((1, ff, d), lambda i, e: (e, 0, 0)),
        ],
        out_specs=pl.BlockSpec((tm, d), lambda i, e: (i, 0)),
        out_shape=jax.ShapeDtypeStruct((n, d), F32),
        scratch_shapes=[pltpu.VMEM((tm, d), BF16), pltpu.VMEM((tm, LANES), F32), pltpu.VMEM((tm, d), F32)],
        compiler_params=_params("arbitrary", "arbitrary"),
        name="moe",
    )(x, g, rw, rb, wg, wu, wd)


def _norm_kernel(x_ref, g_ref, o_ref, *, tok_in):
    x = _load_tok(x_ref, o_ref.shape[0]) if tok_in else x_ref[...]
    o_ref[...] = _rms(x, g_ref[...])


def _final_norm(x, g, tm, tok_in=False):
    d = g.shape[1]
    n = x.shape[0] // TOK_ROWS if tok_in else x.shape[0]
    x_spec = (pl.BlockSpec((tm * TOK_ROWS, LANES), lambda i: (i, 0)) if tok_in
              else pl.BlockSpec((tm, d), lambda i: (i, 0)))
    return pl.pallas_call(
        functools.partial(_norm_kernel, tok_in=tok_in),
        grid=(n // tm,),
        in_specs=[x_spec, _full(g.shape)],
        out_specs=pl.BlockSpec((tm, d), lambda i: (i, 0)),
        out_shape=jax.ShapeDtypeStruct((n, d), F32),
        compiler_params=_params("arbitrary"),
        name="final_norm",
    )(x, g)


def _mix_sample_in_kernel(x_ref, g_ref, win_ref, wlr_ref, gup_ref, gb_ref, cw_ref, b0_ref, b1_ref,
                          yc_ref, u_ref, q_ref, k_ref, v_ref, gate_ref, la_ref):
    hb = _rms(x_ref[...], g_ref[...]).astype(BF16)

    def proj(a, b):
        return _dot(hb, win_ref[:, a:b])

    u = proj(C_CC, C_CH) * proj(C_CH, C_Q)
    cw = cw_ref[...]
    yc_ref[...] = proj(C_CB, C_CC) * (b0_ref[...] * cw[0:1] + b1_ref[...] * cw[1:2] + u * cw[2:3])
    u_ref[...] = u
    q_ref[...] = proj(C_Q, C_K) * (GLA_DK ** -0.5)
    k_ref[...] = proj(C_K, C_V)
    v_ref[...] = proj(C_V, C_G)
    gate_ref[...] = proj(C_G, C_LR)
    lr = _dot(hb, wlr_ref[...])
    gate = _dot(lr.astype(BF16), gup_ref[...]) + gb_ref[...]
    la_ref[...] = _log_sigmoid(gate) * (1.0 / GLA_TAU)


def _mix_sample_in(x, g, win, wlr, gup, gb, cw, b0, b1):
    n = x.shape[0]
    args = (x, g, win, wlr, gup, gb, cw, b0, b1)
    widths = (CONV_W, CONV_W, GLA_K, GLA_K, GLA_V, GLA_V, GLA_K)
    return pl.pallas_call(
        _mix_sample_in_kernel,
        grid=(1,),
        in_specs=[_full(a.shape) for a in args],
        out_specs=[_full((n, w)) for w in widths],
        out_shape=[jax.ShapeDtypeStruct((n, w), F32) for w in widths],
        compiler_params=_params("arbitrary"),
        name="mix_sample_in",
    )(*args)


def _gla_step_kernel(q_ref, k_ref, v_ref, la_ref, s0_ref, s_ref, o_ref):
    nb = q_ref.shape[0]
    a = jnp.exp(la_ref[...])
    kb = k_ref[...].astype(BF16)
    qb = q_ref[...].astype(BF16)
    vf = v_ref[...].astype(BF16).astype(F32)
    rows = lax.broadcasted_iota(jnp.int32, (nb, 1), 0)
    spread = (lax.broadcasted_iota(jnp.int32, (nb, nb * GLA_DV), 0)
              == lax.broadcasted_iota(jnp.int32, (nb, nb * GLA_DV), 1) // GLA_DV)
    spread_b = spread.astype(BF16)
    for h in range(GLA_HEADS):
        ks = slice(h * GLA_DK, (h + 1) * GLA_DK)
        a_cols = _dot_tn(a[:, ks], spread.astype(F32), HIGHEST)
        k_cols = _dot_tn(kb[:, ks], spread_b)
        q_cols = _dot_tn(qb[:, ks], spread_b)
        o_h = jnp.zeros((nb, GLA_DV), F32)
        for n in range(nb):
            blk = slice(n * GLA_DV, (n + 1) * GLA_DV)
            v_row = vf[n:n + 1, h * GLA_DV:(h + 1) * GLA_DV]
            s_new = a_cols[:, blk] * s0_ref[0, n, h] + k_cols[:, blk] * v_row
            s_ref[n, h] = s_new
            o_h = jnp.where(rows == n, jnp.sum(q_cols[:, blk] * s_new, axis=0, keepdims=True), o_h)
        o_ref[:, h * GLA_DV:(h + 1) * GLA_DV] = o_h


def _gla_step(q, k, v, la, state, layer, nb):
    n = q.shape[0]
    sshape = (GLA_HEADS, GLA_DK, GLA_DV)
    return pl.pallas_call(
        _gla_step_kernel,
        grid=(n // nb,),
        in_specs=[
            pl.BlockSpec((nb, GLA_K), lambda i: (i, 0)),
            pl.BlockSpec((nb, GLA_K), lambda i: (i, 0)),
            pl.BlockSpec((nb, GLA_V), lambda i: (i, 0)),
            pl.BlockSpec((nb, GLA_K), lambda i: (i, 0)),
            pl.BlockSpec((1, nb) + sshape, lambda i: (layer, i, 0, 0, 0)),
        ],
        out_specs=[
            pl.BlockSpec((nb,) + sshape, lambda i: (i, 0, 0, 0)),
            pl.BlockSpec((nb, GLA_V), lambda i: (i, 0)),
        ],
        out_shape=[jax.ShapeDtypeStruct((n,) + sshape, F32), jax.ShapeDtypeStruct((n, GLA_V), F32)],
        compiler_params=_params("arbitrary"),
        name="gla_step",
    )(q, k, v, la, state)


def _mix_sample_out_kernel(x_ref, yc_ref, o_ref, gate_ref, gg_ref, wout_ref, gm_ref, wq_ref, x1_ref, q_ref):
    yg = _head_norm_gate(o_ref[...], gate_ref[...], gg_ref[...])
    y = _dot(yc_ref[...].astype(BF16), wout_ref[0:CONV_W, :]) + _dot(yg.astype(BF16), wout_ref[CONV_W:, :])
    x1 = x_ref[...] + y
    x1_ref[...] = x1
    q_ref[...] = _dot(_rms(x1, gm_ref[...]).astype(BF16), wq_ref[...])


def _mix_sample_out(x, yc, o, gate, gg, wout, gm, wq):
    args = (x, yc, o, gate, gg, wout, gm, wq)
    return pl.pallas_call(
        _mix_sample_out_kernel,
        grid=(1,),
        in_specs=[_full(a.shape) for a in args],
        out_specs=[_full(x.shape)] * 2,
        out_shape=[jax.ShapeDtypeStruct(x.shape, F32)] * 2,
        compiler_params=_params("arbitrary"),
        name="mix_sample_out",
    )(*args)


ATT_ROWS = 2 * MEM_HEADS


def _class_allreduce(x, op):
    n = x.shape[-1]
    shift = ATT_ROWS
    while shift < n:
        x = op(x, pltpu.roll(x, shift, axis=1))
        shift *= 2
    return x


def _att_sample_kernel(q_ref, k_ref, v_ref, o_ref):
    nb = q_ref.shape[0]
    ncol = k_ref.shape[2]
    diag = (lax.broadcasted_iota(jnp.int32, (ATT_ROWS, ncol), 0)
            == (lax.broadcasted_iota(jnp.int32, (ATT_ROWS, ncol), 1) & (ATT_ROWS - 1)))
    rows = lax.broadcasted_iota(jnp.int32, (nb, 1), 0)
    t = jnp.zeros((nb, ncol), F32)
    for n in range(nb):
        sc = _dot_nt(q_ref[n].astype(BF16), k_ref[0, n].astype(BF16))
        t = t + jnp.where(rows == n, jnp.sum(jnp.where(diag, sc, 0.0), axis=0, keepdims=True), 0.0)
    valid = (lax.broadcasted_iota(jnp.int32, (nb, ncol), 1) & (ATT_ROWS - 1)) < MEM_HEADS
    s = jnp.where(valid, (t + pltpu.roll(t, ncol - MEM_HEADS, axis=1)) * (MEM_DH ** -0.5), 0.0)
    e = jnp.where(valid, jnp.exp(s - _class_allreduce(s, jnp.maximum)), 0.0)
    den = jnp.where(valid, _class_allreduce(e, jnp.add), 1.0)
    p = e / den
    p = p + pltpu.roll(p, MEM_HEADS, axis=1)
    for n in range(nb):
        p_n = jnp.where(diag, jnp.broadcast_to(p[n:n + 1, :], (ATT_ROWS, ncol)), 0.0)
        o_ref[n] = _dot(p_n.astype(BF16), v_ref[0, n].astype(BF16))


def _att_sample(q, ck, cv, layer, nb):
    n = q.shape[0]
    ncol = ck.shape[2]
    return pl.pallas_call(
        _att_sample_kernel,
        grid=(n // nb,),
        in_specs=[
            pl.BlockSpec((nb, ATT_ROWS, LANES), lambda i: (i, 0, 0)),
            pl.BlockSpec((1, nb, ncol, LANES), lambda i: (layer, i, 0, 0)),
            pl.BlockSpec((1, nb, ncol, LANES), lambda i: (layer, i, 0, 0)),
        ],
        out_specs=pl.BlockSpec((nb, ATT_ROWS, LANES), lambda i: (i, 0, 0)),
        out_shape=jax.ShapeDtypeStruct((n, ATT_ROWS, LANES), F32),
        compiler_params=_params("arbitrary"),
        name="att_sample",
    )(q, ck, cv)


def _oproj_kernel(x_ref, o_ref, wo_ref, out_ref):
    out_ref[...] = x_ref[...] + _dot(o_ref[...].astype(BF16), wo_ref[...])


def _oproj(x, o, wo):
    return pl.pallas_call(
        _oproj_kernel,
        grid=(1,),
        in_specs=[_full(x.shape), _full(o.shape), _full(wo.shape)],
        out_specs=_full(x.shape),
        out_shape=jax.ShapeDtypeStruct(x.shape, F32),
        compiler_params=_params("arbitrary"),
        name="oproj",
    )(x, o, wo)


def kernel(x_prompt, x_sample, state_conv, state_gla, cache_mem_k, cache_mem_v, mem_prompt, norm_mix, w_in, conv_w, gla_gate_up, gla_gate_b, gla_out_norm, w_out, norm_mem, w_q, w_k, w_v, w_o, norm_ffn, router_group, router_group_b, router_expert, router_expert_b, w_gate, w_up, w_down, norm_final):
    depth = w_in.shape[0]
    bsz, t, d = x_prompt.shape
    ns = x_sample.shape[0]
    nm = mem_prompt.shape[1]
    n_tok = bsz * t

    tt = min(512, t)
    tq = min(512, t)
    tm_moe = min(1024, n_tok)
    tm_kv = min(512, bsz * nm)
    tm_sorted = 256
    tr_route = min(512, n_tok)
    rows_perm = min(512, n_tok)
    nb_gla = min(8, ns)
    nb_att = min(8, ns)

    row = lambda a: a.reshape(1, -1)
    mem2 = mem_prompt.reshape(bsz * nm, d)

    def tile_rows(c):
        c = c.reshape(depth, ns, nm, MEM_HEADS, 2, LANES).transpose(0, 1, 2, 4, 3, 5)
        return c.reshape(depth, ns, nm * ATT_ROWS, LANES)

    ck, cv = tile_rows(cache_mem_k), tile_rows(cache_mem_v)

    xp = x_prompt
    xs = x_sample.reshape(ns, d)
    conv_p, gla_p, mk_p, mv_p, conv_s, gla_s = [], [], [], [], [], []
    for l in range(depth):
        win = w_in[l, :, :C_LR].astype(BF16)
        wlr = jnp.pad(w_in[l, :, C_LR:], ((0, 0), (0, LANES - GLA_RANK))).astype(BF16)
        gup = jnp.pad(gla_gate_up[l], ((0, LANES - GLA_RANK), (0, 0))).astype(BF16)
        gb = row(gla_gate_b[l])
        gg = row(gla_out_norm[l])
        wout = w_out[l].astype(BF16)
        wq, wk, wv, wo = (w[l].astype(BF16) for w in (w_q, w_k, w_v, w_o))
        rw = jnp.concatenate([router_group[l], router_expert[l].transpose(1, 0, 2).reshape(d, N_EXPERTS)], axis=1)
        rw = jnp.pad(rw, ((0, 0), (0, LANES - rw.shape[1])))
        rb = jnp.concatenate([router_group_b[l], router_expert_b[l].reshape(-1)])
        rb = row(jnp.pad(rb, (0, LANES - rb.shape[0])))
        wg, wu, wd = w_gate[l].astype(BF16), w_up[l].astype(BF16), w_down[l].astype(BF16)
        moe_w = (row(norm_ffn[l]), rw, rb, wg, wu, wd)

        mk, mv = _mem_kv(mem2, wk, wv, tm_kv)
        mk_p.append(mk.reshape(bsz, nm, MEM_HEADS, MEM_DH))
        mv_p.append(mv.reshape(bsz, nm, MEM_HEADS, MEM_DH))
        xp, nbuf, ns_p = _mix_prompt(xp, (bsz, t, d), row(norm_mix[l]), win, wlr, gup, gb, conv_w[l], gg, wout, tt)
        conv_p.append(nbuf)
        gla_p.append(ns_p)
        xp = _att_prompt(xp, row(norm_mem[l]), wq, wo, mk.reshape(bsz, nm, d), mv.reshape(bsz, nm, d), tq)
        wgu = jnp.concatenate([wg, wu], axis=-1)
        xp = _moe_sparse(xp.reshape(n_tok, d), row(norm_ffn[l]), rw, rb, wgu, wd, tm_sorted, tr_route, rows_perm)

        yc, u, q, k, v, gate, la = _mix_sample_in(
            xs, row(norm_mix[l]), win, wlr, gup, gb, conv_w[l], state_conv[l, :, 0], state_conv[l, :, 1])
        conv_s.append(jnp.stack([state_conv[l, :, 1], u], axis=1))
        s_new, o = _gla_step(q, k, v, la, state_gla, l, nb_gla)
        gla_s.append(s_new)
        wq_s = wq.reshape(d, MEM_HEADS, 2, LANES).transpose(0, 2, 1, 3).reshape(d, d)
        wo_s = wo.reshape(MEM_HEADS, 2, LANES, d).transpose(1, 0, 2, 3).reshape(d, d)
        xs, qa = _mix_sample_out(xs, yc, o, gate, gg, wout, row(norm_mem[l]), wq_s)
        oa = _att_sample(qa.reshape(ns, ATT_ROWS, LANES), ck, cv, l, nb_att)
        xs = _oproj(xs, oa.reshape(ns, d), wo_s)
        xs = _moe(xs, *moe_w, min(tm_moe, ns))

    y_prompt = _final_norm(xp, row(norm_final), tm_moe, tok_in=True).reshape(bsz, t, d)
    y_sample = _final_norm(xs, row(norm_final), ns).reshape(ns, 1, d)
    return (y_prompt, y_sample, jnp.stack(conv_p), jnp.stack(gla_p), jnp.stack(mk_p), jnp.stack(mv_p),
            jnp.stack(conv_s), jnp.stack(gla_s))
```

```python
import functools

import jax
import jax.numpy as jnp
from jax import lax
from jax.experimental import pallas as pl
from jax.experimental.pallas import tpu as pltpu

F32 = jnp.float32
BF16 = jnp.bfloat16
HIGHEST = lax.Precision.HIGHEST

EPS = 1e-6
CONV_W = 512
GLA_HEADS = 4
GLA_DK = 64
GLA_DV = 128
GLA_K = GLA_HEADS * GLA_DK
GLA_V = GLA_HEADS * GLA_DV
GLA_RANK = 16
GLA_TAU = 16.0
GLA_CHUNK = 64
MEM_HEADS = 4
MEM_DH = 256
N_GROUPS = 4
EXP_PER_GROUP = 4
N_EXPERTS = 16
LANES = 128
C_CB, C_CC, C_CH, C_Q, C_K, C_V, C_G, C_LR = 0, 512, 1024, 1536, 1792, 2048, 2560, 3072
VMEM_LIMIT = 52 * 1024 * 1024


def _params(*sem):
    return pltpu.CompilerParams(dimension_semantics=sem, vmem_limit_bytes=VMEM_LIMIT)


def _rms(x, g):
    return x * lax.rsqrt(jnp.mean(x * x, axis=-1, keepdims=True) + EPS) * g


def _dot(a, b, precision=None):
    return jnp.dot(a, b, precision=precision, preferred_element_type=F32)


def _dot_nt(a, b):
    return lax.dot_general(a, b, (((1,), (1,)), ((), ())), preferred_element_type=F32)


def _dot_tn(a, b, precision=None):
    return lax.dot_general(a, b, (((0,), (0,)), ((), ())), precision=precision,
                           preferred_element_type=F32)


def _dot_split(dot, mask, x):
    hi = x.astype(BF16)
    r1 = x - hi.astype(F32)
    mid = r1.astype(BF16)
    lo = (r1 - mid.astype(F32)).astype(BF16)
    return dot(mask, hi) + dot(mask, mid) + dot(mask, lo)


def _silu(x):
    return x / (1.0 + jnp.exp(-x))


def _log_sigmoid(x):
    return jnp.minimum(x, 0.0) - jnp.log1p(jnp.exp(-jnp.abs(x)))


def _head_norm_gate(o, g, gg):
    parts = []
    for h in range(GLA_HEADS):
        sl = slice(h * GLA_DV, (h + 1) * GLA_DV)
        parts.append(_rms(o[:, sl], gg[:, sl]))
    return jnp.concatenate(parts, axis=-1) * _silu(g)


def _full(shape):
    nd = len(shape)
    return pl.BlockSpec(shape, lambda *_: (0,) * nd)


def _mix_prompt_kernel(x_ref, g_ref, win_ref, wlr_ref, gup_ref, gb_ref, cw_ref, gg_ref, wout_ref,
                       x1_ref, conv_ref, s_ref, ubuf, s_scr, o_scr, *, tok_in):
    t = pl.program_id(1)
    tt = x1_ref.shape[1]

    @pl.when(t == 0)
    def _():
        ubuf[0:8, :] = jnp.zeros((8, CONV_W), F32)
        s_scr[...] = jnp.zeros(s_scr.shape, F32)

    x = _load_tok(x_ref, tt) if tok_in else x_ref[0]
    hb = _rms(x, g_ref[...]).astype(BF16)

    def proj(a, b):
        return _dot(hb, win_ref[:, a:b])

    u = proj(C_CC, C_CH) * proj(C_CH, C_Q)
    ubuf[8:8 + tt, :] = u
    cw = cw_ref[...]
    yc = proj(C_CB, C_CC) * (ubuf[6:6 + tt, :] * cw[0:1] + ubuf[7:7 + tt, :] * cw[1:2] + u * cw[2:3])
    ubuf[6:8, :] = u[tt - 2:tt, :]

    qs = proj(C_Q, C_K) * (GLA_DK ** -0.5)
    k = proj(C_K, C_V)
    v = proj(C_V, C_G)
    lr = _dot(hb, wlr_ref[...])
    gate = _dot(lr.astype(BF16), gup_ref[...]) + gb_ref[...]
    la = _log_sigmoid(gate) * (1.0 / GLA_TAU)

    c = GLA_CHUNK
    nc = tt // c
    iota = lambda shape, dim: lax.broadcasted_iota(jnp.int32, shape, dim)
    ltri = (iota((c, c), 0) >= iota((c, c), 1)).astype(BF16)
    b_wide = _dot_split(_dot, ltri, jnp.concatenate([la[j * c:(j + 1) * c] for j in range(nc)], axis=1))
    b_ends = jnp.concatenate([b_wide[c - 1:c, j * GLA_K:(j + 1) * GLA_K] for j in range(nc)], axis=0)
    spread = (iota((nc, nc * GLA_DV), 0) == iota((nc, nc * GLA_DV), 1) // GLA_DV).astype(BF16)
    dec_all = jnp.exp(_dot_split(lambda m, x: _dot_tn(x, m), spread, b_ends))
    head_feat = iota((GLA_K, GLA_K), 0) // c == iota((GLA_K, GLA_K), 1) // GLA_DK
    head_blk = iota((GLA_K, GLA_V), 0) // GLA_DK == iota((GLA_K, GLA_V), 1) // GLA_DV
    causal = iota((c, GLA_K), 0) >= iota((c, GLA_K), 1) % c
    for j in range(nc):
        r = slice(j * c, (j + 1) * c)
        b_c, q_c, k_c, v_c = b_wide[:, j * GLA_K:(j + 1) * GLA_K], qs[r], k[r], v[r]
        b_mid = b_c[c // 2:c // 2 + 1, :]
        b_last = b_c[c - 1:c, :]
        q_i = (q_c * jnp.exp(b_c - b_mid)).astype(BF16)
        k_i = k_c * jnp.exp(b_mid - b_c)
        k_dec = (k_c * jnp.exp(b_last - b_c)).astype(BF16)
        q_b = (q_c * jnp.exp(b_c)).astype(BF16)
        k_rows = jnp.where(head_feat, jnp.concatenate([k_i] * GLA_HEADS, axis=0), 0.0).astype(BF16)
        a = jnp.where(causal, _dot_nt(q_i, k_rows), 0.0).astype(BF16)
        v_blk = jnp.where(head_blk, jnp.concatenate([v_c] * GLA_HEADS, axis=0), 0.0).astype(BF16)
        s_prev = s_scr[...]
        o_scr[r, :] = _dot(jnp.concatenate([a, q_b], axis=1),
                           jnp.concatenate([v_blk, s_prev.astype(BF16)], axis=0))
        dec = dec_all[:, j * GLA_DV:(j + 1) * GLA_DV]
        s_scr[...] = (jnp.concatenate([dec] * GLA_HEADS, axis=1) * s_prev
                      + jnp.where(head_blk, _dot_tn(k_dec, v_c.astype(BF16)), 0.0))

    yg = _head_norm_gate(o_scr[...], proj(C_G, C_LR), gg_ref[...])
    y = _dot(yc.astype(BF16), wout_ref[0:CONV_W, :]) + _dot(yg.astype(BF16), wout_ref[CONV_W:, :])
    x1_ref[0] = x + y

    @pl.when(t == pl.num_programs(1) - 1)
    def _():
        conv_ref[0] = u[tt - 2:tt, :]
        for h in range(GLA_HEADS):
            s_ref[0, h] = s_scr[h * GLA_DK:(h + 1) * GLA_DK, h * GLA_DV:(h + 1) * GLA_DV]


def _mix_prompt(x, shape, g, win, wlr, gup, gb, cw, gg, wout, tt):
    bsz, t, d = shape
    tok_in = x.ndim == 2
    nt = t // tt
    x_spec = (pl.BlockSpec((tt * TOK_ROWS, LANES), lambda b, i: (b * nt + i, 0)) if tok_in
              else pl.BlockSpec((1, tt, d), lambda b, i: (b, i, 0)))
    return pl.pallas_call(
        functools.partial(_mix_prompt_kernel, tok_in=tok_in),
        grid=(bsz, nt),
        in_specs=[
            x_spec,
            _full(g.shape), _full(win.shape), _full(wlr.shape), _full(gup.shape), _full(gb.shape),
            _full(cw.shape), _full(gg.shape), _full(wout.shape),
        ],
        out_specs=[
            pl.BlockSpec((1, tt, d), lambda b, i: (b, i, 0)),
            pl.BlockSpec((1, 2, CONV_W), lambda b, i: (b, 0, 0)),
            pl.BlockSpec((1, GLA_HEADS, GLA_DK, GLA_DV), lambda b, i: (b, 0, 0, 0)),
        ],
        out_shape=[
            jax.ShapeDtypeStruct((bsz, t, d), F32),
            jax.ShapeDtypeStruct((bsz, 2, CONV_W), F32),
            jax.ShapeDtypeStruct((bsz, GLA_HEADS, GLA_DK, GLA_DV), F32),
        ],
        scratch_shapes=[
            pltpu.VMEM((8 + tt, CONV_W), F32),
            pltpu.VMEM((GLA_K, GLA_V), F32),
            pltpu.VMEM((tt, GLA_V), F32),
        ],
        compiler_params=_params("arbitrary", "arbitrary"),
        name="mix_prompt",
    )(x, g, win, wlr, gup, gb, cw, gg, wout)


def _kv_kernel(m_ref, wk_ref, wv_ref, k_ref, v_ref):
    mb = m_ref[...].astype(BF16)
    k_ref[...] = _dot(mb, wk_ref[...])
    v_ref[...] = _dot(mb, wv_ref[...])


def _mem_kv(mem, wk, wv, tm):
    n, d = mem.shape
    return pl.pallas_call(
        _kv_kernel,
        grid=(n // tm,),
        in_specs=[pl.BlockSpec((tm, d), lambda i: (i, 0)), _full(wk.shape), _full(wv.shape)],
        out_specs=[pl.BlockSpec((tm, d), lambda i: (i, 0))] * 2,
        out_shape=[jax.ShapeDtypeStruct((n, d), F32)] * 2,
        compiler_params=_params("arbitrary"),
        name="mem_kv",
    )(mem, wk, wv)


def _att_prompt_kernel(x_ref, g_ref, wq_ref, wo_ref, mk_ref, mv_ref, out_ref):
    x = x_ref[0]
    xb = _rms(x, g_ref[...]).astype(BF16)
    q = _dot(xb, wq_ref[...])
    outs = []
    for h in range(MEM_HEADS):
        sl = slice(h * MEM_DH, (h + 1) * MEM_DH)
        s = _dot_nt(q[:, sl].astype(BF16), mk_ref[0, :, sl].astype(BF16)) * (MEM_DH ** -0.5)
        e = jnp.exp(s - jnp.max(s, axis=-1, keepdims=True))
        p = e / jnp.sum(e, axis=-1, keepdims=True)
        outs.append(_dot(p.astype(BF16), mv_ref[0, :, sl].astype(BF16)))
    o = jnp.concatenate(outs, axis=-1)
    out_ref[0] = x + _dot(o.astype(BF16), wo_ref[...])


def _att_prompt(x, g, wq, wo, mk, mv, tq):
    bsz, t, d = x.shape
    nm = mk.shape[1]
    return pl.pallas_call(
        _att_prompt_kernel,
        grid=(bsz, t // tq),
        in_specs=[
            pl.BlockSpec((1, tq, d), lambda b, i: (b, i, 0)),
            _full(g.shape), _full(wq.shape), _full(wo.shape),
            pl.BlockSpec((1, nm, d), lambda b, i: (b, 0, 0)),
            pl.BlockSpec((1, nm, d), lambda b, i: (b, 0, 0)),
        ],
        out_specs=pl.BlockSpec((1, tq, d), lambda b, i: (b, i, 0)),
        out_shape=jax.ShapeDtypeStruct((bsz, t, d), F32),
        compiler_params=_params("arbitrary", "arbitrary"),
        name="att_prompt",
    )(x, g, wq, wo, mk, mv)


def _route_top2(logits):
    lane = lax.broadcasted_iota(jnp.int32, logits.shape, 1)
    lanef = lane.astype(F32)
    ninf = -jnp.inf
    big = 1e9
    gl = jnp.where(lane < N_GROUPS, logits, ninf)
    gmax = jnp.max(gl, axis=-1, keepdims=True)
    g_idx = jnp.min(jnp.where(gl == gmax, lanef, big), axis=-1, keepdims=True)
    g_w = 1.0 / jnp.sum(jnp.exp(gl - gmax), axis=-1, keepdims=True)
    grp = ((lane - N_GROUPS) >> 2).astype(F32)
    emask = (lane >= N_GROUPS) & (lane < N_GROUPS + N_EXPERTS) & (grp == g_idx)
    el = jnp.where(emask, logits, ninf)
    m1 = jnp.max(el, axis=-1, keepdims=True)
    i1 = jnp.min(jnp.where(el == m1, lanef, big), axis=-1, keepdims=True)
    el2 = jnp.where(lanef == i1, ninf, el)
    m2 = jnp.max(el2, axis=-1, keepdims=True)
    i2 = jnp.min(jnp.where(el2 == m2, lanef, big), axis=-1, keepdims=True)
    tail = jnp.exp(m2 - m1)
    w1 = g_w / (1.0 + tail)
    w2 = g_w * tail / (1.0 + tail)
    return g_idx, i1, i2, w1, w2


def _route(logits):
    _, i1, i2, w1, w2 = _route_top2(logits)
    lanef = lax.broadcasted_iota(jnp.int32, logits.shape, 1).astype(F32)
    return jnp.where(lanef == i1, w1, 0.0) + jnp.where(lanef == i2, w2, 0.0)


N_PAIRS = 6
N_CLASSES = N_GROUPS * N_PAIRS
TOK_ROWS = 8


def _load_tok(ref, n):
    return jnp.concatenate([ref[pl.ds(j, n, stride=TOK_ROWS), :] for j in range(TOK_ROWS)], axis=-1)


def _store_tok(ref, val):
    n = val.shape[0]
    for j in range(TOK_ROWS):
        ref[pl.ds(j, n, stride=TOK_ROWS), :] = val[:, j * LANES:(j + 1) * LANES]


def _lane_dense(col):
    eye = lax.broadcasted_iota(jnp.int32, (LANES, LANES), 0) == lax.broadcasted_iota(jnp.int32, (LANES, LANES), 1)
    rows = [jnp.sum(jnp.where(eye, col[b * LANES:(b + 1) * LANES], 0.0), axis=0, keepdims=True)
            for b in range(col.shape[0] // LANES)]
    return jnp.concatenate(rows, axis=0)


def _moe_route_kernel(x_ref, g_ref, rw_ref, rb_ref, ltri_ref, cls_ref, rank_ref, cnt_ref, carry):
    i = pl.program_id(0)
    tr = x_ref.shape[0]

    @pl.when(i == 0)
    def _():
        carry[...] = jnp.zeros(carry.shape, F32)

    xn = _rms(x_ref[...], g_ref[...])
    g_idx, i1, i2, _, _ = _route_top2(_dot(xn, rw_ref[...], HIGHEST) + rb_ref[...])
    lo = jnp.minimum(i1, i2) - N_GROUPS - EXP_PER_GROUP * g_idx
    hi = jnp.maximum(i1, i2) - N_GROUPS - EXP_PER_GROUP * g_idx
    cls = g_idx * N_PAIRS + lo * (7.0 - lo) * 0.5 + hi - lo - 1.0

    lane = lax.broadcasted_iota(jnp.int32, (tr, LANES), 1)
    onehot = lane.astype(F32) == cls
    prefix = _dot(ltri_ref[...], onehot.astype(BF16)) + carry[...]
    rank = jnp.sum(jnp.where(onehot, prefix, 0.0), axis=-1, keepdims=True) - 1.0
    carry[...] = prefix[tr - 1:tr, :]
    cls_ref[...] = _lane_dense(cls).astype(jnp.int32)
    rank_ref[...] = _lane_dense(rank).astype(jnp.int32)
    cnt_ref[...] = prefix[tr - 1:tr, :].astype(jnp.int32)


def _moe_route(x, g, rw, rb, tr):
    n, d = x.shape
    ltri = jnp.tril(jnp.ones((tr, tr), BF16))
    rows = tr // LANES
    return pl.pallas_call(
        _moe_route_kernel,
        grid=(n // tr,),
        in_specs=[pl.BlockSpec((tr, d), lambda i: (i, 0)), _full(g.shape), _full(rw.shape), _full(rb.shape),
                  _full(ltri.shape)],
        out_specs=[
            pl.BlockSpec((rows, LANES), lambda i: (i, 0)),
            pl.BlockSpec((rows, LANES), lambda i: (i, 0)),
            _full((1, LANES)),
        ],
        out_shape=[
            jax.ShapeDtypeStruct((n // LANES, LANES), jnp.int32),
            jax.ShapeDtypeStruct((n // LANES, LANES), jnp.int32),
            jax.ShapeDtypeStruct((1, LANES), jnp.int32),
        ],
        scratch_shapes=[pltpu.VMEM((1, LANES), F32)],
        compiler_params=_params("arbitrary"),
        name="moe_route",
    )(x, g, rw, rb, ltri)


def _tok_rows(t):
    return pl.ds(pl.multiple_of(t * TOK_ROWS, TOK_ROWS), TOK_ROWS)


def _tok_scatter_kernel(pos, tail, x_ref, dst_ref, stage, sems, *, toks, steps):
    i = pl.program_id(0)
    slot = i % 2
    base = i * toks
    rows = toks * TOK_ROWS

    def wait_slot(s):
        pltpu.make_async_copy(stage.at[s], dst_ref.at[pl.ds(0, rows)], sems.at[s]).wait()

    @pl.when(i == 0)
    def _():
        stage[1] = jnp.zeros((rows, LANES), F32)

        def fill(c):
            return pltpu.make_async_copy(stage.at[1], dst_ref.at[pl.ds(pl.multiple_of(tail[c] * rows, rows), rows)],
                                         sems.at[1])

        for c in range(2 * N_CLASSES):
            @pl.when(tail[c] >= 0)
            def _():
                fill(c).start()
        for c in range(2 * N_CLASSES):
            @pl.when(tail[c] >= 0)
            def _():
                fill(c).wait()

    @pl.when(i >= 2)
    def _():
        wait_slot(slot)

    _store_tok(stage.at[slot], x_ref[...])

    def start(r, c):
        pltpu.make_async_copy(stage.at[slot, _tok_rows(r)], dst_ref.at[_tok_rows(pos[base + r])], sems.at[slot]).start()
        return c

    lax.fori_loop(0, toks, start, 0, unroll=8)

    @pl.when(i == steps - 1)
    def _():
        wait_slot(slot)
        if steps >= 2:
            wait_slot(1 - slot)


def _tok_scatter(x, n_out, pos, tail, toks):
    n, d = x.shape
    steps = n // toks
    return pl.pallas_call(
        functools.partial(_tok_scatter_kernel, toks=toks, steps=steps),
        grid_spec=pltpu.PrefetchScalarGridSpec(
            num_scalar_prefetch=2,
            grid=(steps,),
            in_specs=[pl.BlockSpec((toks, d), lambda i, p, tl: (i, 0))],
            out_specs=pl.BlockSpec(memory_space=pl.ANY),
            scratch_shapes=[pltpu.VMEM((2, toks * TOK_ROWS, LANES), F32), pltpu.SemaphoreType.DMA((2,))],
        ),
        out_shape=jax.ShapeDtypeStruct((n_out * TOK_ROWS, LANES), F32),
        compiler_params=_params("arbitrary"),
        name="tok_scatter",
    )(pos, tail, x)


def _tok_gather_kernel(pos, src_ref, *rest, toks, norm):
    if norm:
        g_ref, out_ref, stage, sem = rest
        dst = stage
    else:
        out_ref, sem = rest
        dst = out_ref
    base = pl.program_id(0) * toks

    def start(r, c):
        pltpu.make_async_copy(src_ref.at[_tok_rows(pos[base + r])], dst.at[_tok_rows(r)], sem).start()
        return c

    lax.fori_loop(0, toks, start, 0, unroll=8)
    pltpu.make_async_copy(src_ref.at[pl.ds(0, toks * TOK_ROWS)], dst, sem).wait()
    if norm:
        out_ref[...] = _rms(_load_tok(stage, toks), g_ref[...])


def _tok_gather(src, pos, toks, norm_g=None):
    n = pos.shape[0]
    norm = norm_g is not None
    d = TOK_ROWS * LANES
    return pl.pallas_call(
        functools.partial(_tok_gather_kernel, toks=toks, norm=norm),
        grid_spec=pltpu.PrefetchScalarGridSpec(
            num_scalar_prefetch=1,
            grid=(n // toks,),
            in_specs=[pl.BlockSpec(memory_space=pl.ANY)]
            + ([pl.BlockSpec(norm_g.shape, lambda i, p: (0, 0))] if norm else []),
            out_specs=(pl.BlockSpec((toks, d), lambda i, p: (i, 0)) if norm
                       else pl.BlockSpec((toks * TOK_ROWS, LANES), lambda i, p: (i, 0))),
            scratch_shapes=([pltpu.VMEM((toks * TOK_ROWS, LANES), F32)] if norm else [])
            + [pltpu.SemaphoreType.DMA(())],
        ),
        out_shape=jax.ShapeDtypeStruct((n, d) if norm else (n * TOK_ROWS, LANES), F32),
        compiler_params=_params("arbitrary"),
        name="tok_gather",
    )(pos, src, *([norm_g] if norm else []))


def _moe_sorted_kernel(e_lo, e_hi, valid, xs_ref, g_ref, rw_ref, rb_ref,
                       wg_lo, wu_lo, wd_lo, wg_hi, wu_hi, wd_hi, out_ref, wgu_scr, wd_scr):
    t = pl.program_id(0)
    tm = xs_ref.shape[0] // TOK_ROWS
    ff = wd_lo.shape[2]
    prev = jnp.maximum(t - 1, 0)

    for slot, ids, wg, wu, wd in ((0, e_lo, wg_lo, wu_lo, wd_lo), (1, e_hi, wg_hi, wu_hi, wd_hi)):
        @pl.when((t == 0) | (ids[t] != ids[prev]))
        def _():
            wgu_scr[slot, :, 0:ff] = wg[0, 0].astype(BF16)
            wgu_scr[slot, :, ff:] = wu[0, 0].astype(BF16)
            wd_scr[slot] = wd[0, 0].astype(BF16)

    @pl.when(valid[t] > 0)
    def _():
        x = _load_tok(xs_ref, tm)
        xb = _rms(x, g_ref[...]).astype(BF16)
        logits = _dot(xb, rw_ref[...]) + rb_ref[...]
        lane = lax.broadcasted_iota(jnp.int32, logits.shape, 1)
        gl = jnp.where(lane < N_GROUPS, logits, -jnp.inf)
        g_w = 1.0 / jnp.sum(jnp.exp(gl - jnp.max(gl, axis=-1, keepdims=True)), axis=-1, keepdims=True)
        l_lo = jnp.sum(jnp.where(lane == e_lo[t] + N_GROUPS, logits, 0.0), axis=-1, keepdims=True)
        l_hi = jnp.sum(jnp.where(lane == e_hi[t] + N_GROUPS, logits, 0.0), axis=-1, keepdims=True)
        tail = jnp.exp(-jnp.abs(l_lo - l_hi))
        w_top = g_w / (1.0 + tail)
        w_oth = g_w * tail / (1.0 + tail)
        lo_top = l_lo >= l_hi
        y = x
        for slot, w in ((0, jnp.where(lo_top, w_top, w_oth)), (1, jnp.where(lo_top, w_oth, w_top))):
            gu = _dot(xb, wgu_scr[slot])
            hid = _silu(gu[:, 0:ff]) * gu[:, ff:] * w
            y = y + _dot(hid.astype(BF16), wd_scr[slot])
        _store_tok(out_ref, y)

    @pl.when(valid[t] == 0)
    def _():
        out_ref[...] = jnp.zeros(out_ref.shape, F32)


def _moe_sorted(xs, g, rw, rb, wg, wu, wd, layer, e_lo, e_hi, valid, tm):
    npad = xs.shape[0] // TOK_ROWS
    _, _, d, ff = wg.shape
    const = lambda t, lo, hi, v: (0, 0)
    at_lo = lambda t, lo, hi, v: (layer, lo[t], 0, 0)
    at_hi = lambda t, lo, hi, v: (layer, hi[t], 0, 0)
    return pl.pallas_call(
        _moe_sorted_kernel,
        grid_spec=pltpu.PrefetchScalarGridSpec(
            num_scalar_prefetch=3,
            grid=(npad // tm,),
            in_specs=[
                pl.BlockSpec((tm * TOK_ROWS, LANES), lambda t, lo, hi, v: (jnp.where(v[t] > 0, t, 0), 0)),
                pl.BlockSpec(g.shape, const), pl.BlockSpec(rw.shape, const), pl.BlockSpec(rb.shape, const),
                pl.BlockSpec((1, 1, d, ff), at_lo), pl.BlockSpec((1, 1, d, ff), at_lo),
                pl.BlockSpec((1, 1, ff, d), at_lo),
                pl.BlockSpec((1, 1, d, ff), at_hi), pl.BlockSpec((1, 1, d, ff), at_hi),
                pl.BlockSpec((1, 1, ff, d), at_hi),
            ],
            out_specs=pl.BlockSpec((tm * TOK_ROWS, LANES), lambda t, lo, hi, v: (t, 0)),
            scratch_shapes=[pltpu.VMEM((2, d, 2 * ff), BF16), pltpu.VMEM((2, ff, d), BF16)],
        ),
        out_shape=jax.ShapeDtypeStruct(xs.shape, F32),
        compiler_params=_params("arbitrary"),
        name="moe_sorted",
    )(e_lo, e_hi, valid, xs, g, rw, rb, wg, wu, wd, wg, wu, wd)


def _moe_sparse(x, g, rw, rb, wg, wu, wd, layer, tm, tr, toks, norm_g=None):
    n, d = x.shape
    cls, rank, counts = _moe_route(x, g, rw, rb, tr)
    cls, rank = cls.reshape(n), rank.reshape(n)
    cnt = counts[0, :N_CLASSES]
    padded = ((cnt + tm - 1) // tm) * tm
    ends = jnp.cumsum(padded)
    pos = (ends - padded)[cls] + rank
    n_tiles = n // tm + N_CLASSES
    tile_start = jnp.arange(n_tiles, dtype=jnp.int32) * tm
    n_valid = ends[-1] // tm
    unused = n_valid + jnp.arange(N_CLASSES, dtype=jnp.int32)
    tail = jnp.concatenate([jnp.where(cnt > 0, ends // tm - 1, -1),
                            jnp.where(unused < n_tiles, unused, -1)]).astype(jnp.int32)
    last_cls = jnp.sum((tile_start[jnp.maximum(n_valid - 1, 0)] >= ends).astype(jnp.int32))
    tile_cls = jnp.sum((tile_start[:, None] >= ends[None, :]).astype(jnp.int32), axis=1)
    valid = (tile_start < ends[-1]).astype(jnp.int32)
    tile_cls = jnp.where(valid > 0, tile_cls, last_cls)
    pair_lo = jnp.array([0, 0, 0, 1, 1, 2], jnp.int32)
    pair_hi = jnp.array([1, 2, 3, 2, 3, 3], jnp.int32)
    e_lo = (tile_cls // N_PAIRS) * EXP_PER_GROUP + pair_lo[tile_cls % N_PAIRS]
    e_hi = (tile_cls // N_PAIRS) * EXP_PER_GROUP + pair_hi[tile_cls % N_PAIRS]

    xs = _tok_scatter(x, n_tiles * tm, pos, tail, tm)
    ys = _moe_sorted(xs, g, rw.astype(BF16), rb, wg, wu, wd, layer, e_lo, e_hi, valid, tm)
    return _tok_gather(ys, pos, toks, norm_g)


def _moe_kernel(x_ref, g_ref, rw_ref, rb_ref, wg_ref, wu_ref, wd_ref, out_ref, xn_scr, comb_scr, acc_scr):
    e = pl.program_id(1)

    @pl.when(e == 0)
    def _():
        x = x_ref[...]
        xn = _rms(x, g_ref[...])
        xn_scr[...] = xn.astype(BF16)
        comb_scr[...] = _route(_dot(xn, rw_ref[...], HIGHEST) + rb_ref[...])
        acc_scr[...] = x

    xb = xn_scr[...]
    lane = lax.broadcasted_iota(jnp.int32, comb_scr.shape, 1)
    cmb = jnp.sum(jnp.where(lane == e + N_GROUPS, comb_scr[...], 0.0), axis=-1, keepdims=True)
    hid = _silu(_dot(xb, wg_ref[0, 0].astype(BF16))) * _dot(xb, wu_ref[0, 0].astype(BF16)) * cmb
    acc_scr[...] += _dot(hid.astype(BF16), wd_ref[0, 0].astype(BF16))

    @pl.when(e == pl.num_programs(1) - 1)
    def _():
        out_ref[...] = acc_scr[...]


def _moe(x, g, rw, rb, wg, wu, wd, layer, tm):
    n, d = x.shape
    _, ne, _, ff = wg.shape
    return pl.pallas_call(
        _moe_kernel,
        grid=(n // tm, ne),
        in_specs=[
            pl.BlockSpec((tm, d), lambda i, e: (i, 0)),
            _full(g.shape), _full(rw.shape), _full(rb.shape),
            pl.BlockSpec((1, 1, d, ff), lambda i, e: (layer, e, 0, 0)),
            pl.BlockSpec((1, 1, d, ff), lambda i, e: (layer, e, 0, 0)),
            pl.BlockSpec((1, 1, ff, d), lambda i, e: (layer, e, 0, 0)),
        ],
        out_specs=pl.BlockSpec((tm, d), lambda i, e: (i, 0)),
        out_shape=jax.ShapeDtypeStruct((n, d), F32),
        scratch_shapes=[pltpu.VMEM((tm, d), BF16), pltpu.VMEM((tm, LANES), F32), pltpu.VMEM((tm, d), F32)],
        compiler_params=_params("arbitrary", "arbitrary"),
        name="moe",
    )(x, g, rw, rb, wg, wu, wd)


def _norm_kernel(x_ref, g_ref, o_ref):
    o_ref[...] = _rms(x_ref[...], g_ref[...])


def _final_norm(x, g, tm):
    n, d = x.shape
    return pl.pallas_call(
        _norm_kernel,
        grid=(n // tm,),
        in_specs=[pl.BlockSpec((tm, d), lambda i: (i, 0)), _full(g.shape)],
        out_specs=pl.BlockSpec((tm, d), lambda i: (i, 0)),
        out_shape=jax.ShapeDtypeStruct((n, d), F32),
        compiler_params=_params("arbitrary"),
        name="final_norm",
    )(x, g)


def _mix_sample_in_kernel(x_ref, g_ref, win_ref, wlr_ref, gup_ref, gb_ref, cw_ref, b0_ref, b1_ref,
                          yc_ref, u_ref, q_ref, k_ref, v_ref, gate_ref, la_ref):
    hb = _rms(x_ref[...], g_ref[...]).astype(BF16)

    def proj(a, b):
        return _dot(hb, win_ref[:, a:b])

    u = proj(C_CC, C_CH) * proj(C_CH, C_Q)
    cw = cw_ref[...]
    yc_ref[...] = proj(C_CB, C_CC) * (b0_ref[...] * cw[0:1] + b1_ref[...] * cw[1:2] + u * cw[2:3])
    u_ref[...] = u
    q_ref[...] = proj(C_Q, C_K) * (GLA_DK ** -0.5)
    k_ref[...] = proj(C_K, C_V)
    v_ref[...] = proj(C_V, C_G)
    gate_ref[...] = proj(C_G, C_LR)
    lr = _dot(hb, wlr_ref[...])
    gate = _dot(lr.astype(BF16), gup_ref[...]) + gb_ref[...]
    la_ref[...] = _log_sigmoid(gate) * (1.0 / GLA_TAU)


def _mix_sample_in(x, g, win, wlr, gup, gb, cw, b0, b1):
    n = x.shape[0]
    args = (x, g, win, wlr, gup, gb, cw, b0, b1)
    widths = (CONV_W, CONV_W, GLA_K, GLA_K, GLA_V, GLA_V, GLA_K)
    return pl.pallas_call(
        _mix_sample_in_kernel,
        grid=(1,),
        in_specs=[_full(a.shape) for a in args],
        out_specs=[_full((n, w)) for w in widths],
        out_shape=[jax.ShapeDtypeStruct((n, w), F32) for w in widths],
        compiler_params=_params("arbitrary"),
        name="mix_sample_in",
    )(*args)


def _gla_step_kernel(q_ref, k_ref, v_ref, la_ref, s0_ref, s_ref, o_ref):
    nb = q_ref.shape[0]
    a = jnp.exp(la_ref[...])
    kb = k_ref[...].astype(BF16)
    qb = q_ref[...].astype(BF16)
    vf = v_ref[...].astype(BF16).astype(F32)
    rows = lax.broadcasted_iota(jnp.int32, (nb, 1), 0)
    spread = (lax.broadcasted_iota(jnp.int32, (nb, nb * GLA_DV), 0)
              == lax.broadcasted_iota(jnp.int32, (nb, nb * GLA_DV), 1) // GLA_DV)
    spread_b = spread.astype(BF16)
    for h in range(GLA_HEADS):
        ks = slice(h * GLA_DK, (h + 1) * GLA_DK)
        a_cols = _dot_tn(a[:, ks], spread.astype(F32), HIGHEST)
        k_cols = _dot_tn(kb[:, ks], spread_b)
        q_cols = _dot_tn(qb[:, ks], spread_b)
        o_h = jnp.zeros((nb, GLA_DV), F32)
        for n in range(nb):
            blk = slice(n * GLA_DV, (n + 1) * GLA_DV)
            v_row = vf[n:n + 1, h * GLA_DV:(h + 1) * GLA_DV]
            s_new = a_cols[:, blk] * s0_ref[0, n, h] + k_cols[:, blk] * v_row
            s_ref[n, h] = s_new
            o_h = jnp.where(rows == n, jnp.sum(q_cols[:, blk] * s_new, axis=0, keepdims=True), o_h)
        o_ref[:, h * GLA_DV:(h + 1) * GLA_DV] = o_h


def _gla_step(q, k, v, la, state, layer, nb):
    n = q.shape[0]
    sshape = (GLA_HEADS, GLA_DK, GLA_DV)
    return pl.pallas_call(
        _gla_step_kernel,
        grid=(n // nb,),
        in_specs=[
            pl.BlockSpec((nb, GLA_K), lambda i: (i, 0)),
            pl.BlockSpec((nb, GLA_K), lambda i: (i, 0)),
            pl.BlockSpec((nb, GLA_V), lambda i: (i, 0)),
            pl.BlockSpec((nb, GLA_K), lambda i: (i, 0)),
            pl.BlockSpec((1, nb) + sshape, lambda i: (layer, i, 0, 0, 0)),
        ],
        out_specs=[
            pl.BlockSpec((nb,) + sshape, lambda i: (i, 0, 0, 0)),
            pl.BlockSpec((nb, GLA_V), lambda i: (i, 0)),
        ],
        out_shape=[jax.ShapeDtypeStruct((n,) + sshape, F32), jax.ShapeDtypeStruct((n, GLA_V), F32)],
        compiler_params=_params("arbitrary"),
        name="gla_step",
    )(q, k, v, la, state)


def _mix_sample_out_kernel(x_ref, yc_ref, o_ref, gate_ref, gg_ref, wout_ref, gm_ref, wq_ref, x1_ref, q_ref):
    yg = _head_norm_gate(o_ref[...], gate_ref[...], gg_ref[...])
    y = _dot(yc_ref[...].astype(BF16), wout_ref[0:CONV_W, :]) + _dot(yg.astype(BF16), wout_ref[CONV_W:, :])
    x1 = x_ref[...] + y
    x1_ref[...] = x1
    q_ref[...] = _dot(_rms(x1, gm_ref[...]).astype(BF16), wq_ref[...])


def _mix_sample_out(x, yc, o, gate, gg, wout, gm, wq):
    args = (x, yc, o, gate, gg, wout, gm, wq)
    return pl.pallas_call(
        _mix_sample_out_kernel,
        grid=(1,),
        in_specs=[_full(a.shape) for a in args],
        out_specs=[_full(x.shape)] * 2,
        out_shape=[jax.ShapeDtypeStruct(x.shape, F32)] * 2,
        compiler_params=_params("arbitrary"),
        name="mix_sample_out",
    )(*args)


ATT_ROWS = 2 * MEM_HEADS


def _class_allreduce(x, op):
    n = x.shape[-1]
    shift = ATT_ROWS
    while shift < n:
        x = op(x, pltpu.roll(x, shift, axis=1))
        shift *= 2
    return x


def _att_sample_kernel(q_ref, k_ref, v_ref, o_ref):
    nb = q_ref.shape[0]
    ncol = k_ref.shape[2]
    diag = (lax.broadcasted_iota(jnp.int32, (ATT_ROWS, ncol), 0)
            == (lax.broadcasted_iota(jnp.int32, (ATT_ROWS, ncol), 1) & (ATT_ROWS - 1)))
    rows = lax.broadcasted_iota(jnp.int32, (nb, 1), 0)
    t = jnp.zeros((nb, ncol), F32)
    for n in range(nb):
        sc = _dot_nt(q_ref[n].astype(BF16), k_ref[0, n].astype(BF16))
        t = t + jnp.where(rows == n, jnp.sum(jnp.where(diag, sc, 0.0), axis=0, keepdims=True), 0.0)
    valid = (lax.broadcasted_iota(jnp.int32, (nb, ncol), 1) & (ATT_ROWS - 1)) < MEM_HEADS
    s = jnp.where(valid, (t + pltpu.roll(t, ncol - MEM_HEADS, axis=1)) * (MEM_DH ** -0.5), 0.0)
    e = jnp.where(valid, jnp.exp(s - _class_allreduce(s, jnp.maximum)), 0.0)
    den = jnp.where(valid, _class_allreduce(e, jnp.add), 1.0)
    p = e / den
    p = p + pltpu.roll(p, MEM_HEADS, axis=1)
    for n in range(nb):
        p_n = jnp.where(diag, jnp.broadcast_to(p[n:n + 1, :], (ATT_ROWS, ncol)), 0.0)
        o_ref[n] = _dot(p_n.astype(BF16), v_ref[0, n].astype(BF16))


def _att_sample(q, ck, cv, layer, nb):
    n = q.shape[0]
    ncol = ck.shape[2]
    return pl.pallas_call(
        _att_sample_kernel,
        grid=(n // nb,),
        in_specs=[
            pl.BlockSpec((nb, ATT_ROWS, LANES), lambda i: (i, 0, 0)),
            pl.BlockSpec((1, nb, ncol, LANES), lambda i: (layer, i, 0, 0)),
            pl.BlockSpec((1, nb, ncol, LANES), lambda i: (layer, i, 0, 0)),
        ],
        out_specs=pl.BlockSpec((nb, ATT_ROWS, LANES), lambda i: (i, 0, 0)),
        out_shape=jax.ShapeDtypeStruct((n, ATT_ROWS, LANES), F32),
        compiler_params=_params("arbitrary"),
        name="att_sample",
    )(q, ck, cv)


def _oproj_kernel(x_ref, o_ref, wo_ref, out_ref):
    out_ref[...] = x_ref[...] + _dot(o_ref[...].astype(BF16), wo_ref[...])


def _oproj(x, o, wo):
    return pl.pallas_call(
        _oproj_kernel,
        grid=(1,),
        in_specs=[_full(x.shape), _full(o.shape), _full(wo.shape)],
        out_specs=_full(x.shape),
        out_shape=jax.ShapeDtypeStruct(x.shape, F32),
        compiler_params=_params("arbitrary"),
        name="oproj",
    )(x, o, wo)


def kernel(x_prompt, x_sample, state_conv, state_gla, cache_mem_k, cache_mem_v, mem_prompt, norm_mix, w_in, conv_w, gla_gate_up, gla_gate_b, gla_out_norm, w_out, norm_mem, w_q, w_k, w_v, w_o, norm_ffn, router_group, router_group_b, router_expert, router_expert_b, w_gate, w_up, w_down, norm_final):
    depth = w_in.shape[0]
    bsz, t, d = x_prompt.shape
    ns = x_sample.shape[0]
    nm = mem_prompt.shape[1]
    n_tok = bsz * t

    tt = min(512, t)
    tq = min(512, t)
    tm_moe = min(1024, n_tok)
    tm_kv = min(512, bsz * nm)
    tm_sorted = 256
    tr_route = min(1024, n_tok)
    rows_perm = min(512, n_tok)
    nb_gla = min(8, ns)
    nb_att = min(8, ns)

    row = lambda a: a.reshape(1, -1)
    mem2 = mem_prompt.reshape(bsz * nm, d)

    def tile_rows(c):
        c = c.reshape(depth, ns, nm, MEM_HEADS, 2, LANES).transpose(0, 1, 2, 4, 3, 5)
        return c.reshape(depth, ns, nm * ATT_ROWS, LANES)

    ck, cv = tile_rows(cache_mem_k), tile_rows(cache_mem_v)

    xp = x_prompt
    xs = x_sample.reshape(ns, d)
    conv_p, gla_p, mk_p, mv_p, conv_s, gla_s = [], [], [], [], [], []
    for l in range(depth):
        win = w_in[l, :, :C_LR].astype(BF16)
        wlr = jnp.pad(w_in[l, :, C_LR:], ((0, 0), (0, LANES - GLA_RANK))).astype(BF16)
        gup = jnp.pad(gla_gate_up[l], ((0, LANES - GLA_RANK), (0, 0))).astype(BF16)
        gb = row(gla_gate_b[l])
        gg = row(gla_out_norm[l])
        wout = w_out[l].astype(BF16)
        wq, wk, wv, wo = (w[l].astype(BF16) for w in (w_q, w_k, w_v, w_o))
        rw = jnp.concatenate([router_group[l], router_expert[l].transpose(1, 0, 2).reshape(d, N_EXPERTS)], axis=1)
        rw = jnp.pad(rw, ((0, 0), (0, LANES - rw.shape[1])))
        rb = jnp.concatenate([router_group_b[l], router_expert_b[l].reshape(-1)])
        rb = row(jnp.pad(rb, (0, LANES - rb.shape[0])))
        moe_w = (row(norm_ffn[l]), rw, rb, w_gate, w_up, w_down, l)

        mk, mv = _mem_kv(mem2, wk, wv, tm_kv)
        mk_p.append(mk.reshape(bsz, nm, MEM_HEADS, MEM_DH))
        mv_p.append(mv.reshape(bsz, nm, MEM_HEADS, MEM_DH))
        xp, nbuf, ns_p = _mix_prompt(xp, (bsz, t, d), row(norm_mix[l]), win, wlr, gup, gb, conv_w[l], gg, wout, tt)
        conv_p.append(nbuf)
        gla_p.append(ns_p)
        xp = _att_prompt(xp, row(norm_mem[l]), wq, wo, mk.reshape(bsz, nm, d), mv.reshape(bsz, nm, d), tq)
        xp = _moe_sparse(xp.reshape(n_tok, d), *moe_w, tm_sorted, tr_route, rows_perm,
                         norm_g=row(norm_final) if l == depth - 1 else None)

        yc, u, q, k, v, gate, la = _mix_sample_in(
            xs, row(norm_mix[l]), win, wlr, gup, gb, conv_w[l], state_conv[l, :, 0], state_conv[l, :, 1])
        conv_s.append(jnp.stack([state_conv[l, :, 1], u], axis=1))
        s_new, o = _gla_step(q, k, v, la, state_gla, l, nb_gla)
        gla_s.append(s_new)
        wq_s = wq.reshape(d, MEM_HEADS, 2, LANES).transpose(0, 2, 1, 3).reshape(d, d)
        wo_s = wo.reshape(MEM_HEADS, 2, LANES, d).transpose(1, 0, 2, 3).reshape(d, d)
        xs, qa = _mix_sample_out(xs, yc, o, gate, gg, wout, row(norm_mem[l]), wq_s)
        oa = _att_sample(qa.reshape(ns, ATT_ROWS, LANES), ck, cv, l, nb_att)
        xs = _oproj(xs, oa.reshape(ns, d), wo_s)
        xs = _moe(xs, *moe_w, min(tm_moe, ns))

    y_prompt = xp.reshape(bsz, t, d)
    y_sample = _final_norm(xs, row(norm_final), ns).reshape(ns, 1, d)
    return (y_prompt, y_sample, jnp.stack(conv_p), jnp.stack(gla_p), jnp.stack(mk_p), jnp.stack(mv_p),
            jnp.stack(conv_s), jnp.stack(gla_s))
```

```python
import functools

import jax
import jax.numpy as jnp
from jax import lax
from jax.experimental import pallas as pl
from jax.experimental.pallas import tpu as pltpu

F32 = jnp.float32
BF16 = jnp.bfloat16
HIGHEST = lax.Precision.HIGHEST

EPS = 1e-6
CONV_W = 512
GLA_HEADS = 4
GLA_DK = 64
GLA_DV = 128
GLA_K = GLA_HEADS * GLA_DK
GLA_V = GLA_HEADS * GLA_DV
GLA_RANK = 16
GLA_TAU = 16.0
GLA_CHUNK = 64
MEM_HEADS = 4
MEM_DH = 256
N_GROUPS = 4
EXP_PER_GROUP = 4
N_EXPERTS = 16
LANES = 128
C_CB, C_CC, C_CH, C_Q, C_K, C_V, C_G, C_LR = 0, 512, 1024, 1536, 1792, 2048, 2560, 3072
VMEM_LIMIT = 52 * 1024 * 1024


def _params(*sem):
    return pltpu.CompilerParams(dimension_semantics=sem, vmem_limit_bytes=VMEM_LIMIT)


def _rms(x, g):
    return x * lax.rsqrt(jnp.mean(x * x, axis=-1, keepdims=True) + EPS) * g


def _dot(a, b, precision=None):
    return jnp.dot(a, b, precision=precision, preferred_element_type=F32)


def _dot_nt(a, b):
    return lax.dot_general(a, b, (((1,), (1,)), ((), ())), preferred_element_type=F32)


def _dot_tn(a, b, precision=None):
    return lax.dot_general(a, b, (((0,), (0,)), ((), ())), precision=precision,
                           preferred_element_type=F32)


def _dot_split(dot, mask, x):
    hi = x.astype(BF16)
    r1 = x - hi.astype(F32)
    mid = r1.astype(BF16)
    lo = (r1 - mid.astype(F32)).astype(BF16)
    return dot(mask, hi) + dot(mask, mid) + dot(mask, lo)


def _silu(x):
    return x / (1.0 + jnp.exp(-x))


def _log_sigmoid(x):
    return jnp.minimum(x, 0.0) - jnp.log1p(jnp.exp(-jnp.abs(x)))


def _head_norm_gate(o, g, gg):
    parts = []
    for h in range(GLA_HEADS):
        sl = slice(h * GLA_DV, (h + 1) * GLA_DV)
        parts.append(_rms(o[:, sl], gg[:, sl]))
    return jnp.concatenate(parts, axis=-1) * _silu(g)


def _full(shape):
    nd = len(shape)
    return pl.BlockSpec(shape, lambda *_: (0,) * nd)


def _mix_prompt_kernel(x_ref, g_ref, win_ref, wlr_ref, gup_ref, gb_ref, cw_ref, gg_ref, wout_ref,
                       x1_ref, conv_ref, s_ref, ubuf, s_scr, o_scr, *, tok_in):
    t = pl.program_id(1)
    tt = x1_ref.shape[1]

    @pl.when(t == 0)
    def _():
        ubuf[0:8, :] = jnp.zeros((8, CONV_W), F32)
        s_scr[...] = jnp.zeros(s_scr.shape, F32)

    x = _load_tok(x_ref, tt) if tok_in else x_ref[0]
    hb = _rms(x, g_ref[...]).astype(BF16)

    def proj(a, b):
        return _dot(hb, win_ref[:, a:b])

    u = proj(C_CC, C_CH) * proj(C_CH, C_Q)
    ubuf[8:8 + tt, :] = u
    cw = cw_ref[...]
    yc = proj(C_CB, C_CC) * (ubuf[6:6 + tt, :] * cw[0:1] + ubuf[7:7 + tt, :] * cw[1:2] + u * cw[2:3])
    ubuf[6:8, :] = u[tt - 2:tt, :]

    qs = proj(C_Q, C_K) * (GLA_DK ** -0.5)
    k = proj(C_K, C_V)
    v = proj(C_V, C_G)
    lr = _dot(hb, wlr_ref[...])
    gate = _dot(lr.astype(BF16), gup_ref[...]) + gb_ref[...]
    la = _log_sigmoid(gate) * (1.0 / GLA_TAU)

    c = GLA_CHUNK
    nc = tt // c
    iota = lambda shape, dim: lax.broadcasted_iota(jnp.int32, shape, dim)
    ltri = (iota((c, c), 0) >= iota((c, c), 1)).astype(BF16)
    b_wide = _dot_split(_dot, ltri, jnp.concatenate([la[j * c:(j + 1) * c] for j in range(nc)], axis=1))
    b_ends = jnp.concatenate([b_wide[c - 1:c, j * GLA_K:(j + 1) * GLA_K] for j in range(nc)], axis=0)
    spread = (iota((nc, nc * GLA_DV), 0) == iota((nc, nc * GLA_DV), 1) // GLA_DV).astype(BF16)
    dec_all = jnp.exp(_dot_split(lambda m, x: _dot_tn(x, m), spread, b_ends))
    head_feat = iota((GLA_K, GLA_K), 0) // c == iota((GLA_K, GLA_K), 1) // GLA_DK
    head_blk = iota((GLA_K, GLA_V), 0) // GLA_DK == iota((GLA_K, GLA_V), 1) // GLA_DV
    causal = iota((c, GLA_K), 0) >= iota((c, GLA_K), 1) % c
    for j in range(nc):
        r = slice(j * c, (j + 1) * c)
        b_c, q_c, k_c, v_c = b_wide[:, j * GLA_K:(j + 1) * GLA_K], qs[r], k[r], v[r]
        b_mid = b_c[c // 2:c // 2 + 1, :]
        b_last = b_c[c - 1:c, :]
        q_i = (q_c * jnp.exp(b_c - b_mid)).astype(BF16)
        k_i = k_c * jnp.exp(b_mid - b_c)
        k_dec = (k_c * jnp.exp(b_last - b_c)).astype(BF16)
        q_b = (q_c * jnp.exp(b_c)).astype(BF16)
        k_rows = jnp.where(head_feat, jnp.concatenate([k_i] * GLA_HEADS, axis=0), 0.0).astype(BF16)
        a = jnp.where(causal, _dot_nt(q_i, k_rows), 0.0).astype(BF16)
        v_blk = jnp.where(head_blk, jnp.concatenate([v_c] * GLA_HEADS, axis=0), 0.0).astype(BF16)
        s_prev = s_scr[...]
        o_scr[r, :] = _dot(jnp.concatenate([a, q_b], axis=1),
                           jnp.concatenate([v_blk, s_prev.astype(BF16)], axis=0))
        dec = dec_all[:, j * GLA_DV:(j + 1) * GLA_DV]
        s_scr[...] = (jnp.concatenate([dec] * GLA_HEADS, axis=1) * s_prev
                      + jnp.where(head_blk, _dot_tn(k_dec, v_c.astype(BF16)), 0.0))

    yg = _head_norm_gate(o_scr[...], proj(C_G, C_LR), gg_ref[...])
    y = _dot(yc.astype(BF16), wout_ref[0:CONV_W, :]) + _dot(yg.astype(BF16), wout_ref[CONV_W:, :])
    x1_ref[0] = x + y

    @pl.when(t == pl.num_programs(1) - 1)
    def _():
        conv_ref[0] = u[tt - 2:tt, :]
        for h in range(GLA_HEADS):
            s_ref[0, h] = s_scr[h * GLA_DK:(h + 1) * GLA_DK, h * GLA_DV:(h + 1) * GLA_DV]


def _mix_prompt(x, shape, g, win, wlr, gup, gb, cw, gg, wout, tt):
    bsz, t, d = shape
    tok_in = x.ndim == 2
    nt = t // tt
    x_spec = (pl.BlockSpec((tt * TOK_ROWS, LANES), lambda b, i: (b * nt + i, 0)) if tok_in
              else pl.BlockSpec((1, tt, d), lambda b, i: (b, i, 0)))
    return pl.pallas_call(
        functools.partial(_mix_prompt_kernel, tok_in=tok_in),
        grid=(bsz, nt),
        in_specs=[
            x_spec,
            _full(g.shape), _full(win.shape), _full(wlr.shape), _full(gup.shape), _full(gb.shape),
            _full(cw.shape), _full(gg.shape), _full(wout.shape),
        ],
        out_specs=[
            pl.BlockSpec((1, tt, d), lambda b, i: (b, i, 0)),
            pl.BlockSpec((1, 2, CONV_W), lambda b, i: (b, 0, 0)),
            pl.BlockSpec((1, GLA_HEADS, GLA_DK, GLA_DV), lambda b, i: (b, 0, 0, 0)),
        ],
        out_shape=[
            jax.ShapeDtypeStruct((bsz, t, d), F32),
            jax.ShapeDtypeStruct((bsz, 2, CONV_W), F32),
            jax.ShapeDtypeStruct((bsz, GLA_HEADS, GLA_DK, GLA_DV), F32),
        ],
        scratch_shapes=[
            pltpu.VMEM((8 + tt, CONV_W), F32),
            pltpu.VMEM((GLA_K, GLA_V), F32),
            pltpu.VMEM((tt, GLA_V), F32),
        ],
        compiler_params=_params("arbitrary", "arbitrary"),
        name="mix_prompt",
    )(x, g, win, wlr, gup, gb, cw, gg, wout)


def _kv_kernel(m_ref, wk_ref, wv_ref, k_ref, v_ref):
    tm = m_ref.shape[0]
    mb = m_ref[...].astype(BF16)
    for w_ref, o_ref in ((wk_ref, k_ref), (wv_ref, v_ref)):
        y = _dot(mb, w_ref[0])
        for h in range(MEM_HEADS):
            for dt in range(MEM_DH // LANES):
                col = h * MEM_DH + dt * LANES
                o_ref[0, pl.ds(dt * MEM_HEADS + h, tm, stride=ATT_ROWS), :] = y[:, col:col + LANES]


def _mem_kv(mem, wk, wv, tm):
    n, d = mem.shape
    depth = wk.shape[0]
    w_spec = pl.BlockSpec((1, d, d), lambda l, i: (l, 0, 0))
    o_spec = pl.BlockSpec((1, tm * ATT_ROWS, LANES), lambda l, i: (l, i, 0))
    return pl.pallas_call(
        _kv_kernel,
        grid=(depth, n // tm),
        in_specs=[pl.BlockSpec((tm, d), lambda l, i: (i, 0)), w_spec, w_spec],
        out_specs=[o_spec] * 2,
        out_shape=[jax.ShapeDtypeStruct((depth, n * ATT_ROWS, LANES), F32)] * 2,
        compiler_params=_params("arbitrary", "arbitrary"),
        name="mem_kv",
    )(mem, wk, wv)


def _att_prompt_kernel(x_ref, g_ref, wq_ref, wo_ref, mk_ref, mv_ref, out_ref):
    x = x_ref[0]
    xb = _rms(x, g_ref[...]).astype(BF16)
    q = _dot(xb, wq_ref[...])
    nm = mk_ref.shape[1] // ATT_ROWS

    def head_rows(ref, h):
        return jnp.concatenate([ref[0, pl.ds(dt * MEM_HEADS + h, nm, stride=ATT_ROWS), :]
                                for dt in range(MEM_DH // LANES)], axis=-1).astype(BF16)

    outs = []
    for h in range(MEM_HEADS):
        sl = slice(h * MEM_DH, (h + 1) * MEM_DH)
        s = _dot_nt(q[:, sl].astype(BF16), head_rows(mk_ref, h)) * (MEM_DH ** -0.5)
        e = jnp.exp(s - jnp.max(s, axis=-1, keepdims=True))
        p = e / jnp.sum(e, axis=-1, keepdims=True)
        outs.append(_dot(p.astype(BF16), head_rows(mv_ref, h)))
    o = jnp.concatenate(outs, axis=-1)
    out_ref[0] = x + _dot(o.astype(BF16), wo_ref[...])


def _att_prompt(x, g, wq, wo, mk, mv, layer, tq):
    bsz, t, d = x.shape
    rows = mk.shape[1] // bsz
    kv_spec = pl.BlockSpec((1, rows, LANES), lambda b, i: (layer, b, 0))
    return pl.pallas_call(
        _att_prompt_kernel,
        grid=(bsz, t // tq),
        in_specs=[
            pl.BlockSpec((1, tq, d), lambda b, i: (b, i, 0)),
            _full(g.shape), _full(wq.shape), _full(wo.shape),
            kv_spec, kv_spec,
        ],
        out_specs=pl.BlockSpec((1, tq, d), lambda b, i: (b, i, 0)),
        out_shape=jax.ShapeDtypeStruct((bsz, t, d), F32),
        compiler_params=_params("arbitrary", "arbitrary"),
        name="att_prompt",
    )(x, g, wq, wo, mk, mv)


def _route_top2(logits):
    lane = lax.broadcasted_iota(jnp.int32, logits.shape, 1)
    lanef = lane.astype(F32)
    ninf = -jnp.inf
    big = 1e9
    gl = jnp.where(lane < N_GROUPS, logits, ninf)
    gmax = jnp.max(gl, axis=-1, keepdims=True)
    g_idx = jnp.min(jnp.where(gl == gmax, lanef, big), axis=-1, keepdims=True)
    g_w = 1.0 / jnp.sum(jnp.exp(gl - gmax), axis=-1, keepdims=True)
    grp = ((lane - N_GROUPS) >> 2).astype(F32)
    emask = (lane >= N_GROUPS) & (lane < N_GROUPS + N_EXPERTS) & (grp == g_idx)
    el = jnp.where(emask, logits, ninf)
    m1 = jnp.max(el, axis=-1, keepdims=True)
    i1 = jnp.min(jnp.where(el == m1, lanef, big), axis=-1, keepdims=True)
    el2 = jnp.where(lanef == i1, ninf, el)
    m2 = jnp.max(el2, axis=-1, keepdims=True)
    i2 = jnp.min(jnp.where(el2 == m2, lanef, big), axis=-1, keepdims=True)
    tail = jnp.exp(m2 - m1)
    w1 = g_w / (1.0 + tail)
    w2 = g_w * tail / (1.0 + tail)
    return g_idx, i1, i2, w1, w2


def _route(logits):
    _, i1, i2, w1, w2 = _route_top2(logits)
    lanef = lax.broadcasted_iota(jnp.int32, logits.shape, 1).astype(F32)
    return jnp.where(lanef == i1, w1, 0.0) + jnp.where(lanef == i2, w2, 0.0)


N_PAIRS = 6
N_CLASSES = N_GROUPS * N_PAIRS
TOK_ROWS = 8


def _load_tok(ref, n):
    return jnp.concatenate([ref[pl.ds(j, n, stride=TOK_ROWS), :] for j in range(TOK_ROWS)], axis=-1)


def _store_tok(ref, val):
    n = val.shape[0]
    for j in range(TOK_ROWS):
        ref[pl.ds(j, n, stride=TOK_ROWS), :] = val[:, j * LANES:(j + 1) * LANES]


def _lane_dense(col):
    eye = lax.broadcasted_iota(jnp.int32, (LANES, LANES), 0) == lax.broadcasted_iota(jnp.int32, (LANES, LANES), 1)
    rows = [jnp.sum(jnp.where(eye, col[b * LANES:(b + 1) * LANES], 0.0), axis=0, keepdims=True)
            for b in range(col.shape[0] // LANES)]
    return jnp.concatenate(rows, axis=0)


def _moe_route_kernel(x_ref, g_ref, rw_ref, rb_ref, ltri_ref, cls_ref, rank_ref, cnt_ref, carry):
    i = pl.program_id(0)
    tr = x_ref.shape[0]

    @pl.when(i == 0)
    def _():
        carry[...] = jnp.zeros(carry.shape, F32)

    xn = _rms(x_ref[...], g_ref[...])
    g_idx, i1, i2, _, _ = _route_top2(_dot(xn, rw_ref[...], HIGHEST) + rb_ref[...])
    lo = jnp.minimum(i1, i2) - N_GROUPS - EXP_PER_GROUP * g_idx
    hi = jnp.maximum(i1, i2) - N_GROUPS - EXP_PER_GROUP * g_idx
    cls = g_idx * N_PAIRS + lo * (7.0 - lo) * 0.5 + hi - lo - 1.0

    lane = lax.broadcasted_iota(jnp.int32, (tr, LANES), 1)
    onehot = lane.astype(F32) == cls
    prefix = _dot(ltri_ref[...], onehot.astype(BF16)) + carry[...]
    rank = jnp.sum(jnp.where(onehot, prefix, 0.0), axis=-1, keepdims=True) - 1.0
    carry[...] = prefix[tr - 1:tr, :]
    cls_ref[...] = _lane_dense(cls).astype(jnp.int32)
    rank_ref[...] = _lane_dense(rank).astype(jnp.int32)
    cnt_ref[...] = prefix[tr - 1:tr, :].astype(jnp.int32)


def _moe_route(x, g, rw, rb, tr):
    n, d = x.shape
    ltri = jnp.tril(jnp.ones((tr, tr), BF16))
    rows = tr // LANES
    return pl.pallas_call(
        _moe_route_kernel,
        grid=(n // tr,),
        in_specs=[pl.BlockSpec((tr, d), lambda i: (i, 0)), _full(g.shape), _full(rw.shape), _full(rb.shape),
                  _full(ltri.shape)],
        out_specs=[
            pl.BlockSpec((rows, LANES), lambda i: (i, 0)),
            pl.BlockSpec((rows, LANES), lambda i: (i, 0)),
            _full((1, LANES)),
        ],
        out_shape=[
            jax.ShapeDtypeStruct((n // LANES, LANES), jnp.int32),
            jax.ShapeDtypeStruct((n // LANES, LANES), jnp.int32),
            jax.ShapeDtypeStruct((1, LANES), jnp.int32),
        ],
        scratch_shapes=[pltpu.VMEM((1, LANES), F32)],
        compiler_params=_params("arbitrary"),
        name="moe_route",
    )(x, g, rw, rb, ltri)


DMA_UNROLL = 8


def _tok_rows(t):
    return pl.ds(pl.multiple_of(t * TOK_ROWS, TOK_ROWS), TOK_ROWS)


def _tok_scatter_kernel(pos, tail, x_ref, dst_ref, stage, sems, *, toks, steps):
    i = pl.program_id(0)
    slot = i % 2
    base = i * toks
    rows = toks * TOK_ROWS

    def wait_slot(s):
        pltpu.make_async_copy(stage.at[s], dst_ref.at[pl.ds(0, rows)], sems.at[s]).wait()

    @pl.when(i == 0)
    def _():
        stage[1] = jnp.zeros((rows, LANES), F32)

        def fill(c):
            return pltpu.make_async_copy(stage.at[1], dst_ref.at[pl.ds(pl.multiple_of(tail[c] * rows, rows), rows)],
                                         sems.at[1])

        for c in range(2 * N_CLASSES):
            @pl.when(tail[c] >= 0)
            def _():
                fill(c).start()
        for c in range(2 * N_CLASSES):
            @pl.when(tail[c] >= 0)
            def _():
                fill(c).wait()

    @pl.when(i >= 2)
    def _():
        wait_slot(slot)

    _store_tok(stage.at[slot], x_ref[...])

    def start(r8, c):
        for u in range(DMA_UNROLL):
            r = r8 * DMA_UNROLL + u
            pltpu.make_async_copy(stage.at[slot, _tok_rows(r)], dst_ref.at[_tok_rows(pos[base + r])],
                                  sems.at[slot]).start(priority=u % 2)
        return c

    lax.fori_loop(0, toks // DMA_UNROLL, start, 0)

    @pl.when(i == steps - 1)
    def _():
        wait_slot(slot)
        if steps >= 2:
            wait_slot(1 - slot)


def _tok_scatter(x, n_out, pos, tail, toks):
    n, d = x.shape
    steps = n // toks
    return pl.pallas_call(
        functools.partial(_tok_scatter_kernel, toks=toks, steps=steps),
        grid_spec=pltpu.PrefetchScalarGridSpec(
            num_scalar_prefetch=2,
            grid=(steps,),
            in_specs=[pl.BlockSpec((toks, d), lambda i, p, tl: (i, 0))],
            out_specs=pl.BlockSpec(memory_space=pl.ANY),
            scratch_shapes=[pltpu.VMEM((2, toks * TOK_ROWS, LANES), F32), pltpu.SemaphoreType.DMA((2,))],
        ),
        out_shape=jax.ShapeDtypeStruct((n_out * TOK_ROWS, LANES), F32),
        compiler_params=_params("arbitrary"),
        name="tok_scatter",
    )(pos, tail, x)


def _tok_gather_kernel(pos, src_ref, *rest, toks, norm):
    if norm:
        g_ref, out_ref, stage, sem = rest
        dst = stage
    else:
        out_ref, sem = rest
        dst = out_ref
    base = pl.program_id(0) * toks

    def start(r8, c):
        for u in range(DMA_UNROLL):
            r = r8 * DMA_UNROLL + u
            pltpu.make_async_copy(src_ref.at[_tok_rows(pos[base + r])], dst.at[_tok_rows(r)],
                                  sem).start(priority=u % 2)
        return c

    lax.fori_loop(0, toks // DMA_UNROLL, start, 0)
    pltpu.make_async_copy(src_ref.at[pl.ds(0, toks * TOK_ROWS)], dst, sem).wait()
    if norm:
        out_ref[...] = _rms(_load_tok(stage, toks), g_ref[...])


def _tok_gather(src, pos, toks, norm_g=None):
    n = pos.shape[0]
    norm = norm_g is not None
    d = TOK_ROWS * LANES
    return pl.pallas_call(
        functools.partial(_tok_gather_kernel, toks=toks, norm=norm),
        grid_spec=pltpu.PrefetchScalarGridSpec(
            num_scalar_prefetch=1,
            grid=(n // toks,),
            in_specs=[pl.BlockSpec(memory_space=pl.ANY)]
            + ([pl.BlockSpec(norm_g.shape, lambda i, p: (0, 0))] if norm else []),
            out_specs=(pl.BlockSpec((toks, d), lambda i, p: (i, 0)) if norm
                       else pl.BlockSpec((toks * TOK_ROWS, LANES), lambda i, p: (i, 0))),
            scratch_shapes=([pltpu.VMEM((toks * TOK_ROWS, LANES), F32)] if norm else [])
            + [pltpu.SemaphoreType.DMA(())],
        ),
        out_shape=jax.ShapeDtypeStruct((n, d) if norm else (n * TOK_ROWS, LANES), F32),
        compiler_params=_params("arbitrary"),
        name="tok_gather",
    )(pos, src, *([norm_g] if norm else []))


def _moe_sorted_kernel(e_lo, e_hi, valid, xs_ref, g_ref, rw_ref, rb_ref,
                       wg_lo, wu_lo, wd_lo, wg_hi, wu_hi, wd_hi, out_ref, wgu_scr, wd_scr):
    t = pl.program_id(0)
    tm = xs_ref.shape[0] // TOK_ROWS
    ff = wd_lo.shape[2]
    prev = jnp.maximum(t - 1, 0)

    for slot, ids, wg, wu, wd in ((0, e_lo, wg_lo, wu_lo, wd_lo), (1, e_hi, wg_hi, wu_hi, wd_hi)):
        @pl.when((t == 0) | (ids[t] != ids[prev]))
        def _():
            wgu_scr[slot, :, 0:ff] = wg[0, 0].astype(BF16)
            wgu_scr[slot, :, ff:] = wu[0, 0].astype(BF16)
            wd_scr[slot] = wd[0, 0].astype(BF16)

    @pl.when(valid[t] > 0)
    def _():
        x = _load_tok(xs_ref, tm)
        xb = _rms(x, g_ref[...]).astype(BF16)
        logits = _dot(xb, rw_ref[...]) + rb_ref[...]
        lane = lax.broadcasted_iota(jnp.int32, logits.shape, 1)
        gl = jnp.where(lane < N_GROUPS, logits, -jnp.inf)
        g_w = 1.0 / jnp.sum(jnp.exp(gl - jnp.max(gl, axis=-1, keepdims=True)), axis=-1, keepdims=True)
        l_lo = jnp.sum(jnp.where(lane == e_lo[t] + N_GROUPS, logits, 0.0), axis=-1, keepdims=True)
        l_hi = jnp.sum(jnp.where(lane == e_hi[t] + N_GROUPS, logits, 0.0), axis=-1, keepdims=True)
        tail = jnp.exp(-jnp.abs(l_lo - l_hi))
        w_top = g_w / (1.0 + tail)
        w_oth = g_w * tail / (1.0 + tail)
        lo_top = l_lo >= l_hi
        y = x
        for slot, w in ((0, jnp.where(lo_top, w_top, w_oth)), (1, jnp.where(lo_top, w_oth, w_top))):
            gu = _dot(xb, wgu_scr[slot])
            hid = _silu(gu[:, 0:ff]) * gu[:, ff:] * w
            y = y + _dot(hid.astype(BF16), wd_scr[slot])
        _store_tok(out_ref, y)

    @pl.when(valid[t] == 0)
    def _():
        out_ref[...] = jnp.zeros(out_ref.shape, F32)


def _moe_sorted(xs, g, rw, rb, wg, wu, wd, layer, e_lo, e_hi, valid, tm):
    npad = xs.shape[0] // TOK_ROWS
    _, _, d, ff = wg.shape
    const = lambda t, lo, hi, v: (0, 0)
    at_lo = lambda t, lo, hi, v: (layer, lo[t], 0, 0)
    at_hi = lambda t, lo, hi, v: (layer, hi[t], 0, 0)
    return pl.pallas_call(
        _moe_sorted_kernel,
        grid_spec=pltpu.PrefetchScalarGridSpec(
            num_scalar_prefetch=3,
            grid=(npad // tm,),
            in_specs=[
                pl.BlockSpec((tm * TOK_ROWS, LANES), lambda t, lo, hi, v: (jnp.where(v[t] > 0, t, 0), 0)),
                pl.BlockSpec(g.shape, const), pl.BlockSpec(rw.shape, const), pl.BlockSpec(rb.shape, const),
                pl.BlockSpec((1, 1, d, ff), at_lo), pl.BlockSpec((1, 1, d, ff), at_lo),
                pl.BlockSpec((1, 1, ff, d), at_lo),
                pl.BlockSpec((1, 1, d, ff), at_hi), pl.BlockSpec((1, 1, d, ff), at_hi),
                pl.BlockSpec((1, 1, ff, d), at_hi),
            ],
            out_specs=pl.BlockSpec((tm * TOK_ROWS, LANES), lambda t, lo, hi, v: (t, 0)),
            scratch_shapes=[pltpu.VMEM((2, d, 2 * ff), BF16), pltpu.VMEM((2, ff, d), BF16)],
        ),
        out_shape=jax.ShapeDtypeStruct(xs.shape, F32),
        compiler_params=_params("arbitrary"),
        name="moe_sorted",
    )(e_lo, e_hi, valid, xs, g, rw, rb, wg, wu, wd, wg, wu, wd)


def _moe_sparse(x, g, rw, rb, wg, wu, wd, layer, tm, tr, toks, norm_g=None):
    n, d = x.shape
    cls, rank, counts = _moe_route(x, g, rw, rb, tr)
    cls, rank = cls.reshape(n), rank.reshape(n)
    cnt = counts[0, :N_CLASSES]
    padded = ((cnt + tm - 1) // tm) * tm
    ends = jnp.cumsum(padded)
    pos = (ends - padded)[cls] + rank
    n_tiles = n // tm + N_CLASSES
    tile_start = jnp.arange(n_tiles, dtype=jnp.int32) * tm
    n_valid = ends[-1] // tm
    unused = n_valid + jnp.arange(N_CLASSES, dtype=jnp.int32)
    tail = jnp.concatenate([jnp.where(cnt > 0, ends // tm - 1, -1),
                            jnp.where(unused < n_tiles, unused, -1)]).astype(jnp.int32)
    last_cls = jnp.sum((tile_start[jnp.maximum(n_valid - 1, 0)] >= ends).astype(jnp.int32))
    tile_cls = jnp.sum((tile_start[:, None] >= ends[None, :]).astype(jnp.int32), axis=1)
    valid = (tile_start < ends[-1]).astype(jnp.int32)
    tile_cls = jnp.where(valid > 0, tile_cls, last_cls)
    pair_lo = jnp.array([0, 0, 0, 1, 1, 2], jnp.int32)
    pair_hi = jnp.array([1, 2, 3, 2, 3, 3], jnp.int32)
    e_lo = (tile_cls // N_PAIRS) * EXP_PER_GROUP + pair_lo[tile_cls % N_PAIRS]
    e_hi = (tile_cls // N_PAIRS) * EXP_PER_GROUP + pair_hi[tile_cls % N_PAIRS]

    xs = _tok_scatter(x, n_tiles * tm, pos, tail, tm)
    ys = _moe_sorted(xs, g, rw.astype(BF16), rb, wg, wu, wd, layer, e_lo, e_hi, valid, tm)
    return _tok_gather(ys, pos, toks, norm_g)


def _moe_kernel(x_ref, g_ref, rw_ref, rb_ref, wg_ref, wu_ref, wd_ref, out_ref, xn_scr, comb_scr, acc_scr):
    e = pl.program_id(1)

    @pl.when(e == 0)
    def _():
        x = x_ref[...]
        xn = _rms(x, g_ref[...])
        xn_scr[...] = xn.astype(BF16)
        comb_scr[...] = _route(_dot(xn, rw_ref[...], HIGHEST) + rb_ref[...])
        acc_scr[...] = x

    xb = xn_scr[...]
    lane = lax.broadcasted_iota(jnp.int32, comb_scr.shape, 1)
    cmb = jnp.sum(jnp.where(lane == e + N_GROUPS, comb_scr[...], 0.0), axis=-1, keepdims=True)
    hid = _silu(_dot(xb, wg_ref[0, 0].astype(BF16))) * _dot(xb, wu_ref[0, 0].astype(BF16)) * cmb
    acc_scr[...] += _dot(hid.astype(BF16), wd_ref[0, 0].astype(BF16))

    @pl.when(e == pl.num_programs(1) - 1)
    def _():
        out_ref[...] = acc_scr[...]


def _moe(x, g, rw, rb, wg, wu, wd, layer, tm):
    n, d = x.shape
    _, ne, _, ff = wg.shape
    return pl.pallas_call(
        _moe_kernel,
        grid=(n // tm, ne),
        in_specs=[
            pl.BlockSpec((tm, d), lambda i, e: (i, 0)),
            _full(g.shape), _full(rw.shape), _full(rb.shape),
            pl.BlockSpec((1, 1, d, ff), lambda i, e: (layer, e, 0, 0)),
            pl.BlockSpec((1, 1, d, ff), lambda i, e: (layer, e, 0, 0)),
            pl.BlockSpec((1, 1, ff, d), lambda i, e: (layer, e, 0, 0)),
        ],
        out_specs=pl.BlockSpec((tm, d), lambda i, e: (i, 0)),
        out_shape=jax.ShapeDtypeStruct((n, d), F32),
        scratch_shapes=[pltpu.VMEM((tm, d), BF16), pltpu.VMEM((tm, LANES), F32), pltpu.VMEM((tm, d), F32)],
        compiler_params=_params("arbitrary", "arbitrary"),
        name="moe",
    )(x, g, rw, rb, wg, wu, wd)


def _norm_kernel(x_ref, g_ref, o_ref):
    o_ref[...] = _rms(x_ref[...], g_ref[...])


def _final_norm(x, g, tm):
    n, d = x.shape
    return pl.pallas_call(
        _norm_kernel,
        grid=(n // tm,),
        in_specs=[pl.BlockSpec((tm, d), lambda i: (i, 0)), _full(g.shape)],
        out_specs=pl.BlockSpec((tm, d), lambda i: (i, 0)),
        out_shape=jax.ShapeDtypeStruct((n, d), F32),
        compiler_params=_params("arbitrary"),
        name="final_norm",
    )(x, g)


def _mix_sample_in_kernel(x_ref, g_ref, win_ref, wlr_ref, gup_ref, gb_ref, cw_ref, b0_ref, b1_ref,
                          yc_ref, u_ref, q_ref, k_ref, v_ref, gate_ref, la_ref):
    hb = _rms(x_ref[...], g_ref[...]).astype(BF16)

    def proj(a, b):
        return _dot(hb, win_ref[:, a:b])

    u = proj(C_CC, C_CH) * proj(C_CH, C_Q)
    cw = cw_ref[...]
    yc_ref[...] = proj(C_CB, C_CC) * (b0_ref[...] * cw[0:1] + b1_ref[...] * cw[1:2] + u * cw[2:3])
    u_ref[...] = u
    q_ref[...] = proj(C_Q, C_K) * (GLA_DK ** -0.5)
    k_ref[...] = proj(C_K, C_V)
    v_ref[...] = proj(C_V, C_G)
    gate_ref[...] = proj(C_G, C_LR)
    lr = _dot(hb, wlr_ref[...])
    gate = _dot(lr.astype(BF16), gup_ref[...]) + gb_ref[...]
    la_ref[...] = _log_sigmoid(gate) * (1.0 / GLA_TAU)


def _mix_sample_in(x, g, win, wlr, gup, gb, cw, b0, b1):
    n = x.shape[0]
    args = (x, g, win, wlr, gup, gb, cw, b0, b1)
    widths = (CONV_W, CONV_W, GLA_K, GLA_K, GLA_V, GLA_V, GLA_K)
    return pl.pallas_call(
        _mix_sample_in_kernel,
        grid=(1,),
        in_specs=[_full(a.shape) for a in args],
        out_specs=[_full((n, w)) for w in widths],
        out_shape=[jax.ShapeDtypeStruct((n, w), F32) for w in widths],
        compiler_params=_params("arbitrary"),
        name="mix_sample_in",
    )(*args)


def _gla_step_kernel(q_ref, k_ref, v_ref, la_ref, s0_ref, s_ref, o_ref):
    nb = q_ref.shape[0]
    a = jnp.exp(la_ref[...])
    kb = k_ref[...].astype(BF16)
    qb = q_ref[...].astype(BF16)
    vf = v_ref[...].astype(BF16).astype(F32)
    rows = lax.broadcasted_iota(jnp.int32, (nb, 1), 0)
    spread = (lax.broadcasted_iota(jnp.int32, (nb, nb * GLA_DV), 0)
              == lax.broadcasted_iota(jnp.int32, (nb, nb * GLA_DV), 1) // GLA_DV)
    spread_b = spread.astype(BF16)
    for h in range(GLA_HEADS):
        ks = slice(h * GLA_DK, (h + 1) * GLA_DK)
        a_cols = _dot_tn(a[:, ks], spread.astype(F32), HIGHEST)
        k_cols = _dot_tn(kb[:, ks], spread_b)
        q_cols = _dot_tn(qb[:, ks], spread_b)
        o_h = jnp.zeros((nb, GLA_DV), F32)
        for n in range(nb):
            blk = slice(n * GLA_DV, (n + 1) * GLA_DV)
            v_row = vf[n:n + 1, h * GLA_DV:(h + 1) * GLA_DV]
            s_new = a_cols[:, blk] * s0_ref[0, n, h] + k_cols[:, blk] * v_row
            s_ref[n, h] = s_new
            o_h = jnp.where(rows == n, jnp.sum(q_cols[:, blk] * s_new, axis=0, keepdims=True), o_h)
        o_ref[:, h * GLA_DV:(h + 1) * GLA_DV] = o_h


def _gla_step(q, k, v, la, state, layer, nb):
    n = q.shape[0]
    sshape = (GLA_HEADS, GLA_DK, GLA_DV)
    return pl.pallas_call(
        _gla_step_kernel,
        grid=(n // nb,),
        in_specs=[
            pl.BlockSpec((nb, GLA_K), lambda i: (i, 0)),
            pl.BlockSpec((nb, GLA_K), lambda i: (i, 0)),
            pl.BlockSpec((nb, GLA_V), lambda i: (i, 0)),
            pl.BlockSpec((nb, GLA_K), lambda i: (i, 0)),
            pl.BlockSpec((1, nb) + sshape, lambda i: (layer, i, 0, 0, 0)),
        ],
        out_specs=[
            pl.BlockSpec((nb,) + sshape, lambda i: (i, 0, 0, 0)),
            pl.BlockSpec((nb, GLA_V), lambda i: (i, 0)),
        ],
        out_shape=[jax.ShapeDtypeStruct((n,) + sshape, F32), jax.ShapeDtypeStruct((n, GLA_V), F32)],
        compiler_params=_params("arbitrary"),
        name="gla_step",
    )(q, k, v, la, state)


def _mix_sample_out_kernel(x_ref, yc_ref, o_ref, gate_ref, gg_ref, wout_ref, gm_ref, wq_ref, x1_ref, q_ref):
    yg = _head_norm_gate(o_ref[...], gate_ref[...], gg_ref[...])
    y = _dot(yc_ref[...].astype(BF16), wout_ref[0:CONV_W, :]) + _dot(yg.astype(BF16), wout_ref[CONV_W:, :])
    x1 = x_ref[...] + y
    x1_ref[...] = x1
    q_ref[...] = _dot(_rms(x1, gm_ref[...]).astype(BF16), wq_ref[...])


def _mix_sample_out(x, yc, o, gate, gg, wout, gm, wq):
    args = (x, yc, o, gate, gg, wout, gm, wq)
    return pl.pallas_call(
        _mix_sample_out_kernel,
        grid=(1,),
        in_specs=[_full(a.shape) for a in args],
        out_specs=[_full(x.shape)] * 2,
        out_shape=[jax.ShapeDtypeStruct(x.shape, F32)] * 2,
        compiler_params=_params("arbitrary"),
        name="mix_sample_out",
    )(*args)


ATT_ROWS = 2 * MEM_HEADS


def _class_allreduce(x, op):
    n = x.shape[-1]
    shift = ATT_ROWS
    while shift < n:
        x = op(x, pltpu.roll(x, shift, axis=1))
        shift *= 2
    return x


def _att_sample_kernel(q_ref, k_ref, v_ref, o_ref):
    nb = q_ref.shape[0]
    ncol = k_ref.shape[2]
    diag = (lax.broadcasted_iota(jnp.int32, (ATT_ROWS, ncol), 0)
            == (lax.broadcasted_iota(jnp.int32, (ATT_ROWS, ncol), 1) & (ATT_ROWS - 1)))
    rows = lax.broadcasted_iota(jnp.int32, (nb, 1), 0)
    t = jnp.zeros((nb, ncol), F32)
    for n in range(nb):
        sc = _dot_nt(q_ref[n].astype(BF16), k_ref[0, n].astype(BF16))
        t = t + jnp.where(rows == n, jnp.sum(jnp.where(diag, sc, 0.0), axis=0, keepdims=True), 0.0)
    valid = (lax.broadcasted_iota(jnp.int32, (nb, ncol), 1) & (ATT_ROWS - 1)) < MEM_HEADS
    s = jnp.where(valid, (t + pltpu.roll(t, ncol - MEM_HEADS, axis=1)) * (MEM_DH ** -0.5), 0.0)
    e = jnp.where(valid, jnp.exp(s - _class_allreduce(s, jnp.maximum)), 0.0)
    den = jnp.where(valid, _class_allreduce(e, jnp.add), 1.0)
    p = e / den
    p = p + pltpu.roll(p, MEM_HEADS, axis=1)
    for n in range(nb):
        p_n = jnp.where(diag, jnp.broadcast_to(p[n:n + 1, :], (ATT_ROWS, ncol)), 0.0)
        o_ref[n] = _dot(p_n.astype(BF16), v_ref[0, n].astype(BF16))


def _att_sample(q, ck, cv, layer, nb):
    n = q.shape[0]
    ncol = ck.shape[2]
    return pl.pallas_call(
        _att_sample_kernel,
        grid=(n // nb,),
        in_specs=[
            pl.BlockSpec((nb, ATT_ROWS, LANES), lambda i: (i, 0, 0)),
            pl.BlockSpec((1, nb, ncol, LANES), lambda i: (layer, i, 0, 0)),
            pl.BlockSpec((1, nb, ncol, LANES), lambda i: (layer, i, 0, 0)),
        ],
        out_specs=pl.BlockSpec((nb, ATT_ROWS, LANES), lambda i: (i, 0, 0)),
        out_shape=jax.ShapeDtypeStruct((n, ATT_ROWS, LANES), F32),
        compiler_params=_params("arbitrary"),
        name="att_sample",
    )(q, ck, cv)


def _oproj_kernel(x_ref, o_ref, wo_ref, out_ref):
    out_ref[...] = x_ref[...] + _dot(o_ref[...].astype(BF16), wo_ref[...])


def _oproj(x, o, wo):
    return pl.pallas_call(
        _oproj_kernel,
        grid=(1,),
        in_specs=[_full(x.shape), _full(o.shape), _full(wo.shape)],
        out_specs=_full(x.shape),
        out_shape=jax.ShapeDtypeStruct(x.shape, F32),
        compiler_params=_params("arbitrary"),
        name="oproj",
    )(x, o, wo)


def kernel(x_prompt, x_sample, state_conv, state_gla, cache_mem_k, cache_mem_v, mem_prompt, norm_mix, w_in, conv_w, gla_gate_up, gla_gate_b, gla_out_norm, w_out, norm_mem, w_q, w_k, w_v, w_o, norm_ffn, router_group, router_group_b, router_expert, router_expert_b, w_gate, w_up, w_down, norm_final):
    depth = w_in.shape[0]
    bsz, t, d = x_prompt.shape
    ns = x_sample.shape[0]
    nm = mem_prompt.shape[1]
    n_tok = bsz * t

    tt = min(512, t)
    tq = min(512, t)
    tm_moe = min(1024, n_tok)
    tm_kv = min(512, bsz * nm)
    tm_sorted = 256
    tr_route = min(1024, n_tok)
    rows_perm = min(512, n_tok)
    nb_gla = min(8, ns)
    nb_att = min(8, ns)

    row = lambda a: a.reshape(1, -1)
    mem2 = mem_prompt.reshape(bsz * nm, d)

    def tile_rows(c):
        c = c.reshape(depth, ns, nm, MEM_HEADS, 2, LANES).transpose(0, 1, 2, 4, 3, 5)
        return c.reshape(depth, ns, nm * ATT_ROWS, LANES)

    ck, cv = tile_rows(cache_mem_k), tile_rows(cache_mem_v)

    def untile_rows(c):
        c = c.reshape(depth, bsz, nm, 2, MEM_HEADS, LANES).transpose(0, 1, 2, 4, 3, 5)
        return c.reshape(depth, bsz, nm, MEM_HEADS, MEM_DH)

    mk_all, mv_all = _mem_kv(mem2, w_k.astype(BF16), w_v.astype(BF16), tm_kv)

    xp = x_prompt
    xs = x_sample.reshape(ns, d)
    conv_p, gla_p, conv_s, gla_s = [], [], [], []
    for l in range(depth):
        win = w_in[l, :, :C_LR].astype(BF16)
        wlr = jnp.pad(w_in[l, :, C_LR:], ((0, 0), (0, LANES - GLA_RANK))).astype(BF16)
        gup = jnp.pad(gla_gate_up[l], ((0, LANES - GLA_RANK), (0, 0))).astype(BF16)
        gb = row(gla_gate_b[l])
        gg = row(gla_out_norm[l])
        wout = w_out[l].astype(BF16)
        wq, wo = w_q[l].astype(BF16), w_o[l].astype(BF16)
        rw = jnp.concatenate([router_group[l], router_expert[l].transpose(1, 0, 2).reshape(d, N_EXPERTS)], axis=1)
        rw = jnp.pad(rw, ((0, 0), (0, LANES - rw.shape[1])))
        rb = jnp.concatenate([router_group_b[l], router_expert_b[l].reshape(-1)])
        rb = row(jnp.pad(rb, (0, LANES - rb.shape[0])))
        moe_w = (row(norm_ffn[l]), rw, rb, w_gate, w_up, w_down, l)

        xp, nbuf, ns_p = _mix_prompt(xp, (bsz, t, d), row(norm_mix[l]), win, wlr, gup, gb, conv_w[l], gg, wout, tt)
        conv_p.append(nbuf)
        gla_p.append(ns_p)
        xp = _att_prompt(xp, row(norm_mem[l]), wq, wo, mk_all, mv_all, l, tq)
        xp = _moe_sparse(xp.reshape(n_tok, d), *moe_w, tm_sorted, tr_route, rows_perm,
                         norm_g=row(norm_final) if l == depth - 1 else None)

        yc, u, q, k, v, gate, la = _mix_sample_in(
            xs, row(norm_mix[l]), win, wlr, gup, gb, conv_w[l], state_conv[l, :, 0], state_conv[l, :, 1])
        conv_s.append(jnp.stack([state_conv[l, :, 1], u], axis=1))
        s_new, o = _gla_step(q, k, v, la, state_gla, l, nb_gla)
        gla_s.append(s_new)
        wq_s = wq.reshape(d, MEM_HEADS, 2, LANES).transpose(0, 2, 1, 3).reshape(d, d)
        wo_s = wo.reshape(MEM_HEADS, 2, LANES, d).transpose(1, 0, 2, 3).reshape(d, d)
        xs, qa = _mix_sample_out(xs, yc, o, gate, gg, wout, row(norm_mem[l]), wq_s)
        oa = _att_sample(qa.reshape(ns, ATT_ROWS, LANES), ck, cv, l, nb_att)
        xs = _oproj(xs, oa.reshape(ns, d), wo_s)
        xs = _moe(xs, *moe_w, min(tm_moe, ns))

    y_prompt = xp.reshape(bsz, t, d)
    y_sample = _final_norm(xs, row(norm_final), ns).reshape(ns, 1, d)
    return (y_prompt, y_sample, jnp.stack(conv_p), jnp.stack(gla_p), untile_rows(mk_all), untile_rows(mv_all),
            jnp.stack(conv_s), jnp.stack(gla_s))
```

```python
import functools

import jax
import jax.numpy as jnp
from jax import lax
from jax.experimental import pallas as pl
from jax.experimental.pallas import tpu as pltpu

F32 = jnp.float32
BF16 = jnp.bfloat16
HIGHEST = lax.Precision.HIGHEST

EPS = 1e-6
CONV_W = 512
GLA_HEADS = 4
GLA_DK = 64
GLA_DV = 128
GLA_K = GLA_HEADS * GLA_DK
GLA_V = GLA_HEADS * GLA_DV
GLA_RANK = 16
GLA_TAU = 16.0
GLA_CHUNK = 64
MEM_HEADS = 4
MEM_DH = 256
N_GROUPS = 4
EXP_PER_GROUP = 4
N_EXPERTS = 16
LANES = 128
C_CB, C_CC, C_CH, C_Q, C_K, C_V, C_G, C_LR = 0, 512, 1024, 1536, 1792, 2048, 2560, 3072
VMEM_LIMIT = 52 * 1024 * 1024


def _params(*sem):
    return pltpu.CompilerParams(dimension_semantics=sem, vmem_limit_bytes=VMEM_LIMIT)


def _rms(x, g):
    return x * lax.rsqrt(jnp.mean(x * x, axis=-1, keepdims=True) + EPS) * g


def _dot(a, b, precision=None):
    return jnp.dot(a, b, precision=precision, preferred_element_type=F32)


def _dot_nt(a, b):
    return lax.dot_general(a, b, (((1,), (1,)), ((), ())), preferred_element_type=F32)


def _dot_tn(a, b, precision=None):
    return lax.dot_general(a, b, (((0,), (0,)), ((), ())), precision=precision,
                           preferred_element_type=F32)


def _dot_split(dot, mask, x):
    hi = x.astype(BF16)
    r1 = x - hi.astype(F32)
    mid = r1.astype(BF16)
    lo = (r1 - mid.astype(F32)).astype(BF16)
    return dot(mask, hi) + dot(mask, mid) + dot(mask, lo)


def _silu(x):
    return x / (1.0 + jnp.exp(-x))


def _log_sigmoid(x):
    return jnp.minimum(x, 0.0) - jnp.log1p(jnp.exp(-jnp.abs(x)))


def _head_norm_gate(o, g, gg):
    parts = []
    for h in range(GLA_HEADS):
        sl = slice(h * GLA_DV, (h + 1) * GLA_DV)
        parts.append(_rms(o[:, sl], gg[:, sl]))
    return jnp.concatenate(parts, axis=-1) * _silu(g)


def _full(shape):
    nd = len(shape)
    return pl.BlockSpec(shape, lambda *_: (0,) * nd)


def _mix_prompt_kernel(x_ref, g_ref, win_ref, wlr_ref, gup_ref, gb_ref, cw_ref, gg_ref, wout_ref,
                       x1_ref, conv_ref, s_ref, ubuf, s_scr, o_scr, *, tok_in):
    t = pl.program_id(1)
    tt = x1_ref.shape[1]

    @pl.when(t == 0)
    def _():
        ubuf[0:8, :] = jnp.zeros((8, CONV_W), F32)
        s_scr[...] = jnp.zeros(s_scr.shape, F32)

    x = _load_tok(x_ref, tt) if tok_in else x_ref[0]
    hb = _rms(x, g_ref[...]).astype(BF16)

    def proj(a, b):
        return _dot(hb, win_ref[:, a:b])

    u = proj(C_CC, C_CH) * proj(C_CH, C_Q)
    ubuf[8:8 + tt, :] = u
    cw = cw_ref[...]
    yc = proj(C_CB, C_CC) * (ubuf[6:6 + tt, :] * cw[0:1] + ubuf[7:7 + tt, :] * cw[1:2] + u * cw[2:3])
    ubuf[6:8, :] = u[tt - 2:tt, :]

    qs = proj(C_Q, C_K) * (GLA_DK ** -0.5)
    k = proj(C_K, C_V)
    v = proj(C_V, C_G)
    lr = _dot(hb, wlr_ref[...])
    gate = _dot(lr.astype(BF16), gup_ref[...]) + gb_ref[...]
    la = _log_sigmoid(gate) * (1.0 / GLA_TAU)

    c = GLA_CHUNK
    nc = tt // c
    iota = lambda shape, dim: lax.broadcasted_iota(jnp.int32, shape, dim)
    ltri = (iota((c, c), 0) >= iota((c, c), 1)).astype(BF16)
    b_wide = _dot_split(_dot, ltri, jnp.concatenate([la[j * c:(j + 1) * c] for j in range(nc)], axis=1))
    b_ends = jnp.concatenate([b_wide[c - 1:c, j * GLA_K:(j + 1) * GLA_K] for j in range(nc)], axis=0)
    spread = (iota((nc, nc * GLA_DV), 0) == iota((nc, nc * GLA_DV), 1) // GLA_DV).astype(BF16)
    dec_all = jnp.exp(_dot_split(lambda m, x: _dot_tn(x, m), spread, b_ends))
    head_feat = iota((GLA_K, GLA_K), 0) // c == iota((GLA_K, GLA_K), 1) // GLA_DK
    head_blk = iota((GLA_K, GLA_V), 0) // GLA_DK == iota((GLA_K, GLA_V), 1) // GLA_DV
    causal = iota((c, GLA_K), 0) >= iota((c, GLA_K), 1) % c
    for j in range(nc):
        r = slice(j * c, (j + 1) * c)
        b_c, q_c, k_c, v_c = b_wide[:, j * GLA_K:(j + 1) * GLA_K], qs[r], k[r], v[r]
        b_mid = b_c[c // 2:c // 2 + 1, :]
        b_last = b_c[c - 1:c, :]
        q_i = (q_c * jnp.exp(b_c - b_mid)).astype(BF16)
        k_i = k_c * jnp.exp(b_mid - b_c)
        k_dec = (k_c * jnp.exp(b_last - b_c)).astype(BF16)
        q_b = (q_c * jnp.exp(b_c)).astype(BF16)
        k_rows = jnp.where(head_feat, jnp.concatenate([k_i] * GLA_HEADS, axis=0), 0.0).astype(BF16)
        a = jnp.where(causal, _dot_nt(q_i, k_rows), 0.0).astype(BF16)
        v_blk = jnp.where(head_blk, jnp.concatenate([v_c] * GLA_HEADS, axis=0), 0.0).astype(BF16)
        s_prev = s_scr[...]
        o_scr[r, :] = _dot(jnp.concatenate([a, q_b], axis=1),
                           jnp.concatenate([v_blk, s_prev.astype(BF16)], axis=0))
        dec = dec_all[:, j * GLA_DV:(j + 1) * GLA_DV]
        s_scr[...] = (jnp.concatenate([dec] * GLA_HEADS, axis=1) * s_prev
                      + jnp.where(head_blk, _dot_tn(k_dec, v_c.astype(BF16)), 0.0))

    yg = _head_norm_gate(o_scr[...], proj(C_G, C_LR), gg_ref[...])
    y = _dot(yc.astype(BF16), wout_ref[0:CONV_W, :]) + _dot(yg.astype(BF16), wout_ref[CONV_W:, :])
    x1_ref[0] = x + y

    @pl.when(t == pl.num_programs(1) - 1)
    def _():
        conv_ref[0] = u[tt - 2:tt, :]
        for h in range(GLA_HEADS):
            s_ref[0, h] = s_scr[h * GLA_DK:(h + 1) * GLA_DK, h * GLA_DV:(h + 1) * GLA_DV]


def _mix_prompt(x, shape, g, win, wlr, gup, gb, cw, gg, wout, tt):
    bsz, t, d = shape
    tok_in = x.ndim == 2
    nt = t // tt
    x_spec = (pl.BlockSpec((tt * TOK_ROWS, LANES), lambda b, i: (b * nt + i, 0)) if tok_in
              else pl.BlockSpec((1, tt, d), lambda b, i: (b, i, 0)))
    return pl.pallas_call(
        functools.partial(_mix_prompt_kernel, tok_in=tok_in),
        grid=(bsz, nt),
        in_specs=[
            x_spec,
            _full(g.shape), _full(win.shape), _full(wlr.shape), _full(gup.shape), _full(gb.shape),
            _full(cw.shape), _full(gg.shape), _full(wout.shape),
        ],
        out_specs=[
            pl.BlockSpec((1, tt, d), lambda b, i: (b, i, 0)),
            pl.BlockSpec((1, 2, CONV_W), lambda b, i: (b, 0, 0)),
            pl.BlockSpec((1, GLA_HEADS, GLA_DK, GLA_DV), lambda b, i: (b, 0, 0, 0)),
        ],
        out_shape=[
            jax.ShapeDtypeStruct((bsz, t, d), F32),
            jax.ShapeDtypeStruct((bsz, 2, CONV_W), F32),
            jax.ShapeDtypeStruct((bsz, GLA_HEADS, GLA_DK, GLA_DV), F32),
        ],
        scratch_shapes=[
            pltpu.VMEM((8 + tt, CONV_W), F32),
            pltpu.VMEM((GLA_K, GLA_V), F32),
            pltpu.VMEM((tt, GLA_V), F32),
        ],
        compiler_params=_params("arbitrary", "arbitrary"),
        name="mix_prompt",
    )(x, g, win, wlr, gup, gb, cw, gg, wout)


def _kv_kernel(m_ref, wk_ref, wv_ref, k_ref, v_ref):
    tm = m_ref.shape[0]
    mb = m_ref[...].astype(BF16)
    for w_ref, o_ref in ((wk_ref, k_ref), (wv_ref, v_ref)):
        y = _dot(mb, w_ref[0])
        for h in range(MEM_HEADS):
            for dt in range(MEM_DH // LANES):
                col = h * MEM_DH + dt * LANES
                o_ref[0, pl.ds(dt * MEM_HEADS + h, tm, stride=ATT_ROWS), :] = y[:, col:col + LANES]


def _mem_kv(mem, wk, wv, tm):
    n, d = mem.shape
    depth = wk.shape[0]
    w_spec = pl.BlockSpec((1, d, d), lambda l, i: (l, 0, 0))
    o_spec = pl.BlockSpec((1, tm * ATT_ROWS, LANES), lambda l, i: (l, i, 0))
    return pl.pallas_call(
        _kv_kernel,
        grid=(depth, n // tm),
        in_specs=[pl.BlockSpec((tm, d), lambda l, i: (i, 0)), w_spec, w_spec],
        out_specs=[o_spec] * 2,
        out_shape=[jax.ShapeDtypeStruct((depth, n * ATT_ROWS, LANES), F32)] * 2,
        compiler_params=_params("arbitrary", "arbitrary"),
        name="mem_kv",
    )(mem, wk, wv)


def _att_prompt_kernel(x_ref, g_ref, wq_ref, wo_ref, mk_ref, mv_ref, gf_ref, rwh_ref, rwl_ref, rb_ref, ltri_ref,
                       out_ref, cls_ref, rank_ref, cnt_ref, carry):
    @pl.when((pl.program_id(0) == 0) & (pl.program_id(1) == 0))
    def _():
        carry[...] = jnp.zeros(carry.shape, F32)

    x = x_ref[0]
    xb = _rms(x, g_ref[...]).astype(BF16)
    q = _dot(xb, wq_ref[...])
    nm = mk_ref.shape[1] // ATT_ROWS

    def head_rows(ref, h):
        return jnp.concatenate([ref[0, pl.ds(dt * MEM_HEADS + h, nm, stride=ATT_ROWS), :]
                                for dt in range(MEM_DH // LANES)], axis=-1).astype(BF16)

    outs = []
    for h in range(MEM_HEADS):
        sl = slice(h * MEM_DH, (h + 1) * MEM_DH)
        s = _dot_nt(q[:, sl].astype(BF16), head_rows(mk_ref, h)) * (MEM_DH ** -0.5)
        e = jnp.exp(s - jnp.max(s, axis=-1, keepdims=True))
        p = e / jnp.sum(e, axis=-1, keepdims=True)
        outs.append(_dot(p.astype(BF16), head_rows(mv_ref, h)))
    o = jnp.concatenate(outs, axis=-1)
    x2 = x + _dot(o.astype(BF16), wo_ref[...])
    _store_tok(out_ref, x2)
    _route_meta(x2, gf_ref, rwh_ref, rwl_ref, rb_ref, ltri_ref, carry, cls_ref, rank_ref, cnt_ref)


def _att_prompt(x, g, wq, wo, mk, mv, layer, gf, rw, rb, tq):
    bsz, t, d = x.shape
    assert d == TOK_ROWS * LANES
    nq = t // tq
    rows = mk.shape[1] // bsz
    kv_spec = pl.BlockSpec((1, rows, LANES), lambda b, i: (layer, b, 0))
    meta_spec = pl.BlockSpec((tq // LANES, LANES), lambda b, i: (b * nq + i, 0))
    rw_hi = rw.astype(BF16)
    rw_lo = (rw - rw_hi.astype(F32)).astype(BF16)
    ltri = jnp.tril(jnp.ones((tq, tq), BF16))
    n_tok = bsz * t
    return pl.pallas_call(
        _att_prompt_kernel,
        grid=(bsz, nq),
        in_specs=[
            pl.BlockSpec((1, tq, d), lambda b, i: (b, i, 0)),
            _full(g.shape), _full(wq.shape), _full(wo.shape),
            kv_spec, kv_spec,
            _full(gf.shape), _full(rw_hi.shape), _full(rw_lo.shape), _full(rb.shape), _full(ltri.shape),
        ],
        out_specs=[
            pl.BlockSpec((tq * TOK_ROWS, LANES), lambda b, i: (b * nq + i, 0)),
            meta_spec, meta_spec, _full((1, LANES)),
        ],
        out_shape=[
            jax.ShapeDtypeStruct((n_tok * TOK_ROWS, LANES), F32),
            jax.ShapeDtypeStruct((n_tok // LANES, LANES), jnp.int32),
            jax.ShapeDtypeStruct((n_tok // LANES, LANES), jnp.int32),
            jax.ShapeDtypeStruct((1, LANES), jnp.int32),
        ],
        scratch_shapes=[pltpu.VMEM((1, LANES), F32)],
        compiler_params=_params("arbitrary", "arbitrary"),
        name="att_prompt",
    )(x, g, wq, wo, mk, mv, gf, rw_hi, rw_lo, rb, ltri)


def _route_top2(logits):
    lane = lax.broadcasted_iota(jnp.int32, logits.shape, 1)
    lanef = lane.astype(F32)
    ninf = -jnp.inf
    big = 1e9
    gl = jnp.where(lane < N_GROUPS, logits, ninf)
    gmax = jnp.max(gl, axis=-1, keepdims=True)
    g_idx = jnp.min(jnp.where(gl == gmax, lanef, big), axis=-1, keepdims=True)
    g_w = 1.0 / jnp.sum(jnp.exp(gl - gmax), axis=-1, keepdims=True)
    grp = ((lane - N_GROUPS) >> 2).astype(F32)
    emask = (lane >= N_GROUPS) & (lane < N_GROUPS + N_EXPERTS) & (grp == g_idx)
    el = jnp.where(emask, logits, ninf)
    m1 = jnp.max(el, axis=-1, keepdims=True)
    i1 = jnp.min(jnp.where(el == m1, lanef, big), axis=-1, keepdims=True)
    el2 = jnp.where(lanef == i1, ninf, el)
    m2 = jnp.max(el2, axis=-1, keepdims=True)
    i2 = jnp.min(jnp.where(el2 == m2, lanef, big), axis=-1, keepdims=True)
    tail = jnp.exp(m2 - m1)
    w1 = g_w / (1.0 + tail)
    w2 = g_w * tail / (1.0 + tail)
    return g_idx, i1, i2, w1, w2


def _route(logits):
    _, i1, i2, w1, w2 = _route_top2(logits)
    lanef = lax.broadcasted_iota(jnp.int32, logits.shape, 1).astype(F32)
    return jnp.where(lanef == i1, w1, 0.0) + jnp.where(lanef == i2, w2, 0.0)


N_PAIRS = 6
N_CLASSES = N_GROUPS * N_PAIRS
TOK_ROWS = 8


def _load_tok(ref, n):
    return jnp.concatenate([ref[pl.ds(j, n, stride=TOK_ROWS), :] for j in range(TOK_ROWS)], axis=-1)


def _store_tok(ref, val):
    n = val.shape[0]
    for j in range(TOK_ROWS):
        ref[pl.ds(j, n, stride=TOK_ROWS), :] = val[:, j * LANES:(j + 1) * LANES]


def _lane_dense(col):
    eye = lax.broadcasted_iota(jnp.int32, (LANES, LANES), 0) == lax.broadcasted_iota(jnp.int32, (LANES, LANES), 1)
    rows = [jnp.sum(jnp.where(eye, col[b * LANES:(b + 1) * LANES], 0.0), axis=0, keepdims=True)
            for b in range(col.shape[0] // LANES)]
    return jnp.concatenate(rows, axis=0)


def _route_meta(x, g_ref, rwh_ref, rwl_ref, rb_ref, ltri_ref, carry, cls_ref, rank_ref, cnt_ref):
    n = x.shape[0]
    xn = _rms(x, g_ref[...])
    x_hi = xn.astype(BF16)
    x_lo = (xn - x_hi.astype(F32)).astype(BF16)
    logits = (_dot(x_hi, rwh_ref[...]) + (_dot(x_lo, rwh_ref[...]) + _dot(x_hi, rwl_ref[...]))) + rb_ref[...]
    g_idx, i1, i2, _, _ = _route_top2(logits)
    lo = jnp.minimum(i1, i2) - N_GROUPS - EXP_PER_GROUP * g_idx
    hi = jnp.maximum(i1, i2) - N_GROUPS - EXP_PER_GROUP * g_idx
    cls = g_idx * N_PAIRS + lo * (7.0 - lo) * 0.5 + hi - lo - 1.0

    lane = lax.broadcasted_iota(jnp.int32, (n, LANES), 1)
    onehot = lane.astype(F32) == cls
    prefix = _dot(ltri_ref[...], onehot.astype(BF16)) + carry[...]
    rank = jnp.sum(jnp.where(onehot, prefix, 0.0), axis=-1, keepdims=True) - 1.0
    carry[...] = prefix[n - 1:n, :]
    cls_ref[...] = _lane_dense(cls).astype(jnp.int32)
    rank_ref[...] = _lane_dense(rank).astype(jnp.int32)
    cnt_ref[...] = prefix[n - 1:n, :].astype(jnp.int32)


DMA_UNROLL = 8


def _tok_rows(t):
    return pl.ds(pl.multiple_of(t * TOK_ROWS, TOK_ROWS), TOK_ROWS)


def _tok_scatter_kernel(pos, tail, x_ref, dst_ref, stage, sems, *, toks, steps):
    i = pl.program_id(0)
    slot = i % 2
    base = i * toks
    rows = toks * TOK_ROWS

    def wait_slot(s):
        pltpu.make_async_copy(stage.at[s], dst_ref.at[pl.ds(0, rows)], sems.at[s]).wait()

    @pl.when(i == 0)
    def _():
        stage[1] = jnp.zeros((rows, LANES), F32)

        def fill(c):
            return pltpu.make_async_copy(stage.at[1], dst_ref.at[pl.ds(pl.multiple_of(tail[c] * rows, rows), rows)],
                                         sems.at[1])

        for c in range(2 * N_CLASSES):
            @pl.when(tail[c] >= 0)
            def _():
                fill(c).start()
        for c in range(2 * N_CLASSES):
            @pl.when(tail[c] >= 0)
            def _():
                fill(c).wait()

    @pl.when(i >= 2)
    def _():
        wait_slot(slot)

    stage[slot] = x_ref[...]

    def start(r8, c):
        for u in range(DMA_UNROLL):
            r = r8 * DMA_UNROLL + u
            pltpu.make_async_copy(stage.at[slot, _tok_rows(r)], dst_ref.at[_tok_rows(pos[base + r])],
                                  sems.at[slot]).start(priority=u % 2)
        return c

    lax.fori_loop(0, toks // DMA_UNROLL, start, 0)

    @pl.when(i == steps - 1)
    def _():
        wait_slot(slot)
        if steps >= 2:
            wait_slot(1 - slot)


def _tok_scatter(x, n_out, pos, tail, toks):
    steps = x.shape[0] // (toks * TOK_ROWS)
    return pl.pallas_call(
        functools.partial(_tok_scatter_kernel, toks=toks, steps=steps),
        grid_spec=pltpu.PrefetchScalarGridSpec(
            num_scalar_prefetch=2,
            grid=(steps,),
            in_specs=[pl.BlockSpec((toks * TOK_ROWS, LANES), lambda i, p, tl: (i, 0))],
            out_specs=pl.BlockSpec(memory_space=pl.ANY),
            scratch_shapes=[pltpu.VMEM((2, toks * TOK_ROWS, LANES), F32), pltpu.SemaphoreType.DMA((2,))],
        ),
        out_shape=jax.ShapeDtypeStruct((n_out * TOK_ROWS, LANES), F32),
        compiler_params=_params("arbitrary"),
        name="tok_scatter",
    )(pos, tail, x)


def _tok_gather_kernel(pos, src_ref, *rest, toks, norm):
    if norm:
        g_ref, out_ref, stage, sem = rest
        dst = stage
    else:
        out_ref, sem = rest
        dst = out_ref
    base = pl.program_id(0) * toks

    def start(r8, c):
        for u in range(DMA_UNROLL):
            r = r8 * DMA_UNROLL + u
            pltpu.make_async_copy(src_ref.at[_tok_rows(pos[base + r])], dst.at[_tok_rows(r)],
                                  sem).start(priority=u % 2)
        return c

    lax.fori_loop(0, toks // DMA_UNROLL, start, 0)
    pltpu.make_async_copy(src_ref.at[pl.ds(0, toks * TOK_ROWS)], dst, sem).wait()
    if norm:
        out_ref[...] = _rms(_load_tok(stage, toks), g_ref[...])


def _tok_gather(src, pos, toks, norm_g=None):
    n = pos.shape[0]
    norm = norm_g is not None
    d = TOK_ROWS * LANES
    return pl.pallas_call(
        functools.partial(_tok_gather_kernel, toks=toks, norm=norm),
        grid_spec=pltpu.PrefetchScalarGridSpec(
            num_scalar_prefetch=1,
            grid=(n // toks,),
            in_specs=[pl.BlockSpec(memory_space=pl.ANY)]
            + ([pl.BlockSpec(norm_g.shape, lambda i, p: (0, 0))] if norm else []),
            out_specs=(pl.BlockSpec((toks, d), lambda i, p: (i, 0)) if norm
                       else pl.BlockSpec((toks * TOK_ROWS, LANES), lambda i, p: (i, 0))),
            scratch_shapes=([pltpu.VMEM((toks * TOK_ROWS, LANES), F32)] if norm else [])
            + [pltpu.SemaphoreType.DMA(())],
        ),
        out_shape=jax.ShapeDtypeStruct((n, d) if norm else (n * TOK_ROWS, LANES), F32),
        compiler_params=_params("arbitrary"),
        name="tok_gather",
    )(pos, src, *([norm_g] if norm else []))


def _moe_sorted_kernel(e_lo, e_hi, valid, xs_ref, g_ref, rw_ref, rb_ref,
                       wg_lo, wu_lo, wd_lo, wg_hi, wu_hi, wd_hi, out_ref, wgu_scr, wd_scr):
    t = pl.program_id(0)
    tm = xs_ref.shape[0] // TOK_ROWS
    ff = wd_lo.shape[2]
    prev = jnp.maximum(t - 1, 0)

    for slot, ids, wg, wu, wd in ((0, e_lo, wg_lo, wu_lo, wd_lo), (1, e_hi, wg_hi, wu_hi, wd_hi)):
        @pl.when((t == 0) | (ids[t] != ids[prev]))
        def _():
            wgu_scr[slot, :, 0:ff] = wg[0, 0].astype(BF16)
            wgu_scr[slot, :, ff:] = wu[0, 0].astype(BF16)
            wd_scr[slot] = wd[0, 0].astype(BF16)

    @pl.when(valid[t] > 0)
    def _():
        x = _load_tok(xs_ref, tm)
        xb = _rms(x, g_ref[...]).astype(BF16)
        logits = _dot(xb, rw_ref[...]) + rb_ref[...]
        lane = lax.broadcasted_iota(jnp.int32, logits.shape, 1)
        gl = jnp.where(lane < N_GROUPS, logits, -jnp.inf)
        g_w = 1.0 / jnp.sum(jnp.exp(gl - jnp.max(gl, axis=-1, keepdims=True)), axis=-1, keepdims=True)
        l_lo = jnp.sum(jnp.where(lane == e_lo[t] + N_GROUPS, logits, 0.0), axis=-1, keepdims=True)
        l_hi = jnp.sum(jnp.where(lane == e_hi[t] + N_GROUPS, logits, 0.0), axis=-1, keepdims=True)
        tail = jnp.exp(-jnp.abs(l_lo - l_hi))
        w_top = g_w / (1.0 + tail)
        w_oth = g_w * tail / (1.0 + tail)
        lo_top = l_lo >= l_hi
        y = x
        for slot, w in ((0, jnp.where(lo_top, w_top, w_oth)), (1, jnp.where(lo_top, w_oth, w_top))):
            gu = _dot(xb, wgu_scr[slot])
            hid = _silu(gu[:, 0:ff]) * gu[:, ff:] * w
            y = y + _dot(hid.astype(BF16), wd_scr[slot])
        _store_tok(out_ref, y)

    @pl.when(valid[t] == 0)
    def _():
        out_ref[...] = jnp.zeros(out_ref.shape, F32)


def _moe_sorted(xs, g, rw, rb, wg, wu, wd, layer, e_lo, e_hi, valid, tm):
    npad = xs.shape[0] // TOK_ROWS
    _, _, d, ff = wg.shape
    const = lambda t, lo, hi, v: (0, 0)
    at_lo = lambda t, lo, hi, v: (layer, lo[t], 0, 0)
    at_hi = lambda t, lo, hi, v: (layer, hi[t], 0, 0)
    return pl.pallas_call(
        _moe_sorted_kernel,
        grid_spec=pltpu.PrefetchScalarGridSpec(
            num_scalar_prefetch=3,
            grid=(npad // tm,),
            in_specs=[
                pl.BlockSpec((tm * TOK_ROWS, LANES), lambda t, lo, hi, v: (jnp.where(v[t] > 0, t, 0), 0)),
                pl.BlockSpec(g.shape, const), pl.BlockSpec(rw.shape, const), pl.BlockSpec(rb.shape, const),
                pl.BlockSpec((1, 1, d, ff), at_lo), pl.BlockSpec((1, 1, d, ff), at_lo),
                pl.BlockSpec((1, 1, ff, d), at_lo),
                pl.BlockSpec((1, 1, d, ff), at_hi), pl.BlockSpec((1, 1, d, ff), at_hi),
                pl.BlockSpec((1, 1, ff, d), at_hi),
            ],
            out_specs=pl.BlockSpec((tm * TOK_ROWS, LANES), lambda t, lo, hi, v: (t, 0)),
            scratch_shapes=[pltpu.VMEM((2, d, 2 * ff), BF16), pltpu.VMEM((2, ff, d), BF16)],
        ),
        out_shape=jax.ShapeDtypeStruct(xs.shape, F32),
        compiler_params=_params("arbitrary"),
        name="moe_sorted",
    )(e_lo, e_hi, valid, xs, g, rw, rb, wg, wu, wd, wg, wu, wd)


def _moe_sparse(x, cls, rank, counts, g, rw, rb, wg, wu, wd, layer, tm, toks, norm_g=None):
    n = x.shape[0] // TOK_ROWS
    cls, rank = cls.reshape(n), rank.reshape(n)
    cnt = counts[0, :N_CLASSES]
    padded = ((cnt + tm - 1) // tm) * tm
    ends = jnp.cumsum(padded)
    pos = (ends - padded)[cls] + rank
    n_tiles = n // tm + N_CLASSES
    tile_start = jnp.arange(n_tiles, dtype=jnp.int32) * tm
    n_valid = ends[-1] // tm
    unused = n_valid + jnp.arange(N_CLASSES, dtype=jnp.int32)
    tail = jnp.concatenate([jnp.where(cnt > 0, ends // tm - 1, -1),
                            jnp.where(unused < n_tiles, unused, -1)]).astype(jnp.int32)
    last_cls = jnp.sum((tile_start[jnp.maximum(n_valid - 1, 0)] >= ends).astype(jnp.int32))
    tile_cls = jnp.sum((tile_start[:, None] >= ends[None, :]).astype(jnp.int32), axis=1)
    valid = (tile_start < ends[-1]).astype(jnp.int32)
    tile_cls = jnp.where(valid > 0, tile_cls, last_cls)
    pair_lo = jnp.array([0, 0, 0, 1, 1, 2], jnp.int32)
    pair_hi = jnp.array([1, 2, 3, 2, 3, 3], jnp.int32)
    e_lo = (tile_cls // N_PAIRS) * EXP_PER_GROUP + pair_lo[tile_cls % N_PAIRS]
    e_hi = (tile_cls // N_PAIRS) * EXP_PER_GROUP + pair_hi[tile_cls % N_PAIRS]

    xs = _tok_scatter(x, n_tiles * tm, pos, tail, tm)
    ys = _moe_sorted(xs, g, rw.astype(BF16), rb, wg, wu, wd, layer, e_lo, e_hi, valid, tm)
    return _tok_gather(ys, pos, toks, norm_g)


def _moe_kernel(x_ref, g_ref, rw_ref, rb_ref, wg_ref, wu_ref, wd_ref, out_ref, xn_scr, comb_scr, acc_scr):
    e = pl.program_id(1)

    @pl.when(e == 0)
    def _():
        x = x_ref[...]
        xn = _rms(x, g_ref[...])
        xn_scr[...] = xn.astype(BF16)
        comb_scr[...] = _route(_dot(xn, rw_ref[...], HIGHEST) + rb_ref[...])
        acc_scr[...] = x

    xb = xn_scr[...]
    lane = lax.broadcasted_iota(jnp.int32, comb_scr.shape, 1)
    cmb = jnp.sum(jnp.where(lane == e + N_GROUPS, comb_scr[...], 0.0), axis=-1, keepdims=True)
    hid = _silu(_dot(xb, wg_ref[0, 0].astype(BF16))) * _dot(xb, wu_ref[0, 0].astype(BF16)) * cmb
    acc_scr[...] += _dot(hid.astype(BF16), wd_ref[0, 0].astype(BF16))

    @pl.when(e == pl.num_programs(1) - 1)
    def _():
        out_ref[...] = acc_scr[...]


def _moe(x, g, rw, rb, wg, wu, wd, layer, tm):
    n, d = x.shape
    _, ne, _, ff = wg.shape
    return pl.pallas_call(
        _moe_kernel,
        grid=(n // tm, ne),
        in_specs=[
            pl.BlockSpec((tm, d), lambda i, e: (i, 0)),
            _full(g.shape), _full(rw.shape), _full(rb.shape),
            pl.BlockSpec((1, 1, d, ff), lambda i, e: (layer, e, 0, 0)),
            pl.BlockSpec((1, 1, d, ff), lambda i, e: (layer, e, 0, 0)),
            pl.BlockSpec((1, 1, ff, d), lambda i, e: (layer, e, 0, 0)),
        ],
        out_specs=pl.BlockSpec((tm, d), lambda i, e: (i, 0)),
        out_shape=jax.ShapeDtypeStruct((n, d), F32),
        scratch_shapes=[pltpu.VMEM((tm, d), BF16), pltpu.VMEM((tm, LANES), F32), pltpu.VMEM((tm, d), F32)],
        compiler_params=_params("arbitrary", "arbitrary"),
        name="moe",
    )(x, g, rw, rb, wg, wu, wd)


def _norm_kernel(x_ref, g_ref, o_ref):
    o_ref[...] = _rms(x_ref[...], g_ref[...])


def _final_norm(x, g, tm):
    n, d = x.shape
    return pl.pallas_call(
        _norm_kernel,
        grid=(n // tm,),
        in_specs=[pl.BlockSpec((tm, d), lambda i: (i, 0)), _full(g.shape)],
        out_specs=pl.BlockSpec((tm, d), lambda i: (i, 0)),
        out_shape=jax.ShapeDtypeStruct((n, d), F32),
        compiler_params=_params("arbitrary"),
        name="final_norm",
    )(x, g)


def _mix_sample_in_kernel(x_ref, g_ref, win_ref, wlr_ref, gup_ref, gb_ref, cw_ref, b0_ref, b1_ref,
                          yc_ref, u_ref, q_ref, k_ref, v_ref, gate_ref, la_ref):
    hb = _rms(x_ref[...], g_ref[...]).astype(BF16)

    def proj(a, b):
        return _dot(hb, win_ref[:, a:b])

    u = proj(C_CC, C_CH) * proj(C_CH, C_Q)
    cw = cw_ref[...]
    yc_ref[...] = proj(C_CB, C_CC) * (b0_ref[...] * cw[0:1] + b1_ref[...] * cw[1:2] + u * cw[2:3])
    u_ref[...] = u
    q_ref[...] = proj(C_Q, C_K) * (GLA_DK ** -0.5)
    k_ref[...] = proj(C_K, C_V)
    v_ref[...] = proj(C_V, C_G)
    gate_ref[...] = proj(C_G, C_LR)
    lr = _dot(hb, wlr_ref[...])
    gate = _dot(lr.astype(BF16), gup_ref[...]) + gb_ref[...]
    la_ref[...] = _log_sigmoid(gate) * (1.0 / GLA_TAU)


def _mix_sample_in(x, g, win, wlr, gup, gb, cw, b0, b1):
    n = x.shape[0]
    args = (x, g, win, wlr, gup, gb, cw, b0, b1)
    widths = (CONV_W, CONV_W, GLA_K, GLA_K, GLA_V, GLA_V, GLA_K)
    return pl.pallas_call(
        _mix_sample_in_kernel,
        grid=(1,),
        in_specs=[_full(a.shape) for a in args],
        out_specs=[_full((n, w)) for w in widths],
        out_shape=[jax.ShapeDtypeStruct((n, w), F32) for w in widths],
        compiler_params=_params("arbitrary"),
        name="mix_sample_in",
    )(*args)


def _gla_step_kernel(q_ref, k_ref, v_ref, la_ref, s0_ref, s_ref, o_ref):
    nb = q_ref.shape[0]
    a = jnp.exp(la_ref[...])
    kb = k_ref[...].astype(BF16)
    qb = q_ref[...].astype(BF16)
    vf = v_ref[...].astype(BF16).astype(F32)
    rows = lax.broadcasted_iota(jnp.int32, (nb, 1), 0)
    spread = (lax.broadcasted_iota(jnp.int32, (nb, nb * GLA_DV), 0)
              == lax.broadcasted_iota(jnp.int32, (nb, nb * GLA_DV), 1) // GLA_DV)
    spread_b = spread.astype(BF16)
    for h in range(GLA_HEADS):
        ks = slice(h * GLA_DK, (h + 1) * GLA_DK)
        a_cols = _dot_tn(a[:, ks], spread.astype(F32), HIGHEST)
        k_cols = _dot_tn(kb[:, ks], spread_b)
        q_cols = _dot_tn(qb[:, ks], spread_b)
        o_h = jnp.zeros((nb, GLA_DV), F32)
        for n in range(nb):
            blk = slice(n * GLA_DV, (n + 1) * GLA_DV)
            v_row = vf[n:n + 1, h * GLA_DV:(h + 1) * GLA_DV]
            s_new = a_cols[:, blk] * s0_ref[0, n, h] + k_cols[:, blk] * v_row
            s_ref[n, h] = s_new
            o_h = jnp.where(rows == n, jnp.sum(q_cols[:, blk] * s_new, axis=0, keepdims=True), o_h)
        o_ref[:, h * GLA_DV:(h + 1) * GLA_DV] = o_h


def _gla_step(q, k, v, la, state, layer, nb):
    n = q.shape[0]
    sshape = (GLA_HEADS, GLA_DK, GLA_DV)
    return pl.pallas_call(
        _gla_step_kernel,
        grid=(n // nb,),
        in_specs=[
            pl.BlockSpec((nb, GLA_K), lambda i: (i, 0)),
            pl.BlockSpec((nb, GLA_K), lambda i: (i, 0)),
            pl.BlockSpec((nb, GLA_V), lambda i: (i, 0)),
            pl.BlockSpec((nb, GLA_K), lambda i: (i, 0)),
            pl.BlockSpec((1, nb) + sshape, lambda i: (layer, i, 0, 0, 0)),
        ],
        out_specs=[
            pl.BlockSpec((nb,) + sshape, lambda i: (i, 0, 0, 0)),
            pl.BlockSpec((nb, GLA_V), lambda i: (i, 0)),
        ],
        out_shape=[jax.ShapeDtypeStruct((n,) + sshape, F32), jax.ShapeDtypeStruct((n, GLA_V), F32)],
        compiler_params=_params("arbitrary"),
        name="gla_step",
    )(q, k, v, la, state)


def _mix_sample_out_kernel(x_ref, yc_ref, o_ref, gate_ref, gg_ref, wout_ref, gm_ref, wq_ref, x1_ref, q_ref):
    yg = _head_norm_gate(o_ref[...], gate_ref[...], gg_ref[...])
    y = _dot(yc_ref[...].astype(BF16), wout_ref[0:CONV_W, :]) + _dot(yg.astype(BF16), wout_ref[CONV_W:, :])
    x1 = x_ref[...] + y
    x1_ref[...] = x1
    q_ref[...] = _dot(_rms(x1, gm_ref[...]).astype(BF16), wq_ref[...])


def _mix_sample_out(x, yc, o, gate, gg, wout, gm, wq):
    args = (x, yc, o, gate, gg, wout, gm, wq)
    return pl.pallas_call(
        _mix_sample_out_kernel,
        grid=(1,),
        in_specs=[_full(a.shape) for a in args],
        out_specs=[_full(x.shape)] * 2,
        out_shape=[jax.ShapeDtypeStruct(x.shape, F32)] * 2,
        compiler_params=_params("arbitrary"),
        name="mix_sample_out",
    )(*args)


ATT_ROWS = 2 * MEM_HEADS


def _class_allreduce(x, op):
    n = x.shape[-1]
    shift = ATT_ROWS
    while shift < n:
        x = op(x, pltpu.roll(x, shift, axis=1))
        shift *= 2
    return x


def _att_sample_kernel(q_ref, k_ref, v_ref, o_ref):
    nb = q_ref.shape[0]
    ncol = k_ref.shape[2]
    diag = (lax.broadcasted_iota(jnp.int32, (ATT_ROWS, ncol), 0)
            == (lax.broadcasted_iota(jnp.int32, (ATT_ROWS, ncol), 1) & (ATT_ROWS - 1)))
    rows = lax.broadcasted_iota(jnp.int32, (nb, 1), 0)
    t = jnp.zeros((nb, ncol), F32)
    for n in range(nb):
        sc = _dot_nt(q_ref[n].astype(BF16), k_ref[0, n].astype(BF16))
        t = t + jnp.where(rows == n, jnp.sum(jnp.where(diag, sc, 0.0), axis=0, keepdims=True), 0.0)
    valid = (lax.broadcasted_iota(jnp.int32, (nb, ncol), 1) & (ATT_ROWS - 1)) < MEM_HEADS
    s = jnp.where(valid, (t + pltpu.roll(t, ncol - MEM_HEADS, axis=1)) * (MEM_DH ** -0.5), 0.0)
    e = jnp.where(valid, jnp.exp(s - _class_allreduce(s, jnp.maximum)), 0.0)
    den = jnp.where(valid, _class_allreduce(e, jnp.add), 1.0)
    p = e / den
    p = p + pltpu.roll(p, MEM_HEADS, axis=1)
    for n in range(nb):
        p_n = jnp.where(diag, jnp.broadcast_to(p[n:n + 1, :], (ATT_ROWS, ncol)), 0.0)
        o_ref[n] = _dot(p_n.astype(BF16), v_ref[0, n].astype(BF16))


def _att_sample(q, ck, cv, layer, nb):
    n = q.shape[0]
    ncol = ck.shape[2]
    return pl.pallas_call(
        _att_sample_kernel,
        grid=(n // nb,),
        in_specs=[
            pl.BlockSpec((nb, ATT_ROWS, LANES), lambda i: (i, 0, 0)),
            pl.BlockSpec((1, nb, ncol, LANES), lambda i: (layer, i, 0, 0)),
            pl.BlockSpec((1, nb, ncol, LANES), lambda i: (layer, i, 0, 0)),
        ],
        out_specs=pl.BlockSpec((nb, ATT_ROWS, LANES), lambda i: (i, 0, 0)),
        out_shape=jax.ShapeDtypeStruct((n, ATT_ROWS, LANES), F32),
        compiler_params=_params("arbitrary"),
        name="att_sample",
    )(q, ck, cv)


def _oproj_kernel(x_ref, o_ref, wo_ref, out_ref):
    out_ref[...] = x_ref[...] + _dot(o_ref[...].astype(BF16), wo_ref[...])


def _oproj(x, o, wo):
    return pl.pallas_call(
        _oproj_kernel,
        grid=(1,),
        in_specs=[_full(x.shape), _full(o.shape), _full(wo.shape)],
        out_specs=_full(x.shape),
        out_shape=jax.ShapeDtypeStruct(x.shape, F32),
        compiler_params=_params("arbitrary"),
        name="oproj",
    )(x, o, wo)


def kernel(x_prompt, x_sample, state_conv, state_gla, cache_mem_k, cache_mem_v, mem_prompt, norm_mix, w_in, conv_w, gla_gate_up, gla_gate_b, gla_out_norm, w_out, norm_mem, w_q, w_k, w_v, w_o, norm_ffn, router_group, router_group_b, router_expert, router_expert_b, w_gate, w_up, w_down, norm_final):
    depth = w_in.shape[0]
    bsz, t, d = x_prompt.shape
    ns = x_sample.shape[0]
    nm = mem_prompt.shape[1]
    n_tok = bsz * t

    tt = min(512, t)
    tq = min(1024, t)
    tm_moe = min(1024, n_tok)
    tm_kv = min(512, bsz * nm)
    tm_sorted = 256
    rows_perm = min(512, n_tok)
    nb_gla = min(8, ns)
    nb_att = min(8, ns)

    row = lambda a: a.reshape(1, -1)
    mem2 = mem_prompt.reshape(bsz * nm, d)

    def tile_rows(c):
        c = c.reshape(depth, ns, nm, MEM_HEADS, 2, LANES).transpose(0, 1, 2, 4, 3, 5)
        return c.reshape(depth, ns, nm * ATT_ROWS, LANES)

    ck, cv = tile_rows(cache_mem_k), tile_rows(cache_mem_v)

    def untile_rows(c):
        c = c.reshape(depth, bsz, nm, 2, MEM_HEADS, LANES).transpose(0, 1, 2, 4, 3, 5)
        return c.reshape(depth, bsz, nm, MEM_HEADS, MEM_DH)

    mk_all, mv_all = _mem_kv(mem2, w_k.astype(BF16), w_v.astype(BF16), tm_kv)

    xp = x_prompt
    xs = x_sample.reshape(ns, d)
    conv_p, gla_p, conv_s, gla_s = [], [], [], []
    for l in range(depth):
        win = w_in[l, :, :C_LR].astype(BF16)
        wlr = jnp.pad(w_in[l, :, C_LR:], ((0, 0), (0, LANES - GLA_RANK))).astype(BF16)
        gup = jnp.pad(gla_gate_up[l], ((0, LANES - GLA_RANK), (0, 0))).astype(BF16)
        gb = row(gla_gate_b[l])
        gg = row(gla_out_norm[l])
        wout = w_out[l].astype(BF16)
        wq, wo = w_q[l].astype(BF16), w_o[l].astype(BF16)
        rw = jnp.concatenate([router_group[l], router_expert[l].transpose(1, 0, 2).reshape(d, N_EXPERTS)], axis=1)
        rw = jnp.pad(rw, ((0, 0), (0, LANES - rw.shape[1])))
        rb = jnp.concatenate([router_group_b[l], router_expert_b[l].reshape(-1)])
        rb = row(jnp.pad(rb, (0, LANES - rb.shape[0])))
        moe_w = (row(norm_ffn[l]), rw, rb, w_gate, w_up, w_down, l)

        xp, nbuf, ns_p = _mix_prompt(xp, (bsz, t, d), row(norm_mix[l]), win, wlr, gup, gb, conv_w[l], gg, wout, tt)
        conv_p.append(nbuf)
        gla_p.append(ns_p)
        xp, cls, rank, counts = _att_prompt(xp, row(norm_mem[l]), wq, wo, mk_all, mv_all, l,
                                            row(norm_ffn[l]), rw, rb, tq)
        xp = _moe_sparse(xp, cls, rank, counts, *moe_w, tm_sorted, rows_perm,
                         norm_g=row(norm_final) if l == depth - 1 else None)

        yc, u, q, k, v, gate, la = _mix_sample_in(
            xs, row(norm_mix[l]), win, wlr, gup, gb, conv_w[l], state_conv[l, :, 0], state_conv[l, :, 1])
        conv_s.append(jnp.stack([state_conv[l, :, 1], u], axis=1))
        s_new, o = _gla_step(q, k, v, la, state_gla, l, nb_gla)
        gla_s.append(s_new)
        wq_s = wq.reshape(d, MEM_HEADS, 2, LANES).transpose(0, 2, 1, 3).reshape(d, d)
        wo_s = wo.reshape(MEM_HEADS, 2, LANES, d).transpose(1, 0, 2, 3).reshape(d, d)
        xs, qa = _mix_sample_out(xs, yc, o, gate, gg, wout, row(norm_mem[l]), wq_s)
        oa = _att_sample(qa.reshape(ns, ATT_ROWS, LANES), ck, cv, l, nb_att)
        xs = _oproj(xs, oa.reshape(ns, d), wo_s)
        xs = _moe(xs, *moe_w, min(tm_moe, ns))

    y_prompt = xp.reshape(bsz, t, d)
    y_sample = _final_norm(xs, row(norm_final), ns).reshape(ns, 1, d)
    return (y_prompt, y_sample, jnp.stack(conv_p), jnp.stack(gla_p), untile_rows(mk_all), untile_rows(mv_all),
            jnp.stack(conv_s), jnp.stack(gla_s))
```

```python
import functools

import jax
import jax.numpy as jnp
from jax import lax
from jax.experimental import pallas as pl
from jax.experimental.pallas import tpu as pltpu

F32 = jnp.float32
BF16 = jnp.bfloat16
HIGHEST = lax.Precision.HIGHEST

EPS = 1e-6
CONV_W = 512
GLA_HEADS = 4
GLA_DK = 64
GLA_DV = 128
GLA_K = GLA_HEADS * GLA_DK
GLA_V = GLA_HEADS * GLA_DV
GLA_RANK = 16
GLA_TAU = 16.0
GLA_CHUNK = 64
MEM_HEADS = 4
MEM_DH = 256
N_GROUPS = 4
EXP_PER_GROUP = 4
N_EXPERTS = 16
LANES = 128
C_CB, C_CC, C_CH, C_Q, C_K, C_V, C_G, C_LR = 0, 512, 1024, 1536, 1792, 2048, 2560, 3072
VMEM_LIMIT = 52 * 1024 * 1024


def _params(*sem):
    return pltpu.CompilerParams(dimension_semantics=sem, vmem_limit_bytes=VMEM_LIMIT)


def _rms(x, g):
    return x * lax.rsqrt(jnp.mean(x * x, axis=-1, keepdims=True) + EPS) * g


def _dot(a, b, precision=None):
    return jnp.dot(a, b, precision=precision, preferred_element_type=F32)


def _dot_nt(a, b):
    return lax.dot_general(a, b, (((1,), (1,)), ((), ())), preferred_element_type=F32)


def _dot_tn(a, b, precision=None):
    return lax.dot_general(a, b, (((0,), (0,)), ((), ())), precision=precision,
                           preferred_element_type=F32)


def _dot_split(dot, mask, x):
    hi = x.astype(BF16)
    r1 = x - hi.astype(F32)
    mid = r1.astype(BF16)
    lo = (r1 - mid.astype(F32)).astype(BF16)
    return dot(mask, hi) + dot(mask, mid) + dot(mask, lo)


def _silu(x):
    return x / (1.0 + jnp.exp(-x))


def _log_sigmoid(x):
    return jnp.minimum(x, 0.0) - jnp.log1p(jnp.exp(-jnp.abs(x)))


def _head_norm_gate(o, g, gg):
    parts = []
    for h in range(GLA_HEADS):
        sl = slice(h * GLA_DV, (h + 1) * GLA_DV)
        parts.append(_rms(o[:, sl], gg[:, sl]))
    return jnp.concatenate(parts, axis=-1) * _silu(g)


def _full(shape):
    nd = len(shape)
    return pl.BlockSpec(shape, lambda *_: (0,) * nd)


def _mix_prompt_kernel(x_ref, g_ref, win_ref, wlr_ref, gup_ref, gb_ref, cw_ref, gg_ref, wout_ref,
                       x1_ref, conv_ref, s_ref, ubuf, s_scr, o_scr, *, tok_in):
    t = pl.program_id(1)
    tt = x1_ref.shape[1]

    @pl.when(t == 0)
    def _():
        ubuf[0:8, :] = jnp.zeros((8, CONV_W), F32)
        s_scr[...] = jnp.zeros(s_scr.shape, F32)

    x = _load_tok(x_ref, tt) if tok_in else x_ref[0]
    hb = _rms(x, g_ref[...]).astype(BF16)

    def proj(a, b):
        return _dot(hb, win_ref[:, a:b])

    u = proj(C_CC, C_CH) * proj(C_CH, C_Q)
    ubuf[8:8 + tt, :] = u
    cw = cw_ref[...]
    yc = proj(C_CB, C_CC) * (ubuf[6:6 + tt, :] * cw[0:1] + ubuf[7:7 + tt, :] * cw[1:2] + u * cw[2:3])
    ubuf[6:8, :] = u[tt - 2:tt, :]

    qs = proj(C_Q, C_K) * (GLA_DK ** -0.5)
    k = proj(C_K, C_V)
    v = proj(C_V, C_G)
    lr = _dot(hb, wlr_ref[...])
    gate = _dot(lr.astype(BF16), gup_ref[...]) + gb_ref[...]
    la = _log_sigmoid(gate) * (1.0 / GLA_TAU)

    c = GLA_CHUNK
    nc = tt // c
    iota = lambda shape, dim: lax.broadcasted_iota(jnp.int32, shape, dim)
    ltri = (iota((c, c), 0) >= iota((c, c), 1)).astype(BF16)
    b_wide = _dot_split(_dot, ltri, jnp.concatenate([la[j * c:(j + 1) * c] for j in range(nc)], axis=1))
    b_ends = jnp.concatenate([b_wide[c - 1:c, j * GLA_K:(j + 1) * GLA_K] for j in range(nc)], axis=0)
    spread = (iota((nc, nc * GLA_DV), 0) == iota((nc, nc * GLA_DV), 1) // GLA_DV).astype(BF16)
    dec_all = jnp.exp(_dot_split(lambda m, x: _dot_tn(x, m), spread, b_ends))
    head_feat = iota((GLA_K, GLA_K), 0) // c == iota((GLA_K, GLA_K), 1) // GLA_DK
    head_blk = iota((GLA_K, GLA_V), 0) // GLA_DK == iota((GLA_K, GLA_V), 1) // GLA_DV
    causal = iota((c, GLA_K), 0) >= iota((c, GLA_K), 1) % c
    for j in range(nc):
        r = slice(j * c, (j + 1) * c)
        b_c, q_c, k_c, v_c = b_wide[:, j * GLA_K:(j + 1) * GLA_K], qs[r], k[r], v[r]
        b_mid = b_c[c // 2:c // 2 + 1, :]
        b_last = b_c[c - 1:c, :]
        q_i = (q_c * jnp.exp(b_c - b_mid)).astype(BF16)
        k_i = k_c * jnp.exp(b_mid - b_c)
        k_dec = (k_c * jnp.exp(b_last - b_c)).astype(BF16)
        q_b = (q_c * jnp.exp(b_c)).astype(BF16)
        k_rows = jnp.where(head_feat, jnp.concatenate([k_i] * GLA_HEADS, axis=0), 0.0).astype(BF16)
        a = jnp.where(causal, _dot_nt(q_i, k_rows), 0.0).astype(BF16)
        v_blk = jnp.where(head_blk, jnp.concatenate([v_c] * GLA_HEADS, axis=0), 0.0).astype(BF16)
        s_prev = s_scr[...]
        o_scr[r, :] = _dot(jnp.concatenate([a, q_b], axis=1),
                           jnp.concatenate([v_blk, s_prev.astype(BF16)], axis=0))
        dec = dec_all[:, j * GLA_DV:(j + 1) * GLA_DV]
        s_scr[...] = (jnp.concatenate([dec] * GLA_HEADS, axis=1) * s_prev
                      + jnp.where(head_blk, _dot_tn(k_dec, v_c.astype(BF16)), 0.0))

    yg = _head_norm_gate(o_scr[...], proj(C_G, C_LR), gg_ref[...])
    y = _dot(yc.astype(BF16), wout_ref[0:CONV_W, :]) + _dot(yg.astype(BF16), wout_ref[CONV_W:, :])
    x1_ref[0] = x + y

    @pl.when(t == pl.num_programs(1) - 1)
    def _():
        conv_ref[0] = u[tt - 2:tt, :]
        for h in range(GLA_HEADS):
            s_ref[0, h] = s_scr[h * GLA_DK:(h + 1) * GLA_DK, h * GLA_DV:(h + 1) * GLA_DV]


def _mix_prompt(x, shape, g, win, wlr, gup, gb, cw, gg, wout, tt):
    bsz, t, d = shape
    tok_in = x.ndim == 2
    nt = t // tt
    x_spec = (pl.BlockSpec((tt * TOK_ROWS, LANES), lambda b, i: (b * nt + i, 0)) if tok_in
              else pl.BlockSpec((1, tt, d), lambda b, i: (b, i, 0)))
    return pl.pallas_call(
        functools.partial(_mix_prompt_kernel, tok_in=tok_in),
        grid=(bsz, nt),
        in_specs=[
            x_spec,
            _full(g.shape), _full(win.shape), _full(wlr.shape), _full(gup.shape), _full(gb.shape),
            _full(cw.shape), _full(gg.shape), _full(wout.shape),
        ],
        out_specs=[
            pl.BlockSpec((1, tt, d), lambda b, i: (b, i, 0)),
            pl.BlockSpec((1, 2, CONV_W), lambda b, i: (b, 0, 0)),
            pl.BlockSpec((1, GLA_HEADS, GLA_DK, GLA_DV), lambda b, i: (b, 0, 0, 0)),
        ],
        out_shape=[
            jax.ShapeDtypeStruct((bsz, t, d), F32),
            jax.ShapeDtypeStruct((bsz, 2, CONV_W), F32),
            jax.ShapeDtypeStruct((bsz, GLA_HEADS, GLA_DK, GLA_DV), F32),
        ],
        scratch_shapes=[
            pltpu.VMEM((8 + tt, CONV_W), F32),
            pltpu.VMEM((GLA_K, GLA_V), F32),
            pltpu.VMEM((tt, GLA_V), F32),
        ],
        compiler_params=_params("arbitrary", "arbitrary"),
        name="mix_prompt",
    )(x, g, win, wlr, gup, gb, cw, gg, wout)


def _kv_kernel(m_ref, wk_ref, wv_ref, k_ref, v_ref):
    tm = m_ref.shape[0]
    mb = m_ref[...].astype(BF16)
    for w_ref, o_ref in ((wk_ref, k_ref), (wv_ref, v_ref)):
        y = _dot(mb, w_ref[0])
        for h in range(MEM_HEADS):
            for dt in range(MEM_DH // LANES):
                col = h * MEM_DH + dt * LANES
                o_ref[0, pl.ds(dt * MEM_HEADS + h, tm, stride=ATT_ROWS), :] = y[:, col:col + LANES]


def _mem_kv(mem, wk, wv, tm):
    n, d = mem.shape
    depth = wk.shape[0]
    w_spec = pl.BlockSpec((1, d, d), lambda l, i: (l, 0, 0))
    o_spec = pl.BlockSpec((1, tm * ATT_ROWS, LANES), lambda l, i: (l, i, 0))
    return pl.pallas_call(
        _kv_kernel,
        grid=(depth, n // tm),
        in_specs=[pl.BlockSpec((tm, d), lambda l, i: (i, 0)), w_spec, w_spec],
        out_specs=[o_spec] * 2,
        out_shape=[jax.ShapeDtypeStruct((depth, n * ATT_ROWS, LANES), F32)] * 2,
        compiler_params=_params("arbitrary", "arbitrary"),
        name="mem_kv",
    )(mem, wk, wv)


def _att_prompt_kernel(x_ref, g_ref, wq_ref, wo_ref, mk_ref, mv_ref, gf_ref, rw_ref, rb_ref, ltri_ref,
                       out_ref, cls_ref, rank_ref, cnt_ref, carry):
    @pl.when((pl.program_id(0) == 0) & (pl.program_id(1) == 0))
    def _():
        carry[...] = jnp.zeros(carry.shape, F32)

    x = x_ref[0]
    xb = _rms(x, g_ref[...]).astype(BF16)
    q = _dot(xb, wq_ref[...])
    nm = mk_ref.shape[1] // ATT_ROWS

    def head_rows(ref, h):
        return jnp.concatenate([ref[0, pl.ds(dt * MEM_HEADS + h, nm, stride=ATT_ROWS), :]
                                for dt in range(MEM_DH // LANES)], axis=-1).astype(BF16)

    outs = []
    for h in range(MEM_HEADS):
        sl = slice(h * MEM_DH, (h + 1) * MEM_DH)
        s = _dot_nt(q[:, sl].astype(BF16), head_rows(mk_ref, h)) * (MEM_DH ** -0.5)
        e = jnp.exp(s - jnp.max(s, axis=-1, keepdims=True))
        p = e / jnp.sum(e, axis=-1, keepdims=True)
        outs.append(_dot(p.astype(BF16), head_rows(mv_ref, h)))
    o = jnp.concatenate(outs, axis=-1)
    x2 = x + _dot(o.astype(BF16), wo_ref[...])
    _store_tok(out_ref, x2)
    _route_meta(x2, gf_ref, rw_ref, rb_ref, ltri_ref, carry, cls_ref, rank_ref, cnt_ref)


def _att_prompt(x, g, wq, wo, mk, mv, layer, gf, rw, rb, tq):
    bsz, t, d = x.shape
    assert d == TOK_ROWS * LANES
    nq = t // tq
    rows = mk.shape[1] // bsz
    kv_spec = pl.BlockSpec((1, rows, LANES), lambda b, i: (layer, b, 0))
    meta_spec = pl.BlockSpec((tq // LANES, LANES), lambda b, i: (b * nq + i, 0))
    rw_hi = rw.astype(BF16)
    rw_split = jnp.concatenate([rw_hi, (rw - rw_hi.astype(F32)).astype(BF16)], axis=1)
    ltri = jnp.tril(jnp.ones((tq, tq), BF16))
    n_tok = bsz * t
    return pl.pallas_call(
        _att_prompt_kernel,
        grid=(bsz, nq),
        in_specs=[
            pl.BlockSpec((1, tq, d), lambda b, i: (b, i, 0)),
            _full(g.shape), _full(wq.shape), _full(wo.shape),
            kv_spec, kv_spec,
            _full(gf.shape), _full(rw_split.shape), _full(rb.shape), _full(ltri.shape),
        ],
        out_specs=[
            pl.BlockSpec((tq * TOK_ROWS, LANES), lambda b, i: (b * nq + i, 0)),
            meta_spec, meta_spec, _full((1, LANES)),
        ],
        out_shape=[
            jax.ShapeDtypeStruct((n_tok * TOK_ROWS, LANES), F32),
            jax.ShapeDtypeStruct((n_tok // LANES, LANES), jnp.int32),
            jax.ShapeDtypeStruct((n_tok // LANES, LANES), jnp.int32),
            jax.ShapeDtypeStruct((1, LANES), jnp.int32),
        ],
        scratch_shapes=[pltpu.VMEM((1, LANES), F32)],
        compiler_params=_params("arbitrary", "arbitrary"),
        name="att_prompt",
    )(x, g, wq, wo, mk, mv, gf, rw_split, rb, ltri)


def _route_top2(logits):
    lane = lax.broadcasted_iota(jnp.int32, logits.shape, 1)
    lanef = lane.astype(F32)
    ninf = -jnp.inf
    big = 1e9
    gl = jnp.where(lane < N_GROUPS, logits, ninf)
    gmax = jnp.max(gl, axis=-1, keepdims=True)
    g_idx = jnp.min(jnp.where(gl == gmax, lanef, big), axis=-1, keepdims=True)
    g_w = 1.0 / jnp.sum(jnp.exp(gl - gmax), axis=-1, keepdims=True)
    grp = ((lane - N_GROUPS) >> 2).astype(F32)
    emask = (lane >= N_GROUPS) & (lane < N_GROUPS + N_EXPERTS) & (grp == g_idx)
    el = jnp.where(emask, logits, ninf)
    m1 = jnp.max(el, axis=-1, keepdims=True)
    i1 = jnp.min(jnp.where(el == m1, lanef, big), axis=-1, keepdims=True)
    el2 = jnp.where(lanef == i1, ninf, el)
    m2 = jnp.max(el2, axis=-1, keepdims=True)
    i2 = jnp.min(jnp.where(el2 == m2, lanef, big), axis=-1, keepdims=True)
    tail = jnp.exp(m2 - m1)
    w1 = g_w / (1.0 + tail)
    w2 = g_w * tail / (1.0 + tail)
    return g_idx, i1, i2, w1, w2


def _route(logits):
    _, i1, i2, w1, w2 = _route_top2(logits)
    lanef = lax.broadcasted_iota(jnp.int32, logits.shape, 1).astype(F32)
    return jnp.where(lanef == i1, w1, 0.0) + jnp.where(lanef == i2, w2, 0.0)


N_PAIRS = 6
N_CLASSES = N_GROUPS * N_PAIRS
TOK_ROWS = 8


def _load_tok(ref, n):
    return jnp.concatenate([ref[pl.ds(j, n, stride=TOK_ROWS), :] for j in range(TOK_ROWS)], axis=-1)


def _store_tok(ref, val):
    n = val.shape[0]
    for j in range(TOK_ROWS):
        ref[pl.ds(j, n, stride=TOK_ROWS), :] = val[:, j * LANES:(j + 1) * LANES]


def _lane_dense(col):
    eye = lax.broadcasted_iota(jnp.int32, (LANES, LANES), 0) == lax.broadcasted_iota(jnp.int32, (LANES, LANES), 1)
    rows = [jnp.sum(jnp.where(eye, col[b * LANES:(b + 1) * LANES], 0.0), axis=0, keepdims=True)
            for b in range(col.shape[0] // LANES)]
    return jnp.concatenate(rows, axis=0)


def _route_meta(x, g_ref, rw_ref, rb_ref, ltri_ref, carry, cls_ref, rank_ref, cnt_ref):
    n = x.shape[0]
    xn = _rms(x, g_ref[...])
    x_hi = xn.astype(BF16)
    x_lo = (xn - x_hi.astype(F32)).astype(BF16)
    both = _dot(x_hi, rw_ref[...])
    logits = (both[:, :LANES] + (_dot(x_lo, rw_ref[:, :LANES]) + both[:, LANES:])) + rb_ref[...]
    g_idx, i1, i2, _, _ = _route_top2(logits)
    lo = jnp.minimum(i1, i2) - N_GROUPS - EXP_PER_GROUP * g_idx
    hi = jnp.maximum(i1, i2) - N_GROUPS - EXP_PER_GROUP * g_idx
    cls = g_idx * N_PAIRS + lo * (7.0 - lo) * 0.5 + hi - lo - 1.0

    lane = lax.broadcasted_iota(jnp.int32, (n, LANES), 1)
    onehot = lane.astype(F32) == cls
    prefix = _dot(ltri_ref[...], onehot.astype(BF16)) + carry[...]
    rank = jnp.sum(jnp.where(onehot, prefix, 0.0), axis=-1, keepdims=True) - 1.0
    carry[...] = prefix[n - 1:n, :]
    cls_ref[...] = _lane_dense(cls).astype(jnp.int32)
    rank_ref[...] = _lane_dense(rank).astype(jnp.int32)
    cnt_ref[...] = prefix[n - 1:n, :].astype(jnp.int32)


DMA_UNROLL = 8


def _tok_rows(t):
    return pl.ds(pl.multiple_of(t * TOK_ROWS, TOK_ROWS), TOK_ROWS)


def _tok_scatter_kernel(pos, tail, x_ref, dst_ref, stage, zeros, sems, *, toks, steps):
    i = pl.program_id(0)
    slot = i % 2
    base = i * toks
    rows = toks * TOK_ROWS
    tile_rows = zeros.shape[0]

    def wait_slot(s):
        pltpu.make_async_copy(stage.at[s], dst_ref.at[pl.ds(0, rows)], sems.at[s]).wait()

    @pl.when(i == 0)
    def _():
        zeros[...] = jnp.zeros(zeros.shape, F32)

        def fill(c):
            return pltpu.make_async_copy(
                zeros, dst_ref.at[pl.ds(pl.multiple_of(tail[c] * tile_rows, tile_rows), tile_rows)], sems.at[2])

        for c in range(2 * N_CLASSES):
            @pl.when(tail[c] >= 0)
            def _():
                fill(c).start()
        for c in range(2 * N_CLASSES):
            @pl.when(tail[c] >= 0)
            def _():
                fill(c).wait()

    @pl.when(i >= 2)
    def _():
        wait_slot(slot)

    stage[slot] = x_ref[...]

    def start(r8, c):
        for u in range(DMA_UNROLL):
            r = r8 * DMA_UNROLL + u
            pltpu.make_async_copy(stage.at[slot, _tok_rows(r)], dst_ref.at[_tok_rows(pos[base + r])],
                                  sems.at[slot]).start(priority=u % 2)
        return c

    lax.fori_loop(0, toks // DMA_UNROLL, start, 0)

    @pl.when(i == steps - 1)
    def _():
        wait_slot(slot)
        if steps >= 2:
            wait_slot(1 - slot)


def _tok_scatter(x, n_out, pos, tail, tile, toks):
    steps = x.shape[0] // (toks * TOK_ROWS)
    return pl.pallas_call(
        functools.partial(_tok_scatter_kernel, toks=toks, steps=steps),
        grid_spec=pltpu.PrefetchScalarGridSpec(
            num_scalar_prefetch=2,
            grid=(steps,),
            in_specs=[pl.BlockSpec((toks * TOK_ROWS, LANES), lambda i, p, tl: (i, 0))],
            out_specs=pl.BlockSpec(memory_space=pl.ANY),
            scratch_shapes=[pltpu.VMEM((2, toks * TOK_ROWS, LANES), F32), pltpu.VMEM((tile * TOK_ROWS, LANES), F32),
                            pltpu.SemaphoreType.DMA((3,))],
        ),
        out_shape=jax.ShapeDtypeStruct((n_out * TOK_ROWS, LANES), F32),
        compiler_params=_params("arbitrary"),
        name="tok_scatter",
    )(pos, tail, x)


def _tok_gather_kernel(pos, src_ref, *rest, toks, norm):
    if norm:
        g_ref, out_ref, stage, sem = rest
        dst = stage
    else:
        out_ref, sem = rest
        dst = out_ref
    base = pl.program_id(0) * toks

    def start(r8, c):
        for u in range(DMA_UNROLL):
            r = r8 * DMA_UNROLL + u
            pltpu.make_async_copy(src_ref.at[_tok_rows(pos[base + r])], dst.at[_tok_rows(r)],
                                  sem).start(priority=u % 2)
        return c

    lax.fori_loop(0, toks // DMA_UNROLL, start, 0)
    pltpu.make_async_copy(src_ref.at[pl.ds(0, toks * TOK_ROWS)], dst, sem).wait()
    if norm:
        out_ref[...] = _rms(_load_tok(stage, toks), g_ref[...])


def _tok_gather(src, pos, toks, norm_g=None):
    n = pos.shape[0]
    norm = norm_g is not None
    d = TOK_ROWS * LANES
    return pl.pallas_call(
        functools.partial(_tok_gather_kernel, toks=toks, norm=norm),
        grid_spec=pltpu.PrefetchScalarGridSpec(
            num_scalar_prefetch=1,
            grid=(n // toks,),
            in_specs=[pl.BlockSpec(memory_space=pl.ANY)]
            + ([pl.BlockSpec(norm_g.shape, lambda i, p: (0, 0))] if norm else []),
            out_specs=(pl.BlockSpec((toks, d), lambda i, p: (i, 0)) if norm
                       else pl.BlockSpec((toks * TOK_ROWS, LANES), lambda i, p: (i, 0))),
            scratch_shapes=([pltpu.VMEM((toks * TOK_ROWS, LANES), F32)] if norm else [])
            + [pltpu.SemaphoreType.DMA(())],
        ),
        out_shape=jax.ShapeDtypeStruct((n, d) if norm else (n * TOK_ROWS, LANES), F32),
        compiler_params=_params("arbitrary"),
        name="tok_gather",
    )(pos, src, *([norm_g] if norm else []))


def _moe_sorted_kernel(e_lo, e_hi, valid, xs_ref, g_ref, rw_ref, rb_ref,
                       wg_lo, wu_lo, wd_lo, wg_hi, wu_hi, wd_hi, out_ref, wgu_scr, wd_scr):
    t = pl.program_id(0)
    tm = xs_ref.shape[0] // TOK_ROWS
    ff = wd_lo.shape[2]
    prev = jnp.maximum(t - 1, 0)

    for slot, ids, wg, wu, wd in ((0, e_lo, wg_lo, wu_lo, wd_lo), (1, e_hi, wg_hi, wu_hi, wd_hi)):
        @pl.when((t == 0) | (ids[t] != ids[prev]))
        def _():
            wgu_scr[slot, :, 0:ff] = wg[0, 0].astype(BF16)
            wgu_scr[slot, :, ff:] = wu[0, 0].astype(BF16)
            wd_scr[slot] = wd[0, 0].astype(BF16)

    @pl.when(valid[t] > 0)
    def _():
        x = _load_tok(xs_ref, tm)
        xb = _rms(x, g_ref[...]).astype(BF16)
        logits = _dot(xb, rw_ref[...]) + rb_ref[...]
        lane = lax.broadcasted_iota(jnp.int32, logits.shape, 1)
        gl = jnp.where(lane < N_GROUPS, logits, -jnp.inf)
        g_w = 1.0 / jnp.sum(jnp.exp(gl - jnp.max(gl, axis=-1, keepdims=True)), axis=-1, keepdims=True)
        l_lo = jnp.sum(jnp.where(lane == e_lo[t] + N_GROUPS, logits, 0.0), axis=-1, keepdims=True)
        l_hi = jnp.sum(jnp.where(lane == e_hi[t] + N_GROUPS, logits, 0.0), axis=-1, keepdims=True)
        tail = jnp.exp(-jnp.abs(l_lo - l_hi))
        w_top = g_w / (1.0 + tail)
        w_oth = g_w * tail / (1.0 + tail)
        lo_top = l_lo >= l_hi
        y = x
        for slot, w in ((0, jnp.where(lo_top, w_top, w_oth)), (1, jnp.where(lo_top, w_oth, w_top))):
            gu = _dot(xb, wgu_scr[slot])
            hid = _silu(gu[:, 0:ff]) * gu[:, ff:] * w
            y = y + _dot(hid.astype(BF16), wd_scr[slot])
        _store_tok(out_ref, y)

    @pl.when(valid[t] == 0)
    def _():
        out_ref[...] = jnp.zeros(out_ref.shape, F32)


def _moe_sorted(xs, g, rw, rb, wg, wu, wd, layer, e_lo, e_hi, valid, tm):
    npad = xs.shape[0] // TOK_ROWS
    _, _, d, ff = wg.shape
    const = lambda t, lo, hi, v: (0, 0)
    at_lo = lambda t, lo, hi, v: (layer, lo[t], 0, 0)
    at_hi = lambda t, lo, hi, v: (layer, hi[t], 0, 0)
    return pl.pallas_call(
        _moe_sorted_kernel,
        grid_spec=pltpu.PrefetchScalarGridSpec(
            num_scalar_prefetch=3,
            grid=(npad // tm,),
            in_specs=[
                pl.BlockSpec((tm * TOK_ROWS, LANES), lambda t, lo, hi, v: (jnp.where(v[t] > 0, t, 0), 0)),
                pl.BlockSpec(g.shape, const), pl.BlockSpec(rw.shape, const), pl.BlockSpec(rb.shape, const),
                pl.BlockSpec((1, 1, d, ff), at_lo), pl.BlockSpec((1, 1, d, ff), at_lo),
                pl.BlockSpec((1, 1, ff, d), at_lo),
                pl.BlockSpec((1, 1, d, ff), at_hi), pl.BlockSpec((1, 1, d, ff), at_hi),
                pl.BlockSpec((1, 1, ff, d), at_hi),
            ],
            out_specs=pl.BlockSpec((tm * TOK_ROWS, LANES), lambda t, lo, hi, v: (t, 0)),
            scratch_shapes=[pltpu.VMEM((2, d, 2 * ff), BF16), pltpu.VMEM((2, ff, d), BF16)],
        ),
        out_shape=jax.ShapeDtypeStruct(xs.shape, F32),
        compiler_params=_params("arbitrary"),
        name="moe_sorted",
    )(e_lo, e_hi, valid, xs, g, rw, rb, wg, wu, wd, wg, wu, wd)


def _moe_sparse(x, cls, rank, counts, g, rw, rb, wg, wu, wd, layer, tm, toks, norm_g=None):
    n = x.shape[0] // TOK_ROWS
    cls, rank = cls.reshape(n), rank.reshape(n)
    cnt = counts[0, :N_CLASSES]
    padded = ((cnt + tm - 1) // tm) * tm
    ends = jnp.cumsum(padded)
    pos = (ends - padded)[cls] + rank
    n_tiles = n // tm + N_CLASSES
    tile_start = jnp.arange(n_tiles, dtype=jnp.int32) * tm
    n_valid = ends[-1] // tm
    unused = n_valid + jnp.arange(N_CLASSES, dtype=jnp.int32)
    tail = jnp.concatenate([jnp.where(cnt > 0, ends // tm - 1, -1),
                            jnp.where(unused < n_tiles, unused, -1)]).astype(jnp.int32)
    last_cls = jnp.sum((tile_start[jnp.maximum(n_valid - 1, 0)] >= ends).astype(jnp.int32))
    tile_cls = jnp.sum((tile_start[:, None] >= ends[None, :]).astype(jnp.int32), axis=1)
    valid = (tile_start < ends[-1]).astype(jnp.int32)
    tile_cls = jnp.where(valid > 0, tile_cls, last_cls)
    pair_lo = jnp.array([0, 0, 0, 1, 1, 2], jnp.int32)
    pair_hi = jnp.array([1, 2, 3, 2, 3, 3], jnp.int32)
    e_lo = (tile_cls // N_PAIRS) * EXP_PER_GROUP + pair_lo[tile_cls % N_PAIRS]
    e_hi = (tile_cls // N_PAIRS) * EXP_PER_GROUP + pair_hi[tile_cls % N_PAIRS]

    xs = _tok_scatter(x, n_tiles * tm, pos, tail, tm, toks[0])
    ys = _moe_sorted(xs, g, rw.astype(BF16), rb, wg, wu, wd, layer, e_lo, e_hi, valid, tm)
    return _tok_gather(ys, pos, toks[1], norm_g)


def _moe_kernel(x_ref, g_ref, rw_ref, rb_ref, wg_ref, wu_ref, wd_ref, out_ref, xn_scr, comb_scr, acc_scr):
    e = pl.program_id(1)

    @pl.when(e == 0)
    def _():
        x = x_ref[...]
        xn = _rms(x, g_ref[...])
        xn_scr[...] = xn.astype(BF16)
        comb_scr[...] = _route(_dot(xn, rw_ref[...], HIGHEST) + rb_ref[...])
        acc_scr[...] = x

    xb = xn_scr[...]
    lane = lax.broadcasted_iota(jnp.int32, comb_scr.shape, 1)
    cmb = jnp.sum(jnp.where(lane == e + N_GROUPS, comb_scr[...], 0.0), axis=-1, keepdims=True)
    hid = _silu(_dot(xb, wg_ref[0, 0].astype(BF16))) * _dot(xb, wu_ref[0, 0].astype(BF16)) * cmb
    acc_scr[...] += _dot(hid.astype(BF16), wd_ref[0, 0].astype(BF16))

    @pl.when(e == pl.num_programs(1) - 1)
    def _():
        out_ref[...] = acc_scr[...]


def _moe(x, g, rw, rb, wg, wu, wd, layer, tm):
    n, d = x.shape
    _, ne, _, ff = wg.shape
    return pl.pallas_call(
        _moe_kernel,
        grid=(n // tm, ne),
        in_specs=[
            pl.BlockSpec((tm, d), lambda i, e: (i, 0)),
            _full(g.shape), _full(rw.shape), _full(rb.shape),
            pl.BlockSpec((1, 1, d, ff), lambda i, e: (layer, e, 0, 0)),
            pl.BlockSpec((1, 1, d, ff), lambda i, e: (layer, e, 0, 0)),
            pl.BlockSpec((1, 1, ff, d), lambda i, e: (layer, e, 0, 0)),
        ],
        out_specs=pl.BlockSpec((tm, d), lambda i, e: (i, 0)),
        out_shape=jax.ShapeDtypeStruct((n, d), F32),
        scratch_shapes=[pltpu.VMEM((tm, d), BF16), pltpu.VMEM((tm, LANES), F32), pltpu.VMEM((tm, d), F32)],
        compiler_params=_params("arbitrary", "arbitrary"),
        name="moe",
    )(x, g, rw, rb, wg, wu, wd)


def _norm_kernel(x_ref, g_ref, o_ref):
    o_ref[...] = _rms(x_ref[...], g_ref[...])


def _final_norm(x, g, tm):
    n, d = x.shape
    return pl.pallas_call(
        _norm_kernel,
        grid=(n // tm,),
        in_specs=[pl.BlockSpec((tm, d), lambda i: (i, 0)), _full(g.shape)],
        out_specs=pl.BlockSpec((tm, d), lambda i: (i, 0)),
        out_shape=jax.ShapeDtypeStruct((n, d), F32),
        compiler_params=_params("arbitrary"),
        name="final_norm",
    )(x, g)


def _mix_sample_in_kernel(x_ref, g_ref, win_ref, wlr_ref, gup_ref, gb_ref, cw_ref, b0_ref, b1_ref,
                          yc_ref, u_ref, q_ref, k_ref, v_ref, gate_ref, la_ref):
    hb = _rms(x_ref[...], g_ref[...]).astype(BF16)

    def proj(a, b):
        return _dot(hb, win_ref[:, a:b])

    u = proj(C_CC, C_CH) * proj(C_CH, C_Q)
    cw = cw_ref[...]
    yc_ref[...] = proj(C_CB, C_CC) * (b0_ref[...] * cw[0:1] + b1_ref[...] * cw[1:2] + u * cw[2:3])
    u_ref[...] = u
    q_ref[...] = proj(C_Q, C_K) * (GLA_DK ** -0.5)
    k_ref[...] = proj(C_K, C_V)
    v_ref[...] = proj(C_V, C_G)
    gate_ref[...] = proj(C_G, C_LR)
    lr = _dot(hb, wlr_ref[...])
    gate = _dot(lr.astype(BF16), gup_ref[...]) + gb_ref[...]
    la_ref[...] = _log_sigmoid(gate) * (1.0 / GLA_TAU)


def _mix_sample_in(x, g, win, wlr, gup, gb, cw, b0, b1):
    n = x.shape[0]
    args = (x, g, win, wlr, gup, gb, cw, b0, b1)
    widths = (CONV_W, CONV_W, GLA_K, GLA_K, GLA_V, GLA_V, GLA_K)
    return pl.pallas_call(
        _mix_sample_in_kernel,
        grid=(1,),
        in_specs=[_full(a.shape) for a in args],
        out_specs=[_full((n, w)) for w in widths],
        out_shape=[jax.ShapeDtypeStruct((n, w), F32) for w in widths],
        compiler_params=_params("arbitrary"),
        name="mix_sample_in",
    )(*args)


def _gla_step_kernel(q_ref, k_ref, v_ref, la_ref, s0_ref, s_ref, o_ref):
    nb = q_ref.shape[0]
    a = jnp.exp(la_ref[...])
    kb = k_ref[...].astype(BF16)
    qb = q_ref[...].astype(BF16)
    vf = v_ref[...].astype(BF16).astype(F32)
    rows = lax.broadcasted_iota(jnp.int32, (nb, 1), 0)
    spread = (lax.broadcasted_iota(jnp.int32, (nb, nb * GLA_DV), 0)
              == lax.broadcasted_iota(jnp.int32, (nb, nb * GLA_DV), 1) // GLA_DV)
    spread_b = spread.astype(BF16)
    for h in range(GLA_HEADS):
        ks = slice(h * GLA_DK, (h + 1) * GLA_DK)
        a_cols = _dot_tn(a[:, ks], spread.astype(F32), HIGHEST)
        k_cols = _dot_tn(kb[:, ks], spread_b)
        q_cols = _dot_tn(qb[:, ks], spread_b)
        o_h = jnp.zeros((nb, GLA_DV), F32)
        for n in range(nb):
            blk = slice(n * GLA_DV, (n + 1) * GLA_DV)
            v_row = vf[n:n + 1, h * GLA_DV:(h + 1) * GLA_DV]
            s_new = a_cols[:, blk] * s0_ref[0, n, h] + k_cols[:, blk] * v_row
            s_ref[n, h] = s_new
            o_h = jnp.where(rows == n, jnp.sum(q_cols[:, blk] * s_new, axis=0, keepdims=True), o_h)
        o_ref[:, h * GLA_DV:(h + 1) * GLA_DV] = o_h


def _gla_step(q, k, v, la, state, layer, nb):
    n = q.shape[0]
    sshape = (GLA_HEADS, GLA_DK, GLA_DV)
    return pl.pallas_call(
        _gla_step_kernel,
        grid=(n // nb,),
        in_specs=[
            pl.BlockSpec((nb, GLA_K), lambda i: (i, 0)),
            pl.BlockSpec((nb, GLA_K), lambda i: (i, 0)),
            pl.BlockSpec((nb, GLA_V), lambda i: (i, 0)),
            pl.BlockSpec((nb, GLA_K), lambda i: (i, 0)),
            pl.BlockSpec((1, nb) + sshape, lambda i: (layer, i, 0, 0, 0)),
        ],
        out_specs=[
            pl.BlockSpec((nb,) + sshape, lambda i: (i, 0, 0, 0)),
            pl.BlockSpec((nb, GLA_V), lambda i: (i, 0)),
        ],
        out_shape=[jax.ShapeDtypeStruct((n,) + sshape, F32), jax.ShapeDtypeStruct((n, GLA_V), F32)],
        compiler_params=_params("arbitrary"),
        name="gla_step",
    )(q, k, v, la, state)


def _mix_sample_out_kernel(x_ref, yc_ref, o_ref, gate_ref, gg_ref, wout_ref, gm_ref, wq_ref, x1_ref, q_ref):
    yg = _head_norm_gate(o_ref[...], gate_ref[...], gg_ref[...])
    y = _dot(yc_ref[...].astype(BF16), wout_ref[0:CONV_W, :]) + _dot(yg.astype(BF16), wout_ref[CONV_W:, :])
    x1 = x_ref[...] + y
    x1_ref[...] = x1
    q_ref[...] = _dot(_rms(x1, gm_ref[...]).astype(BF16), wq_ref[...])


def _mix_sample_out(x, yc, o, gate, gg, wout, gm, wq):
    args = (x, yc, o, gate, gg, wout, gm, wq)
    return pl.pallas_call(
        _mix_sample_out_kernel,
        grid=(1,),
        in_specs=[_full(a.shape) for a in args],
        out_specs=[_full(x.shape)] * 2,
        out_shape=[jax.ShapeDtypeStruct(x.shape, F32)] * 2,
        compiler_params=_params("arbitrary"),
        name="mix_sample_out",
    )(*args)


ATT_ROWS = 2 * MEM_HEADS


def _class_allreduce(x, op):
    n = x.shape[-1]
    shift = ATT_ROWS
    while shift < n:
        x = op(x, pltpu.roll(x, shift, axis=1))
        shift *= 2
    return x


def _att_sample_kernel(q_ref, k_ref, v_ref, o_ref):
    nb = q_ref.shape[0]
    ncol = k_ref.shape[2]
    diag = (lax.broadcasted_iota(jnp.int32, (ATT_ROWS, ncol), 0)
            == (lax.broadcasted_iota(jnp.int32, (ATT_ROWS, ncol), 1) & (ATT_ROWS - 1)))
    rows = lax.broadcasted_iota(jnp.int32, (nb, 1), 0)
    t = jnp.zeros((nb, ncol), F32)
    for n in range(nb):
        sc = _dot_nt(q_ref[n].astype(BF16), k_ref[0, n].astype(BF16))
        t = t + jnp.where(rows == n, jnp.sum(jnp.where(diag, sc, 0.0), axis=0, keepdims=True), 0.0)
    valid = (lax.broadcasted_iota(jnp.int32, (nb, ncol), 1) & (ATT_ROWS - 1)) < MEM_HEADS
    s = jnp.where(valid, (t + pltpu.roll(t, ncol - MEM_HEADS, axis=1)) * (MEM_DH ** -0.5), 0.0)
    e = jnp.where(valid, jnp.exp(s - _class_allreduce(s, jnp.maximum)), 0.0)
    den = jnp.where(valid, _class_allreduce(e, jnp.add), 1.0)
    p = e / den
    p = p + pltpu.roll(p, MEM_HEADS, axis=1)
    for n in range(nb):
        p_n = jnp.where(diag, jnp.broadcast_to(p[n:n + 1, :], (ATT_ROWS, ncol)), 0.0)
        o_ref[n] = _dot(p_n.astype(BF16), v_ref[0, n].astype(BF16))


def _att_sample(q, ck, cv, layer, nb):
    n = q.shape[0]
    ncol = ck.shape[2]
    return pl.pallas_call(
        _att_sample_kernel,
        grid=(n // nb,),
        in_specs=[
            pl.BlockSpec((nb, ATT_ROWS, LANES), lambda i: (i, 0, 0)),
            pl.BlockSpec((1, nb, ncol, LANES), lambda i: (layer, i, 0, 0)),
            pl.BlockSpec((1, nb, ncol, LANES), lambda i: (layer, i, 0, 0)),
        ],
        out_specs=pl.BlockSpec((nb, ATT_ROWS, LANES), lambda i: (i, 0, 0)),
        out_shape=jax.ShapeDtypeStruct((n, ATT_ROWS, LANES), F32),
        compiler_params=_params("arbitrary"),
        name="att_sample",
    )(q, ck, cv)


def _oproj_kernel(x_ref, o_ref, wo_ref, out_ref):
    out_ref[...] = x_ref[...] + _dot(o_ref[...].astype(BF16), wo_ref[...])


def _oproj(x, o, wo):
    return pl.pallas_call(
        _oproj_kernel,
        grid=(1,),
        in_specs=[_full(x.shape), _full(o.shape), _full(wo.shape)],
        out_specs=_full(x.shape),
        out_shape=jax.ShapeDtypeStruct(x.shape, F32),
        compiler_params=_params("arbitrary"),
        name="oproj",
    )(x, o, wo)


def kernel(x_prompt, x_sample, state_conv, state_gla, cache_mem_k, cache_mem_v, mem_prompt, norm_mix, w_in, conv_w, gla_gate_up, gla_gate_b, gla_out_norm, w_out, norm_mem, w_q, w_k, w_v, w_o, norm_ffn, router_group, router_group_b, router_expert, router_expert_b, w_gate, w_up, w_down, norm_final):
    depth = w_in.shape[0]
    bsz, t, d = x_prompt.shape
    ns = x_sample.shape[0]
    nm = mem_prompt.shape[1]
    n_tok = bsz * t

    tt = min(512, t)
    tq = min(1024, t)
    tm_moe = min(1024, n_tok)
    tm_kv = min(512, bsz * nm)
    tm_sorted = 256
    rows_perm = (min(1024, n_tok), min(2048, n_tok))
    nb_gla = min(8, ns)
    nb_att = min(8, ns)

    row = lambda a: a.reshape(1, -1)
    mem2 = mem_prompt.reshape(bsz * nm, d)

    def tile_rows(c):
        c = c.reshape(depth, ns, nm, MEM_HEADS, 2, LANES).transpose(0, 1, 2, 4, 3, 5)
        return c.reshape(depth, ns, nm * ATT_ROWS, LANES)

    ck, cv = tile_rows(cache_mem_k), tile_rows(cache_mem_v)

    def untile_rows(c):
        c = c.reshape(depth, bsz, nm, 2, MEM_HEADS, LANES).transpose(0, 1, 2, 4, 3, 5)
        return c.reshape(depth, bsz, nm, MEM_HEADS, MEM_DH)

    mk_all, mv_all = _mem_kv(mem2, w_k.astype(BF16), w_v.astype(BF16), tm_kv)

    xp = x_prompt
    xs = x_sample.reshape(ns, d)
    conv_p, gla_p, conv_s, gla_s = [], [], [], []
    for l in range(depth):
        win = w_in[l, :, :C_LR].astype(BF16)
        wlr = jnp.pad(w_in[l, :, C_LR:], ((0, 0), (0, LANES - GLA_RANK))).astype(BF16)
        gup = jnp.pad(gla_gate_up[l], ((0, LANES - GLA_RANK), (0, 0))).astype(BF16)
        gb = row(gla_gate_b[l])
        gg = row(gla_out_norm[l])
        wout = w_out[l].astype(BF16)
        wq, wo = w_q[l].astype(BF16), w_o[l].astype(BF16)
        rw = jnp.concatenate([router_group[l], router_expert[l].transpose(1, 0, 2).reshape(d, N_EXPERTS)], axis=1)
        rw = jnp.pad(rw, ((0, 0), (0, LANES - rw.shape[1])))
        rb = jnp.concatenate([router_group_b[l], router_expert_b[l].reshape(-1)])
        rb = row(jnp.pad(rb, (0, LANES - rb.shape[0])))
        moe_w = (row(norm_ffn[l]), rw, rb, w_gate, w_up, w_down, l)

        xp, nbuf, ns_p = _mix_prompt(xp, (bsz, t, d), row(norm_mix[l]), win, wlr, gup, gb, conv_w[l], gg, wout, tt)
        conv_p.append(nbuf)
        gla_p.append(ns_p)
        xp, cls, rank, counts = _att_prompt(xp, row(norm_mem[l]), wq, wo, mk_all, mv_all, l,
                                            row(norm_ffn[l]), rw, rb, tq)
        xp = _moe_sparse(xp, cls, rank, counts, *moe_w, tm_sorted, rows_perm,
                         norm_g=row(norm_final) if l == depth - 1 else None)

        yc, u, q, k, v, gate, la = _mix_sample_in(
            xs, row(norm_mix[l]), win, wlr, gup, gb, conv_w[l], state_conv[l, :, 0], state_conv[l, :, 1])
        conv_s.append(jnp.stack([state_conv[l, :, 1], u], axis=1))
        s_new, o = _gla_step(q, k, v, la, state_gla, l, nb_gla)
        gla_s.append(s_new)
        wq_s = wq.reshape(d, MEM_HEADS, 2, LANES).transpose(0, 2, 1, 3).reshape(d, d)
        wo_s = wo.reshape(MEM_HEADS, 2, LANES, d).transpose(1, 0, 2, 3).reshape(d, d)
        xs, qa = _mix_sample_out(xs, yc, o, gate, gg, wout, row(norm_mem[l]), wq_s)
        oa = _att_sample(qa.reshape(ns, ATT_ROWS, LANES), ck, cv, l, nb_att)
        xs = _oproj(xs, oa.reshape(ns, d), wo_s)
        xs = _moe(xs, *moe_w, min(tm_moe, ns))

    y_prompt = xp.reshape(bsz, t, d)
    y_sample = _final_norm(xs, row(norm_final), ns).reshape(ns, 1, d)
    return (y_prompt, y_sample, jnp.stack(conv_p), jnp.stack(gla_p), untile_rows(mk_all), untile_rows(mv_all),
            jnp.stack(conv_s), jnp.stack(gla_s))
```

```python
import functools

import jax
import jax.numpy as jnp
from jax import lax
from jax.experimental import pallas as pl
from jax.experimental.pallas import tpu as pltpu

F32 = jnp.float32
BF16 = jnp.bfloat16
HIGHEST = lax.Precision.HIGHEST

EPS = 1e-6
CONV_W = 512
GLA_HEADS = 4
GLA_DK = 64
GLA_DV = 128
GLA_K = GLA_HEADS * GLA_DK
GLA_V = GLA_HEADS * GLA_DV
GLA_RANK = 16
GLA_TAU = 16.0
GLA_CHUNK = 64
MEM_HEADS = 4
MEM_DH = 256
N_GROUPS = 4
EXP_PER_GROUP = 4
N_EXPERTS = 16
LANES = 128
C_CB, C_CC, C_CH, C_Q, C_K, C_V, C_G, C_LR = 0, 512, 1024, 1536, 1792, 2048, 2560, 3072
VMEM_LIMIT = 52 * 1024 * 1024


def _params(*sem):
    return pltpu.CompilerParams(dimension_semantics=sem, vmem_limit_bytes=VMEM_LIMIT)


def _rms(x, g):
    return x * lax.rsqrt(jnp.mean(x * x, axis=-1, keepdims=True) + EPS) * g


def _dot(a, b, precision=None):
    return jnp.dot(a, b, precision=precision, preferred_element_type=F32)


def _dot_nt(a, b):
    return lax.dot_general(a, b, (((1,), (1,)), ((), ())), preferred_element_type=F32)


def _dot_tn(a, b, precision=None):
    return lax.dot_general(a, b, (((0,), (0,)), ((), ())), precision=precision,
                           preferred_element_type=F32)


def _dot_split(dot, mask, x):
    hi = x.astype(BF16)
    r1 = x - hi.astype(F32)
    mid = r1.astype(BF16)
    lo = (r1 - mid.astype(F32)).astype(BF16)
    return dot(mask, hi) + dot(mask, mid) + dot(mask, lo)


def _silu(x):
    return x / (1.0 + jnp.exp(-x))


def _log_sigmoid(x):
    return jnp.minimum(x, 0.0) - jnp.log1p(jnp.exp(-jnp.abs(x)))


def _head_norm_gate(o, g, gg):
    parts = []
    for h in range(GLA_HEADS):
        sl = slice(h * GLA_DV, (h + 1) * GLA_DV)
        parts.append(_rms(o[:, sl], gg[:, sl]))
    return jnp.concatenate(parts, axis=-1) * _silu(g)


def _full(shape):
    nd = len(shape)
    return pl.BlockSpec(shape, lambda *_: (0,) * nd)


def _mix_prompt_kernel(x_ref, g_ref, win_ref, wlr_ref, gup_ref, gb_ref, cw_ref, gg_ref, wout_ref,
                       x1_ref, conv_ref, s_ref, ubuf, s_scr, o_scr, *, tok_in):
    t = pl.program_id(1)
    tt = x1_ref.shape[1]

    @pl.when(t == 0)
    def _():
        ubuf[0:8, :] = jnp.zeros((8, CONV_W), F32)
        s_scr[...] = jnp.zeros(s_scr.shape, F32)

    x = _load_tok(x_ref, tt) if tok_in else x_ref[0]
    hb = _rms(x, g_ref[...]).astype(BF16)

    def proj(a, b):
        return _dot(hb, win_ref[:, a:b])

    u = proj(C_CC, C_CH) * proj(C_CH, C_Q)
    ubuf[8:8 + tt, :] = u
    cw = cw_ref[...]
    yc = proj(C_CB, C_CC) * (ubuf[6:6 + tt, :] * cw[0:1] + ubuf[7:7 + tt, :] * cw[1:2] + u * cw[2:3])
    ubuf[6:8, :] = u[tt - 2:tt, :]

    qs = proj(C_Q, C_K) * (GLA_DK ** -0.5)
    k = proj(C_K, C_V)
    v = proj(C_V, C_G)
    lr = _dot(hb, wlr_ref[...])
    gate = _dot(lr.astype(BF16), gup_ref[...]) + gb_ref[...]
    la = _log_sigmoid(gate) * (1.0 / GLA_TAU)

    c = GLA_CHUNK
    nc = tt // c
    iota = lambda shape, dim: lax.broadcasted_iota(jnp.int32, shape, dim)
    ltri = (iota((c, c), 0) >= iota((c, c), 1)).astype(BF16)
    b_wide = _dot_split(_dot, ltri, jnp.concatenate([la[j * c:(j + 1) * c] for j in range(nc)], axis=1))
    b_ends = jnp.concatenate([b_wide[c - 1:c, j * GLA_K:(j + 1) * GLA_K] for j in range(nc)], axis=0)
    spread = (iota((nc, LANES), 0) == iota((nc, LANES), 1)).astype(BF16)
    log_dec = _dot_split(lambda m, x: _dot_tn(x, m), spread, b_ends)
    head_feat = iota((GLA_K, GLA_K), 0) // c == iota((GLA_K, GLA_K), 1) // GLA_DK
    head_blk = iota((GLA_K, GLA_V), 0) // GLA_DK == iota((GLA_K, GLA_V), 1) // GLA_DV
    causal = iota((c, GLA_K), 0) >= iota((c, GLA_K), 1) % c
    for j in range(nc):
        r = slice(j * c, (j + 1) * c)
        b_c, q_c, k_c, v_c = b_wide[:, j * GLA_K:(j + 1) * GLA_K], qs[r], k[r], v[r]
        b_mid = b_c[c // 2:c // 2 + 1, :]
        b_last = b_c[c - 1:c, :]
        q_i = (q_c * jnp.exp(b_c - b_mid)).astype(BF16)
        k_i = k_c * jnp.exp(b_mid - b_c)
        k_dec = (k_c * jnp.exp(b_last - b_c)).astype(BF16)
        q_b = (q_c * jnp.exp(b_c)).astype(BF16)
        k_rows = jnp.where(head_feat, jnp.concatenate([k_i] * GLA_HEADS, axis=0), 0.0).astype(BF16)
        a = jnp.where(causal, _dot_nt(q_i, k_rows), 0.0).astype(BF16)
        v_blk = jnp.where(head_blk, jnp.concatenate([v_c] * GLA_HEADS, axis=0), 0.0).astype(BF16)
        s_prev = s_scr[...]
        o_scr[r, :] = _dot(jnp.concatenate([a, q_b], axis=1),
                           jnp.concatenate([v_blk, s_prev.astype(BF16)], axis=0))
        dec = jnp.exp(jnp.broadcast_to(log_dec[:, j:j + 1], (GLA_K, GLA_DV)))
        s_scr[...] = (jnp.concatenate([dec] * GLA_HEADS, axis=1) * s_prev
                      + jnp.where(head_blk, _dot_tn(k_dec, v_c.astype(BF16)), 0.0))

    yg = _head_norm_gate(o_scr[...], proj(C_G, C_LR), gg_ref[...])
    y = _dot(yc.astype(BF16), wout_ref[0:CONV_W, :]) + _dot(yg.astype(BF16), wout_ref[CONV_W:, :])
    x1_ref[0] = x + y

    @pl.when(t == pl.num_programs(1) - 1)
    def _():
        conv_ref[0] = u[tt - 2:tt, :]
        for h in range(GLA_HEADS):
            s_ref[0, h] = s_scr[h * GLA_DK:(h + 1) * GLA_DK, h * GLA_DV:(h + 1) * GLA_DV]


def _mix_prompt(x, shape, g, win, wlr, gup, gb, cw, gg, wout, tt):
    bsz, t, d = shape
    tok_in = x.ndim == 2
    nt = t // tt
    x_spec = (pl.BlockSpec((tt * TOK_ROWS, LANES), lambda b, i: (b * nt + i, 0)) if tok_in
              else pl.BlockSpec((1, tt, d), lambda b, i: (b, i, 0)))
    return pl.pallas_call(
        functools.partial(_mix_prompt_kernel, tok_in=tok_in),
        grid=(bsz, nt),
        in_specs=[
            x_spec,
            _full(g.shape), _full(win.shape), _full(wlr.shape), _full(gup.shape), _full(gb.shape),
            _full(cw.shape), _full(gg.shape), _full(wout.shape),
        ],
        out_specs=[
            pl.BlockSpec((1, tt, d), lambda b, i: (b, i, 0)),
            pl.BlockSpec((1, 2, CONV_W), lambda b, i: (b, 0, 0)),
            pl.BlockSpec((1, GLA_HEADS, GLA_DK, GLA_DV), lambda b, i: (b, 0, 0, 0)),
        ],
        out_shape=[
            jax.ShapeDtypeStruct((bsz, t, d), F32),
            jax.ShapeDtypeStruct((bsz, 2, CONV_W), F32),
            jax.ShapeDtypeStruct((bsz, GLA_HEADS, GLA_DK, GLA_DV), F32),
        ],
        scratch_shapes=[
            pltpu.VMEM((8 + tt, CONV_W), F32),
            pltpu.VMEM((GLA_K, GLA_V), F32),
            pltpu.VMEM((tt, GLA_V), F32),
        ],
        compiler_params=_params("arbitrary", "arbitrary"),
        name="mix_prompt",
    )(x, g, win, wlr, gup, gb, cw, gg, wout)


def _kv_kernel(m_ref, wk_ref, wv_ref, k_ref, v_ref):
    tm = m_ref.shape[0]
    mb = m_ref[...].astype(BF16)
    for w_ref, o_ref in ((wk_ref, k_ref), (wv_ref, v_ref)):
        y = _dot(mb, w_ref[0])
        for h in range(MEM_HEADS):
            for dt in range(MEM_DH // LANES):
                col = h * MEM_DH + dt * LANES
                o_ref[0, pl.ds(dt * MEM_HEADS + h, tm, stride=ATT_ROWS), :] = y[:, col:col + LANES]


def _mem_kv(mem, wk, wv, tm):
    n, d = mem.shape
    depth = wk.shape[0]
    w_spec = pl.BlockSpec((1, d, d), lambda l, i: (l, 0, 0))
    o_spec = pl.BlockSpec((1, tm * ATT_ROWS, LANES), lambda l, i: (l, i, 0))
    return pl.pallas_call(
        _kv_kernel,
        grid=(depth, n // tm),
        in_specs=[pl.BlockSpec((tm, d), lambda l, i: (i, 0)), w_spec, w_spec],
        out_specs=[o_spec] * 2,
        out_shape=[jax.ShapeDtypeStruct((depth, n * ATT_ROWS, LANES), F32)] * 2,
        compiler_params=_params("arbitrary", "arbitrary"),
        name="mem_kv",
    )(mem, wk, wv)


def _att_prompt_kernel(x_ref, g_ref, wq_ref, wo_ref, mk_ref, mv_ref, gf_ref, rw_ref, rb_ref, ltri_ref,
                       out_ref, cls_ref, rank_ref, cnt_ref, carry):
    @pl.when((pl.program_id(0) == 0) & (pl.program_id(1) == 0))
    def _():
        carry[...] = jnp.zeros(carry.shape, F32)

    x = x_ref[0]
    xb = _rms(x, g_ref[...]).astype(BF16)
    q = _dot(xb, wq_ref[...])
    nm = mk_ref.shape[1] // ATT_ROWS

    def head_rows(ref, h):
        return jnp.concatenate([ref[0, pl.ds(dt * MEM_HEADS + h, nm, stride=ATT_ROWS), :]
                                for dt in range(MEM_DH // LANES)], axis=-1).astype(BF16)

    outs = []
    for h in range(MEM_HEADS):
        sl = slice(h * MEM_DH, (h + 1) * MEM_DH)
        s = _dot_nt(q[:, sl].astype(BF16), head_rows(mk_ref, h)) * (MEM_DH ** -0.5)
        e = jnp.exp(s - jnp.max(s, axis=-1, keepdims=True))
        p = e / jnp.sum(e, axis=-1, keepdims=True)
        outs.append(_dot(p.astype(BF16), head_rows(mv_ref, h)))
    o = jnp.concatenate(outs, axis=-1)
    x2 = x + _dot(o.astype(BF16), wo_ref[...])
    _store_tok(out_ref, x2)
    _route_meta(x2, gf_ref, rw_ref, rb_ref, ltri_ref, carry, cls_ref, rank_ref, cnt_ref)


def _att_prompt(x, g, wq, wo, mk, mv, layer, gf, rw, rb, tq):
    bsz, t, d = x.shape
    assert d == TOK_ROWS * LANES
    nq = t // tq
    rows = mk.shape[1] // bsz
    kv_spec = pl.BlockSpec((1, rows, LANES), lambda b, i: (layer, b, 0))
    meta_spec = pl.BlockSpec((tq // LANES, LANES), lambda b, i: (b * nq + i, 0))
    rw_hi = rw.astype(BF16)
    rw_split = jnp.concatenate([rw_hi, (rw - rw_hi.astype(F32)).astype(BF16)], axis=1)
    ltri = jnp.tril(jnp.ones((tq, tq), BF16))
    n_tok = bsz * t
    return pl.pallas_call(
        _att_prompt_kernel,
        grid=(bsz, nq),
        in_specs=[
            pl.BlockSpec((1, tq, d), lambda b, i: (b, i, 0)),
            _full(g.shape), _full(wq.shape), _full(wo.shape),
            kv_spec, kv_spec,
            _full(gf.shape), _full(rw_split.shape), _full(rb.shape), _full(ltri.shape),
        ],
        out_specs=[
            pl.BlockSpec((tq * TOK_ROWS, LANES), lambda b, i: (b * nq + i, 0)),
            meta_spec, meta_spec, _full((1, LANES)),
        ],
        out_shape=[
            jax.ShapeDtypeStruct((n_tok * TOK_ROWS, LANES), F32),
            jax.ShapeDtypeStruct((n_tok // LANES, LANES), jnp.int32),
            jax.ShapeDtypeStruct((n_tok // LANES, LANES), jnp.int32),
            jax.ShapeDtypeStruct((1, LANES), jnp.int32),
        ],
        scratch_shapes=[pltpu.VMEM((1, LANES), F32)],
        compiler_params=_params("arbitrary", "arbitrary"),
        name="att_prompt",
    )(x, g, wq, wo, mk, mv, gf, rw_split, rb, ltri)


def _route_top2(logits):
    lane = lax.broadcasted_iota(jnp.int32, logits.shape, 1)
    lanef = lane.astype(F32)
    ninf = -jnp.inf
    big = 1e9
    gl = jnp.where(lane < N_GROUPS, logits, ninf)
    gmax = jnp.max(gl, axis=-1, keepdims=True)
    g_idx = jnp.min(jnp.where(gl == gmax, lanef, big), axis=-1, keepdims=True)
    g_w = 1.0 / jnp.sum(jnp.exp(gl - gmax), axis=-1, keepdims=True)
    grp = ((lane - N_GROUPS) >> 2).astype(F32)
    emask = (lane >= N_GROUPS) & (lane < N_GROUPS + N_EXPERTS) & (grp == g_idx)
    el = jnp.where(emask, logits, ninf)
    m1 = jnp.max(el, axis=-1, keepdims=True)
    i1 = jnp.min(jnp.where(el == m1, lanef, big), axis=-1, keepdims=True)
    el2 = jnp.where(lanef == i1, ninf, el)
    m2 = jnp.max(el2, axis=-1, keepdims=True)
    i2 = jnp.min(jnp.where(el2 == m2, lanef, big), axis=-1, keepdims=True)
    tail = jnp.exp(m2 - m1)
    w1 = g_w / (1.0 + tail)
    w2 = g_w * tail / (1.0 + tail)
    return g_idx, i1, i2, w1, w2


def _route(logits):
    _, i1, i2, w1, w2 = _route_top2(logits)
    lanef = lax.broadcasted_iota(jnp.int32, logits.shape, 1).astype(F32)
    return jnp.where(lanef == i1, w1, 0.0) + jnp.where(lanef == i2, w2, 0.0)


N_PAIRS = 6
N_CLASSES = N_GROUPS * N_PAIRS
TOK_ROWS = 8


def _load_tok(ref, n):
    return jnp.concatenate([ref[pl.ds(j, n, stride=TOK_ROWS), :] for j in range(TOK_ROWS)], axis=-1)


def _store_tok(ref, val):
    n = val.shape[0]
    for j in range(TOK_ROWS):
        ref[pl.ds(j, n, stride=TOK_ROWS), :] = val[:, j * LANES:(j + 1) * LANES]


def _lane_dense(col):
    eye = lax.broadcasted_iota(jnp.int32, (LANES, LANES), 0) == lax.broadcasted_iota(jnp.int32, (LANES, LANES), 1)
    rows = [jnp.sum(jnp.where(eye, col[b * LANES:(b + 1) * LANES], 0.0), axis=0, keepdims=True)
            for b in range(col.shape[0] // LANES)]
    return jnp.concatenate(rows, axis=0)


def _route_meta(x, g_ref, rw_ref, rb_ref, ltri_ref, carry, cls_ref, rank_ref, cnt_ref):
    n = x.shape[0]
    xn = _rms(x, g_ref[...])
    x_hi = xn.astype(BF16)
    x_lo = (xn - x_hi.astype(F32)).astype(BF16)
    both = _dot(x_hi, rw_ref[...])
    logits = (both[:, :LANES] + (_dot(x_lo, rw_ref[:, :LANES]) + both[:, LANES:])) + rb_ref[...]
    g_idx, i1, i2, _, _ = _route_top2(logits)
    lo = jnp.minimum(i1, i2) - N_GROUPS - EXP_PER_GROUP * g_idx
    hi = jnp.maximum(i1, i2) - N_GROUPS - EXP_PER_GROUP * g_idx
    cls = g_idx * N_PAIRS + lo * (7.0 - lo) * 0.5 + hi - lo - 1.0

    lane = lax.broadcasted_iota(jnp.int32, (n, LANES), 1)
    onehot = lane.astype(F32) == cls
    prefix = _dot(ltri_ref[...], onehot.astype(BF16)) + carry[...]
    rank = jnp.sum(jnp.where(onehot, prefix, 0.0), axis=-1, keepdims=True) - 1.0
    carry[...] = prefix[n - 1:n, :]
    cls_ref[...] = _lane_dense(cls).astype(jnp.int32)
    rank_ref[...] = _lane_dense(rank).astype(jnp.int32)
    cnt_ref[...] = prefix[n - 1:n, :].astype(jnp.int32)


DMA_UNROLL = 8


def _tok_rows(t):
    return pl.ds(pl.multiple_of(t * TOK_ROWS, TOK_ROWS), TOK_ROWS)


def _tok_scatter_kernel(pos, tail, x_ref, dst_ref, stage, zeros, sems, *, toks, steps):
    i = pl.program_id(0)
    slot = i % 2
    base = i * toks
    rows = toks * TOK_ROWS
    tile_rows = zeros.shape[0]

    def wait_slot(s):
        pltpu.make_async_copy(stage.at[s], dst_ref.at[pl.ds(0, rows)], sems.at[s]).wait()

    @pl.when(i == 0)
    def _():
        zeros[...] = jnp.zeros(zeros.shape, F32)

        def fill(c):
            return pltpu.make_async_copy(
                zeros, dst_ref.at[pl.ds(pl.multiple_of(tail[c] * tile_rows, tile_rows), tile_rows)], sems.at[2])

        for c in range(2 * N_CLASSES):
            @pl.when(tail[c] >= 0)
            def _():
                fill(c).start()
        for c in range(2 * N_CLASSES):
            @pl.when(tail[c] >= 0)
            def _():
                fill(c).wait()

    @pl.when(i >= 2)
    def _():
        wait_slot(slot)

    stage[slot] = x_ref[...]

    def start(r8, c):
        for u in range(DMA_UNROLL):
            r = r8 * DMA_UNROLL + u
            pltpu.make_async_copy(stage.at[slot, _tok_rows(r)], dst_ref.at[_tok_rows(pos[base + r])],
                                  sems.at[slot]).start(priority=u % 2)
        return c

    lax.fori_loop(0, toks // DMA_UNROLL, start, 0)

    @pl.when(i == steps - 1)
    def _():
        wait_slot(slot)
        if steps >= 2:
            wait_slot(1 - slot)


def _tok_scatter(x, n_out, pos, tail, tile, toks):
    steps = x.shape[0] // (toks * TOK_ROWS)
    return pl.pallas_call(
        functools.partial(_tok_scatter_kernel, toks=toks, steps=steps),
        grid_spec=pltpu.PrefetchScalarGridSpec(
            num_scalar_prefetch=2,
            grid=(steps,),
            in_specs=[pl.BlockSpec((toks * TOK_ROWS, LANES), lambda i, p, tl: (i, 0))],
            out_specs=pl.BlockSpec(memory_space=pl.ANY),
            scratch_shapes=[pltpu.VMEM((2, toks * TOK_ROWS, LANES), F32), pltpu.VMEM((tile * TOK_ROWS, LANES), F32),
                            pltpu.SemaphoreType.DMA((3,))],
        ),
        out_shape=jax.ShapeDtypeStruct((n_out * TOK_ROWS, LANES), F32),
        compiler_params=_params("arbitrary"),
        name="tok_scatter",
    )(pos, tail, x)


def _tok_gather_kernel(pos, src_ref, *rest, toks, norm):
    if norm:
        g_ref, out_ref, stage, sem = rest
        dst = stage
    else:
        out_ref, sem = rest
        dst = out_ref
    base = pl.program_id(0) * toks

    def start(r8, c):
        for u in range(DMA_UNROLL):
            r = r8 * DMA_UNROLL + u
            pltpu.make_async_copy(src_ref.at[_tok_rows(pos[base + r])], dst.at[_tok_rows(r)],
                                  sem).start(priority=u % 2)
        return c

    lax.fori_loop(0, toks // DMA_UNROLL, start, 0)
    pltpu.make_async_copy(src_ref.at[pl.ds(0, toks * TOK_ROWS)], dst, sem).wait()
    if norm:
        out_ref[...] = _rms(_load_tok(stage, toks), g_ref[...])


def _tok_gather(src, pos, toks, norm_g=None):
    n = pos.shape[0]
    norm = norm_g is not None
    d = TOK_ROWS * LANES
    return pl.pallas_call(
        functools.partial(_tok_gather_kernel, toks=toks, norm=norm),
        grid_spec=pltpu.PrefetchScalarGridSpec(
            num_scalar_prefetch=1,
            grid=(n // toks,),
            in_specs=[pl.BlockSpec(memory_space=pl.ANY)]
            + ([pl.BlockSpec(norm_g.shape, lambda i, p: (0, 0))] if norm else []),
            out_specs=(pl.BlockSpec((toks, d), lambda i, p: (i, 0)) if norm
                       else pl.BlockSpec((toks * TOK_ROWS, LANES), lambda i, p: (i, 0))),
            scratch_shapes=([pltpu.VMEM((toks * TOK_ROWS, LANES), F32)] if norm else [])
            + [pltpu.SemaphoreType.DMA(())],
        ),
        out_shape=jax.ShapeDtypeStruct((n, d) if norm else (n * TOK_ROWS, LANES), F32),
        compiler_params=_params("arbitrary"),
        name="tok_gather",
    )(pos, src, *([norm_g] if norm else []))


def _moe_sorted_kernel(e_lo, e_hi, valid, xs_ref, g_ref, rw_ref, rb_ref,
                       wg_lo, wu_lo, wd_lo, wg_hi, wu_hi, wd_hi, out_ref, wgu_scr, wd_scr):
    t = pl.program_id(0)
    tm = xs_ref.shape[0] // TOK_ROWS
    ff = wd_lo.shape[2]
    prev = jnp.maximum(t - 1, 0)

    for slot, ids, wg, wu, wd in ((0, e_lo, wg_lo, wu_lo, wd_lo), (1, e_hi, wg_hi, wu_hi, wd_hi)):
        @pl.when((t == 0) | (ids[t] != ids[prev]))
        def _():
            wgu_scr[slot, :, 0:ff] = wg[0, 0].astype(BF16)
            wgu_scr[slot, :, ff:] = wu[0, 0].astype(BF16)
            wd_scr[slot] = wd[0, 0].astype(BF16)

    @pl.when(valid[t] > 0)
    def _():
        x = _load_tok(xs_ref, tm)
        xb = _rms(x, g_ref[...]).astype(BF16)
        logits = _dot(xb, rw_ref[...]) + rb_ref[...]
        lane = lax.broadcasted_iota(jnp.int32, logits.shape, 1)
        gl = jnp.where(lane < N_GROUPS, logits, -jnp.inf)
        g_w = 1.0 / jnp.sum(jnp.exp(gl - jnp.max(gl, axis=-1, keepdims=True)), axis=-1, keepdims=True)
        l_lo = jnp.sum(jnp.where(lane == e_lo[t] + N_GROUPS, logits, 0.0), axis=-1, keepdims=True)
        l_hi = jnp.sum(jnp.where(lane == e_hi[t] + N_GROUPS, logits, 0.0), axis=-1, keepdims=True)
        tail = jnp.exp(-jnp.abs(l_lo - l_hi))
        w_top = g_w / (1.0 + tail)
        w_oth = g_w * tail / (1.0 + tail)
        lo_top = l_lo >= l_hi
        y = x
        for slot, w in ((0, jnp.where(lo_top, w_top, w_oth)), (1, jnp.where(lo_top, w_oth, w_top))):
            gu = _dot(xb, wgu_scr[slot])
            hid = _silu(gu[:, 0:ff]) * gu[:, ff:] * w
            y = y + _dot(hid.astype(BF16), wd_scr[slot])
        _store_tok(out_ref, y)

    @pl.when(valid[t] == 0)
    def _():
        out_ref[...] = jnp.zeros(out_ref.shape, F32)


def _moe_sorted(xs, g, rw, rb, wg, wu, wd, layer, e_lo, e_hi, valid, tm):
    npad = xs.shape[0] // TOK_ROWS
    _, _, d, ff = wg.shape
    const = lambda t, lo, hi, v: (0, 0)
    at_lo = lambda t, lo, hi, v: (layer, lo[t], 0, 0)
    at_hi = lambda t, lo, hi, v: (layer, hi[t], 0, 0)
    return pl.pallas_call(
        _moe_sorted_kernel,
        grid_spec=pltpu.PrefetchScalarGridSpec(
            num_scalar_prefetch=3,
            grid=(npad // tm,),
            in_specs=[
                pl.BlockSpec((tm * TOK_ROWS, LANES), lambda t, lo, hi, v: (jnp.where(v[t] > 0, t, 0), 0)),
                pl.BlockSpec(g.shape, const), pl.BlockSpec(rw.shape, const), pl.BlockSpec(rb.shape, const),
                pl.BlockSpec((1, 1, d, ff), at_lo), pl.BlockSpec((1, 1, d, ff), at_lo),
                pl.BlockSpec((1, 1, ff, d), at_lo),
                pl.BlockSpec((1, 1, d, ff), at_hi), pl.BlockSpec((1, 1, d, ff), at_hi),
                pl.BlockSpec((1, 1, ff, d), at_hi),
            ],
            out_specs=pl.BlockSpec((tm * TOK_ROWS, LANES), lambda t, lo, hi, v: (t, 0)),
            scratch_shapes=[pltpu.VMEM((2, d, 2 * ff), BF16), pltpu.VMEM((2, ff, d), BF16)],
        ),
        out_shape=jax.ShapeDtypeStruct(xs.shape, F32),
        compiler_params=_params("arbitrary"),
        name="moe_sorted",
    )(e_lo, e_hi, valid, xs, g, rw, rb, wg, wu, wd, wg, wu, wd)


def _moe_sparse(x, cls, rank, counts, g, rw, rb, wg, wu, wd, layer, tm, toks, norm_g=None):
    n = x.shape[0] // TOK_ROWS
    cls, rank = cls.reshape(n), rank.reshape(n)
    cnt = counts[0, :N_CLASSES]
    padded = ((cnt + tm - 1) // tm) * tm
    ends = jnp.cumsum(padded)
    pos = (ends - padded)[cls] + rank
    n_tiles = n // tm + N_CLASSES
    tile_start = jnp.arange(n_tiles, dtype=jnp.int32) * tm
    n_valid = ends[-1] // tm
    unused = n_valid + jnp.arange(N_CLASSES, dtype=jnp.int32)
    tail = jnp.concatenate([jnp.where(cnt > 0, ends // tm - 1, -1),
                            jnp.where(unused < n_tiles, unused, -1)]).astype(jnp.int32)
    last_cls = jnp.sum((tile_start[jnp.maximum(n_valid - 1, 0)] >= ends).astype(jnp.int32))
    tile_cls = jnp.sum((tile_start[:, None] >= ends[None, :]).astype(jnp.int32), axis=1)
    valid = (tile_start < ends[-1]).astype(jnp.int32)
    tile_cls = jnp.where(valid > 0, tile_cls, last_cls)
    pair_lo = jnp.array([0, 0, 0, 1, 1, 2], jnp.int32)
    pair_hi = jnp.array([1, 2, 3, 2, 3, 3], jnp.int32)
    e_lo = (tile_cls // N_PAIRS) * EXP_PER_GROUP + pair_lo[tile_cls % N_PAIRS]
    e_hi = (tile_cls // N_PAIRS) * EXP_PER_GROUP + pair_hi[tile_cls % N_PAIRS]

    xs = _tok_scatter(x, n_tiles * tm, pos, tail, tm, toks[0])
    ys = _moe_sorted(xs, g, rw.astype(BF16), rb, wg, wu, wd, layer, e_lo, e_hi, valid, tm)
    return _tok_gather(ys, pos, toks[1], norm_g)


def _moe_kernel(x_ref, g_ref, rw_ref, rb_ref, wg_ref, wu_ref, wd_ref, out_ref, xn_scr, comb_scr, acc_scr):
    e = pl.program_id(1)

    @pl.when(e == 0)
    def _():
        x = x_ref[...]
        xn = _rms(x, g_ref[...])
        xn_scr[...] = xn.astype(BF16)
        comb_scr[...] = _route(_dot(xn, rw_ref[...], HIGHEST) + rb_ref[...])
        acc_scr[...] = x

    xb = xn_scr[...]
    lane = lax.broadcasted_iota(jnp.int32, comb_scr.shape, 1)
    cmb = jnp.sum(jnp.where(lane == e + N_GROUPS, comb_scr[...], 0.0), axis=-1, keepdims=True)
    hid = _silu(_dot(xb, wg_ref[0, 0].astype(BF16))) * _dot(xb, wu_ref[0, 0].astype(BF16)) * cmb
    acc_scr[...] += _dot(hid.astype(BF16), wd_ref[0, 0].astype(BF16))

    @pl.when(e == pl.num_programs(1) - 1)
    def _():
        out_ref[...] = acc_scr[...]


def _moe(x, g, rw, rb, wg, wu, wd, layer, tm):
    n, d = x.shape
    _, ne, _, ff = wg.shape
    return pl.pallas_call(
        _moe_kernel,
        grid=(n // tm, ne),
        in_specs=[
            pl.BlockSpec((tm, d), lambda i, e: (i, 0)),
            _full(g.shape), _full(rw.shape), _full(rb.shape),
            pl.BlockSpec((1, 1, d, ff), lambda i, e: (layer, e, 0, 0)),
            pl.BlockSpec((1, 1, d, ff), lambda i, e: (layer, e, 0, 0)),
            pl.BlockSpec((1, 1, ff, d), lambda i, e: (layer, e, 0, 0)),
        ],
        out_specs=pl.BlockSpec((tm, d), lambda i, e: (i, 0)),
        out_shape=jax.ShapeDtypeStruct((n, d), F32),
        scratch_shapes=[pltpu.VMEM((tm, d), BF16), pltpu.VMEM((tm, LANES), F32), pltpu.VMEM((tm, d), F32)],
        compiler_params=_params("arbitrary", "arbitrary"),
        name="moe",
    )(x, g, rw, rb, wg, wu, wd)


def _norm_kernel(x_ref, g_ref, o_ref):
    o_ref[...] = _rms(x_ref[...], g_ref[...])


def _final_norm(x, g, tm):
    n, d = x.shape
    return pl.pallas_call(
        _norm_kernel,
        grid=(n // tm,),
        in_specs=[pl.BlockSpec((tm, d), lambda i: (i, 0)), _full(g.shape)],
        out_specs=pl.BlockSpec((tm, d), lambda i: (i, 0)),
        out_shape=jax.ShapeDtypeStruct((n, d), F32),
        compiler_params=_params("arbitrary"),
        name="final_norm",
    )(x, g)


def _mix_sample_in_kernel(x_ref, g_ref, win_ref, wlr_ref, gup_ref, gb_ref, cw_ref, b0_ref, b1_ref,
                          yc_ref, u_ref, q_ref, k_ref, v_ref, gate_ref, la_ref):
    hb = _rms(x_ref[...], g_ref[...]).astype(BF16)

    def proj(a, b):
        return _dot(hb, win_ref[:, a:b])

    u = proj(C_CC, C_CH) * proj(C_CH, C_Q)
    cw = cw_ref[...]
    yc_ref[...] = proj(C_CB, C_CC) * (b0_ref[...] * cw[0:1] + b1_ref[...] * cw[1:2] + u * cw[2:3])
    u_ref[...] = u
    q_ref[...] = proj(C_Q, C_K) * (GLA_DK ** -0.5)
    k_ref[...] = proj(C_K, C_V)
    v_ref[...] = proj(C_V, C_G)
    gate_ref[...] = proj(C_G, C_LR)
    lr = _dot(hb, wlr_ref[...])
    gate = _dot(lr.astype(BF16), gup_ref[...]) + gb_ref[...]
    la_ref[...] = _log_sigmoid(gate) * (1.0 / GLA_TAU)


def _mix_sample_in(x, g, win, wlr, gup, gb, cw, b0, b1):
    n = x.shape[0]
    args = (x, g, win, wlr, gup, gb, cw, b0, b1)
    widths = (CONV_W, CONV_W, GLA_K, GLA_K, GLA_V, GLA_V, GLA_K)
    return pl.pallas_call(
        _mix_sample_in_kernel,
        grid=(1,),
        in_specs=[_full(a.shape) for a in args],
        out_specs=[_full((n, w)) for w in widths],
        out_shape=[jax.ShapeDtypeStruct((n, w), F32) for w in widths],
        compiler_params=_params("arbitrary"),
        name="mix_sample_in",
    )(*args)


def _gla_step_kernel(q_ref, k_ref, v_ref, la_ref, s0_ref, s_ref, o_ref):
    nb = q_ref.shape[0]
    a = jnp.exp(la_ref[...])
    kb = k_ref[...].astype(BF16)
    qb = q_ref[...].astype(BF16)
    vf = v_ref[...].astype(BF16).astype(F32)
    rows = lax.broadcasted_iota(jnp.int32, (nb, 1), 0)
    spread = (lax.broadcasted_iota(jnp.int32, (nb, nb * GLA_DV), 0)
              == lax.broadcasted_iota(jnp.int32, (nb, nb * GLA_DV), 1) // GLA_DV)
    spread_b = spread.astype(BF16)
    for h in range(GLA_HEADS):
        ks = slice(h * GLA_DK, (h + 1) * GLA_DK)
        a_cols = _dot_tn(a[:, ks], spread.astype(F32), HIGHEST)
        k_cols = _dot_tn(kb[:, ks], spread_b)
        q_cols = _dot_tn(qb[:, ks], spread_b)
        o_h = jnp.zeros((nb, GLA_DV), F32)
        for n in range(nb):
            blk = slice(n * GLA_DV, (n + 1) * GLA_DV)
            v_row = vf[n:n + 1, h * GLA_DV:(h + 1) * GLA_DV]
            s_new = a_cols[:, blk] * s0_ref[0, n, h] + k_cols[:, blk] * v_row
            s_ref[n, h] = s_new
            o_h = jnp.where(rows == n, jnp.sum(q_cols[:, blk] * s_new, axis=0, keepdims=True), o_h)
        o_ref[:, h * GLA_DV:(h + 1) * GLA_DV] = o_h


def _gla_step(q, k, v, la, state, layer, nb):
    n = q.shape[0]
    sshape = (GLA_HEADS, GLA_DK, GLA_DV)
    return pl.pallas_call(
        _gla_step_kernel,
        grid=(n // nb,),
        in_specs=[
            pl.BlockSpec((nb, GLA_K), lambda i: (i, 0)),
            pl.BlockSpec((nb, GLA_K), lambda i: (i, 0)),
            pl.BlockSpec((nb, GLA_V), lambda i: (i, 0)),
            pl.BlockSpec((nb, GLA_K), lambda i: (i, 0)),
            pl.BlockSpec((1, nb) + sshape, lambda i: (layer, i, 0, 0, 0)),
        ],
        out_specs=[
            pl.BlockSpec((nb,) + sshape, lambda i: (i, 0, 0, 0)),
            pl.BlockSpec((nb, GLA_V), lambda i: (i, 0)),
        ],
        out_shape=[jax.ShapeDtypeStruct((n,) + sshape, F32), jax.ShapeDtypeStruct((n, GLA_V), F32)],
        compiler_params=_params("arbitrary"),
        name="gla_step",
    )(q, k, v, la, state)


def _mix_sample_out_kernel(x_ref, yc_ref, o_ref, gate_ref, gg_ref, wout_ref, gm_ref, wq_ref, x1_ref, q_ref):
    yg = _head_norm_gate(o_ref[...], gate_ref[...], gg_ref[...])
    y = _dot(yc_ref[...].astype(BF16), wout_ref[0:CONV_W, :]) + _dot(yg.astype(BF16), wout_ref[CONV_W:, :])
    x1 = x_ref[...] + y
    x1_ref[...] = x1
    q_ref[...] = _dot(_rms(x1, gm_ref[...]).astype(BF16), wq_ref[...])


def _mix_sample_out(x, yc, o, gate, gg, wout, gm, wq):
    args = (x, yc, o, gate, gg, wout, gm, wq)
    return pl.pallas_call(
        _mix_sample_out_kernel,
        grid=(1,),
        in_specs=[_full(a.shape) for a in args],
        out_specs=[_full(x.shape)] * 2,
        out_shape=[jax.ShapeDtypeStruct(x.shape, F32)] * 2,
        compiler_params=_params("arbitrary"),
        name="mix_sample_out",
    )(*args)


ATT_ROWS = 2 * MEM_HEADS


def _class_allreduce(x, op):
    n = x.shape[-1]
    shift = ATT_ROWS
    while shift < n:
        x = op(x, pltpu.roll(x, shift, axis=1))
        shift *= 2
    return x


def _att_sample_kernel(q_ref, k_ref, v_ref, o_ref):
    nb = q_ref.shape[0]
    ncol = k_ref.shape[2]
    diag = (lax.broadcasted_iota(jnp.int32, (ATT_ROWS, ncol), 0)
            == (lax.broadcasted_iota(jnp.int32, (ATT_ROWS, ncol), 1) & (ATT_ROWS - 1)))
    rows = lax.broadcasted_iota(jnp.int32, (nb, 1), 0)
    t = jnp.zeros((nb, ncol), F32)
    for n in range(nb):
        sc = _dot_nt(q_ref[n].astype(BF16), k_ref[0, n].astype(BF16))
        t = t + jnp.where(rows == n, jnp.sum(jnp.where(diag, sc, 0.0), axis=0, keepdims=True), 0.0)
    valid = (lax.broadcasted_iota(jnp.int32, (nb, ncol), 1) & (ATT_ROWS - 1)) < MEM_HEADS
    s = jnp.where(valid, (t + pltpu.roll(t, ncol - MEM_HEADS, axis=1)) * (MEM_DH ** -0.5), 0.0)
    e = jnp.where(valid, jnp.exp(s - _class_allreduce(s, jnp.maximum)), 0.0)
    den = jnp.where(valid, _class_allreduce(e, jnp.add), 1.0)
    p = e / den
    p = p + pltpu.roll(p, MEM_HEADS, axis=1)
    for n in range(nb):
        p_n = jnp.where(diag, jnp.broadcast_to(p[n:n + 1, :], (ATT_ROWS, ncol)), 0.0)
        o_ref[n] = _dot(p_n.astype(BF16), v_ref[0, n].astype(BF16))


def _att_sample(q, ck, cv, layer, nb):
    n = q.shape[0]
    ncol = ck.shape[2]
    return pl.pallas_call(
        _att_sample_kernel,
        grid=(n // nb,),
        in_specs=[
            pl.BlockSpec((nb, ATT_ROWS, LANES), lambda i: (i, 0, 0)),
            pl.BlockSpec((1, nb, ncol, LANES), lambda i: (layer, i, 0, 0)),
            pl.BlockSpec((1, nb, ncol, LANES), lambda i: (layer, i, 0, 0)),
        ],
        out_specs=pl.BlockSpec((nb, ATT_ROWS, LANES), lambda i: (i, 0, 0)),
        out_shape=jax.ShapeDtypeStruct((n, ATT_ROWS, LANES), F32),
        compiler_params=_params("arbitrary"),
        name="att_sample",
    )(q, ck, cv)


def _oproj_kernel(x_ref, o_ref, wo_ref, out_ref):
    out_ref[...] = x_ref[...] + _dot(o_ref[...].astype(BF16), wo_ref[...])


def _oproj(x, o, wo):
    return pl.pallas_call(
        _oproj_kernel,
        grid=(1,),
        in_specs=[_full(x.shape), _full(o.shape), _full(wo.shape)],
        out_specs=_full(x.shape),
        out_shape=jax.ShapeDtypeStruct(x.shape, F32),
        compiler_params=_params("arbitrary"),
        name="oproj",
    )(x, o, wo)


def kernel(x_prompt, x_sample, state_conv, state_gla, cache_mem_k, cache_mem_v, mem_prompt, norm_mix, w_in, conv_w, gla_gate_up, gla_gate_b, gla_out_norm, w_out, norm_mem, w_q, w_k, w_v, w_o, norm_ffn, router_group, router_group_b, router_expert, router_expert_b, w_gate, w_up, w_down, norm_final):
    depth = w_in.shape[0]
    bsz, t, d = x_prompt.shape
    ns = x_sample.shape[0]
    nm = mem_prompt.shape[1]
    n_tok = bsz * t

    tt = min(1024, t)
    tq = min(1024, t)
    tm_moe = min(1024, n_tok)
    tm_kv = min(512, bsz * nm)
    tm_sorted = 256
    rows_perm = (min(2048, n_tok), min(2048, n_tok))
    nb_gla = min(8, ns)
    nb_att = min(8, ns)

    row = lambda a: a.reshape(1, -1)
    mem2 = mem_prompt.reshape(bsz * nm, d)

    def tile_rows(c):
        c = c.reshape(depth, ns, nm, MEM_HEADS, 2, LANES).transpose(0, 1, 2, 4, 3, 5)
        return c.reshape(depth, ns, nm * ATT_ROWS, LANES)

    ck, cv = tile_rows(cache_mem_k), tile_rows(cache_mem_v)

    def untile_rows(c):
        c = c.reshape(depth, bsz, nm, 2, MEM_HEADS, LANES).transpose(0, 1, 2, 4, 3, 5)
        return c.reshape(depth, bsz, nm, MEM_HEADS, MEM_DH)

    mk_all, mv_all = _mem_kv(mem2, w_k.astype(BF16), w_v.astype(BF16), tm_kv)

    xp = x_prompt
    xs = x_sample.reshape(ns, d)
    conv_p, gla_p, conv_s, gla_s = [], [], [], []
    for l in range(depth):
        win = w_in[l, :, :C_LR].astype(BF16)
        wlr = jnp.pad(w_in[l, :, C_LR:], ((0, 0), (0, LANES - GLA_RANK))).astype(BF16)
        gup = jnp.pad(gla_gate_up[l], ((0, LANES - GLA_RANK), (0, 0))).astype(BF16)
        gb = row(gla_gate_b[l])
        gg = row(gla_out_norm[l])
        wout = w_out[l].astype(BF16)
        wq, wo = w_q[l].astype(BF16), w_o[l].astype(BF16)
        rw = jnp.concatenate([router_group[l], router_expert[l].transpose(1, 0, 2).reshape(d, N_EXPERTS)], axis=1)
        rw = jnp.pad(rw, ((0, 0), (0, LANES - rw.shape[1])))
        rb = jnp.concatenate([router_group_b[l], router_expert_b[l].reshape(-1)])
        rb = row(jnp.pad(rb, (0, LANES - rb.shape[0])))
        moe_w = (row(norm_ffn[l]), rw, rb, w_gate, w_up, w_down, l)

        xp, nbuf, ns_p = _mix_prompt(xp, (bsz, t, d), row(norm_mix[l]), win, wlr, gup, gb, conv_w[l], gg, wout, tt)
        conv_p.append(nbuf)
        gla_p.append(ns_p)
        xp, cls, rank, counts = _att_prompt(xp, row(norm_mem[l]), wq, wo, mk_all, mv_all, l,
                                            row(norm_ffn[l]), rw, rb, tq)
        xp = _moe_sparse(xp, cls, rank, counts, *moe_w, tm_sorted, rows_perm,
                         norm_g=row(norm_final) if l == depth - 1 else None)

        yc, u, q, k, v, gate, la = _mix_sample_in(
            xs, row(norm_mix[l]), win, wlr, gup, gb, conv_w[l], state_conv[l, :, 0], state_conv[l, :, 1])
        conv_s.append(jnp.stack([state_conv[l, :, 1], u], axis=1))
        s_new, o = _gla_step(q, k, v, la, state_gla, l, nb_gla)
        gla_s.append(s_new)
        wq_s = wq.reshape(d, MEM_HEADS, 2, LANES).transpose(0, 2, 1, 3).reshape(d, d)
        wo_s = wo.reshape(MEM_HEADS, 2, LANES, d).transpose(1, 0, 2, 3).reshape(d, d)
        xs, qa = _mix_sample_out(xs, yc, o, gate, gg, wout, row(norm_mem[l]), wq_s)
        oa = _att_sample(qa.reshape(ns, ATT_ROWS, LANES), ck, cv, l, nb_att)
        xs = _oproj(xs, oa.reshape(ns, d), wo_s)
        xs = _moe(xs, *moe_w, min(tm_moe, ns))

    y_prompt = xp.reshape(bsz, t, d)
    y_sample = _final_norm(xs, row(norm_final), ns).reshape(ns, 1, d)
    return (y_prompt, y_sample, jnp.stack(conv_p), jnp.stack(gla_p), untile_rows(mk_all), untile_rows(mv_all),
            jnp.stack(conv_s), jnp.stack(gla_s))
```

```python
import functools

import jax
import jax.numpy as jnp
from jax import lax
from jax.experimental import pallas as pl
from jax.experimental.pallas import tpu as pltpu

F32 = jnp.float32
BF16 = jnp.bfloat16
HIGHEST = lax.Precision.HIGHEST

EPS = 1e-6
CONV_W = 512
GLA_HEADS = 4
GLA_DK = 64
GLA_DV = 128
GLA_K = GLA_HEADS * GLA_DK
GLA_V = GLA_HEADS * GLA_DV
GLA_RANK = 16
GLA_TAU = 16.0
GLA_CHUNK = 64
MEM_HEADS = 4
MEM_DH = 256
N_GROUPS = 4
EXP_PER_GROUP = 4
N_EXPERTS = 16
LANES = 128
C_CB, C_CC, C_CH, C_Q, C_K, C_V, C_G, C_LR = 0, 512, 1024, 1536, 1792, 2048, 2560, 3072
VMEM_LIMIT = 52 * 1024 * 1024


def _params(*sem):
    return pltpu.CompilerParams(dimension_semantics=sem, vmem_limit_bytes=VMEM_LIMIT)


def _rms(x, g):
    return x * lax.rsqrt(jnp.mean(x * x, axis=-1, keepdims=True) + EPS) * g


def _dot(a, b, precision=None):
    return jnp.dot(a, b, precision=precision, preferred_element_type=F32)


def _dot_nt(a, b):
    return lax.dot_general(a, b, (((1,), (1,)), ((), ())), preferred_element_type=F32)


def _dot_tn(a, b, precision=None):
    return lax.dot_general(a, b, (((0,), (0,)), ((), ())), precision=precision,
                           preferred_element_type=F32)


def _dot_split(dot, mask, x):
    hi = x.astype(BF16)
    r1 = x - hi.astype(F32)
    mid = r1.astype(BF16)
    lo = (r1 - mid.astype(F32)).astype(BF16)
    return dot(mask, hi) + dot(mask, mid) + dot(mask, lo)


def _silu(x):
    return x / (1.0 + jnp.exp(-x))


def _log_sigmoid(x):
    return jnp.minimum(x, 0.0) - jnp.log1p(jnp.exp(-jnp.abs(x)))


def _head_norm_gate(o, g, gg):
    parts = []
    for h in range(GLA_HEADS):
        sl = slice(h * GLA_DV, (h + 1) * GLA_DV)
        parts.append(_rms(o[:, sl], gg[:, sl]))
    return jnp.concatenate(parts, axis=-1) * _silu(g)


def _full(shape):
    nd = len(shape)
    return pl.BlockSpec(shape, lambda *_: (0,) * nd)


def _mix_prompt_kernel(x_ref, g_ref, win_ref, wlr_ref, gup_ref, gb_ref, cw_ref, gg_ref, wout_ref,
                       x1_ref, conv_ref, s_ref, ubuf, s_scr, o_scr, *, tok_in):
    t = pl.program_id(1)
    tt = x1_ref.shape[1]

    @pl.when(t == 0)
    def _():
        ubuf[0:8, :] = jnp.zeros((8, CONV_W), F32)
        s_scr[...] = jnp.zeros(s_scr.shape, F32)

    x = _load_tok(x_ref, tt) if tok_in else x_ref[0]
    hb = _rms(x, g_ref[...]).astype(BF16)

    def proj(a, b):
        return _dot(hb, win_ref[:, a:b])

    u = proj(C_CC, C_CH) * proj(C_CH, C_Q)
    ubuf[8:8 + tt, :] = u
    cw = cw_ref[...]
    yc = proj(C_CB, C_CC) * (ubuf[6:6 + tt, :] * cw[0:1] + ubuf[7:7 + tt, :] * cw[1:2] + u * cw[2:3])
    ubuf[6:8, :] = u[tt - 2:tt, :]

    qs = proj(C_Q, C_K) * (GLA_DK ** -0.5)
    k = proj(C_K, C_V)
    v = proj(C_V, C_G)
    lr = _dot(hb, wlr_ref[...])
    gate = _dot(lr.astype(BF16), gup_ref[...]) + gb_ref[...]
    la = _log_sigmoid(gate) * (1.0 / GLA_TAU)

    c = GLA_CHUNK
    nc = tt // c
    iota = lambda shape, dim: lax.broadcasted_iota(jnp.int32, shape, dim)
    ltri = (iota((c, c), 0) >= iota((c, c), 1)).astype(BF16)
    b_wide = _dot_split(_dot, ltri, jnp.concatenate([la[j * c:(j + 1) * c] for j in range(nc)], axis=1))
    b_ends = jnp.concatenate([b_wide[c - 1:c, j * GLA_K:(j + 1) * GLA_K] for j in range(nc)], axis=0)
    spread = (iota((nc, LANES), 0) == iota((nc, LANES), 1)).astype(BF16)
    log_dec = _dot_split(lambda m, x: _dot_tn(x, m), spread, b_ends)
    head_feat = iota((GLA_K, GLA_K), 0) // c == iota((GLA_K, GLA_K), 1) // GLA_DK
    head_blk = iota((GLA_K, GLA_V), 0) // GLA_DK == iota((GLA_K, GLA_V), 1) // GLA_DV
    causal = iota((c, GLA_K), 0) >= iota((c, GLA_K), 1) % c
    for j in range(nc):
        r = slice(j * c, (j + 1) * c)
        b_c, q_c, k_c, v_c = b_wide[:, j * GLA_K:(j + 1) * GLA_K], qs[r], k[r], v[r]
        b_mid = b_c[c // 2:c // 2 + 1, :]
        b_last = b_c[c - 1:c, :]
        q_i = (q_c * jnp.exp(b_c - b_mid)).astype(BF16)
        k_i = k_c * jnp.exp(b_mid - b_c)
        k_dec = (k_c * jnp.exp(b_last - b_c)).astype(BF16)
        q_b = (q_c * jnp.exp(b_c)).astype(BF16)
        k_rows = jnp.where(head_feat, jnp.concatenate([k_i] * GLA_HEADS, axis=0), 0.0).astype(BF16)
        a = jnp.where(causal, _dot_nt(q_i, k_rows), 0.0).astype(BF16)
        v_blk = jnp.where(head_blk, jnp.concatenate([v_c] * GLA_HEADS, axis=0), 0.0).astype(BF16)
        s_prev = s_scr[...]
        o_scr[r, :] = _dot(jnp.concatenate([a, q_b], axis=1),
                           jnp.concatenate([v_blk, s_prev.astype(BF16)], axis=0))
        dec = jnp.exp(jnp.broadcast_to(log_dec[:, j:j + 1], (GLA_K, GLA_DV)))
        s_scr[...] = (jnp.concatenate([dec] * GLA_HEADS, axis=1) * s_prev
                      + jnp.where(head_blk, _dot_tn(k_dec, v_c.astype(BF16)), 0.0))

    yg = _head_norm_gate(o_scr[...], proj(C_G, C_LR), gg_ref[...])
    y = _dot(yc.astype(BF16), wout_ref[0:CONV_W, :]) + _dot(yg.astype(BF16), wout_ref[CONV_W:, :])
    x1_ref[0] = x + y

    @pl.when(t == pl.num_programs(1) - 1)
    def _():
        conv_ref[0] = u[tt - 2:tt, :]
        for h in range(GLA_HEADS):
            s_ref[0, h] = s_scr[h * GLA_DK:(h + 1) * GLA_DK, h * GLA_DV:(h + 1) * GLA_DV]


def _mix_prompt(x, shape, g, win, wlr, gup, gb, cw, gg, wout, tt):
    bsz, t, d = shape
    tok_in = x.ndim == 2
    nt = t // tt
    x_spec = (pl.BlockSpec((tt * TOK_ROWS, LANES), lambda b, i: (b * nt + i, 0)) if tok_in
              else pl.BlockSpec((1, tt, d), lambda b, i: (b, i, 0)))
    return pl.pallas_call(
        functools.partial(_mix_prompt_kernel, tok_in=tok_in),
        grid=(bsz, nt),
        in_specs=[
            x_spec,
            _full(g.shape), _full(win.shape), _full(wlr.shape), _full(gup.shape), _full(gb.shape),
            _full(cw.shape), _full(gg.shape), _full(wout.shape),
        ],
        out_specs=[
            pl.BlockSpec((1, tt, d), lambda b, i: (b, i, 0)),
            pl.BlockSpec((1, 2, CONV_W), lambda b, i: (b, 0, 0)),
            pl.BlockSpec((1, GLA_HEADS, GLA_DK, GLA_DV), lambda b, i: (b, 0, 0, 0)),
        ],
        out_shape=[
            jax.ShapeDtypeStruct((bsz, t, d), F32),
            jax.ShapeDtypeStruct((bsz, 2, CONV_W), F32),
            jax.ShapeDtypeStruct((bsz, GLA_HEADS, GLA_DK, GLA_DV), F32),
        ],
        scratch_shapes=[
            pltpu.VMEM((8 + tt, CONV_W), F32),
            pltpu.VMEM((GLA_K, GLA_V), F32),
            pltpu.VMEM((tt, GLA_V), F32),
        ],
        compiler_params=_params("arbitrary", "arbitrary"),
        name="mix_prompt",
    )(x, g, win, wlr, gup, gb, cw, gg, wout)


def _kv_kernel(m_ref, wk_ref, wv_ref, k_ref, v_ref):
    tm = m_ref.shape[0]
    mb = m_ref[...].astype(BF16)
    for w_ref, o_ref in ((wk_ref, k_ref), (wv_ref, v_ref)):
        y = _dot(mb, w_ref[0])
        for h in range(MEM_HEADS):
            for dt in range(MEM_DH // LANES):
                col = h * MEM_DH + dt * LANES
                o_ref[0, pl.ds(dt * MEM_HEADS + h, tm, stride=ATT_ROWS), :] = y[:, col:col + LANES]


def _mem_kv(mem, wk, wv, tm):
    n, d = mem.shape
    depth = wk.shape[0]
    w_spec = pl.BlockSpec((1, d, d), lambda l, i: (l, 0, 0))
    o_spec = pl.BlockSpec((1, tm * ATT_ROWS, LANES), lambda l, i: (l, i, 0))
    return pl.pallas_call(
        _kv_kernel,
        grid=(depth, n // tm),
        in_specs=[pl.BlockSpec((tm, d), lambda l, i: (i, 0)), w_spec, w_spec],
        out_specs=[o_spec] * 2,
        out_shape=[jax.ShapeDtypeStruct((depth, n * ATT_ROWS, LANES), F32)] * 2,
        compiler_params=_params("arbitrary", "arbitrary"),
        name="mem_kv",
    )(mem, wk, wv)


def _att_prompt_kernel(x_ref, g_ref, wq_ref, wo_ref, mk_ref, mv_ref, gf_ref, rw_ref, rb_ref, ltri_ref,
                       out_ref, cls_ref, rank_ref, cnt_ref, carry):
    @pl.when((pl.program_id(0) == 0) & (pl.program_id(1) == 0))
    def _():
        carry[...] = jnp.zeros(carry.shape, F32)

    x = x_ref[0]
    xb = _rms(x, g_ref[...]).astype(BF16)
    q = _dot(xb, wq_ref[...])
    nm = mk_ref.shape[1] // ATT_ROWS

    def head_rows(ref, h):
        return jnp.concatenate([ref[0, pl.ds(dt * MEM_HEADS + h, nm, stride=ATT_ROWS), :]
                                for dt in range(MEM_DH // LANES)], axis=-1).astype(BF16)

    outs = []
    for h in range(MEM_HEADS):
        sl = slice(h * MEM_DH, (h + 1) * MEM_DH)
        s = _dot_nt(q[:, sl].astype(BF16), head_rows(mk_ref, h)) * (MEM_DH ** -0.5)
        e = jnp.exp(s - jnp.max(s, axis=-1, keepdims=True))
        p = e / jnp.sum(e, axis=-1, keepdims=True)
        outs.append(_dot(p.astype(BF16), head_rows(mv_ref, h)))
    o = jnp.concatenate(outs, axis=-1)
    x2 = x + _dot(o.astype(BF16), wo_ref[...])
    _store_tok(out_ref, x2)
    _route_meta(x2, gf_ref, rw_ref, rb_ref, ltri_ref, carry, cls_ref, rank_ref, cnt_ref)


def _att_prompt(x, g, wq, wo, mk, mv, layer, gf, rw, rb, tq):
    bsz, t, d = x.shape
    assert d == TOK_ROWS * LANES
    nq = t // tq
    rows = mk.shape[1] // bsz
    kv_spec = pl.BlockSpec((1, rows, LANES), lambda b, i: (layer, b, 0))
    meta_spec = pl.BlockSpec((tq // LANES, LANES), lambda b, i: (b * nq + i, 0))
    rw_hi = rw.astype(BF16)
    rw_split = jnp.concatenate([rw_hi, (rw - rw_hi.astype(F32)).astype(BF16)], axis=1)
    ltri = jnp.tril(jnp.ones((LANES, LANES), BF16))
    n_tok = bsz * t
    return pl.pallas_call(
        _att_prompt_kernel,
        grid=(bsz, nq),
        in_specs=[
            pl.BlockSpec((1, tq, d), lambda b, i: (b, i, 0)),
            _full(g.shape), _full(wq.shape), _full(wo.shape),
            kv_spec, kv_spec,
            _full(gf.shape), _full(rw_split.shape), _full(rb.shape), _full(ltri.shape),
        ],
        out_specs=[
            pl.BlockSpec((tq * TOK_ROWS, LANES), lambda b, i: (b * nq + i, 0)),
            meta_spec, meta_spec, _full((1, LANES)),
        ],
        out_shape=[
            jax.ShapeDtypeStruct((n_tok * TOK_ROWS, LANES), F32),
            jax.ShapeDtypeStruct((n_tok // LANES, LANES), jnp.int32),
            jax.ShapeDtypeStruct((n_tok // LANES, LANES), jnp.int32),
            jax.ShapeDtypeStruct((1, LANES), jnp.int32),
        ],
        scratch_shapes=[pltpu.VMEM((1, LANES), F32)],
        compiler_params=_params("arbitrary", "arbitrary"),
        name="att_prompt",
    )(x, g, wq, wo, mk, mv, gf, rw_split, rb, ltri)


def _route_top2(logits):
    lane = lax.broadcasted_iota(jnp.int32, logits.shape, 1)
    lanef = lane.astype(F32)
    ninf = -jnp.inf
    big = 1e9
    gl = jnp.where(lane < N_GROUPS, logits, ninf)
    gmax = jnp.max(gl, axis=-1, keepdims=True)
    g_idx = jnp.min(jnp.where(gl == gmax, lanef, big), axis=-1, keepdims=True)
    g_w = 1.0 / jnp.sum(jnp.exp(gl - gmax), axis=-1, keepdims=True)
    grp = ((lane - N_GROUPS) >> 2).astype(F32)
    emask = (lane >= N_GROUPS) & (lane < N_GROUPS + N_EXPERTS) & (grp == g_idx)
    el = jnp.where(emask, logits, ninf)
    m1 = jnp.max(el, axis=-1, keepdims=True)
    i1 = jnp.min(jnp.where(el == m1, lanef, big), axis=-1, keepdims=True)
    el2 = jnp.where(lanef == i1, ninf, el)
    m2 = jnp.max(el2, axis=-1, keepdims=True)
    i2 = jnp.min(jnp.where(el2 == m2, lanef, big), axis=-1, keepdims=True)
    tail = jnp.exp(m2 - m1)
    w1 = g_w / (1.0 + tail)
    w2 = g_w * tail / (1.0 + tail)
    return g_idx, i1, i2, w1, w2


def _route(logits):
    _, i1, i2, w1, w2 = _route_top2(logits)
    lanef = lax.broadcasted_iota(jnp.int32, logits.shape, 1).astype(F32)
    return jnp.where(lanef == i1, w1, 0.0) + jnp.where(lanef == i2, w2, 0.0)


N_PAIRS = 6
N_CLASSES = N_GROUPS * N_PAIRS
TOK_ROWS = 8


def _load_tok(ref, n):
    return jnp.concatenate([ref[pl.ds(j, n, stride=TOK_ROWS), :] for j in range(TOK_ROWS)], axis=-1)


def _store_tok(ref, val):
    n = val.shape[0]
    for j in range(TOK_ROWS):
        ref[pl.ds(j, n, stride=TOK_ROWS), :] = val[:, j * LANES:(j + 1) * LANES]


def _lane_dense(col):
    eye = lax.broadcasted_iota(jnp.int32, (LANES, LANES), 0) == lax.broadcasted_iota(jnp.int32, (LANES, LANES), 1)
    rows = [jnp.sum(jnp.where(eye, col[b * LANES:(b + 1) * LANES], 0.0), axis=0, keepdims=True)
            for b in range(col.shape[0] // LANES)]
    return jnp.concatenate(rows, axis=0)


def _route_meta(x, g_ref, rw_ref, rb_ref, ltri_ref, carry, cls_ref, rank_ref, cnt_ref):
    n = x.shape[0]
    xn = _rms(x, g_ref[...])
    x_hi = xn.astype(BF16)
    x_lo = (xn - x_hi.astype(F32)).astype(BF16)
    both = _dot(x_hi, rw_ref[...])
    logits = (both[:, :LANES] + (_dot(x_lo, rw_ref[:, :LANES]) + both[:, LANES:])) + rb_ref[...]
    g_idx, i1, i2, _, _ = _route_top2(logits)
    lo = jnp.minimum(i1, i2) - N_GROUPS - EXP_PER_GROUP * g_idx
    hi = jnp.maximum(i1, i2) - N_GROUPS - EXP_PER_GROUP * g_idx
    cls = g_idx * N_PAIRS + lo * (7.0 - lo) * 0.5 + hi - lo - 1.0

    lane = lax.broadcasted_iota(jnp.int32, (n, LANES), 1)
    onehot = lane.astype(F32) == cls
    onehot_b = onehot.astype(BF16)
    blocks = []
    run = carry[...]
    for b in range(n // LANES):
        p_b = _dot(ltri_ref[...], onehot_b[b * LANES:(b + 1) * LANES]) + run
        blocks.append(p_b)
        run = p_b[LANES - 1:LANES, :]
    prefix = jnp.concatenate(blocks, axis=0)
    rank = jnp.sum(jnp.where(onehot, prefix, 0.0), axis=-1, keepdims=True) - 1.0
    carry[...] = prefix[n - 1:n, :]
    cls_ref[...] = _lane_dense(cls).astype(jnp.int32)
    rank_ref[...] = _lane_dense(rank).astype(jnp.int32)
    cnt_ref[...] = prefix[n - 1:n, :].astype(jnp.int32)


DMA_UNROLL = 8


def _tok_rows(t):
    return pl.ds(pl.multiple_of(t * TOK_ROWS, TOK_ROWS), TOK_ROWS)


def _tok_scatter_kernel(pos, tail, x_ref, dst_ref, stage, zeros, sems, *, toks, steps):
    i = pl.program_id(0)
    slot = i % 2
    base = i * toks
    rows = toks * TOK_ROWS
    tile_rows = zeros.shape[0]

    def wait_slot(s):
        pltpu.make_async_copy(stage.at[s], dst_ref.at[pl.ds(0, rows)], sems.at[s]).wait()

    @pl.when(i == 0)
    def _():
        zeros[...] = jnp.zeros(zeros.shape, F32)

        def fill(c):
            return pltpu.make_async_copy(
                zeros, dst_ref.at[pl.ds(pl.multiple_of(tail[c] * tile_rows, tile_rows), tile_rows)], sems.at[2])

        for c in range(2 * N_CLASSES):
            @pl.when(tail[c] >= 0)
            def _():
                fill(c).start()
        for c in range(2 * N_CLASSES):
            @pl.when(tail[c] >= 0)
            def _():
                fill(c).wait()

    @pl.when(i >= 2)
    def _():
        wait_slot(slot)

    stage[slot] = x_ref[...]

    def start(r8, c):
        for u in range(DMA_UNROLL):
            r = r8 * DMA_UNROLL + u
            pltpu.make_async_copy(stage.at[slot, _tok_rows(r)], dst_ref.at[_tok_rows(pos[base + r])],
                                  sems.at[slot]).start(priority=u % 2)
        return c

    lax.fori_loop(0, toks // DMA_UNROLL, start, 0)

    @pl.when(i == steps - 1)
    def _():
        wait_slot(slot)
        if steps >= 2:
            wait_slot(1 - slot)


def _tok_scatter(x, n_out, pos, tail, tile, toks):
    steps = x.shape[0] // (toks * TOK_ROWS)
    return pl.pallas_call(
        functools.partial(_tok_scatter_kernel, toks=toks, steps=steps),
        grid_spec=pltpu.PrefetchScalarGridSpec(
            num_scalar_prefetch=2,
            grid=(steps,),
            in_specs=[pl.BlockSpec((toks * TOK_ROWS, LANES), lambda i, p, tl: (i, 0))],
            out_specs=pl.BlockSpec(memory_space=pl.ANY),
            scratch_shapes=[pltpu.VMEM((2, toks * TOK_ROWS, LANES), F32), pltpu.VMEM((tile * TOK_ROWS, LANES), F32),
                            pltpu.SemaphoreType.DMA((3,))],
        ),
        out_shape=jax.ShapeDtypeStruct((n_out * TOK_ROWS, LANES), F32),
        compiler_params=_params("arbitrary"),
        name="tok_scatter",
    )(pos, tail, x)


def _tok_gather_kernel(pos, src_ref, *rest, toks, norm):
    if norm:
        g_ref, out_ref, stage, sem = rest
        dst = stage
    else:
        out_ref, sem = rest
        dst = out_ref
    base = pl.program_id(0) * toks

    def start(r8, c):
        for u in range(DMA_UNROLL):
            r = r8 * DMA_UNROLL + u
            pltpu.make_async_copy(src_ref.at[_tok_rows(pos[base + r])], dst.at[_tok_rows(r)],
                                  sem).start(priority=u % 2)
        return c

    lax.fori_loop(0, toks // DMA_UNROLL, start, 0)
    pltpu.make_async_copy(src_ref.at[pl.ds(0, toks * TOK_ROWS)], dst, sem).wait()
    if norm:
        out_ref[...] = _rms(_load_tok(stage, toks), g_ref[...])


def _tok_gather(src, pos, toks, norm_g=None):
    n = pos.shape[0]
    norm = norm_g is not None
    d = TOK_ROWS * LANES
    return pl.pallas_call(
        functools.partial(_tok_gather_kernel, toks=toks, norm=norm),
        grid_spec=pltpu.PrefetchScalarGridSpec(
            num_scalar_prefetch=1,
            grid=(n // toks,),
            in_specs=[pl.BlockSpec(memory_space=pl.ANY)]
            + ([pl.BlockSpec(norm_g.shape, lambda i, p: (0, 0))] if norm else []),
            out_specs=(pl.BlockSpec((toks, d), lambda i, p: (i, 0)) if norm
                       else pl.BlockSpec((toks * TOK_ROWS, LANES), lambda i, p: (i, 0))),
            scratch_shapes=([pltpu.VMEM((toks * TOK_ROWS, LANES), F32)] if norm else [])
            + [pltpu.SemaphoreType.DMA(())],
        ),
        out_shape=jax.ShapeDtypeStruct((n, d) if norm else (n * TOK_ROWS, LANES), F32),
        compiler_params=_params("arbitrary"),
        name="tok_gather",
    )(pos, src, *([norm_g] if norm else []))


def _moe_sorted_kernel(e_lo, e_hi, valid, xs_ref, g_ref, rw_ref, rb_ref,
                       wg_lo, wu_lo, wd_lo, wg_hi, wu_hi, wd_hi, out_ref, wgu_scr, wd_scr):
    t = pl.program_id(0)
    tm = xs_ref.shape[0] // TOK_ROWS
    ff = wd_lo.shape[2]
    prev = jnp.maximum(t - 1, 0)

    for slot, ids, wg, wu, wd in ((0, e_lo, wg_lo, wu_lo, wd_lo), (1, e_hi, wg_hi, wu_hi, wd_hi)):
        @pl.when((t == 0) | (ids[t] != ids[prev]))
        def _():
            wgu_scr[slot, :, 0:ff] = wg[0, 0].astype(BF16)
            wgu_scr[slot, :, ff:] = wu[0, 0].astype(BF16)
            wd_scr[slot] = wd[0, 0].astype(BF16)

    @pl.when(valid[t] > 0)
    def _():
        x = _load_tok(xs_ref, tm)
        xb = _rms(x, g_ref[...]).astype(BF16)
        logits = _dot(xb, rw_ref[...]) + rb_ref[...]
        lane = lax.broadcasted_iota(jnp.int32, logits.shape, 1)
        gl = jnp.where(lane < N_GROUPS, logits, -jnp.inf)
        g_w = 1.0 / jnp.sum(jnp.exp(gl - jnp.max(gl, axis=-1, keepdims=True)), axis=-1, keepdims=True)
        l_lo = jnp.sum(jnp.where(lane == e_lo[t] + N_GROUPS, logits, 0.0), axis=-1, keepdims=True)
        l_hi = jnp.sum(jnp.where(lane == e_hi[t] + N_GROUPS, logits, 0.0), axis=-1, keepdims=True)
        tail = jnp.exp(-jnp.abs(l_lo - l_hi))
        w_top = g_w / (1.0 + tail)
        w_oth = g_w * tail / (1.0 + tail)
        lo_top = l_lo >= l_hi
        y = x
        for slot, w in ((0, jnp.where(lo_top, w_top, w_oth)), (1, jnp.where(lo_top, w_oth, w_top))):
            gu = _dot(xb, wgu_scr[slot])
            hid = _silu(gu[:, 0:ff]) * gu[:, ff:] * w
            y = y + _dot(hid.astype(BF16), wd_scr[slot])
        _store_tok(out_ref, y)

    @pl.when(valid[t] == 0)
    def _():
        out_ref[...] = jnp.zeros(out_ref.shape, F32)


def _moe_sorted(xs, g, rw, rb, wg, wu, wd, layer, e_lo, e_hi, valid, tm):
    npad = xs.shape[0] // TOK_ROWS
    _, _, d, ff = wg.shape
    const = lambda t, lo, hi, v: (0, 0)
    at_lo = lambda t, lo, hi, v: (layer, lo[t], 0, 0)
    at_hi = lambda t, lo, hi, v: (layer, hi[t], 0, 0)
    return pl.pallas_call(
        _moe_sorted_kernel,
        grid_spec=pltpu.PrefetchScalarGridSpec(
            num_scalar_prefetch=3,
            grid=(npad // tm,),
            in_specs=[
                pl.BlockSpec((tm * TOK_ROWS, LANES), lambda t, lo, hi, v: (jnp.where(v[t] > 0, t, 0), 0)),
                pl.BlockSpec(g.shape, const), pl.BlockSpec(rw.shape, const), pl.BlockSpec(rb.shape, const),
                pl.BlockSpec((1, 1, d, ff), at_lo), pl.BlockSpec((1, 1, d, ff), at_lo),
                pl.BlockSpec((1, 1, ff, d), at_lo),
                pl.BlockSpec((1, 1, d, ff), at_hi), pl.BlockSpec((1, 1, d, ff), at_hi),
                pl.BlockSpec((1, 1, ff, d), at_hi),
            ],
            out_specs=pl.BlockSpec((tm * TOK_ROWS, LANES), lambda t, lo, hi, v: (t, 0)),
            scratch_shapes=[pltpu.VMEM((2, d, 2 * ff), BF16), pltpu.VMEM((2, ff, d), BF16)],
        ),
        out_shape=jax.ShapeDtypeStruct(xs.shape, F32),
        compiler_params=_params("arbitrary"),
        name="moe_sorted",
    )(e_lo, e_hi, valid, xs, g, rw, rb, wg, wu, wd, wg, wu, wd)


def _moe_sparse(x, cls, rank, counts, g, rw, rb, wg, wu, wd, layer, tm, toks, norm_g=None):
    n = x.shape[0] // TOK_ROWS
    cls, rank = cls.reshape(n), rank.reshape(n)
    cnt = counts[0, :N_CLASSES]
    padded = ((cnt + tm - 1) // tm) * tm
    ends = jnp.cumsum(padded)
    pos = (ends - padded)[cls] + rank
    n_tiles = n // tm + N_CLASSES
    tile_start = jnp.arange(n_tiles, dtype=jnp.int32) * tm
    n_valid = ends[-1] // tm
    unused = n_valid + jnp.arange(N_CLASSES, dtype=jnp.int32)
    tail = jnp.concatenate([jnp.where(cnt > 0, ends // tm - 1, -1),
                            jnp.where(unused < n_tiles, unused, -1)]).astype(jnp.int32)
    last_cls = jnp.sum((tile_start[jnp.maximum(n_valid - 1, 0)] >= ends).astype(jnp.int32))
    tile_cls = jnp.sum((tile_start[:, None] >= ends[None, :]).astype(jnp.int32), axis=1)
    valid = (tile_start < ends[-1]).astype(jnp.int32)
    tile_cls = jnp.where(valid > 0, tile_cls, last_cls)
    pair_lo = jnp.array([0, 0, 0, 1, 1, 2], jnp.int32)
    pair_hi = jnp.array([1, 2, 3, 2, 3, 3], jnp.int32)
    e_lo = (tile_cls // N_PAIRS) * EXP_PER_GROUP + pair_lo[tile_cls % N_PAIRS]
    e_hi = (tile_cls // N_PAIRS) * EXP_PER_GROUP + pair_hi[tile_cls % N_PAIRS]

    xs = _tok_scatter(x, n_tiles * tm, pos, tail, tm, toks[0])
    ys = _moe_sorted(xs, g, rw.astype(BF16), rb, wg, wu, wd, layer, e_lo, e_hi, valid, tm)
    return _tok_gather(ys, pos, toks[1], norm_g)


def _moe_kernel(x_ref, g_ref, rw_ref, rb_ref, wg_ref, wu_ref, wd_ref, out_ref, xn_scr, comb_scr, acc_scr):
    e = pl.program_id(1)

    @pl.when(e == 0)
    def _():
        x = x_ref[...]
        xn = _rms(x, g_ref[...])
        xn_scr[...] = xn.astype(BF16)
        comb_scr[...] = _route(_dot(xn, rw_ref[...], HIGHEST) + rb_ref[...])
        acc_scr[...] = x

    xb = xn_scr[...]
    lane = lax.broadcasted_iota(jnp.int32, comb_scr.shape, 1)
    cmb = jnp.sum(jnp.where(lane == e + N_GROUPS, comb_scr[...], 0.0), axis=-1, keepdims=True)
    hid = _silu(_dot(xb, wg_ref[0, 0].astype(BF16))) * _dot(xb, wu_ref[0, 0].astype(BF16)) * cmb
    acc_scr[...] += _dot(hid.astype(BF16), wd_ref[0, 0].astype(BF16))

    @pl.when(e == pl.num_programs(1) - 1)
    def _():
        out_ref[...] = acc_scr[...]


def _moe(x, g, rw, rb, wg, wu, wd, layer, tm):
    n, d = x.shape
    _, ne, _, ff = wg.shape
    return pl.pallas_call(
        _moe_kernel,
        grid=(n // tm, ne),
        in_specs=[
            pl.BlockSpec((tm, d), lambda i, e: (i, 0)),
            _full(g.shape), _full(rw.shape), _full(rb.shape),
            pl.BlockSpec((1, 1, d, ff), lambda i, e: (layer, e, 0, 0)),
            pl.BlockSpec((1, 1, d, ff), lambda i, e: (layer, e, 0, 0)),
            pl.BlockSpec((1, 1, ff, d), lambda i, e: (layer, e, 0, 0)),
        ],
        out_specs=pl.BlockSpec((tm, d), lambda i, e: (i, 0)),
        out_shape=jax.ShapeDtypeStruct((n, d), F32),
        scratch_shapes=[pltpu.VMEM((tm, d), BF16), pltpu.VMEM((tm, LANES), F32), pltpu.VMEM((tm, d), F32)],
        compiler_params=_params("arbitrary", "arbitrary"),
        name="moe",
    )(x, g, rw, rb, wg, wu, wd)


def _norm_kernel(x_ref, g_ref, o_ref):
    o_ref[...] = _rms(x_ref[...], g_ref[...])


def _final_norm(x, g, tm):
    n, d = x.shape
    return pl.pallas_call(
        _norm_kernel,
        grid=(n // tm,),
        in_specs=[pl.BlockSpec((tm, d), lambda i: (i, 0)), _full(g.shape)],
        out_specs=pl.BlockSpec((tm, d), lambda i: (i, 0)),
        out_shape=jax.ShapeDtypeStruct((n, d), F32),
        compiler_params=_params("arbitrary"),
        name="final_norm",
    )(x, g)


def _mix_sample_in_kernel(x_ref, g_ref, win_ref, wlr_ref, gup_ref, gb_ref, cw_ref, b0_ref, b1_ref,
                          yc_ref, u_ref, q_ref, k_ref, v_ref, gate_ref, la_ref):
    hb = _rms(x_ref[...], g_ref[...]).astype(BF16)

    def proj(a, b):
        return _dot(hb, win_ref[:, a:b])

    u = proj(C_CC, C_CH) * proj(C_CH, C_Q)
    cw = cw_ref[...]
    yc_ref[...] = proj(C_CB, C_CC) * (b0_ref[...] * cw[0:1] + b1_ref[...] * cw[1:2] + u * cw[2:3])
    u_ref[...] = u
    q_ref[...] = proj(C_Q, C_K) * (GLA_DK ** -0.5)
    k_ref[...] = proj(C_K, C_V)
    v_ref[...] = proj(C_V, C_G)
    gate_ref[...] = proj(C_G, C_LR)
    lr = _dot(hb, wlr_ref[...])
    gate = _dot(lr.astype(BF16), gup_ref[...]) + gb_ref[...]
    la_ref[...] = _log_sigmoid(gate) * (1.0 / GLA_TAU)


def _mix_sample_in(x, g, win, wlr, gup, gb, cw, b0, b1):
    n = x.shape[0]
    args = (x, g, win, wlr, gup, gb, cw, b0, b1)
    widths = (CONV_W, CONV_W, GLA_K, GLA_K, GLA_V, GLA_V, GLA_K)
    return pl.pallas_call(
        _mix_sample_in_kernel,
        grid=(1,),
        in_specs=[_full(a.shape) for a in args],
        out_specs=[_full((n, w)) for w in widths],
        out_shape=[jax.ShapeDtypeStruct((n, w), F32) for w in widths],
        compiler_params=_params("arbitrary"),
        name="mix_sample_in",
    )(*args)


def _gla_step_kernel(q_ref, k_ref, v_ref, la_ref, s0_ref, s_ref, o_ref):
    nb = q_ref.shape[0]
    a = jnp.exp(la_ref[...])
    kb = k_ref[...].astype(BF16)
    qb = q_ref[...].astype(BF16)
    vf = v_ref[...].astype(BF16).astype(F32)
    rows = lax.broadcasted_iota(jnp.int32, (nb, 1), 0)
    spread = (lax.broadcasted_iota(jnp.int32, (nb, nb * GLA_DV), 0)
              == lax.broadcasted_iota(jnp.int32, (nb, nb * GLA_DV), 1) // GLA_DV)
    spread_b = spread.astype(BF16)
    for h in range(GLA_HEADS):
        ks = slice(h * GLA_DK, (h + 1) * GLA_DK)
        a_cols = _dot_tn(a[:, ks], spread.astype(F32), HIGHEST)
        k_cols = _dot_tn(kb[:, ks], spread_b)
        q_cols = _dot_tn(qb[:, ks], spread_b)
        o_h = jnp.zeros((nb, GLA_DV), F32)
        for n in range(nb):
            blk = slice(n * GLA_DV, (n + 1) * GLA_DV)
            v_row = vf[n:n + 1, h * GLA_DV:(h + 1) * GLA_DV]
            s_new = a_cols[:, blk] * s0_ref[0, n, h] + k_cols[:, blk] * v_row
            s_ref[n, h] = s_new
            o_h = jnp.where(rows == n, jnp.sum(q_cols[:, blk] * s_new, axis=0, keepdims=True), o_h)
        o_ref[:, h * GLA_DV:(h + 1) * GLA_DV] = o_h


def _gla_step(q, k, v, la, state, layer, nb):
    n = q.shape[0]
    sshape = (GLA_HEADS, GLA_DK, GLA_DV)
    return pl.pallas_call(
        _gla_step_kernel,
        grid=(n // nb,),
        in_specs=[
            pl.BlockSpec((nb, GLA_K), lambda i: (i, 0)),
            pl.BlockSpec((nb, GLA_K), lambda i: (i, 0)),
            pl.BlockSpec((nb, GLA_V), lambda i: (i, 0)),
            pl.BlockSpec((nb, GLA_K), lambda i: (i, 0)),
            pl.BlockSpec((1, nb) + sshape, lambda i: (layer, i, 0, 0, 0)),
        ],
        out_specs=[
            pl.BlockSpec((nb,) + sshape, lambda i: (i, 0, 0, 0)),
            pl.BlockSpec((nb, GLA_V), lambda i: (i, 0)),
        ],
        out_shape=[jax.ShapeDtypeStruct((n,) + sshape, F32), jax.ShapeDtypeStruct((n, GLA_V), F32)],
        compiler_params=_params("arbitrary"),
        name="gla_step",
    )(q, k, v, la, state)


def _mix_sample_out_kernel(x_ref, yc_ref, o_ref, gate_ref, gg_ref, wout_ref, gm_ref, wq_ref, x1_ref, q_ref):
    yg = _head_norm_gate(o_ref[...], gate_ref[...], gg_ref[...])
    y = _dot(yc_ref[...].astype(BF16), wout_ref[0:CONV_W, :]) + _dot(yg.astype(BF16), wout_ref[CONV_W:, :])
    x1 = x_ref[...] + y
    x1_ref[...] = x1
    q_ref[...] = _dot(_rms(x1, gm_ref[...]).astype(BF16), wq_ref[...])


def _mix_sample_out(x, yc, o, gate, gg, wout, gm, wq):
    args = (x, yc, o, gate, gg, wout, gm, wq)
    return pl.pallas_call(
        _mix_sample_out_kernel,
        grid=(1,),
        in_specs=[_full(a.shape) for a in args],
        out_specs=[_full(x.shape)] * 2,
        out_shape=[jax.ShapeDtypeStruct(x.shape, F32)] * 2,
        compiler_params=_params("arbitrary"),
        name="mix_sample_out",
    )(*args)


ATT_ROWS = 2 * MEM_HEADS


def _class_allreduce(x, op):
    n = x.shape[-1]
    shift = ATT_ROWS
    while shift < n:
        x = op(x, pltpu.roll(x, shift, axis=1))
        shift *= 2
    return x


def _att_sample_kernel(q_ref, k_ref, v_ref, o_ref):
    nb = q_ref.shape[0]
    ncol = k_ref.shape[2]
    diag = (lax.broadcasted_iota(jnp.int32, (ATT_ROWS, ncol), 0)
            == (lax.broadcasted_iota(jnp.int32, (ATT_ROWS, ncol), 1) & (ATT_ROWS - 1)))
    rows = lax.broadcasted_iota(jnp.int32, (nb, 1), 0)
    t = jnp.zeros((nb, ncol), F32)
    for n in range(nb):
        sc = _dot_nt(q_ref[n].astype(BF16), k_ref[0, n].astype(BF16))
        t = t + jnp.where(rows == n, jnp.sum(jnp.where(diag, sc, 0.0), axis=0, keepdims=True), 0.0)
    valid = (lax.broadcasted_iota(jnp.int32, (nb, ncol), 1) & (ATT_ROWS - 1)) < MEM_HEADS
    s = jnp.where(valid, (t + pltpu.roll(t, ncol - MEM_HEADS, axis=1)) * (MEM_DH ** -0.5), 0.0)
    e = jnp.where(valid, jnp.exp(s - _class_allreduce(s, jnp.maximum)), 0.0)
    den = jnp.where(valid, _class_allreduce(e, jnp.add), 1.0)
    p = e / den
    p = p + pltpu.roll(p, MEM_HEADS, axis=1)
    for n in range(nb):
        p_n = jnp.where(diag, jnp.broadcast_to(p[n:n + 1, :], (ATT_ROWS, ncol)), 0.0)
        o_ref[n] = _dot(p_n.astype(BF16), v_ref[0, n].astype(BF16))


def _att_sample(q, ck, cv, layer, nb):
    n = q.shape[0]
    ncol = ck.shape[2]
    return pl.pallas_call(
        _att_sample_kernel,
        grid=(n // nb,),
        in_specs=[
            pl.BlockSpec((nb, ATT_ROWS, LANES), lambda i: (i, 0, 0)),
            pl.BlockSpec((1, nb, ncol, LANES), lambda i: (layer, i, 0, 0)),
            pl.BlockSpec((1, nb, ncol, LANES), lambda i: (layer, i, 0, 0)),
        ],
        out_specs=pl.BlockSpec((nb, ATT_ROWS, LANES), lambda i: (i, 0, 0)),
        out_shape=jax.ShapeDtypeStruct((n, ATT_ROWS, LANES), F32),
        compiler_params=_params("arbitrary"),
        name="att_sample",
    )(q, ck, cv)


def _oproj_kernel(x_ref, o_ref, wo_ref, out_ref):
    out_ref[...] = x_ref[...] + _dot(o_ref[...].astype(BF16), wo_ref[...])


def _oproj(x, o, wo):
    return pl.pallas_call(
        _oproj_kernel,
        grid=(1,),
        in_specs=[_full(x.shape), _full(o.shape), _full(wo.shape)],
        out_specs=_full(x.shape),
        out_shape=jax.ShapeDtypeStruct(x.shape, F32),
        compiler_params=_params("arbitrary"),
        name="oproj",
    )(x, o, wo)


def kernel(x_prompt, x_sample, state_conv, state_gla, cache_mem_k, cache_mem_v, mem_prompt, norm_mix, w_in, conv_w, gla_gate_up, gla_gate_b, gla_out_norm, w_out, norm_mem, w_q, w_k, w_v, w_o, norm_ffn, router_group, router_group_b, router_expert, router_expert_b, w_gate, w_up, w_down, norm_final):
    depth = w_in.shape[0]
    bsz, t, d = x_prompt.shape
    ns = x_sample.shape[0]
    nm = mem_prompt.shape[1]
    n_tok = bsz * t

    tt = min(1024, t)
    tq = min(1024, t)
    tm_moe = min(1024, n_tok)
    tm_kv = min(512, bsz * nm)
    tm_sorted = 256
    rows_perm = (min(2048, n_tok), min(2048, n_tok))
    nb_gla = min(8, ns)
    nb_att = min(8, ns)

    row = lambda a: a.reshape(1, -1)
    mem2 = mem_prompt.reshape(bsz * nm, d)

    def tile_rows(c):
        c = c.reshape(depth, ns, nm, MEM_HEADS, 2, LANES).transpose(0, 1, 2, 4, 3, 5)
        return c.reshape(depth, ns, nm * ATT_ROWS, LANES)

    ck, cv = tile_rows(cache_mem_k), tile_rows(cache_mem_v)

    def untile_rows(c):
        c = c.reshape(depth, bsz, nm, 2, MEM_HEADS, LANES).transpose(0, 1, 2, 4, 3, 5)
        return c.reshape(depth, bsz, nm, MEM_HEADS, MEM_DH)

    mk_all, mv_all = _mem_kv(mem2, w_k.astype(BF16), w_v.astype(BF16), tm_kv)

    xp = x_prompt
    xs = x_sample.reshape(ns, d)
    conv_p, gla_p, conv_s, gla_s = [], [], [], []
    for l in range(depth):
        win = w_in[l, :, :C_LR].astype(BF16)
        wlr = jnp.pad(w_in[l, :, C_LR:], ((0, 0), (0, LANES - GLA_RANK))).astype(BF16)
        gup = jnp.pad(gla_gate_up[l], ((0, LANES - GLA_RANK), (0, 0))).astype(BF16)
        gb = row(gla_gate_b[l])
        gg = row(gla_out_norm[l])
        wout = w_out[l].astype(BF16)
        wq, wo = w_q[l].astype(BF16), w_o[l].astype(BF16)
        rw = jnp.concatenate([router_group[l], router_expert[l].transpose(1, 0, 2).reshape(d, N_EXPERTS)], axis=1)
        rw = jnp.pad(rw, ((0, 0), (0, LANES - rw.shape[1])))
        rb = jnp.concatenate([router_group_b[l], router_expert_b[l].reshape(-1)])
        rb = row(jnp.pad(rb, (0, LANES - rb.shape[0])))
        moe_w = (row(norm_ffn[l]), rw, rb, w_gate, w_up, w_down, l)

        xp, nbuf, ns_p = _mix_prompt(xp, (bsz, t, d), row(norm_mix[l]), win, wlr, gup, gb, conv_w[l], gg, wout, tt)
        conv_p.append(nbuf)
        gla_p.append(ns_p)
        xp, cls, rank, counts = _att_prompt(xp, row(norm_mem[l]), wq, wo, mk_all, mv_all, l,
                                            row(norm_ffn[l]), rw, rb, tq)
        xp = _moe_sparse(xp, cls, rank, counts, *moe_w, tm_sorted, rows_perm,
                         norm_g=row(norm_final) if l == depth - 1 else None)

        yc, u, q, k, v, gate, la = _mix_sample_in(
            xs, row(norm_mix[l]), win, wlr, gup, gb, conv_w[l], state_conv[l, :, 0], state_conv[l, :, 1])
        conv_s.append(jnp.stack([state_conv[l, :, 1], u], axis=1))
        s_new, o = _gla_step(q, k, v, la, state_gla, l, nb_gla)
        gla_s.append(s_new)
        wq_s = wq.reshape(d, MEM_HEADS, 2, LANES).transpose(0, 2, 1, 3).reshape(d, d)
        wo_s = wo.reshape(MEM_HEADS, 2, LANES, d).transpose(1, 0, 2, 3).reshape(d, d)
        xs, qa = _mix_sample_out(xs, yc, o, gate, gg, wout, row(norm_mem[l]), wq_s)
        oa = _att_sample(qa.reshape(ns, ATT_ROWS, LANES), ck, cv, l, nb_att)
        xs = _oproj(xs, oa.reshape(ns, d), wo_s)
        xs = _moe(xs, *moe_w, min(tm_moe, ns))

    y_prompt = xp.reshape(bsz, t, d)
    y_sample = _final_norm(xs, row(norm_final), ns).reshape(ns, 1, d)
    return (y_prompt, y_sample, jnp.stack(conv_p), jnp.stack(gla_p), untile_rows(mk_all), untile_rows(mv_all),
            jnp.stack(conv_s), jnp.stack(gla_s))
```

```python
import functools

import jax
import jax.numpy as jnp
from jax import lax
from jax.experimental import pallas as pl
from jax.experimental.pallas import tpu as pltpu

F32 = jnp.float32
BF16 = jnp.bfloat16
HIGHEST = lax.Precision.HIGHEST

EPS = 1e-6
CONV_W = 512
GLA_HEADS = 4
GLA_DK = 64
GLA_DV = 128
GLA_K = GLA_HEADS * GLA_DK
GLA_V = GLA_HEADS * GLA_DV
GLA_RANK = 16
GLA_TAU = 16.0
GLA_CHUNK = 64
MEM_HEADS = 4
MEM_DH = 256
N_GROUPS = 4
EXP_PER_GROUP = 4
N_EXPERTS = 16
LANES = 128
C_CB, C_CC, C_CH, C_Q, C_K, C_V, C_G, C_LR = 0, 512, 1024, 1536, 1792, 2048, 2560, 3072
VMEM_LIMIT = 52 * 1024 * 1024


def _params(*sem):
    return pltpu.CompilerParams(dimension_semantics=sem, vmem_limit_bytes=VMEM_LIMIT)


def _rms(x, g):
    return x * lax.rsqrt(jnp.mean(x * x, axis=-1, keepdims=True) + EPS) * g


def _dot(a, b, precision=None):
    return jnp.dot(a, b, precision=precision, preferred_element_type=F32)


def _dot_nt(a, b):
    return lax.dot_general(a, b, (((1,), (1,)), ((), ())), preferred_element_type=F32)


def _dot_tn(a, b, precision=None):
    return lax.dot_general(a, b, (((0,), (0,)), ((), ())), precision=precision,
                           preferred_element_type=F32)


def _dot_split(dot, mask, x):
    hi = x.astype(BF16)
    r1 = x - hi.astype(F32)
    mid = r1.astype(BF16)
    lo = (r1 - mid.astype(F32)).astype(BF16)
    return dot(mask, hi) + dot(mask, mid) + dot(mask, lo)


def _silu(x):
    return x / (1.0 + jnp.exp(-x))


def _log_sigmoid(x):
    return jnp.minimum(x, 0.0) - jnp.log1p(jnp.exp(-jnp.abs(x)))


def _head_norm_gate(o, g, gg):
    parts = []
    for h in range(GLA_HEADS):
        sl = slice(h * GLA_DV, (h + 1) * GLA_DV)
        parts.append(_rms(o[:, sl], gg[:, sl]))
    return jnp.concatenate(parts, axis=-1) * _silu(g)


def _full(shape):
    nd = len(shape)
    return pl.BlockSpec(shape, lambda *_: (0,) * nd)


def _mix_prompt_kernel(x_ref, g_ref, win_ref, wlr_ref, gup_ref, gb_ref, cw_ref, gg_ref, wout_ref,
                       x1_ref, conv_ref, s_ref, ubuf, s_scr, o_scr, *, tok_in):
    t = pl.program_id(1)
    tt = x1_ref.shape[1]

    @pl.when(t == 0)
    def _():
        ubuf[0:8, :] = jnp.zeros((8, CONV_W), F32)
        s_scr[...] = jnp.zeros(s_scr.shape, F32)

    x = _load_tok(x_ref, tt) if tok_in else x_ref[0]
    hb = _rms(x, g_ref[...]).astype(BF16)

    def proj(a, b):
        return _dot(hb, win_ref[:, a:b])

    u = proj(C_CC, C_CH) * proj(C_CH, C_Q)
    ubuf[8:8 + tt, :] = u
    cw = cw_ref[...]
    yc = proj(C_CB, C_CC) * (ubuf[6:6 + tt, :] * cw[0:1] + ubuf[7:7 + tt, :] * cw[1:2] + u * cw[2:3])
    ubuf[6:8, :] = u[tt - 2:tt, :]

    qs = proj(C_Q, C_K) * (GLA_DK ** -0.5)
    k = proj(C_K, C_V)
    v = proj(C_V, C_G)
    lr = _dot(hb, wlr_ref[...])
    gate = _dot(lr.astype(BF16), gup_ref[...]) + gb_ref[...]
    la = _log_sigmoid(gate) * (1.0 / GLA_TAU)

    c = GLA_CHUNK
    nc = tt // c
    iota = lambda shape, dim: lax.broadcasted_iota(jnp.int32, shape, dim)
    ltri = (iota((c, c), 0) >= iota((c, c), 1)).astype(BF16)
    b_wide = _dot_split(_dot, ltri, jnp.concatenate([la[j * c:(j + 1) * c] for j in range(nc)], axis=1))
    b_ends = jnp.concatenate([b_wide[c - 1:c, j * GLA_K:(j + 1) * GLA_K] for j in range(nc)], axis=0)
    spread = (iota((nc, LANES), 0) == iota((nc, LANES), 1)).astype(BF16)
    log_dec = _dot_split(lambda m, x: _dot_tn(x, m), spread, b_ends)
    head_feat = iota((GLA_K, GLA_K), 0) // c == iota((GLA_K, GLA_K), 1) // GLA_DK
    head_blk = iota((GLA_K, GLA_V), 0) // GLA_DK == iota((GLA_K, GLA_V), 1) // GLA_DV
    causal = iota((c, GLA_K), 0) >= iota((c, GLA_K), 1) % c
    for j in range(nc):
        r = slice(j * c, (j + 1) * c)
        b_c, q_c, k_c, v_c = b_wide[:, j * GLA_K:(j + 1) * GLA_K], qs[r], k[r], v[r]
        b_mid = b_c[c // 2:c // 2 + 1, :]
        b_last = b_c[c - 1:c, :]
        q_i = (q_c * jnp.exp(b_c - b_mid)).astype(BF16)
        k_i = k_c * jnp.exp(b_mid - b_c)
        k_dec = (k_c * jnp.exp(b_last - b_c)).astype(BF16)
        q_b = (q_c * jnp.exp(b_c)).astype(BF16)
        k_rows = jnp.where(head_feat, jnp.concatenate([k_i] * GLA_HEADS, axis=0), 0.0).astype(BF16)
        a = jnp.where(causal, _dot_nt(q_i, k_rows), 0.0).astype(BF16)
        v_blk = jnp.where(head_blk, jnp.concatenate([v_c] * GLA_HEADS, axis=0), 0.0).astype(BF16)
        s_prev = s_scr[...]
        o_scr[r, :] = _dot(jnp.concatenate([a, q_b], axis=1),
                           jnp.concatenate([v_blk, s_prev.astype(BF16)], axis=0))
        dec = jnp.exp(jnp.broadcast_to(log_dec[:, j:j + 1], (GLA_K, GLA_DV)))
        s_scr[...] = (jnp.concatenate([dec] * GLA_HEADS, axis=1) * s_prev
                      + jnp.where(head_blk, _dot_tn(k_dec, v_c.astype(BF16)), 0.0))

    yg = _head_norm_gate(o_scr[...], proj(C_G, C_LR), gg_ref[...])
    y = _dot(yc.astype(BF16), wout_ref[0:CONV_W, :]) + _dot(yg.astype(BF16), wout_ref[CONV_W:, :])
    x1_ref[0] = x + y

    @pl.when(t == pl.num_programs(1) - 1)
    def _():
        conv_ref[0] = u[tt - 2:tt, :]
        for h in range(GLA_HEADS):
            s_ref[0, h] = s_scr[h * GLA_DK:(h + 1) * GLA_DK, h * GLA_DV:(h + 1) * GLA_DV]


def _mix_prompt(x, shape, g, win, wlr, gup, gb, cw, gg, wout, tt):
    bsz, t, d = shape
    tok_in = x.ndim == 2
    nt = t // tt
    x_spec = (pl.BlockSpec((tt * TOK_ROWS, LANES), lambda b, i: (b * nt + i, 0)) if tok_in
              else pl.BlockSpec((1, tt, d), lambda b, i: (b, i, 0)))
    return pl.pallas_call(
        functools.partial(_mix_prompt_kernel, tok_in=tok_in),
        grid=(bsz, nt),
        in_specs=[
            x_spec,
            _full(g.shape), _full(win.shape), _full(wlr.shape), _full(gup.shape), _full(gb.shape),
            _full(cw.shape), _full(gg.shape), _full(wout.shape),
        ],
        out_specs=[
            pl.BlockSpec((1, tt, d), lambda b, i: (b, i, 0)),
            pl.BlockSpec((1, 2, CONV_W), lambda b, i: (b, 0, 0)),
            pl.BlockSpec((1, GLA_HEADS, GLA_DK, GLA_DV), lambda b, i: (b, 0, 0, 0)),
        ],
        out_shape=[
            jax.ShapeDtypeStruct((bsz, t, d), F32),
            jax.ShapeDtypeStruct((bsz, 2, CONV_W), F32),
            jax.ShapeDtypeStruct((bsz, GLA_HEADS, GLA_DK, GLA_DV), F32),
        ],
        scratch_shapes=[
            pltpu.VMEM((8 + tt, CONV_W), F32),
            pltpu.VMEM((GLA_K, GLA_V), F32),
            pltpu.VMEM((tt, GLA_V), F32),
        ],
        compiler_params=_params("arbitrary", "arbitrary"),
        name="mix_prompt",
    )(x, g, win, wlr, gup, gb, cw, gg, wout)


def _kv_kernel(m_ref, wk_ref, wv_ref, k_ref, v_ref):
    tm = m_ref.shape[0]
    mb = m_ref[...].astype(BF16)
    for w_ref, o_ref in ((wk_ref, k_ref), (wv_ref, v_ref)):
        y = _dot(mb, w_ref[0])
        for h in range(MEM_HEADS):
            for dt in range(MEM_DH // LANES):
                col = h * MEM_DH + dt * LANES
                o_ref[0, pl.ds(dt * MEM_HEADS + h, tm, stride=ATT_ROWS), :] = y[:, col:col + LANES]


def _mem_kv(mem, wk, wv, tm):
    n, d = mem.shape
    depth = wk.shape[0]
    w_spec = pl.BlockSpec((1, d, d), lambda l, i: (l, 0, 0))
    o_spec = pl.BlockSpec((1, tm * ATT_ROWS, LANES), lambda l, i: (l, i, 0))
    return pl.pallas_call(
        _kv_kernel,
        grid=(depth, n // tm),
        in_specs=[pl.BlockSpec((tm, d), lambda l, i: (i, 0)), w_spec, w_spec],
        out_specs=[o_spec] * 2,
        out_shape=[jax.ShapeDtypeStruct((depth, n * ATT_ROWS, LANES), F32)] * 2,
        compiler_params=_params("arbitrary", "arbitrary"),
        name="mem_kv",
    )(mem, wk, wv)


def _att_prompt_kernel(x_ref, g_ref, wq_ref, wo_ref, mk_ref, mv_ref, gf_ref, rw_ref, rb_ref, ltri_ref,
                       out_ref, cls_ref, rank_ref, cnt_ref, carry):
    @pl.when((pl.program_id(0) == 0) & (pl.program_id(1) == 0))
    def _():
        carry[...] = jnp.zeros(carry.shape, F32)

    x = x_ref[0]
    xb = _rms(x, g_ref[...]).astype(BF16)
    q = _dot(xb, wq_ref[...])
    nm = mk_ref.shape[1] // ATT_ROWS

    def head_rows(ref, h):
        return jnp.concatenate([ref[0, pl.ds(dt * MEM_HEADS + h, nm, stride=ATT_ROWS), :]
                                for dt in range(MEM_DH // LANES)], axis=-1).astype(BF16)

    outs = []
    for h in range(MEM_HEADS):
        sl = slice(h * MEM_DH, (h + 1) * MEM_DH)
        s = _dot_nt(q[:, sl].astype(BF16), head_rows(mk_ref, h)) * (MEM_DH ** -0.5)
        e = jnp.exp(s - jnp.max(s, axis=-1, keepdims=True))
        p = e / jnp.sum(e, axis=-1, keepdims=True)
        outs.append(_dot(p.astype(BF16), head_rows(mv_ref, h)))
    o = jnp.concatenate(outs, axis=-1)
    x2 = x + _dot(o.astype(BF16), wo_ref[...])
    _store_tok(out_ref, x2)
    _route_meta(x2, gf_ref, rw_ref, rb_ref, ltri_ref, carry, cls_ref, rank_ref, cnt_ref)


def _att_prompt(x, g, wq, wo, mk, mv, layer, gf, rw, rb, tq):
    bsz, t, d = x.shape
    assert d == TOK_ROWS * LANES
    nq = t // tq
    rows = mk.shape[1] // bsz
    kv_spec = pl.BlockSpec((1, rows, LANES), lambda b, i: (layer, b, 0))
    meta_spec = pl.BlockSpec((tq // LANES, LANES), lambda b, i: (b * nq + i, 0))
    rw_hi = rw.astype(BF16)
    rw_split = jnp.concatenate([rw_hi, (rw - rw_hi.astype(F32)).astype(BF16)], axis=1)
    ltri = jnp.tril(jnp.ones((LANES, LANES), BF16))
    n_tok = bsz * t
    return pl.pallas_call(
        _att_prompt_kernel,
        grid=(bsz, nq),
        in_specs=[
            pl.BlockSpec((1, tq, d), lambda b, i: (b, i, 0)),
            _full(g.shape), _full(wq.shape), _full(wo.shape),
            kv_spec, kv_spec,
            _full(gf.shape), _full(rw_split.shape), _full(rb.shape), _full(ltri.shape),
        ],
        out_specs=[
            pl.BlockSpec((tq * TOK_ROWS, LANES), lambda b, i: (b * nq + i, 0)),
            meta_spec, meta_spec, _full((1, LANES)),
        ],
        out_shape=[
            jax.ShapeDtypeStruct((n_tok * TOK_ROWS, LANES), F32),
            jax.ShapeDtypeStruct((n_tok // LANES, LANES), jnp.int32),
            jax.ShapeDtypeStruct((n_tok // LANES, LANES), jnp.int32),
            jax.ShapeDtypeStruct((1, LANES), jnp.int32),
        ],
        scratch_shapes=[pltpu.VMEM((1, LANES), F32)],
        compiler_params=_params("arbitrary", "arbitrary"),
        name="att_prompt",
    )(x, g, wq, wo, mk, mv, gf, rw_split, rb, ltri)


def _route_top2(logits):
    lane = lax.broadcasted_iota(jnp.int32, logits.shape, 1)
    lanef = lane.astype(F32)
    ninf = -jnp.inf
    big = 1e9
    gl = jnp.where(lane < N_GROUPS, logits, ninf)
    gmax = jnp.max(gl, axis=-1, keepdims=True)
    g_idx = jnp.min(jnp.where(gl == gmax, lanef, big), axis=-1, keepdims=True)
    g_w = 1.0 / jnp.sum(jnp.exp(gl - gmax), axis=-1, keepdims=True)
    grp = ((lane - N_GROUPS) >> 2).astype(F32)
    emask = (lane >= N_GROUPS) & (lane < N_GROUPS + N_EXPERTS) & (grp == g_idx)
    el = jnp.where(emask, logits, ninf)
    m1 = jnp.max(el, axis=-1, keepdims=True)
    i1 = jnp.min(jnp.where(el == m1, lanef, big), axis=-1, keepdims=True)
    el2 = jnp.where(lanef == i1, ninf, el)
    m2 = jnp.max(el2, axis=-1, keepdims=True)
    i2 = jnp.min(jnp.where(el2 == m2, lanef, big), axis=-1, keepdims=True)
    tail = jnp.exp(m2 - m1)
    w1 = g_w / (1.0 + tail)
    w2 = g_w * tail / (1.0 + tail)
    return g_idx, i1, i2, w1, w2


def _route(logits):
    _, i1, i2, w1, w2 = _route_top2(logits)
    lanef = lax.broadcasted_iota(jnp.int32, logits.shape, 1).astype(F32)
    return jnp.where(lanef == i1, w1, 0.0) + jnp.where(lanef == i2, w2, 0.0)


N_PAIRS = 6
N_CLASSES = N_GROUPS * N_PAIRS
TOK_ROWS = 8


def _load_tok(ref, n):
    return jnp.concatenate([ref[pl.ds(j, n, stride=TOK_ROWS), :] for j in range(TOK_ROWS)], axis=-1)


def _store_tok(ref, val):
    n = val.shape[0]
    for j in range(TOK_ROWS):
        ref[pl.ds(j, n, stride=TOK_ROWS), :] = val[:, j * LANES:(j + 1) * LANES]


def _lane_dense(col):
    eye = lax.broadcasted_iota(jnp.int32, (LANES, LANES), 0) == lax.broadcasted_iota(jnp.int32, (LANES, LANES), 1)
    rows = [jnp.sum(jnp.where(eye, col[b * LANES:(b + 1) * LANES], 0.0), axis=0, keepdims=True)
            for b in range(col.shape[0] // LANES)]
    return jnp.concatenate(rows, axis=0)


def _route_meta(x, g_ref, rw_ref, rb_ref, ltri_ref, carry, cls_ref, rank_ref, cnt_ref):
    n = x.shape[0]
    xn = _rms(x, g_ref[...])
    x_hi = xn.astype(BF16)
    x_lo = (xn - x_hi.astype(F32)).astype(BF16)
    both = _dot(x_hi, rw_ref[...])
    logits = (both[:, :LANES] + (_dot(x_lo, rw_ref[:, :LANES]) + both[:, LANES:])) + rb_ref[...]
    g_idx, i1, i2, _, _ = _route_top2(logits)
    lo = jnp.minimum(i1, i2) - N_GROUPS - EXP_PER_GROUP * g_idx
    hi = jnp.maximum(i1, i2) - N_GROUPS - EXP_PER_GROUP * g_idx
    cls = g_idx * N_PAIRS + lo * (7.0 - lo) * 0.5 + hi - lo - 1.0

    lane = lax.broadcasted_iota(jnp.int32, (n, LANES), 1)
    onehot = lane.astype(F32) == cls
    onehot_b = onehot.astype(BF16)
    blocks = []
    run = carry[...]
    for b in range(n // LANES):
        p_b = _dot(ltri_ref[...], onehot_b[b * LANES:(b + 1) * LANES]) + run
        blocks.append(p_b)
        run = p_b[LANES - 1:LANES, :]
    prefix = jnp.concatenate(blocks, axis=0)
    rank = jnp.sum(jnp.where(onehot, prefix, 0.0), axis=-1, keepdims=True) - 1.0
    carry[...] = prefix[n - 1:n, :]
    cls_ref[...] = _lane_dense(cls).astype(jnp.int32)
    rank_ref[...] = _lane_dense(rank).astype(jnp.int32)
    cnt_ref[...] = prefix[n - 1:n, :].astype(jnp.int32)


DMA_UNROLL = 8


def _tok_rows(t):
    return pl.ds(pl.multiple_of(t * TOK_ROWS, TOK_ROWS), TOK_ROWS)


def _tok_scatter_kernel(cls, rank, starts, tail, x_ref, dst_ref, stage, zeros, sems, *, toks, steps):
    i = pl.program_id(0)
    slot = i % 2
    base = i * toks
    rows = toks * TOK_ROWS
    tile_rows = zeros.shape[0]

    def wait_slot(s):
        pltpu.make_async_copy(stage.at[s], dst_ref.at[pl.ds(0, rows)], sems.at[s]).wait()

    @pl.when(i == 0)
    def _():
        zeros[...] = jnp.zeros(zeros.shape, F32)

        def fill(c):
            return pltpu.make_async_copy(
                zeros, dst_ref.at[pl.ds(pl.multiple_of(tail[c] * tile_rows, tile_rows), tile_rows)], sems.at[2])

        for c in range(2 * N_CLASSES):
            @pl.when(tail[c] >= 0)
            def _():
                fill(c).start()
        for c in range(2 * N_CLASSES):
            @pl.when(tail[c] >= 0)
            def _():
                fill(c).wait()

    @pl.when(i >= 2)
    def _():
        wait_slot(slot)

    stage[slot] = x_ref[...]

    def start(r8, c):
        for u in range(DMA_UNROLL):
            r = r8 * DMA_UNROLL + u
            dst = starts[cls[base + r]] + rank[base + r]
            pltpu.make_async_copy(stage.at[slot, _tok_rows(r)], dst_ref.at[_tok_rows(dst)],
                                  sems.at[slot]).start(priority=u % 2)
        return c

    lax.fori_loop(0, toks // DMA_UNROLL, start, 0)

    @pl.when(i == steps - 1)
    def _():
        wait_slot(slot)
        if steps >= 2:
            wait_slot(1 - slot)


def _tok_scatter(x, n_out, cls, rank, starts, tail, tile, toks):
    steps = x.shape[0] // (toks * TOK_ROWS)
    return pl.pallas_call(
        functools.partial(_tok_scatter_kernel, toks=toks, steps=steps),
        grid_spec=pltpu.PrefetchScalarGridSpec(
            num_scalar_prefetch=4,
            grid=(steps,),
            in_specs=[pl.BlockSpec((toks * TOK_ROWS, LANES), lambda i, *_: (i, 0))],
            out_specs=pl.BlockSpec(memory_space=pl.ANY),
            scratch_shapes=[pltpu.VMEM((2, toks * TOK_ROWS, LANES), F32), pltpu.VMEM((tile * TOK_ROWS, LANES), F32),
                            pltpu.SemaphoreType.DMA((3,))],
        ),
        out_shape=jax.ShapeDtypeStruct((n_out * TOK_ROWS, LANES), F32),
        compiler_params=_params("arbitrary"),
        name="tok_scatter",
    )(cls, rank, starts, tail, x)


def _tok_gather_kernel(cls, rank, starts, src_ref, *rest, toks, norm):
    if norm:
        g_ref, out_ref, stage, sem = rest
        dst = stage
    else:
        out_ref, sem = rest
        dst = out_ref
    base = pl.program_id(0) * toks

    def start(r8, c):
        for u in range(DMA_UNROLL):
            r = r8 * DMA_UNROLL + u
            src = starts[cls[base + r]] + rank[base + r]
            pltpu.make_async_copy(src_ref.at[_tok_rows(src)], dst.at[_tok_rows(r)], sem).start(priority=u % 2)
        return c

    lax.fori_loop(0, toks // DMA_UNROLL, start, 0)
    pltpu.make_async_copy(src_ref.at[pl.ds(0, toks * TOK_ROWS)], dst, sem).wait()
    if norm:
        out_ref[...] = _rms(_load_tok(stage, toks), g_ref[...])


def _tok_gather(src, cls, rank, starts, toks, norm_g=None):
    n = cls.shape[0]
    norm = norm_g is not None
    d = TOK_ROWS * LANES
    return pl.pallas_call(
        functools.partial(_tok_gather_kernel, toks=toks, norm=norm),
        grid_spec=pltpu.PrefetchScalarGridSpec(
            num_scalar_prefetch=3,
            grid=(n // toks,),
            in_specs=[pl.BlockSpec(memory_space=pl.ANY)]
            + ([pl.BlockSpec(norm_g.shape, lambda i, *_: (0, 0))] if norm else []),
            out_specs=(pl.BlockSpec((toks, d), lambda i, *_: (i, 0)) if norm
                       else pl.BlockSpec((toks * TOK_ROWS, LANES), lambda i, *_: (i, 0))),
            scratch_shapes=([pltpu.VMEM((toks * TOK_ROWS, LANES), F32)] if norm else [])
            + [pltpu.SemaphoreType.DMA(())],
        ),
        out_shape=jax.ShapeDtypeStruct((n, d) if norm else (n * TOK_ROWS, LANES), F32),
        compiler_params=_params("arbitrary"),
        name="tok_gather",
    )(cls, rank, starts, src, *([norm_g] if norm else []))


def _moe_sorted_kernel(e_lo, e_hi, valid, xs_ref, g_ref, rw_ref, rb_ref,
                       wg_lo, wu_lo, wd_lo, wg_hi, wu_hi, wd_hi, out_ref, wgu_scr, wd_scr):
    t = pl.program_id(0)
    tm = xs_ref.shape[0] // TOK_ROWS
    ff = wd_lo.shape[2]
    prev = jnp.maximum(t - 1, 0)

    for slot, ids, wg, wu, wd in ((0, e_lo, wg_lo, wu_lo, wd_lo), (1, e_hi, wg_hi, wu_hi, wd_hi)):
        @pl.when((t == 0) | (ids[t] != ids[prev]))
        def _():
            wgu_scr[slot, :, 0:ff] = wg[0, 0].astype(BF16)
            wgu_scr[slot, :, ff:] = wu[0, 0].astype(BF16)
            wd_scr[slot] = wd[0, 0].astype(BF16)

    @pl.when(valid[t] > 0)
    def _():
        x = _load_tok(xs_ref, tm)
        xb = _rms(x, g_ref[...]).astype(BF16)
        logits = _dot(xb, rw_ref[...]) + rb_ref[...]
        lane = lax.broadcasted_iota(jnp.int32, logits.shape, 1)
        gl = jnp.where(lane < N_GROUPS, logits, -jnp.inf)
        g_w = 1.0 / jnp.sum(jnp.exp(gl - jnp.max(gl, axis=-1, keepdims=True)), axis=-1, keepdims=True)
        l_lo = jnp.sum(jnp.where(lane == e_lo[t] + N_GROUPS, logits, 0.0), axis=-1, keepdims=True)
        l_hi = jnp.sum(jnp.where(lane == e_hi[t] + N_GROUPS, logits, 0.0), axis=-1, keepdims=True)
        tail = jnp.exp(-jnp.abs(l_lo - l_hi))
        w_top = g_w / (1.0 + tail)
        w_oth = g_w * tail / (1.0 + tail)
        lo_top = l_lo >= l_hi
        y = x
        for slot, w in ((0, jnp.where(lo_top, w_top, w_oth)), (1, jnp.where(lo_top, w_oth, w_top))):
            gu = _dot(xb, wgu_scr[slot])
            hid = _silu(gu[:, 0:ff]) * gu[:, ff:] * w
            y = y + _dot(hid.astype(BF16), wd_scr[slot])
        _store_tok(out_ref, y)

    @pl.when(valid[t] == 0)
    def _():
        out_ref[...] = jnp.zeros(out_ref.shape, F32)


def _moe_sorted(xs, g, rw, rb, wg, wu, wd, layer, e_lo, e_hi, valid, tm):
    npad = xs.shape[0] // TOK_ROWS
    _, _, d, ff = wg.shape
    const = lambda t, lo, hi, v: (0, 0)
    at_lo = lambda t, lo, hi, v: (layer, lo[t], 0, 0)
    at_hi = lambda t, lo, hi, v: (layer, hi[t], 0, 0)
    return pl.pallas_call(
        _moe_sorted_kernel,
        grid_spec=pltpu.PrefetchScalarGridSpec(
            num_scalar_prefetch=3,
            grid=(npad // tm,),
            in_specs=[
                pl.BlockSpec((tm * TOK_ROWS, LANES), lambda t, lo, hi, v: (jnp.where(v[t] > 0, t, 0), 0)),
                pl.BlockSpec(g.shape, const), pl.BlockSpec(rw.shape, const), pl.BlockSpec(rb.shape, const),
                pl.BlockSpec((1, 1, d, ff), at_lo), pl.BlockSpec((1, 1, d, ff), at_lo),
                pl.BlockSpec((1, 1, ff, d), at_lo),
                pl.BlockSpec((1, 1, d, ff), at_hi), pl.BlockSpec((1, 1, d, ff), at_hi),
                pl.BlockSpec((1, 1, ff, d), at_hi),
            ],
            out_specs=pl.BlockSpec((tm * TOK_ROWS, LANES), lambda t, lo, hi, v: (t, 0)),
            scratch_shapes=[pltpu.VMEM((2, d, 2 * ff), BF16), pltpu.VMEM((2, ff, d), BF16)],
        ),
        out_shape=jax.ShapeDtypeStruct(xs.shape, F32),
        compiler_params=_params("arbitrary"),
        name="moe_sorted",
    )(e_lo, e_hi, valid, xs, g, rw, rb, wg, wu, wd, wg, wu, wd)


def _moe_sparse(x, cls, rank, counts, g, rw, rb, wg, wu, wd, layer, tm, toks, norm_g=None):
    n = x.shape[0] // TOK_ROWS
    cls, rank = cls.reshape(n), rank.reshape(n)
    cnt = counts[0, :N_CLASSES]
    padded = ((cnt + tm - 1) // tm) * tm
    ends = jnp.cumsum(padded)
    starts = (ends - padded).astype(jnp.int32)
    n_tiles = n // tm + N_CLASSES
    tile_start = jnp.arange(n_tiles, dtype=jnp.int32) * tm
    n_valid = ends[-1] // tm
    unused = n_valid + jnp.arange(N_CLASSES, dtype=jnp.int32)
    tail = jnp.concatenate([jnp.where(cnt > 0, ends // tm - 1, -1),
                            jnp.where(unused < n_tiles, unused, -1)]).astype(jnp.int32)
    last_cls = jnp.sum((tile_start[jnp.maximum(n_valid - 1, 0)] >= ends).astype(jnp.int32))
    tile_cls = jnp.sum((tile_start[:, None] >= ends[None, :]).astype(jnp.int32), axis=1)
    valid = (tile_start < ends[-1]).astype(jnp.int32)
    tile_cls = jnp.where(valid > 0, tile_cls, last_cls)
    pair_lo = jnp.array([0, 0, 0, 1, 1, 2], jnp.int32)
    pair_hi = jnp.array([1, 2, 3, 2, 3, 3], jnp.int32)
    e_lo = (tile_cls // N_PAIRS) * EXP_PER_GROUP + pair_lo[tile_cls % N_PAIRS]
    e_hi = (tile_cls // N_PAIRS) * EXP_PER_GROUP + pair_hi[tile_cls % N_PAIRS]

    xs = _tok_scatter(x, n_tiles * tm, cls, rank, starts, tail, tm, toks[0])
    ys = _moe_sorted(xs, g, rw.astype(BF16), rb, wg, wu, wd, layer, e_lo, e_hi, valid, tm)
    return _tok_gather(ys, cls, rank, starts, toks[1], norm_g)


def _moe_kernel(x_ref, g_ref, rw_ref, rb_ref, wg_ref, wu_ref, wd_ref, out_ref, xn_scr, comb_scr, acc_scr):
    e = pl.program_id(1)

    @pl.when(e == 0)
    def _():
        x = x_ref[...]
        xn = _rms(x, g_ref[...])
        xn_scr[...] = xn.astype(BF16)
        comb_scr[...] = _route(_dot(xn, rw_ref[...], HIGHEST) + rb_ref[...])
        acc_scr[...] = x

    xb = xn_scr[...]
    lane = lax.broadcasted_iota(jnp.int32, comb_scr.shape, 1)
    cmb = jnp.sum(jnp.where(lane == e + N_GROUPS, comb_scr[...], 0.0), axis=-1, keepdims=True)
    hid = _silu(_dot(xb, wg_ref[0, 0].astype(BF16))) * _dot(xb, wu_ref[0, 0].astype(BF16)) * cmb
    acc_scr[...] += _dot(hid.astype(BF16), wd_ref[0, 0].astype(BF16))

    @pl.when(e == pl.num_programs(1) - 1)
    def _():
        out_ref[...] = acc_scr[...]


def _moe(x, g, rw, rb, wg, wu, wd, layer, tm):
    n, d = x.shape
    _, ne, _, ff = wg.shape
    return pl.pallas_call(
        _moe_kernel,
        grid=(n // tm, ne),
        in_specs=[
            pl.BlockSpec((tm, d), lambda i, e: (i, 0)),
            _full(g.shape), _full(rw.shape), _full(rb.shape),
            pl.BlockSpec((1, 1, d, ff), lambda i, e: (layer, e, 0, 0)),
            pl.BlockSpec((1, 1, d, ff), lambda i, e: (layer, e, 0, 0)),
            pl.BlockSpec((1, 1, ff, d), lambda i, e: (layer, e, 0, 0)),
        ],
        out_specs=pl.BlockSpec((tm, d), lambda i, e: (i, 0)),
        out_shape=jax.ShapeDtypeStruct((n, d), F32),
        scratch_shapes=[pltpu.VMEM((tm, d), BF16), pltpu.VMEM((tm, LANES), F32), pltpu.VMEM((tm, d), F32)],
        compiler_params=_params("arbitrary", "arbitrary"),
        name="moe",
    )(x, g, rw, rb, wg, wu, wd)


def _norm_kernel(x_ref, g_ref, o_ref):
    o_ref[...] = _rms(x_ref[...], g_ref[...])


def _final_norm(x, g, tm):
    n, d = x.shape
    return pl.pallas_call(
        _norm_kernel,
        grid=(n // tm,),
        in_specs=[pl.BlockSpec((tm, d), lambda i: (i, 0)), _full(g.shape)],
        out_specs=pl.BlockSpec((tm, d), lambda i: (i, 0)),
        out_shape=jax.ShapeDtypeStruct((n, d), F32),
        compiler_params=_params("arbitrary"),
        name="final_norm",
    )(x, g)


def _mix_sample_in_kernel(x_ref, g_ref, win_ref, wlr_ref, gup_ref, gb_ref, cw_ref, b0_ref, b1_ref,
                          yc_ref, u_ref, q_ref, k_ref, v_ref, gate_ref, la_ref):
    hb = _rms(x_ref[...], g_ref[...]).astype(BF16)

    def proj(a, b):
        return _dot(hb, win_ref[:, a:b])

    u = proj(C_CC, C_CH) * proj(C_CH, C_Q)
    cw = cw_ref[...]
    yc_ref[...] = proj(C_CB, C_CC) * (b0_ref[...] * cw[0:1] + b1_ref[...] * cw[1:2] + u * cw[2:3])
    u_ref[...] = u
    q_ref[...] = proj(C_Q, C_K) * (GLA_DK ** -0.5)
    k_ref[...] = proj(C_K, C_V)
    v_ref[...] = proj(C_V, C_G)
    gate_ref[...] = proj(C_G, C_LR)
    lr = _dot(hb, wlr_ref[...])
    gate = _dot(lr.astype(BF16), gup_ref[...]) + gb_ref[...]
    la_ref[...] = _log_sigmoid(gate) * (1.0 / GLA_TAU)


def _mix_sample_in(x, g, win, wlr, gup, gb, cw, b0, b1):
    n = x.shape[0]
    args = (x, g, win, wlr, gup, gb, cw, b0, b1)
    widths = (CONV_W, CONV_W, GLA_K, GLA_K, GLA_V, GLA_V, GLA_K)
    return pl.pallas_call(
        _mix_sample_in_kernel,
        grid=(1,),
        in_specs=[_full(a.shape) for a in args],
        out_specs=[_full((n, w)) for w in widths],
        out_shape=[jax.ShapeDtypeStruct((n, w), F32) for w in widths],
        compiler_params=_params("arbitrary"),
        name="mix_sample_in",
    )(*args)


def _gla_step_kernel(q_ref, k_ref, v_ref, la_ref, s0_ref, s_ref, o_ref):
    nb = q_ref.shape[0]
    a = jnp.exp(la_ref[...])
    kb = k_ref[...].astype(BF16)
    qb = q_ref[...].astype(BF16)
    vf = v_ref[...].astype(BF16).astype(F32)
    rows = lax.broadcasted_iota(jnp.int32, (nb, 1), 0)
    spread = (lax.broadcasted_iota(jnp.int32, (nb, nb * GLA_DV), 0)
              == lax.broadcasted_iota(jnp.int32, (nb, nb * GLA_DV), 1) // GLA_DV)
    spread_b = spread.astype(BF16)
    for h in range(GLA_HEADS):
        ks = slice(h * GLA_DK, (h + 1) * GLA_DK)
        a_cols = _dot_tn(a[:, ks], spread.astype(F32), HIGHEST)
        k_cols = _dot_tn(kb[:, ks], spread_b)
        q_cols = _dot_tn(qb[:, ks], spread_b)
        o_h = jnp.zeros((nb, GLA_DV), F32)
        for n in range(nb):
            blk = slice(n * GLA_DV, (n + 1) * GLA_DV)
            v_row = vf[n:n + 1, h * GLA_DV:(h + 1) * GLA_DV]
            s_new = a_cols[:, blk] * s0_ref[0, n, h] + k_cols[:, blk] * v_row
            s_ref[n, h] = s_new
            o_h = jnp.where(rows == n, jnp.sum(q_cols[:, blk] * s_new, axis=0, keepdims=True), o_h)
        o_ref[:, h * GLA_DV:(h + 1) * GLA_DV] = o_h


def _gla_step(q, k, v, la, state, layer, nb):
    n = q.shape[0]
    sshape = (GLA_HEADS, GLA_DK, GLA_DV)
    return pl.pallas_call(
        _gla_step_kernel,
        grid=(n // nb,),
        in_specs=[
            pl.BlockSpec((nb, GLA_K), lambda i: (i, 0)),
            pl.BlockSpec((nb, GLA_K), lambda i: (i, 0)),
            pl.BlockSpec((nb, GLA_V), lambda i: (i, 0)),
            pl.BlockSpec((nb, GLA_K), lambda i: (i, 0)),
            pl.BlockSpec((1, nb) + sshape, lambda i: (layer, i, 0, 0, 0)),
        ],
        out_specs=[
            pl.BlockSpec((nb,) + sshape, lambda i: (i, 0, 0, 0)),
            pl.BlockSpec((nb, GLA_V), lambda i: (i, 0)),
        ],
        out_shape=[jax.ShapeDtypeStruct((n,) + sshape, F32), jax.ShapeDtypeStruct((n, GLA_V), F32)],
        compiler_params=_params("arbitrary"),
        name="gla_step",
    )(q, k, v, la, state)


def _mix_sample_out_kernel(x_ref, yc_ref, o_ref, gate_ref, gg_ref, wout_ref, gm_ref, wq_ref, x1_ref, q_ref):
    yg = _head_norm_gate(o_ref[...], gate_ref[...], gg_ref[...])
    y = _dot(yc_ref[...].astype(BF16), wout_ref[0:CONV_W, :]) + _dot(yg.astype(BF16), wout_ref[CONV_W:, :])
    x1 = x_ref[...] + y
    x1_ref[...] = x1
    q_ref[...] = _dot(_rms(x1, gm_ref[...]).astype(BF16), wq_ref[...])


def _mix_sample_out(x, yc, o, gate, gg, wout, gm, wq):
    args = (x, yc, o, gate, gg, wout, gm, wq)
    return pl.pallas_call(
        _mix_sample_out_kernel,
        grid=(1,),
        in_specs=[_full(a.shape) for a in args],
        out_specs=[_full(x.shape)] * 2,
        out_shape=[jax.ShapeDtypeStruct(x.shape, F32)] * 2,
        compiler_params=_params("arbitrary"),
        name="mix_sample_out",
    )(*args)


ATT_ROWS = 2 * MEM_HEADS


def _class_allreduce(x, op):
    n = x.shape[-1]
    shift = ATT_ROWS
    while shift < n:
        x = op(x, pltpu.roll(x, shift, axis=1))
        shift *= 2
    return x


def _att_sample_kernel(q_ref, k_ref, v_ref, o_ref):
    nb = q_ref.shape[0]
    ncol = k_ref.shape[2]
    diag = (lax.broadcasted_iota(jnp.int32, (ATT_ROWS, ncol), 0)
            == (lax.broadcasted_iota(jnp.int32, (ATT_ROWS, ncol), 1) & (ATT_ROWS - 1)))
    rows = lax.broadcasted_iota(jnp.int32, (nb, 1), 0)
    t = jnp.zeros((nb, ncol), F32)
    for n in range(nb):
        sc = _dot_nt(q_ref[n].astype(BF16), k_ref[0, n].astype(BF16))
        t = t + jnp.where(rows == n, jnp.sum(jnp.where(diag, sc, 0.0), axis=0, keepdims=True), 0.0)
    valid = (lax.broadcasted_iota(jnp.int32, (nb, ncol), 1) & (ATT_ROWS - 1)) < MEM_HEADS
    s = jnp.where(valid, (t + pltpu.roll(t, ncol - MEM_HEADS, axis=1)) * (MEM_DH ** -0.5), 0.0)
    e = jnp.where(valid, jnp.exp(s - _class_allreduce(s, jnp.maximum)), 0.0)
    den = jnp.where(valid, _class_allreduce(e, jnp.add), 1.0)
    p = e / den
    p = p + pltpu.roll(p, MEM_HEADS, axis=1)
    for n in range(nb):
        p_n = jnp.where(diag, jnp.broadcast_to(p[n:n + 1, :], (ATT_ROWS, ncol)), 0.0)
        o_ref[n] = _dot(p_n.astype(BF16), v_ref[0, n].astype(BF16))


def _att_sample(q, ck, cv, layer, nb):
    n = q.shape[0]
    ncol = ck.shape[2]
    return pl.pallas_call(
        _att_sample_kernel,
        grid=(n // nb,),
        in_specs=[
            pl.BlockSpec((nb, ATT_ROWS, LANES), lambda i: (i, 0, 0)),
            pl.BlockSpec((1, nb, ncol, LANES), lambda i: (layer, i, 0, 0)),
            pl.BlockSpec((1, nb, ncol, LANES), lambda i: (layer, i, 0, 0)),
        ],
        out_specs=pl.BlockSpec((nb, ATT_ROWS, LANES), lambda i: (i, 0, 0)),
        out_shape=jax.ShapeDtypeStruct((n, ATT_ROWS, LANES), F32),
        compiler_params=_params("arbitrary"),
        name="att_sample",
    )(q, ck, cv)


def _oproj_kernel(x_ref, o_ref, wo_ref, out_ref):
    out_ref[...] = x_ref[...] + _dot(o_ref[...].astype(BF16), wo_ref[...])


def _oproj(x, o, wo):
    return pl.pallas_call(
        _oproj_kernel,
        grid=(1,),
        in_specs=[_full(x.shape), _full(o.shape), _full(wo.shape)],
        out_specs=_full(x.shape),
        out_shape=jax.ShapeDtypeStruct(x.shape, F32),
        compiler_params=_params("arbitrary"),
        name="oproj",
    )(x, o, wo)


def kernel(x_prompt, x_sample, state_conv, state_gla, cache_mem_k, cache_mem_v, mem_prompt, norm_mix, w_in, conv_w, gla_gate_up, gla_gate_b, gla_out_norm, w_out, norm_mem, w_q, w_k, w_v, w_o, norm_ffn, router_group, router_group_b, router_expert, router_expert_b, w_gate, w_up, w_down, norm_final):
    depth = w_in.shape[0]
    bsz, t, d = x_prompt.shape
    ns = x_sample.shape[0]
    nm = mem_prompt.shape[1]
    n_tok = bsz * t

    tt = min(1024, t)
    tq = min(1024, t)
    tm_moe = min(1024, n_tok)
    tm_kv = min(512, bsz * nm)
    tm_sorted = 256
    rows_perm = (min(2048, n_tok), min(2048, n_tok))
    nb_gla = min(8, ns)
    nb_att = min(8, ns)

    row = lambda a: a.reshape(1, -1)
    mem2 = mem_prompt.reshape(bsz * nm, d)

    def tile_rows(c):
        c = c.reshape(depth, ns, nm, MEM_HEADS, 2, LANES).transpose(0, 1, 2, 4, 3, 5)
        return c.reshape(depth, ns, nm * ATT_ROWS, LANES)

    ck, cv = tile_rows(cache_mem_k), tile_rows(cache_mem_v)

    def untile_rows(c):
        c = c.reshape(depth, bsz, nm, 2, MEM_HEADS, LANES).transpose(0, 1, 2, 4, 3, 5)
        return c.reshape(depth, bsz, nm, MEM_HEADS, MEM_DH)

    mk_all, mv_all = _mem_kv(mem2, w_k.astype(BF16), w_v.astype(BF16), tm_kv)

    xp = x_prompt
    xs = x_sample.reshape(ns, d)
    conv_p, gla_p, conv_s, gla_s = [], [], [], []
    for l in range(depth):
        win = w_in[l, :, :C_LR].astype(BF16)
        wlr = jnp.pad(w_in[l, :, C_LR:], ((0, 0), (0, LANES - GLA_RANK))).astype(BF16)
        gup = jnp.pad(gla_gate_up[l], ((0, LANES - GLA_RANK), (0, 0))).astype(BF16)
        gb = row(gla_gate_b[l])
        gg = row(gla_out_norm[l])
        wout = w_out[l].astype(BF16)
        wq, wo = w_q[l].astype(BF16), w_o[l].astype(BF16)
        rw = jnp.concatenate([router_group[l], router_expert[l].transpose(1, 0, 2).reshape(d, N_EXPERTS)], axis=1)
        rw = jnp.pad(rw, ((0, 0), (0, LANES - rw.shape[1])))
        rb = jnp.concatenate([router_group_b[l], router_expert_b[l].reshape(-1)])
        rb = row(jnp.pad(rb, (0, LANES - rb.shape[0])))
        moe_w = (row(norm_ffn[l]), rw, rb, w_gate, w_up, w_down, l)

        xp, nbuf, ns_p = _mix_prompt(xp, (bsz, t, d), row(norm_mix[l]), win, wlr, gup, gb, conv_w[l], gg, wout, tt)
        conv_p.append(nbuf)
        gla_p.append(ns_p)
        xp, cls, rank, counts = _att_prompt(xp, row(norm_mem[l]), wq, wo, mk_all, mv_all, l,
                                            row(norm_ffn[l]), rw, rb, tq)
        xp = _moe_sparse(xp, cls, rank, counts, *moe_w, tm_sorted, rows_perm,
                         norm_g=row(norm_final) if l == depth - 1 else None)

        yc, u, q, k, v, gate, la = _mix_sample_in(
            xs, row(norm_mix[l]), win, wlr, gup, gb, conv_w[l], state_conv[l, :, 0], state_conv[l, :, 1])
        conv_s.append(jnp.stack([state_conv[l, :, 1], u], axis=1))
        s_new, o = _gla_step(q, k, v, la, state_gla, l, nb_gla)
        gla_s.append(s_new)
        wq_s = wq.reshape(d, MEM_HEADS, 2, LANES).transpose(0, 2, 1, 3).reshape(d, d)
        wo_s = wo.reshape(MEM_HEADS, 2, LANES, d).transpose(1, 0, 2, 3).reshape(d, d)
        xs, qa = _mix_sample_out(xs, yc, o, gate, gg, wout, row(norm_mem[l]), wq_s)
        oa = _att_sample(qa.reshape(ns, ATT_ROWS, LANES), ck, cv, l, nb_att)
        xs = _oproj(xs, oa.reshape(ns, d), wo_s)
        xs = _moe(xs, *moe_w, min(tm_moe, ns))

    y_prompt = xp.reshape(bsz, t, d)
    y_sample = _final_norm(xs, row(norm_final), ns).reshape(ns, 1, d)
    return (y_prompt, y_sample, jnp.stack(conv_p), jnp.stack(gla_p), untile_rows(mk_all), untile_rows(mv_all),
            jnp.stack(conv_s), jnp.stack(gla_s))
```

```python
import functools

import jax
import jax.numpy as jnp
from jax import lax
from jax.experimental import pallas as pl
from jax.experimental.pallas import tpu as pltpu

F32 = jnp.float32
BF16 = jnp.bfloat16
HIGHEST = lax.Precision.HIGHEST

EPS = 1e-6
CONV_W = 512
GLA_HEADS = 4
GLA_DK = 64
GLA_DV = 128
GLA_K = GLA_HEADS * GLA_DK
GLA_V = GLA_HEADS * GLA_DV
GLA_RANK = 16
GLA_TAU = 16.0
GLA_CHUNK = 64
MEM_HEADS = 4
MEM_DH = 256
N_GROUPS = 4
EXP_PER_GROUP = 4
N_EXPERTS = 16
LANES = 128
C_CB, C_CC, C_CH, C_Q, C_K, C_V, C_G, C_LR = 0, 512, 1024, 1536, 1792, 2048, 2560, 3072
VMEM_LIMIT = 52 * 1024 * 1024


def _params(*sem):
    return pltpu.CompilerParams(dimension_semantics=sem, vmem_limit_bytes=VMEM_LIMIT)


def _rms(x, g):
    return x * lax.rsqrt(jnp.mean(x * x, axis=-1, keepdims=True) + EPS) * g


def _dot(a, b, precision=None):
    return jnp.dot(a, b, precision=precision, preferred_element_type=F32)


def _dot_nt(a, b):
    return lax.dot_general(a, b, (((1,), (1,)), ((), ())), preferred_element_type=F32)


def _dot_tn(a, b, precision=None):
    return lax.dot_general(a, b, (((0,), (0,)), ((), ())), precision=precision,
                           preferred_element_type=F32)


def _dot_split(dot, mask, x):
    hi = x.astype(BF16)
    r1 = x - hi.astype(F32)
    mid = r1.astype(BF16)
    lo = (r1 - mid.astype(F32)).astype(BF16)
    return dot(mask, hi) + dot(mask, mid) + dot(mask, lo)


def _silu(x):
    return x / (1.0 + jnp.exp(-x))


def _log_sigmoid(x):
    return jnp.minimum(x, 0.0) - jnp.log1p(jnp.exp(-jnp.abs(x)))


def _head_norm_gate(o, g, gg):
    parts = []
    for h in range(GLA_HEADS):
        sl = slice(h * GLA_DV, (h + 1) * GLA_DV)
        parts.append(_rms(o[:, sl], gg[:, sl]))
    return jnp.concatenate(parts, axis=-1) * _silu(g)


def _full(shape):
    nd = len(shape)
    return pl.BlockSpec(shape, lambda *_: (0,) * nd)


def _mix_prompt_kernel(x_ref, g_ref, win_ref, wlr_ref, gup_ref, gb_ref, cw_ref, gg_ref, wout_ref,
                       x1_ref, conv_ref, s_ref, ubuf, s_scr, o_scr, *, tok_in):
    t = pl.program_id(1)
    tt = x1_ref.shape[1]

    @pl.when(t == 0)
    def _():
        ubuf[0:8, :] = jnp.zeros((8, CONV_W), F32)
        s_scr[...] = jnp.zeros(s_scr.shape, F32)

    x = _load_tok(x_ref, tt) if tok_in else x_ref[0]
    hb = _rms(x, g_ref[...]).astype(BF16)

    def proj(a, b):
        return _dot(hb, win_ref[0, :, a:b])

    u = proj(C_CC, C_CH) * proj(C_CH, C_Q)
    ubuf[8:8 + tt, :] = u
    cw = cw_ref[...]
    yc = proj(C_CB, C_CC) * (ubuf[6:6 + tt, :] * cw[0:1] + ubuf[7:7 + tt, :] * cw[1:2] + u * cw[2:3])
    ubuf[6:8, :] = u[tt - 2:tt, :]

    qs = proj(C_Q, C_K) * (GLA_DK ** -0.5)
    k = proj(C_K, C_V)
    v = proj(C_V, C_G)
    lr = _dot(hb, wlr_ref[...])
    gate = _dot(lr.astype(BF16), gup_ref[...]) + gb_ref[...]
    la = _log_sigmoid(gate) * (1.0 / GLA_TAU)

    c = GLA_CHUNK
    nc = tt // c
    iota = lambda shape, dim: lax.broadcasted_iota(jnp.int32, shape, dim)
    ltri = (iota((c, c), 0) >= iota((c, c), 1)).astype(BF16)
    b_wide = _dot_split(_dot, ltri, jnp.concatenate([la[j * c:(j + 1) * c] for j in range(nc)], axis=1))
    b_ends = jnp.concatenate([b_wide[c - 1:c, j * GLA_K:(j + 1) * GLA_K] for j in range(nc)], axis=0)
    spread = (iota((nc, LANES), 0) == iota((nc, LANES), 1)).astype(BF16)
    log_dec = _dot_split(lambda m, x: _dot_tn(x, m), spread, b_ends)
    head_feat = iota((GLA_K, GLA_K), 0) // c == iota((GLA_K, GLA_K), 1) // GLA_DK
    head_blk = iota((GLA_K, GLA_V), 0) // GLA_DK == iota((GLA_K, GLA_V), 1) // GLA_DV
    causal = iota((c, GLA_K), 0) >= iota((c, GLA_K), 1) % c
    for j in range(nc):
        r = slice(j * c, (j + 1) * c)
        b_c, q_c, k_c, v_c = b_wide[:, j * GLA_K:(j + 1) * GLA_K], qs[r], k[r], v[r]
        b_mid = b_c[c // 2:c // 2 + 1, :]
        b_last = b_c[c - 1:c, :]
        q_i = (q_c * jnp.exp(b_c - b_mid)).astype(BF16)
        k_i = k_c * jnp.exp(b_mid - b_c)
        k_dec = (k_c * jnp.exp(b_last - b_c)).astype(BF16)
        q_b = (q_c * jnp.exp(b_c)).astype(BF16)
        k_rows = jnp.where(head_feat, jnp.concatenate([k_i] * GLA_HEADS, axis=0), 0.0).astype(BF16)
        a = jnp.where(causal, _dot_nt(q_i, k_rows), 0.0).astype(BF16)
        v_blk = jnp.where(head_blk, jnp.concatenate([v_c] * GLA_HEADS, axis=0), 0.0).astype(BF16)
        s_prev = s_scr[...]
        o_scr[r, :] = _dot(jnp.concatenate([a, q_b], axis=1),
                           jnp.concatenate([v_blk, s_prev.astype(BF16)], axis=0))
        dec = jnp.exp(jnp.broadcast_to(log_dec[:, j:j + 1], (GLA_K, GLA_DV)))
        s_scr[...] = (jnp.concatenate([dec] * GLA_HEADS, axis=1) * s_prev
                      + jnp.where(head_blk, _dot_tn(k_dec, v_c.astype(BF16)), 0.0))

    yg = _head_norm_gate(o_scr[...], proj(C_G, C_LR), gg_ref[...])
    y = _dot(yc.astype(BF16), wout_ref[0:CONV_W, :]) + _dot(yg.astype(BF16), wout_ref[CONV_W:, :])
    x1_ref[0] = x + y

    @pl.when(t == pl.num_programs(1) - 1)
    def _():
        conv_ref[0] = u[tt - 2:tt, :]
        for h in range(GLA_HEADS):
            s_ref[0, h] = s_scr[h * GLA_DK:(h + 1) * GLA_DK, h * GLA_DV:(h + 1) * GLA_DV]


def _mix_prompt(x, shape, g, win, layer, wlr, gup, gb, cw, gg, wout, tt):
    bsz, t, d = shape
    tok_in = x.ndim == 2
    nt = t // tt
    x_spec = (pl.BlockSpec((tt * TOK_ROWS, LANES), lambda b, i: (b * nt + i, 0)) if tok_in
              else pl.BlockSpec((1, tt, d), lambda b, i: (b, i, 0)))
    return pl.pallas_call(
        functools.partial(_mix_prompt_kernel, tok_in=tok_in),
        grid=(bsz, nt),
        in_specs=[
            x_spec,
            _full(g.shape), pl.BlockSpec((1,) + win.shape[1:], lambda b, i: (layer, 0, 0)), _full(wlr.shape),
            _full(gup.shape), _full(gb.shape), _full(cw.shape), _full(gg.shape), _full(wout.shape),
        ],
        out_specs=[
            pl.BlockSpec((1, tt, d), lambda b, i: (b, i, 0)),
            pl.BlockSpec((1, 2, CONV_W), lambda b, i: (b, 0, 0)),
            pl.BlockSpec((1, GLA_HEADS, GLA_DK, GLA_DV), lambda b, i: (b, 0, 0, 0)),
        ],
        out_shape=[
            jax.ShapeDtypeStruct((bsz, t, d), F32),
            jax.ShapeDtypeStruct((bsz, 2, CONV_W), F32),
            jax.ShapeDtypeStruct((bsz, GLA_HEADS, GLA_DK, GLA_DV), F32),
        ],
        scratch_shapes=[
            pltpu.VMEM((8 + tt, CONV_W), F32),
            pltpu.VMEM((GLA_K, GLA_V), F32),
            pltpu.VMEM((tt, GLA_V), F32),
        ],
        compiler_params=_params("arbitrary", "arbitrary"),
        name="mix_prompt",
    )(x, g, win, wlr, gup, gb, cw, gg, wout)


def _kv_kernel(m_ref, wk_ref, wv_ref, k_ref, v_ref):
    tm = m_ref.shape[0]
    mb = m_ref[...].astype(BF16)
    for w_ref, o_ref in ((wk_ref, k_ref), (wv_ref, v_ref)):
        y = _dot(mb, w_ref[0])
        for h in range(MEM_HEADS):
            for dt in range(MEM_DH // LANES):
                col = h * MEM_DH + dt * LANES
                o_ref[0, pl.ds(dt * MEM_HEADS + h, tm, stride=ATT_ROWS), :] = y[:, col:col + LANES]


def _mem_kv(mem, wk, wv, tm):
    n, d = mem.shape
    depth = wk.shape[0]
    w_spec = pl.BlockSpec((1, d, d), lambda l, i: (l, 0, 0))
    o_spec = pl.BlockSpec((1, tm * ATT_ROWS, LANES), lambda l, i: (l, i, 0))
    return pl.pallas_call(
        _kv_kernel,
        grid=(depth, n // tm),
        in_specs=[pl.BlockSpec((tm, d), lambda l, i: (i, 0)), w_spec, w_spec],
        out_specs=[o_spec] * 2,
        out_shape=[jax.ShapeDtypeStruct((depth, n * ATT_ROWS, LANES), F32)] * 2,
        compiler_params=_params("arbitrary", "arbitrary"),
        name="mem_kv",
    )(mem, wk, wv)


def _att_prompt_kernel(x_ref, g_ref, wq_ref, wo_ref, mk_ref, mv_ref, gf_ref, rw_ref, rb_ref, ltri_ref,
                       out_ref, cls_ref, rank_ref, cnt_ref, carry):
    @pl.when((pl.program_id(0) == 0) & (pl.program_id(1) == 0))
    def _():
        carry[...] = jnp.zeros(carry.shape, F32)

    x = x_ref[0]
    xb = _rms(x, g_ref[...]).astype(BF16)
    q = _dot(xb, wq_ref[...])
    nm = mk_ref.shape[1] // ATT_ROWS

    def head_rows(ref, h):
        return jnp.concatenate([ref[0, pl.ds(dt * MEM_HEADS + h, nm, stride=ATT_ROWS), :]
                                for dt in range(MEM_DH // LANES)], axis=-1).astype(BF16)

    outs = []
    for h in range(MEM_HEADS):
        sl = slice(h * MEM_DH, (h + 1) * MEM_DH)
        s = _dot_nt(q[:, sl].astype(BF16), head_rows(mk_ref, h)) * (MEM_DH ** -0.5)
        e = jnp.exp(s - jnp.max(s, axis=-1, keepdims=True))
        p = e / jnp.sum(e, axis=-1, keepdims=True)
        outs.append(_dot(p.astype(BF16), head_rows(mv_ref, h)))
    o = jnp.concatenate(outs, axis=-1)
    x2 = x + _dot(o.astype(BF16), wo_ref[...])
    _store_tok(out_ref, x2)
    _route_meta(x2, gf_ref, rw_ref, rb_ref, ltri_ref, carry, cls_ref, rank_ref, cnt_ref)


def _att_prompt(x, g, wq, wo, mk, mv, layer, gf, rw, rb, tq):
    bsz, t, d = x.shape
    assert d == TOK_ROWS * LANES
    nq = t // tq
    rows = mk.shape[1] // bsz
    kv_spec = pl.BlockSpec((1, rows, LANES), lambda b, i: (layer, b, 0))
    meta_spec = pl.BlockSpec((tq // LANES, LANES), lambda b, i: (b * nq + i, 0))
    rw_hi = rw.astype(BF16)
    rw_split = jnp.concatenate([rw_hi, (rw - rw_hi.astype(F32)).astype(BF16)], axis=1)
    ltri = jnp.tril(jnp.ones((LANES, LANES), BF16))
    n_tok = bsz * t
    return pl.pallas_call(
        _att_prompt_kernel,
        grid=(bsz, nq),
        in_specs=[
            pl.BlockSpec((1, tq, d), lambda b, i: (b, i, 0)),
            _full(g.shape), _full(wq.shape), _full(wo.shape),
            kv_spec, kv_spec,
            _full(gf.shape), _full(rw_split.shape), _full(rb.shape), _full(ltri.shape),
        ],
        out_specs=[
            pl.BlockSpec((tq * TOK_ROWS, LANES), lambda b, i: (b * nq + i, 0)),
            meta_spec, meta_spec, _full((1, LANES)),
        ],
        out_shape=[
            jax.ShapeDtypeStruct((n_tok * TOK_ROWS, LANES), F32),
            jax.ShapeDtypeStruct((n_tok // LANES, LANES), jnp.int32),
            jax.ShapeDtypeStruct((n_tok // LANES, LANES), jnp.int32),
            jax.ShapeDtypeStruct((1, LANES), jnp.int32),
        ],
        scratch_shapes=[pltpu.VMEM((1, LANES), F32)],
        compiler_params=_params("arbitrary", "arbitrary"),
        name="att_prompt",
    )(x, g, wq, wo, mk, mv, gf, rw_split, rb, ltri)


def _route_top2(logits):
    lane = lax.broadcasted_iota(jnp.int32, logits.shape, 1)
    lanef = lane.astype(F32)
    ninf = -jnp.inf
    big = 1e9
    gl = jnp.where(lane < N_GROUPS, logits, ninf)
    gmax = jnp.max(gl, axis=-1, keepdims=True)
    g_idx = jnp.min(jnp.where(gl == gmax, lanef, big), axis=-1, keepdims=True)
    g_w = 1.0 / jnp.sum(jnp.exp(gl - gmax), axis=-1, keepdims=True)
    grp = ((lane - N_GROUPS) >> 2).astype(F32)
    emask = (lane >= N_GROUPS) & (lane < N_GROUPS + N_EXPERTS) & (grp == g_idx)
    el = jnp.where(emask, logits, ninf)
    m1 = jnp.max(el, axis=-1, keepdims=True)
    i1 = jnp.min(jnp.where(el == m1, lanef, big), axis=-1, keepdims=True)
    el2 = jnp.where(lanef == i1, ninf, el)
    m2 = jnp.max(el2, axis=-1, keepdims=True)
    i2 = jnp.min(jnp.where(el2 == m2, lanef, big), axis=-1, keepdims=True)
    tail = jnp.exp(m2 - m1)
    w1 = g_w / (1.0 + tail)
    w2 = g_w * tail / (1.0 + tail)
    return g_idx, i1, i2, w1, w2


def _route(logits):
    _, i1, i2, w1, w2 = _route_top2(logits)
    lanef = lax.broadcasted_iota(jnp.int32, logits.shape, 1).astype(F32)
    return jnp.where(lanef == i1, w1, 0.0) + jnp.where(lanef == i2, w2, 0.0)


N_PAIRS = 6
N_CLASSES = N_GROUPS * N_PAIRS
TOK_ROWS = 8


def _load_tok(ref, n):
    return jnp.concatenate([ref[pl.ds(j, n, stride=TOK_ROWS), :] for j in range(TOK_ROWS)], axis=-1)


def _store_tok(ref, val):
    n = val.shape[0]
    for j in range(TOK_ROWS):
        ref[pl.ds(j, n, stride=TOK_ROWS), :] = val[:, j * LANES:(j + 1) * LANES]


def _lane_dense(col):
    eye = lax.broadcasted_iota(jnp.int32, (LANES, LANES), 0) == lax.broadcasted_iota(jnp.int32, (LANES, LANES), 1)
    rows = [jnp.sum(jnp.where(eye, col[b * LANES:(b + 1) * LANES], 0.0), axis=0, keepdims=True)
            for b in range(col.shape[0] // LANES)]
    return jnp.concatenate(rows, axis=0)


def _route_meta(x, g_ref, rw_ref, rb_ref, ltri_ref, carry, cls_ref, rank_ref, cnt_ref):
    n = x.shape[0]
    xn = _rms(x, g_ref[...])
    x_hi = xn.astype(BF16)
    x_lo = (xn - x_hi.astype(F32)).astype(BF16)
    both = _dot(x_hi, rw_ref[...])
    logits = (both[:, :LANES] + (_dot(x_lo, rw_ref[:, :LANES]) + both[:, LANES:])) + rb_ref[...]
    g_idx, i1, i2, _, _ = _route_top2(logits)
    lo = jnp.minimum(i1, i2) - N_GROUPS - EXP_PER_GROUP * g_idx
    hi = jnp.maximum(i1, i2) - N_GROUPS - EXP_PER_GROUP * g_idx
    cls = g_idx * N_PAIRS + lo * (2 * EXP_PER_GROUP - 1.0 - lo) * 0.5 + hi - lo - 1.0

    lane = lax.broadcasted_iota(jnp.int32, (n, LANES), 1)
    onehot = lane.astype(F32) == cls
    onehot_b = onehot.astype(BF16)
    blocks = []
    run = carry[...]
    for b in range(n // LANES):
        p_b = _dot(ltri_ref[...], onehot_b[b * LANES:(b + 1) * LANES]) + run
        blocks.append(p_b)
        run = p_b[LANES - 1:LANES, :]
    prefix = jnp.concatenate(blocks, axis=0)
    rank = jnp.sum(jnp.where(onehot, prefix, 0.0), axis=-1, keepdims=True) - 1.0
    carry[...] = prefix[n - 1:n, :]
    cls_ref[...] = _lane_dense(cls).astype(jnp.int32)
    rank_ref[...] = _lane_dense(rank).astype(jnp.int32)
    cnt_ref[...] = prefix[n - 1:n, :].astype(jnp.int32)


DMA_UNROLL = 8


def _tok_rows(t):
    return pl.ds(pl.multiple_of(t * TOK_ROWS, TOK_ROWS), TOK_ROWS)


def _tok_scatter_kernel(cls, rank, starts, tail, x_ref, dst_ref, stage, zeros, sems, *, toks, steps):
    i = pl.program_id(0)
    slot = i % 2
    base = i * toks
    rows = toks * TOK_ROWS
    tile_rows = zeros.shape[0]

    def wait_slot(s):
        pltpu.make_async_copy(stage.at[s], dst_ref.at[pl.ds(0, rows)], sems.at[s]).wait()

    @pl.when(i == 0)
    def _():
        zeros[...] = jnp.zeros(zeros.shape, F32)

        def fill(c):
            return pltpu.make_async_copy(
                zeros, dst_ref.at[pl.ds(pl.multiple_of(tail[c] * tile_rows, tile_rows), tile_rows)], sems.at[2])

        for c in range(2 * N_CLASSES):
            @pl.when(tail[c] >= 0)
            def _():
                fill(c).start()
        for c in range(2 * N_CLASSES):
            @pl.when(tail[c] >= 0)
            def _():
                fill(c).wait()

    @pl.when(i >= 2)
    def _():
        wait_slot(slot)

    stage[slot] = x_ref[...]

    def start(r8, c):
        for u in range(DMA_UNROLL):
            r = r8 * DMA_UNROLL + u
            dst = starts[cls[base + r]] + rank[base + r]
            pltpu.make_async_copy(stage.at[slot, _tok_rows(r)], dst_ref.at[_tok_rows(dst)],
                                  sems.at[slot]).start(priority=u % 2)
        return c

    lax.fori_loop(0, toks // DMA_UNROLL, start, 0)

    @pl.when(i == steps - 1)
    def _():
        wait_slot(slot)
        if steps >= 2:
            wait_slot(1 - slot)


def _tok_scatter(x, n_out, cls, rank, starts, tail, tile, toks):
    steps = x.shape[0] // (toks * TOK_ROWS)
    return pl.pallas_call(
        functools.partial(_tok_scatter_kernel, toks=toks, steps=steps),
        grid_spec=pltpu.PrefetchScalarGridSpec(
            num_scalar_prefetch=4,
            grid=(steps,),
            in_specs=[pl.BlockSpec((toks * TOK_ROWS, LANES), lambda i, *_: (i, 0))],
            out_specs=pl.BlockSpec(memory_space=pl.ANY),
            scratch_shapes=[pltpu.VMEM((2, toks * TOK_ROWS, LANES), F32), pltpu.VMEM((tile * TOK_ROWS, LANES), F32),
                            pltpu.SemaphoreType.DMA((3,))],
        ),
        out_shape=jax.ShapeDtypeStruct((n_out * TOK_ROWS, LANES), F32),
        compiler_params=_params("arbitrary"),
        name="tok_scatter",
    )(cls, rank, starts, tail, x)


def _tok_gather_kernel(cls, rank, starts, src_ref, *rest, toks, norm):
    if norm:
        g_ref, out_ref, stage, sem = rest
        dst = stage
    else:
        out_ref, sem = rest
        dst = out_ref
    base = pl.program_id(0) * toks

    def start(r8, c):
        for u in range(DMA_UNROLL):
            r = r8 * DMA_UNROLL + u
            src = starts[cls[base + r]] + rank[base + r]
            pltpu.make_async_copy(src_ref.at[_tok_rows(src)], dst.at[_tok_rows(r)], sem).start(priority=u % 2)
        return c

    lax.fori_loop(0, toks // DMA_UNROLL, start, 0)
    pltpu.make_async_copy(src_ref.at[pl.ds(0, toks * TOK_ROWS)], dst, sem).wait()
    if norm:
        out_ref[...] = _rms(_load_tok(stage, toks), g_ref[...])


def _tok_gather(src, cls, rank, starts, toks, norm_g=None):
    n = cls.shape[0]
    norm = norm_g is not None
    d = TOK_ROWS * LANES
    return pl.pallas_call(
        functools.partial(_tok_gather_kernel, toks=toks, norm=norm),
        grid_spec=pltpu.PrefetchScalarGridSpec(
            num_scalar_prefetch=3,
            grid=(n // toks,),
            in_specs=[pl.BlockSpec(memory_space=pl.ANY)]
            + ([pl.BlockSpec(norm_g.shape, lambda i, *_: (0, 0))] if norm else []),
            out_specs=(pl.BlockSpec((toks, d), lambda i, *_: (i, 0)) if norm
                       else pl.BlockSpec((toks * TOK_ROWS, LANES), lambda i, *_: (i, 0))),
            scratch_shapes=([pltpu.VMEM((toks * TOK_ROWS, LANES), F32)] if norm else [])
            + [pltpu.SemaphoreType.DMA(())],
        ),
        out_shape=jax.ShapeDtypeStruct((n, d) if norm else (n * TOK_ROWS, LANES), F32),
        compiler_params=_params("arbitrary"),
        name="tok_gather",
    )(cls, rank, starts, src, *([norm_g] if norm else []))


def _moe_sorted_kernel(e_lo, e_hi, valid, xs_ref, g_ref, rw_ref, rb_ref,
                       wg_lo, wu_lo, wd_lo, wg_hi, wu_hi, wd_hi, out_ref, wgu_scr, wd_scr):
    t = pl.program_id(0)
    tm = xs_ref.shape[0] // TOK_ROWS
    ff = wd_lo.shape[2]
    prev = jnp.maximum(t - 1, 0)

    for slot, ids, wg, wu, wd in ((0, e_lo, wg_lo, wu_lo, wd_lo), (1, e_hi, wg_hi, wu_hi, wd_hi)):
        @pl.when((t == 0) | (ids[t] != ids[prev]))
        def _():
            wgu_scr[slot, :, 0:ff] = wg[0, 0].astype(BF16)
            wgu_scr[slot, :, ff:] = wu[0, 0].astype(BF16)
            wd_scr[slot] = wd[0, 0].astype(BF16)

    @pl.when(valid[t] > 0)
    def _():
        x = _load_tok(xs_ref, tm)
        xb = _rms(x, g_ref[...]).astype(BF16)
        logits = _dot(xb, rw_ref[...]) + rb_ref[...]
        lane = lax.broadcasted_iota(jnp.int32, logits.shape, 1)
        gl = jnp.where(lane < N_GROUPS, logits, -jnp.inf)
        g_w = 1.0 / jnp.sum(jnp.exp(gl - jnp.max(gl, axis=-1, keepdims=True)), axis=-1, keepdims=True)
        l_lo = jnp.sum(jnp.where(lane == e_lo[t] + N_GROUPS, logits, 0.0), axis=-1, keepdims=True)
        l_hi = jnp.sum(jnp.where(lane == e_hi[t] + N_GROUPS, logits, 0.0), axis=-1, keepdims=True)
        tail = jnp.exp(-jnp.abs(l_lo - l_hi))
        w_top = g_w / (1.0 + tail)
        w_oth = g_w * tail / (1.0 + tail)
        lo_top = l_lo >= l_hi
        y = x
        for slot, w in ((0, jnp.where(lo_top, w_top, w_oth)), (1, jnp.where(lo_top, w_oth, w_top))):
            gu = _dot(xb, wgu_scr[slot])
            hid = _silu(gu[:, 0:ff]) * gu[:, ff:] * w
            y = y + _dot(hid.astype(BF16), wd_scr[slot])
        _store_tok(out_ref, y)

    @pl.when(valid[t] == 0)
    def _():
        out_ref[...] = jnp.zeros(out_ref.shape, F32)


def _moe_sorted(xs, g, rw, rb, wg, wu, wd, layer, e_lo, e_hi, valid, tm):
    npad = xs.shape[0] // TOK_ROWS
    _, _, d, ff = wg.shape
    const = lambda t, lo, hi, v: (0, 0)
    at_lo = lambda t, lo, hi, v: (layer, lo[t], 0, 0)
    at_hi = lambda t, lo, hi, v: (layer, hi[t], 0, 0)
    return pl.pallas_call(
        _moe_sorted_kernel,
        grid_spec=pltpu.PrefetchScalarGridSpec(
            num_scalar_prefetch=3,
            grid=(npad // tm,),
            in_specs=[
                pl.BlockSpec((tm * TOK_ROWS, LANES), lambda t, lo, hi, v: (jnp.where(v[t] > 0, t, 0), 0)),
                pl.BlockSpec(g.shape, const), pl.BlockSpec(rw.shape, const), pl.BlockSpec(rb.shape, const),
                pl.BlockSpec((1, 1, d, ff), at_lo), pl.BlockSpec((1, 1, d, ff), at_lo),
                pl.BlockSpec((1, 1, ff, d), at_lo),
                pl.BlockSpec((1, 1, d, ff), at_hi), pl.BlockSpec((1, 1, d, ff), at_hi),
                pl.BlockSpec((1, 1, ff, d), at_hi),
            ],
            out_specs=pl.BlockSpec((tm * TOK_ROWS, LANES), lambda t, lo, hi, v: (t, 0)),
            scratch_shapes=[pltpu.VMEM((2, d, 2 * ff), BF16), pltpu.VMEM((2, ff, d), BF16)],
        ),
        out_shape=jax.ShapeDtypeStruct(xs.shape, F32),
        compiler_params=_params("arbitrary"),
        name="moe_sorted",
    )(e_lo, e_hi, valid, xs, g, rw, rb, wg, wu, wd, wg, wu, wd)


def _moe_sparse(x, cls, rank, counts, g, rw, rb, wg, wu, wd, layer, tm, toks, norm_g=None):
    n = x.shape[0] // TOK_ROWS
    cls, rank = cls.reshape(n), rank.reshape(n)
    cnt = counts[0, :N_CLASSES]
    padded = ((cnt + tm - 1) // tm) * tm
    ends = jnp.cumsum(padded)
    starts = (ends - padded).astype(jnp.int32)
    n_tiles = n // tm + N_CLASSES
    tile_start = jnp.arange(n_tiles, dtype=jnp.int32) * tm
    n_valid = ends[-1] // tm
    unused = n_valid + jnp.arange(N_CLASSES, dtype=jnp.int32)
    tail = jnp.concatenate([jnp.where(cnt > 0, ends // tm - 1, -1),
                            jnp.where(unused < n_tiles, unused, -1)]).astype(jnp.int32)
    last_cls = jnp.sum((tile_start[jnp.maximum(n_valid - 1, 0)] >= ends).astype(jnp.int32))
    tile_cls = jnp.sum((tile_start[:, None] >= ends[None, :]).astype(jnp.int32), axis=1)
    valid = (tile_start < ends[-1]).astype(jnp.int32)
    tile_cls = jnp.where(valid > 0, tile_cls, last_cls)
    pair_lo = jnp.array([0, 0, 0, 1, 1, 2], jnp.int32)
    pair_hi = jnp.array([1, 2, 3, 2, 3, 3], jnp.int32)
    e_lo = (tile_cls // N_PAIRS) * EXP_PER_GROUP + pair_lo[tile_cls % N_PAIRS]
    e_hi = (tile_cls // N_PAIRS) * EXP_PER_GROUP + pair_hi[tile_cls % N_PAIRS]

    xs = _tok_scatter(x, n_tiles * tm, cls, rank, starts, tail, tm, toks[0])
    ys = _moe_sorted(xs, g, rw.astype(BF16), rb, wg, wu, wd, layer, e_lo, e_hi, valid, tm)
    return _tok_gather(ys, cls, rank, starts, toks[1], norm_g)


def _moe_kernel(x_ref, g_ref, rw_ref, rb_ref, wg_ref, wu_ref, wd_ref, out_ref, xn_scr, comb_scr, acc_scr):
    e = pl.program_id(1)

    @pl.when(e == 0)
    def _():
        x = x_ref[...]
        xn = _rms(x, g_ref[...])
        xn_scr[...] = xn.astype(BF16)
        comb_scr[...] = _route(_dot(xn, rw_ref[...], HIGHEST) + rb_ref[...])
        acc_scr[...] = x

    xb = xn_scr[...]
    lane = lax.broadcasted_iota(jnp.int32, comb_scr.shape, 1)
    cmb = jnp.sum(jnp.where(lane == e + N_GROUPS, comb_scr[...], 0.0), axis=-1, keepdims=True)
    hid = _silu(_dot(xb, wg_ref[0, 0].astype(BF16))) * _dot(xb, wu_ref[0, 0].astype(BF16)) * cmb
    acc_scr[...] += _dot(hid.astype(BF16), wd_ref[0, 0].astype(BF16))

    @pl.when(e == pl.num_programs(1) - 1)
    def _():
        out_ref[...] = acc_scr[...]


def _moe(x, g, rw, rb, wg, wu, wd, layer, tm):
    n, d = x.shape
    _, ne, _, ff = wg.shape
    return pl.pallas_call(
        _moe_kernel,
        grid=(n // tm, ne),
        in_specs=[
            pl.BlockSpec((tm, d), lambda i, e: (i, 0)),
            _full(g.shape), _full(rw.shape), _full(rb.shape),
            pl.BlockSpec((1, 1, d, ff), lambda i, e: (layer, e, 0, 0)),
            pl.BlockSpec((1, 1, d, ff), lambda i, e: (layer, e, 0, 0)),
            pl.BlockSpec((1, 1, ff, d), lambda i, e: (layer, e, 0, 0)),
        ],
        out_specs=pl.BlockSpec((tm, d), lambda i, e: (i, 0)),
        out_shape=jax.ShapeDtypeStruct((n, d), F32),
        scratch_shapes=[pltpu.VMEM((tm, d), BF16), pltpu.VMEM((tm, LANES), F32), pltpu.VMEM((tm, d), F32)],
        compiler_params=_params("arbitrary", "arbitrary"),
        name="moe",
    )(x, g, rw, rb, wg, wu, wd)


def _norm_kernel(x_ref, g_ref, o_ref):
    o_ref[...] = _rms(x_ref[...], g_ref[...])


def _final_norm(x, g, tm):
    n, d = x.shape
    return pl.pallas_call(
        _norm_kernel,
        grid=(n // tm,),
        in_specs=[pl.BlockSpec((tm, d), lambda i: (i, 0)), _full(g.shape)],
        out_specs=pl.BlockSpec((tm, d), lambda i: (i, 0)),
        out_shape=jax.ShapeDtypeStruct((n, d), F32),
        compiler_params=_params("arbitrary"),
        name="final_norm",
    )(x, g)


def _mix_sample_in_kernel(x_ref, g_ref, win_ref, wlr_ref, gup_ref, gb_ref, cw_ref, b0_ref, b1_ref,
                          yc_ref, u_ref, q_ref, k_ref, v_ref, gate_ref, la_ref):
    hb = _rms(x_ref[...], g_ref[...]).astype(BF16)

    def proj(a, b):
        return _dot(hb, win_ref[0, :, a:b])

    u = proj(C_CC, C_CH) * proj(C_CH, C_Q)
    cw = cw_ref[...]
    yc_ref[...] = proj(C_CB, C_CC) * (b0_ref[...] * cw[0:1] + b1_ref[...] * cw[1:2] + u * cw[2:3])
    u_ref[...] = u
    q_ref[...] = proj(C_Q, C_K) * (GLA_DK ** -0.5)
    k_ref[...] = proj(C_K, C_V)
    v_ref[...] = proj(C_V, C_G)
    gate_ref[...] = proj(C_G, C_LR)
    lr = _dot(hb, wlr_ref[...])
    gate = _dot(lr.astype(BF16), gup_ref[...]) + gb_ref[...]
    la_ref[...] = _log_sigmoid(gate) * (1.0 / GLA_TAU)


def _mix_sample_in(x, g, win, layer, wlr, gup, gb, cw, b0, b1):
    n = x.shape[0]
    args = (x, g, win, wlr, gup, gb, cw, b0, b1)
    specs = [_full(a.shape) for a in args]
    specs[2] = pl.BlockSpec((1,) + win.shape[1:], lambda i: (layer, 0, 0))
    widths = (CONV_W, CONV_W, GLA_K, GLA_K, GLA_V, GLA_V, GLA_K)
    return pl.pallas_call(
        _mix_sample_in_kernel,
        grid=(1,),
        in_specs=specs,
        out_specs=[_full((n, w)) for w in widths],
        out_shape=[jax.ShapeDtypeStruct((n, w), F32) for w in widths],
        compiler_params=_params("arbitrary"),
        name="mix_sample_in",
    )(*args)


def _gla_step_kernel(q_ref, k_ref, v_ref, la_ref, s0_ref, s_ref, o_ref):
    nb = q_ref.shape[0]
    a = jnp.exp(la_ref[...])
    kb = k_ref[...].astype(BF16)
    qb = q_ref[...].astype(BF16)
    vf = v_ref[...].astype(BF16).astype(F32)
    rows = lax.broadcasted_iota(jnp.int32, (nb, 1), 0)
    spread = (lax.broadcasted_iota(jnp.int32, (nb, nb * GLA_DV), 0)
              == lax.broadcasted_iota(jnp.int32, (nb, nb * GLA_DV), 1) // GLA_DV)
    spread_b = spread.astype(BF16)
    for h in range(GLA_HEADS):
        ks = slice(h * GLA_DK, (h + 1) * GLA_DK)
        a_cols = _dot_tn(a[:, ks], spread.astype(F32), HIGHEST)
        k_cols = _dot_tn(kb[:, ks], spread_b)
        q_cols = _dot_tn(qb[:, ks], spread_b)
        o_h = jnp.zeros((nb, GLA_DV), F32)
        for n in range(nb):
            blk = slice(n * GLA_DV, (n + 1) * GLA_DV)
            v_row = vf[n:n + 1, h * GLA_DV:(h + 1) * GLA_DV]
            s_new = a_cols[:, blk] * s0_ref[0, n, h] + k_cols[:, blk] * v_row
            s_ref[n, h] = s_new
            o_h = jnp.where(rows == n, jnp.sum(q_cols[:, blk] * s_new, axis=0, keepdims=True), o_h)
        o_ref[:, h * GLA_DV:(h + 1) * GLA_DV] = o_h


def _gla_step(q, k, v, la, state, layer, nb):
    n = q.shape[0]
    sshape = (GLA_HEADS, GLA_DK, GLA_DV)
    return pl.pallas_call(
        _gla_step_kernel,
        grid=(n // nb,),
        in_specs=[
            pl.BlockSpec((nb, GLA_K), lambda i: (i, 0)),
            pl.BlockSpec((nb, GLA_K), lambda i: (i, 0)),
            pl.BlockSpec((nb, GLA_V), lambda i: (i, 0)),
            pl.BlockSpec((nb, GLA_K), lambda i: (i, 0)),
            pl.BlockSpec((1, nb) + sshape, lambda i: (layer, i, 0, 0, 0)),
        ],
        out_specs=[
            pl.BlockSpec((nb,) + sshape, lambda i: (i, 0, 0, 0)),
            pl.BlockSpec((nb, GLA_V), lambda i: (i, 0)),
        ],
        out_shape=[jax.ShapeDtypeStruct((n,) + sshape, F32), jax.ShapeDtypeStruct((n, GLA_V), F32)],
        compiler_params=_params("arbitrary"),
        name="gla_step",
    )(q, k, v, la, state)


def _mix_sample_out_kernel(x_ref, yc_ref, o_ref, gate_ref, gg_ref, wout_ref, gm_ref, wq_ref, x1_ref, q_ref):
    yg = _head_norm_gate(o_ref[...], gate_ref[...], gg_ref[...])
    y = _dot(yc_ref[...].astype(BF16), wout_ref[0:CONV_W, :]) + _dot(yg.astype(BF16), wout_ref[CONV_W:, :])
    x1 = x_ref[...] + y
    x1_ref[...] = x1
    q_ref[...] = _dot(_rms(x1, gm_ref[...]).astype(BF16), wq_ref[...])


def _mix_sample_out(x, yc, o, gate, gg, wout, gm, wq):
    args = (x, yc, o, gate, gg, wout, gm, wq)
    return pl.pallas_call(
        _mix_sample_out_kernel,
        grid=(1,),
        in_specs=[_full(a.shape) for a in args],
        out_specs=[_full(x.shape)] * 2,
        out_shape=[jax.ShapeDtypeStruct(x.shape, F32)] * 2,
        compiler_params=_params("arbitrary"),
        name="mix_sample_out",
    )(*args)


ATT_ROWS = 2 * MEM_HEADS


def _class_allreduce(x, op):
    n = x.shape[-1]
    shift = ATT_ROWS
    while shift < n:
        x = op(x, pltpu.roll(x, shift, axis=1))
        shift *= 2
    return x


def _att_sample_kernel(q_ref, k_ref, v_ref, o_ref):
    nb = q_ref.shape[0]
    ncol = k_ref.shape[2]
    diag = (lax.broadcasted_iota(jnp.int32, (ATT_ROWS, ncol), 0)
            == (lax.broadcasted_iota(jnp.int32, (ATT_ROWS, ncol), 1) & (ATT_ROWS - 1)))
    rows = lax.broadcasted_iota(jnp.int32, (nb, 1), 0)
    t = jnp.zeros((nb, ncol), F32)
    for n in range(nb):
        sc = _dot_nt(q_ref[n].astype(BF16), k_ref[0, n].astype(BF16))
        t = t + jnp.where(rows == n, jnp.sum(jnp.where(diag, sc, 0.0), axis=0, keepdims=True), 0.0)
    valid = (lax.broadcasted_iota(jnp.int32, (nb, ncol), 1) & (ATT_ROWS - 1)) < MEM_HEADS
    s = jnp.where(valid, (t + pltpu.roll(t, ncol - MEM_HEADS, axis=1)) * (MEM_DH ** -0.5), 0.0)
    e = jnp.where(valid, jnp.exp(s - _class_allreduce(s, jnp.maximum)), 0.0)
    den = jnp.where(valid, _class_allreduce(e, jnp.add), 1.0)
    p = e / den
    p = p + pltpu.roll(p, MEM_HEADS, axis=1)
    for n in range(nb):
        p_n = jnp.where(diag, jnp.broadcast_to(p[n:n + 1, :], (ATT_ROWS, ncol)), 0.0)
        o_ref[n] = _dot(p_n.astype(BF16), v_ref[0, n].astype(BF16))


def _att_sample(q, ck, cv, layer, nb):
    n = q.shape[0]
    ncol = ck.shape[2]
    return pl.pallas_call(
        _att_sample_kernel,
        grid=(n // nb,),
        in_specs=[
            pl.BlockSpec((nb, ATT_ROWS, LANES), lambda i: (i, 0, 0)),
            pl.BlockSpec((1, nb, ncol, LANES), lambda i: (layer, i, 0, 0)),
            pl.BlockSpec((1, nb, ncol, LANES), lambda i: (layer, i, 0, 0)),
        ],
        out_specs=pl.BlockSpec((nb, ATT_ROWS, LANES), lambda i: (i, 0, 0)),
        out_shape=jax.ShapeDtypeStruct((n, ATT_ROWS, LANES), F32),
        compiler_params=_params("arbitrary"),
        name="att_sample",
    )(q, ck, cv)


def _oproj_kernel(x_ref, o_ref, wo_ref, out_ref):
    out_ref[...] = x_ref[...] + _dot(o_ref[...].astype(BF16), wo_ref[...])


def _oproj(x, o, wo):
    return pl.pallas_call(
        _oproj_kernel,
        grid=(1,),
        in_specs=[_full(x.shape), _full(o.shape), _full(wo.shape)],
        out_specs=_full(x.shape),
        out_shape=jax.ShapeDtypeStruct(x.shape, F32),
        compiler_params=_params("arbitrary"),
        name="oproj",
    )(x, o, wo)


def kernel(x_prompt, x_sample, state_conv, state_gla, cache_mem_k, cache_mem_v, mem_prompt, norm_mix, w_in, conv_w, gla_gate_up, gla_gate_b, gla_out_norm, w_out, norm_mem, w_q, w_k, w_v, w_o, norm_ffn, router_group, router_group_b, router_expert, router_expert_b, w_gate, w_up, w_down, norm_final):
    depth = w_in.shape[0]
    bsz, t, d = x_prompt.shape
    ns = x_sample.shape[0]
    nm = mem_prompt.shape[1]
    n_tok = bsz * t

    tt = min(1024, t)
    tq = min(1024, t)
    tm_moe = min(1024, n_tok)
    tm_kv = min(512, bsz * nm)
    tm_sorted = 256
    rows_perm = (min(2048, n_tok), min(2048, n_tok))
    nb_gla = min(8, ns)
    nb_att = min(8, ns)

    row = lambda a: a.reshape(1, -1)
    mem2 = mem_prompt.reshape(bsz * nm, d)

    def tile_rows(c):
        c = c.reshape(depth, ns, nm, MEM_HEADS, 2, LANES).transpose(0, 1, 2, 4, 3, 5)
        return c.reshape(depth, ns, nm * ATT_ROWS, LANES)

    ck, cv = tile_rows(cache_mem_k), tile_rows(cache_mem_v)

    def untile_rows(c):
        c = c.reshape(depth, bsz, nm, 2, MEM_HEADS, LANES).transpose(0, 1, 2, 4, 3, 5)
        return c.reshape(depth, bsz, nm, MEM_HEADS, MEM_DH)

    win_all = w_in.astype(BF16)
    mk_all, mv_all = _mem_kv(mem2, w_k.astype(BF16), w_v.astype(BF16), tm_kv)

    xp = x_prompt
    xs = x_sample.reshape(ns, d)
    conv_p, gla_p, conv_s, gla_s = [], [], [], []
    for l in range(depth):
        wlr = jnp.pad(w_in[l, :, C_LR:], ((0, 0), (0, LANES - GLA_RANK))).astype(BF16)
        gup = jnp.pad(gla_gate_up[l], ((0, LANES - GLA_RANK), (0, 0))).astype(BF16)
        gb = row(gla_gate_b[l])
        gg = row(gla_out_norm[l])
        wout = w_out[l].astype(BF16)
        wq, wo = w_q[l].astype(BF16), w_o[l].astype(BF16)
        rw = jnp.concatenate([router_group[l], router_expert[l].transpose(1, 0, 2).reshape(d, N_EXPERTS)], axis=1)
        rw = jnp.pad(rw, ((0, 0), (0, LANES - rw.shape[1])))
        rb = jnp.concatenate([router_group_b[l], router_expert_b[l].reshape(-1)])
        rb = row(jnp.pad(rb, (0, LANES - rb.shape[0])))
        moe_w = (row(norm_ffn[l]), rw, rb, w_gate, w_up, w_down, l)

        xp, nbuf, ns_p = _mix_prompt(xp, (bsz, t, d), row(norm_mix[l]), win_all, l, wlr, gup, gb, conv_w[l], gg, wout, tt)
        conv_p.append(nbuf)
        gla_p.append(ns_p)
        xp, cls, rank, counts = _att_prompt(xp, row(norm_mem[l]), wq, wo, mk_all, mv_all, l,
                                            row(norm_ffn[l]), rw, rb, tq)
        xp = _moe_sparse(xp, cls, rank, counts, *moe_w, tm_sorted, rows_perm,
                         norm_g=row(norm_final) if l == depth - 1 else None)

        yc, u, q, k, v, gate, la = _mix_sample_in(
            xs, row(norm_mix[l]), win_all, l, wlr, gup, gb, conv_w[l], state_conv[l, :, 0], state_conv[l, :, 1])
        conv_s.append(jnp.stack([state_conv[l, :, 1], u], axis=1))
        s_new, o = _gla_step(q, k, v, la, state_gla, l, nb_gla)
        gla_s.append(s_new)
        wq_s = wq.reshape(d, MEM_HEADS, 2, LANES).transpose(0, 2, 1, 3).reshape(d, d)
        wo_s = wo.reshape(MEM_HEADS, 2, LANES, d).transpose(1, 0, 2, 3).reshape(d, d)
        xs, qa = _mix_sample_out(xs, yc, o, gate, gg, wout, row(norm_mem[l]), wq_s)
        oa = _att_sample(qa.reshape(ns, ATT_ROWS, LANES), ck, cv, l, nb_att)
        xs = _oproj(xs, oa.reshape(ns, d), wo_s)
        xs = _moe(xs, *moe_w, min(tm_moe, ns))

    y_prompt = xp.reshape(bsz, t, d)
    y_sample = _final_norm(xs, row(norm_final), ns).reshape(ns, 1, d)
    return (y_prompt, y_sample, jnp.stack(conv_p), jnp.stack(gla_p), untile_rows(mk_all), untile_rows(mv_all),
            jnp.stack(conv_s), jnp.stack(gla_s))
```

```python
import functools

import jax
import jax.numpy as jnp
from jax import lax
from jax.experimental import pallas as pl
from jax.experimental.pallas import tpu as pltpu

F32 = jnp.float32
BF16 = jnp.bfloat16
HIGHEST = lax.Precision.HIGHEST

EPS = 1e-6
CONV_W = 512
GLA_HEADS = 4
GLA_DK = 64
GLA_DV = 128
GLA_K = GLA_HEADS * GLA_DK
GLA_V = GLA_HEADS * GLA_DV
GLA_RANK = 16
GLA_TAU = 16.0
GLA_CHUNK = 64
MEM_HEADS = 4
MEM_DH = 256
N_GROUPS = 4
EXP_PER_GROUP = 4
N_EXPERTS = 16
LANES = 128
C_CB, C_CC, C_CH, C_Q, C_K, C_V, C_G, C_LR = 0, 512, 1024, 1536, 1792, 2048, 2560, 3072
VMEM_LIMIT = 52 * 1024 * 1024


def _params(*sem):
    return pltpu.CompilerParams(dimension_semantics=sem, vmem_limit_bytes=VMEM_LIMIT)


def _rms(x, g):
    return x * lax.rsqrt(jnp.mean(x * x, axis=-1, keepdims=True) + EPS) * g


def _dot(a, b, precision=None):
    return jnp.dot(a, b, precision=precision, preferred_element_type=F32)


def _dot_nt(a, b):
    return lax.dot_general(a, b, (((1,), (1,)), ((), ())), preferred_element_type=F32)


def _dot_tn(a, b, precision=None):
    return lax.dot_general(a, b, (((0,), (0,)), ((), ())), precision=precision,
                           preferred_element_type=F32)


def _dot_split(dot, mask, x):
    hi = x.astype(BF16)
    r1 = x - hi.astype(F32)
    mid = r1.astype(BF16)
    lo = (r1 - mid.astype(F32)).astype(BF16)
    return dot(mask, hi) + dot(mask, mid) + dot(mask, lo)


def _silu(x):
    return x / (1.0 + jnp.exp(-x))


def _log_sigmoid(x):
    return jnp.minimum(x, 0.0) - jnp.log1p(jnp.exp(-jnp.abs(x)))


def _head_norm_gate(o, g, gg):
    parts = []
    for h in range(GLA_HEADS):
        sl = slice(h * GLA_DV, (h + 1) * GLA_DV)
        parts.append(_rms(o[:, sl], gg[:, sl]))
    return jnp.concatenate(parts, axis=-1) * _silu(g)


def _full(shape):
    nd = len(shape)
    return pl.BlockSpec(shape, lambda *_: (0,) * nd)


def _mix_prompt_kernel(x_ref, g_ref, win_ref, wlr_ref, gup_ref, gb_ref, cw_ref, gg_ref, wout_ref,
                       x1_ref, conv_ref, s_ref, ubuf, s_scr, o_scr, *, tok_in):
    t = pl.program_id(1)
    tt = x1_ref.shape[1]

    @pl.when(t == 0)
    def _():
        ubuf[0:8, :] = jnp.zeros((8, CONV_W), F32)
        s_scr[...] = jnp.zeros(s_scr.shape, F32)

    x = _load_tok(x_ref, tt) if tok_in else x_ref[0]
    hb = _rms(x, g_ref[...]).astype(BF16)

    def proj(a, b):
        return _dot(hb, win_ref[0, :, a:b])

    u = proj(C_CC, C_CH) * proj(C_CH, C_Q)
    ubuf[8:8 + tt, :] = u
    cw = cw_ref[...]
    yc = proj(C_CB, C_CC) * (ubuf[6:6 + tt, :] * cw[0:1] + ubuf[7:7 + tt, :] * cw[1:2] + u * cw[2:3])
    ubuf[6:8, :] = u[tt - 2:tt, :]

    qs = proj(C_Q, C_K) * (GLA_DK ** -0.5)
    k = proj(C_K, C_V)
    v = proj(C_V, C_G)
    lr = _dot(hb, wlr_ref[...])
    gate = _dot(lr.astype(BF16), gup_ref[...]) + gb_ref[...]
    la = _log_sigmoid(gate) * (1.0 / GLA_TAU)

    c = GLA_CHUNK
    nc = tt // c
    iota = lambda shape, dim: lax.broadcasted_iota(jnp.int32, shape, dim)
    ltri = (iota((c, c), 0) >= iota((c, c), 1)).astype(BF16)
    b_wide = _dot_split(_dot, ltri, jnp.concatenate([la[j * c:(j + 1) * c] for j in range(nc)], axis=1))
    b_ends = jnp.concatenate([b_wide[c - 1:c, j * GLA_K:(j + 1) * GLA_K] for j in range(nc)], axis=0)
    spread = (iota((nc, LANES), 0) == iota((nc, LANES), 1)).astype(BF16)
    log_dec = _dot_split(lambda m, x: _dot_tn(x, m), spread, b_ends)
    head_feat = iota((GLA_K, GLA_K), 0) // c == iota((GLA_K, GLA_K), 1) // GLA_DK
    head_blk = iota((GLA_K, GLA_V), 0) // GLA_DK == iota((GLA_K, GLA_V), 1) // GLA_DV
    causal = iota((c, GLA_K), 0) >= iota((c, GLA_K), 1) % c
    for j in range(nc):
        r = slice(j * c, (j + 1) * c)
        b_c, q_c, k_c, v_c = b_wide[:, j * GLA_K:(j + 1) * GLA_K], qs[r], k[r], v[r]
        b_mid = b_c[c // 2:c // 2 + 1, :]
        b_last = b_c[c - 1:c, :]
        q_i = (q_c * jnp.exp(b_c - b_mid)).astype(BF16)
        k_i = k_c * jnp.exp(b_mid - b_c)
        k_dec = (k_c * jnp.exp(b_last - b_c)).astype(BF16)
        q_b = (q_c * jnp.exp(b_c)).astype(BF16)
        k_rows = jnp.where(head_feat, jnp.concatenate([k_i] * GLA_HEADS, axis=0), 0.0).astype(BF16)
        a = jnp.where(causal, _dot_nt(q_i, k_rows), 0.0).astype(BF16)
        v_blk = jnp.where(head_blk, jnp.concatenate([v_c] * GLA_HEADS, axis=0), 0.0).astype(BF16)
        s_prev = s_scr[...]
        o_scr[r, :] = _dot(jnp.concatenate([a, q_b], axis=1),
                           jnp.concatenate([v_blk, s_prev.astype(BF16)], axis=0))
        dec = jnp.exp(jnp.broadcast_to(log_dec[:, j:j + 1], (GLA_K, GLA_DV)))
        s_scr[...] = (jnp.concatenate([dec] * GLA_HEADS, axis=1) * s_prev
                      + jnp.where(head_blk, _dot_tn(k_dec, v_c.astype(BF16)), 0.0))

    yg = _head_norm_gate(o_scr[...], proj(C_G, C_LR), gg_ref[...])
    y = _dot(jnp.concatenate([yc.astype(BF16), yg.astype(BF16)], axis=1), wout_ref[...])
    x1_ref[0] = x + y

    @pl.when(t == pl.num_programs(1) - 1)
    def _():
        conv_ref[0] = u[tt - 2:tt, :]
        for h in range(GLA_HEADS):
            s_ref[0, h] = s_scr[h * GLA_DK:(h + 1) * GLA_DK, h * GLA_DV:(h + 1) * GLA_DV]


def _mix_prompt(x, shape, g, win, layer, wlr, gup, gb, cw, gg, wout, tt):
    bsz, t, d = shape
    tok_in = x.ndim == 2
    nt = t // tt
    x_spec = (pl.BlockSpec((tt * TOK_ROWS, LANES), lambda b, i: (b * nt + i, 0)) if tok_in
              else pl.BlockSpec((1, tt, d), lambda b, i: (b, i, 0)))
    return pl.pallas_call(
        functools.partial(_mix_prompt_kernel, tok_in=tok_in),
        grid=(bsz, nt),
        in_specs=[
            x_spec,
            _full(g.shape), pl.BlockSpec((1,) + win.shape[1:], lambda b, i: (layer, 0, 0)), _full(wlr.shape),
            _full(gup.shape), _full(gb.shape), _full(cw.shape), _full(gg.shape), _full(wout.shape),
        ],
        out_specs=[
            pl.BlockSpec((1, tt, d), lambda b, i: (b, i, 0)),
            pl.BlockSpec((1, 2, CONV_W), lambda b, i: (b, 0, 0)),
            pl.BlockSpec((1, GLA_HEADS, GLA_DK, GLA_DV), lambda b, i: (b, 0, 0, 0)),
        ],
        out_shape=[
            jax.ShapeDtypeStruct((bsz, t, d), F32),
            jax.ShapeDtypeStruct((bsz, 2, CONV_W), F32),
            jax.ShapeDtypeStruct((bsz, GLA_HEADS, GLA_DK, GLA_DV), F32),
        ],
        scratch_shapes=[
            pltpu.VMEM((8 + tt, CONV_W), F32),
            pltpu.VMEM((GLA_K, GLA_V), F32),
            pltpu.VMEM((tt, GLA_V), F32),
        ],
        compiler_params=_params("arbitrary", "arbitrary"),
        name="mix_prompt",
    )(x, g, win, wlr, gup, gb, cw, gg, wout)


def _kv_kernel(m_ref, wk_ref, wv_ref, k_ref, v_ref):
    tm = m_ref.shape[0]
    mb = m_ref[...].astype(BF16)
    for w_ref, o_ref in ((wk_ref, k_ref), (wv_ref, v_ref)):
        y = _dot(mb, w_ref[0])
        for h in range(MEM_HEADS):
            for dt in range(MEM_DH // LANES):
                col = h * MEM_DH + dt * LANES
                o_ref[0, pl.ds(dt * MEM_HEADS + h, tm, stride=ATT_ROWS), :] = y[:, col:col + LANES]


def _mem_kv(mem, wk, wv, tm):
    n, d = mem.shape
    depth = wk.shape[0]
    w_spec = pl.BlockSpec((1, d, d), lambda l, i: (l, 0, 0))
    o_spec = pl.BlockSpec((1, tm * ATT_ROWS, LANES), lambda l, i: (l, i, 0))
    return pl.pallas_call(
        _kv_kernel,
        grid=(depth, n // tm),
        in_specs=[pl.BlockSpec((tm, d), lambda l, i: (i, 0)), w_spec, w_spec],
        out_specs=[o_spec] * 2,
        out_shape=[jax.ShapeDtypeStruct((depth, n * ATT_ROWS, LANES), F32)] * 2,
        compiler_params=_params("arbitrary", "arbitrary"),
        name="mem_kv",
    )(mem, wk, wv)


def _att_prompt_kernel(x_ref, g_ref, wq_ref, wo_ref, mk_ref, mv_ref, gf_ref, rw_ref, rb_ref, ltri_ref,
                       out_ref, cls_ref, rank_ref, cnt_ref, carry):
    @pl.when((pl.program_id(0) == 0) & (pl.program_id(1) == 0))
    def _():
        carry[...] = jnp.zeros(carry.shape, F32)

    x = x_ref[0]
    xb = _rms(x, g_ref[...]).astype(BF16)
    q = _dot(xb, wq_ref[...])
    nm = mk_ref.shape[1] // ATT_ROWS

    def head_rows(ref, h):
        return jnp.concatenate([ref[0, pl.ds(dt * MEM_HEADS + h, nm, stride=ATT_ROWS), :]
                                for dt in range(MEM_DH // LANES)], axis=-1).astype(BF16)

    outs = []
    for h in range(MEM_HEADS):
        sl = slice(h * MEM_DH, (h + 1) * MEM_DH)
        s = _dot_nt(q[:, sl].astype(BF16), head_rows(mk_ref, h)) * (MEM_DH ** -0.5)
        e = jnp.exp(s - jnp.max(s, axis=-1, keepdims=True))
        p = e / jnp.sum(e, axis=-1, keepdims=True)
        outs.append(_dot(p.astype(BF16), head_rows(mv_ref, h)))
    o = jnp.concatenate(outs, axis=-1)
    x2 = x + _dot(o.astype(BF16), wo_ref[...])
    _store_tok(out_ref, x2)
    _route_meta(x2, gf_ref, rw_ref, rb_ref, ltri_ref, carry, cls_ref, rank_ref, cnt_ref)


def _att_prompt(x, g, wq, wo, mk, mv, layer, gf, rw, rb, tq):
    bsz, t, d = x.shape
    assert d == TOK_ROWS * LANES
    nq = t // tq
    rows = mk.shape[1] // bsz
    kv_spec = pl.BlockSpec((1, rows, LANES), lambda b, i: (layer, b, 0))
    meta_spec = pl.BlockSpec((tq // LANES, LANES), lambda b, i: (b * nq + i, 0))
    rw_hi = rw.astype(BF16)
    rw_split = jnp.concatenate([rw_hi, (rw - rw_hi.astype(F32)).astype(BF16)], axis=1)
    ltri = jnp.tril(jnp.ones((LANES, LANES), BF16))
    n_tok = bsz * t
    return pl.pallas_call(
        _att_prompt_kernel,
        grid=(bsz, nq),
        in_specs=[
            pl.BlockSpec((1, tq, d), lambda b, i: (b, i, 0)),
            _full(g.shape), _full(wq.shape), _full(wo.shape),
            kv_spec, kv_spec,
            _full(gf.shape), _full(rw_split.shape), _full(rb.shape), _full(ltri.shape),
        ],
        out_specs=[
            pl.BlockSpec((tq * TOK_ROWS, LANES), lambda b, i: (b * nq + i, 0)),
            meta_spec, meta_spec, _full((1, LANES)),
        ],
        out_shape=[
            jax.ShapeDtypeStruct((n_tok * TOK_ROWS, LANES), F32),
            jax.ShapeDtypeStruct((n_tok // LANES, LANES), jnp.int32),
            jax.ShapeDtypeStruct((n_tok // LANES, LANES), jnp.int32),
            jax.ShapeDtypeStruct((1, LANES), jnp.int32),
        ],
        scratch_shapes=[pltpu.VMEM((1, LANES), F32)],
        compiler_params=_params("arbitrary", "arbitrary"),
        name="att_prompt",
    )(x, g, wq, wo, mk, mv, gf, rw_split, rb, ltri)


def _route_top2(logits):
    lane = lax.broadcasted_iota(jnp.int32, logits.shape, 1)
    lanef = lane.astype(F32)
    ninf = -jnp.inf
    big = 1e9
    gl = jnp.where(lane < N_GROUPS, logits, ninf)
    gmax = jnp.max(gl, axis=-1, keepdims=True)
    g_idx = jnp.min(jnp.where(gl == gmax, lanef, big), axis=-1, keepdims=True)
    g_w = 1.0 / jnp.sum(jnp.exp(gl - gmax), axis=-1, keepdims=True)
    grp = ((lane - N_GROUPS) >> 2).astype(F32)
    emask = (lane >= N_GROUPS) & (lane < N_GROUPS + N_EXPERTS) & (grp == g_idx)
    el = jnp.where(emask, logits, ninf)
    m1 = jnp.max(el, axis=-1, keepdims=True)
    i1 = jnp.min(jnp.where(el == m1, lanef, big), axis=-1, keepdims=True)
    el2 = jnp.where(lanef == i1, ninf, el)
    m2 = jnp.max(el2, axis=-1, keepdims=True)
    i2 = jnp.min(jnp.where(el2 == m2, lanef, big), axis=-1, keepdims=True)
    tail = jnp.exp(m2 - m1)
    w1 = g_w / (1.0 + tail)
    w2 = g_w * tail / (1.0 + tail)
    return g_idx, i1, i2, w1, w2


def _route(logits):
    _, i1, i2, w1, w2 = _route_top2(logits)
    lanef = lax.broadcasted_iota(jnp.int32, logits.shape, 1).astype(F32)
    return jnp.where(lanef == i1, w1, 0.0) + jnp.where(lanef == i2, w2, 0.0)


N_PAIRS = 6
N_CLASSES = N_GROUPS * N_PAIRS
TOK_ROWS = 8


def _load_tok(ref, n):
    return jnp.concatenate([ref[pl.ds(j, n, stride=TOK_ROWS), :] for j in range(TOK_ROWS)], axis=-1)


def _store_tok(ref, val):
    n = val.shape[0]
    for j in range(TOK_ROWS):
        ref[pl.ds(j, n, stride=TOK_ROWS), :] = val[:, j * LANES:(j + 1) * LANES]


def _lane_dense(col):
    eye = lax.broadcasted_iota(jnp.int32, (LANES, LANES), 0) == lax.broadcasted_iota(jnp.int32, (LANES, LANES), 1)
    rows = [jnp.sum(jnp.where(eye, col[b * LANES:(b + 1) * LANES], 0.0), axis=0, keepdims=True)
            for b in range(col.shape[0] // LANES)]
    return jnp.concatenate(rows, axis=0)


def _route_meta(x, g_ref, rw_ref, rb_ref, ltri_ref, carry, cls_ref, rank_ref, cnt_ref):
    n = x.shape[0]
    xn = _rms(x, g_ref[...])
    x_hi = xn.astype(BF16)
    x_lo = (xn - x_hi.astype(F32)).astype(BF16)
    both = _dot(x_hi, rw_ref[...])
    logits = (both[:, :LANES] + (_dot(x_lo, rw_ref[:, :LANES]) + both[:, LANES:])) + rb_ref[...]
    g_idx, i1, i2, _, _ = _route_top2(logits)
    lo = jnp.minimum(i1, i2) - N_GROUPS - EXP_PER_GROUP * g_idx
    hi = jnp.maximum(i1, i2) - N_GROUPS - EXP_PER_GROUP * g_idx
    cls = g_idx * N_PAIRS + lo * (2 * EXP_PER_GROUP - 1.0 - lo) * 0.5 + hi - lo - 1.0

    lane = lax.broadcasted_iota(jnp.int32, (n, LANES), 1)
    onehot = lane.astype(F32) == cls
    onehot_b = onehot.astype(BF16)
    blocks = []
    run = carry[...]
    for b in range(n // LANES):
        p_b = _dot(ltri_ref[...], onehot_b[b * LANES:(b + 1) * LANES]) + run
        blocks.append(p_b)
        run = p_b[LANES - 1:LANES, :]
    prefix = jnp.concatenate(blocks, axis=0)
    rank = jnp.sum(jnp.where(onehot, prefix, 0.0), axis=-1, keepdims=True) - 1.0
    carry[...] = prefix[n - 1:n, :]
    cls_ref[...] = _lane_dense(cls).astype(jnp.int32)
    rank_ref[...] = _lane_dense(rank).astype(jnp.int32)
    cnt_ref[...] = prefix[n - 1:n, :].astype(jnp.int32)


DMA_UNROLL = 8


def _tok_rows(t):
    return pl.ds(pl.multiple_of(t * TOK_ROWS, TOK_ROWS), TOK_ROWS)


def _tok_scatter_kernel(cls, rank, starts, tail, x_ref, dst_ref, stage, zeros, sems, *, toks, steps):
    i = pl.program_id(0)
    slot = i % 2
    base = i * toks
    rows = toks * TOK_ROWS
    tile_rows = zeros.shape[0]

    def wait_slot(s):
        pltpu.make_async_copy(stage.at[s], dst_ref.at[pl.ds(0, rows)], sems.at[s]).wait()

    @pl.when(i == 0)
    def _():
        zeros[...] = jnp.zeros(zeros.shape, F32)

        def fill(c):
            return pltpu.make_async_copy(
                zeros, dst_ref.at[pl.ds(pl.multiple_of(tail[c] * tile_rows, tile_rows), tile_rows)], sems.at[2])

        for c in range(2 * N_CLASSES):
            @pl.when(tail[c] >= 0)
            def _():
                fill(c).start()
        for c in range(2 * N_CLASSES):
            @pl.when(tail[c] >= 0)
            def _():
                fill(c).wait()

    @pl.when(i >= 2)
    def _():
        wait_slot(slot)

    stage[slot] = x_ref[...]

    def start(r8, c):
        for u in range(DMA_UNROLL):
            r = r8 * DMA_UNROLL + u
            dst = starts[cls[base + r]] + rank[base + r]
            pltpu.make_async_copy(stage.at[slot, _tok_rows(r)], dst_ref.at[_tok_rows(dst)],
                                  sems.at[slot]).start(priority=u % 2)
        return c

    lax.fori_loop(0, toks // DMA_UNROLL, start, 0)

    @pl.when(i == steps - 1)
    def _():
        wait_slot(slot)
        if steps >= 2:
            wait_slot(1 - slot)


def _tok_scatter(x, n_out, cls, rank, starts, tail, tile, toks):
    steps = x.shape[0] // (toks * TOK_ROWS)
    return pl.pallas_call(
        functools.partial(_tok_scatter_kernel, toks=toks, steps=steps),
        grid_spec=pltpu.PrefetchScalarGridSpec(
            num_scalar_prefetch=4,
            grid=(steps,),
            in_specs=[pl.BlockSpec((toks * TOK_ROWS, LANES), lambda i, *_: (i, 0))],
            out_specs=pl.BlockSpec(memory_space=pl.ANY),
            scratch_shapes=[pltpu.VMEM((2, toks * TOK_ROWS, LANES), F32), pltpu.VMEM((tile * TOK_ROWS, LANES), F32),
                            pltpu.SemaphoreType.DMA((3,))],
        ),
        out_shape=jax.ShapeDtypeStruct((n_out * TOK_ROWS, LANES), F32),
        compiler_params=_params("arbitrary"),
        name="tok_scatter",
    )(cls, rank, starts, tail, x)


def _tok_gather_kernel(cls, rank, starts, src_ref, *rest, toks, norm):
    if norm:
        g_ref, out_ref, stage, sem = rest
        dst = stage
    else:
        out_ref, sem = rest
        dst = out_ref
    base = pl.program_id(0) * toks

    def start(r8, c):
        for u in range(DMA_UNROLL):
            r = r8 * DMA_UNROLL + u
            src = starts[cls[base + r]] + rank[base + r]
            pltpu.make_async_copy(src_ref.at[_tok_rows(src)], dst.at[_tok_rows(r)], sem).start(priority=u % 2)
        return c

    lax.fori_loop(0, toks // DMA_UNROLL, start, 0)
    pltpu.make_async_copy(src_ref.at[pl.ds(0, toks * TOK_ROWS)], dst, sem).wait()
    if norm:
        out_ref[...] = _rms(_load_tok(stage, toks), g_ref[...])


def _tok_gather(src, cls, rank, starts, toks, norm_g=None):
    n = cls.shape[0]
    norm = norm_g is not None
    d = TOK_ROWS * LANES
    return pl.pallas_call(
        functools.partial(_tok_gather_kernel, toks=toks, norm=norm),
        grid_spec=pltpu.PrefetchScalarGridSpec(
            num_scalar_prefetch=3,
            grid=(n // toks,),
            in_specs=[pl.BlockSpec(memory_space=pl.ANY)]
            + ([pl.BlockSpec(norm_g.shape, lambda i, *_: (0, 0))] if norm else []),
            out_specs=(pl.BlockSpec((toks, d), lambda i, *_: (i, 0)) if norm
                       else pl.BlockSpec((toks * TOK_ROWS, LANES), lambda i, *_: (i, 0))),
            scratch_shapes=([pltpu.VMEM((toks * TOK_ROWS, LANES), F32)] if norm else [])
            + [pltpu.SemaphoreType.DMA(())],
        ),
        out_shape=jax.ShapeDtypeStruct((n, d) if norm else (n * TOK_ROWS, LANES), F32),
        compiler_params=_params("arbitrary"),
        name="tok_gather",
    )(cls, rank, starts, src, *([norm_g] if norm else []))


def _moe_sorted_kernel(e_lo, e_hi, valid, xs_ref, g_ref, rw_ref, rb_ref,
                       wg_lo, wu_lo, wd_lo, wg_hi, wu_hi, wd_hi, out_ref, wgu_scr, wd_scr):
    t = pl.program_id(0)
    tm = xs_ref.shape[0] // TOK_ROWS
    ff = wd_lo.shape[2]
    prev = jnp.maximum(t - 1, 0)

    for slot, ids, wg, wu, wd in ((0, e_lo, wg_lo, wu_lo, wd_lo), (1, e_hi, wg_hi, wu_hi, wd_hi)):
        @pl.when((t == 0) | (ids[t] != ids[prev]))
        def _():
            wgu_scr[:, (2 * slot) * ff:(2 * slot + 1) * ff] = wg[0, 0].astype(BF16)
            wgu_scr[:, (2 * slot + 1) * ff:(2 * slot + 2) * ff] = wu[0, 0].astype(BF16)
            wd_scr[slot * ff:(slot + 1) * ff, :] = wd[0, 0].astype(BF16)

    @pl.when(valid[t] > 0)
    def _():
        x = _load_tok(xs_ref, tm)
        xb = _rms(x, g_ref[...]).astype(BF16)
        logits = _dot(xb, rw_ref[...]) + rb_ref[...]
        lane = lax.broadcasted_iota(jnp.int32, logits.shape, 1)
        gl = jnp.where(lane < N_GROUPS, logits, -jnp.inf)
        g_w = 1.0 / jnp.sum(jnp.exp(gl - jnp.max(gl, axis=-1, keepdims=True)), axis=-1, keepdims=True)
        l_lo = jnp.sum(jnp.where(lane == e_lo[t] + N_GROUPS, logits, 0.0), axis=-1, keepdims=True)
        l_hi = jnp.sum(jnp.where(lane == e_hi[t] + N_GROUPS, logits, 0.0), axis=-1, keepdims=True)
        tail = jnp.exp(-jnp.abs(l_lo - l_hi))
        w_top = g_w / (1.0 + tail)
        w_oth = g_w * tail / (1.0 + tail)
        lo_top = l_lo >= l_hi
        gu = _dot(xb, wgu_scr[...])
        hid = [_silu(gu[:, 2 * s * ff:(2 * s + 1) * ff]) * gu[:, (2 * s + 1) * ff:(2 * s + 2) * ff] * w
               for s, w in ((0, jnp.where(lo_top, w_top, w_oth)), (1, jnp.where(lo_top, w_oth, w_top)))]
        _store_tok(out_ref, x + _dot(jnp.concatenate(hid, axis=1).astype(BF16), wd_scr[...]))

    @pl.when(valid[t] == 0)
    def _():
        out_ref[...] = jnp.zeros(out_ref.shape, F32)


def _moe_sorted(xs, g, rw, rb, wg, wu, wd, layer, e_lo, e_hi, valid, tm):
    npad = xs.shape[0] // TOK_ROWS
    _, _, d, ff = wg.shape
    const = lambda t, lo, hi, v: (0, 0)
    at_lo = lambda t, lo, hi, v: (layer, lo[t], 0, 0)
    at_hi = lambda t, lo, hi, v: (layer, hi[t], 0, 0)
    return pl.pallas_call(
        _moe_sorted_kernel,
        grid_spec=pltpu.PrefetchScalarGridSpec(
            num_scalar_prefetch=3,
            grid=(npad // tm,),
            in_specs=[
                pl.BlockSpec((tm * TOK_ROWS, LANES), lambda t, lo, hi, v: (jnp.where(v[t] > 0, t, 0), 0)),
                pl.BlockSpec(g.shape, const), pl.BlockSpec(rw.shape, const), pl.BlockSpec(rb.shape, const),
                pl.BlockSpec((1, 1, d, ff), at_lo), pl.BlockSpec((1, 1, d, ff), at_lo),
                pl.BlockSpec((1, 1, ff, d), at_lo),
                pl.BlockSpec((1, 1, d, ff), at_hi), pl.BlockSpec((1, 1, d, ff), at_hi),
                pl.BlockSpec((1, 1, ff, d), at_hi),
            ],
            out_specs=pl.BlockSpec((tm * TOK_ROWS, LANES), lambda t, lo, hi, v: (t, 0)),
            scratch_shapes=[pltpu.VMEM((d, 4 * ff), BF16), pltpu.VMEM((2 * ff, d), BF16)],
        ),
        out_shape=jax.ShapeDtypeStruct(xs.shape, F32),
        compiler_params=_params("arbitrary"),
        name="moe_sorted",
    )(e_lo, e_hi, valid, xs, g, rw, rb, wg, wu, wd, wg, wu, wd)


def _moe_sparse(x, cls, rank, counts, g, rw, rb, wg, wu, wd, layer, tm, toks, norm_g=None):
    n = x.shape[0] // TOK_ROWS
    cls, rank = cls.reshape(n), rank.reshape(n)
    cnt = counts[0, :N_CLASSES]
    padded = ((cnt + tm - 1) // tm) * tm
    ends = jnp.cumsum(padded)
    starts = (ends - padded).astype(jnp.int32)
    n_tiles = n // tm + N_CLASSES
    tile_start = jnp.arange(n_tiles, dtype=jnp.int32) * tm
    n_valid = ends[-1] // tm
    unused = n_valid + jnp.arange(N_CLASSES, dtype=jnp.int32)
    tail = jnp.concatenate([jnp.where(cnt > 0, ends // tm - 1, -1),
                            jnp.where(unused < n_tiles, unused, -1)]).astype(jnp.int32)
    last_cls = jnp.sum((tile_start[jnp.maximum(n_valid - 1, 0)] >= ends).astype(jnp.int32))
    tile_cls = jnp.sum((tile_start[:, None] >= ends[None, :]).astype(jnp.int32), axis=1)
    valid = (tile_start < ends[-1]).astype(jnp.int32)
    tile_cls = jnp.where(valid > 0, tile_cls, last_cls)
    pair_lo = jnp.array([0, 0, 0, 1, 1, 2], jnp.int32)
    pair_hi = jnp.array([1, 2, 3, 2, 3, 3], jnp.int32)
    e_lo = (tile_cls // N_PAIRS) * EXP_PER_GROUP + pair_lo[tile_cls % N_PAIRS]
    e_hi = (tile_cls // N_PAIRS) * EXP_PER_GROUP + pair_hi[tile_cls % N_PAIRS]

    xs = _tok_scatter(x, n_tiles * tm, cls, rank, starts, tail, tm, toks[0])
    ys = _moe_sorted(xs, g, rw.astype(BF16), rb, wg, wu, wd, layer, e_lo, e_hi, valid, tm)
    return _tok_gather(ys, cls, rank, starts, toks[1], norm_g)


def _moe_kernel(x_ref, g_ref, rw_ref, rb_ref, wg_ref, wu_ref, wd_ref, out_ref, xn_scr, comb_scr, acc_scr):
    e = pl.program_id(1)

    @pl.when(e == 0)
    def _():
        x = x_ref[...]
        xn = _rms(x, g_ref[...])
        xn_scr[...] = xn.astype(BF16)
        comb_scr[...] = _route(_dot(xn, rw_ref[...], HIGHEST) + rb_ref[...])
        acc_scr[...] = x

    xb = xn_scr[...]
    lane = lax.broadcasted_iota(jnp.int32, comb_scr.shape, 1)
    cmb = jnp.sum(jnp.where(lane == e + N_GROUPS, comb_scr[...], 0.0), axis=-1, keepdims=True)
    hid = _silu(_dot(xb, wg_ref[0, 0].astype(BF16))) * _dot(xb, wu_ref[0, 0].astype(BF16)) * cmb
    acc_scr[...] += _dot(hid.astype(BF16), wd_ref[0, 0].astype(BF16))

    @pl.when(e == pl.num_programs(1) - 1)
    def _():
        out_ref[...] = acc_scr[...]


def _moe(x, g, rw, rb, wg, wu, wd, layer, tm):
    n, d = x.shape
    _, ne, _, ff = wg.shape
    return pl.pallas_call(
        _moe_kernel,
        grid=(n // tm, ne),
        in_specs=[
            pl.BlockSpec((tm, d), lambda i, e: (i, 0)),
            _full(g.shape), _full(rw.shape), _full(rb.shape),
            pl.BlockSpec((1, 1, d, ff), lambda i, e: (layer, e, 0, 0)),
            pl.BlockSpec((1, 1, d, ff), lambda i, e: (layer, e, 0, 0)),
            pl.BlockSpec((1, 1, ff, d), lambda i, e: (layer, e, 0, 0)),
        ],
        out_specs=pl.BlockSpec((tm, d), lambda i, e: (i, 0)),
        out_shape=jax.ShapeDtypeStruct((n, d), F32),
        scratch_shapes=[pltpu.VMEM((tm, d), BF16), pltpu.VMEM((tm, LANES), F32), pltpu.VMEM((tm, d), F32)],
        compiler_params=_params("arbitrary", "arbitrary"),
        name="moe",
    )(x, g, rw, rb, wg, wu, wd)


def _norm_kernel(x_ref, g_ref, o_ref):
    o_ref[...] = _rms(x_ref[...], g_ref[...])


def _final_norm(x, g, tm):
    n, d = x.shape
    return pl.pallas_call(
        _norm_kernel,
        grid=(n // tm,),
        in_specs=[pl.BlockSpec((tm, d), lambda i: (i, 0)), _full(g.shape)],
        out_specs=pl.BlockSpec((tm, d), lambda i: (i, 0)),
        out_shape=jax.ShapeDtypeStruct((n, d), F32),
        compiler_params=_params("arbitrary"),
        name="final_norm",
    )(x, g)


def _mix_sample_in_kernel(x_ref, g_ref, win_ref, wlr_ref, gup_ref, gb_ref, cw_ref, b0_ref, b1_ref,
                          yc_ref, u_ref, q_ref, k_ref, v_ref, gate_ref, la_ref):
    hb = _rms(x_ref[...], g_ref[...]).astype(BF16)

    def proj(a, b):
        return _dot(hb, win_ref[0, :, a:b])

    u = proj(C_CC, C_CH) * proj(C_CH, C_Q)
    cw = cw_ref[...]
    yc_ref[...] = proj(C_CB, C_CC) * (b0_ref[...] * cw[0:1] + b1_ref[...] * cw[1:2] + u * cw[2:3])
    u_ref[...] = u
    q_ref[...] = proj(C_Q, C_K) * (GLA_DK ** -0.5)
    k_ref[...] = proj(C_K, C_V)
    v_ref[...] = proj(C_V, C_G)
    gate_ref[...] = proj(C_G, C_LR)
    lr = _dot(hb, wlr_ref[...])
    gate = _dot(lr.astype(BF16), gup_ref[...]) + gb_ref[...]
    la_ref[...] = _log_sigmoid(gate) * (1.0 / GLA_TAU)


def _mix_sample_in(x, g, win, layer, wlr, gup, gb, cw, b0, b1):
    n = x.shape[0]
    args = (x, g, win, wlr, gup, gb, cw, b0, b1)
    specs = [_full(a.shape) for a in args]
    specs[2] = pl.BlockSpec((1,) + win.shape[1:], lambda i: (layer, 0, 0))
    widths = (CONV_W, CONV_W, GLA_K, GLA_K, GLA_V, GLA_V, GLA_K)
    return pl.pallas_call(
        _mix_sample_in_kernel,
        grid=(1,),
        in_specs=specs,
        out_specs=[_full((n, w)) for w in widths],
        out_shape=[jax.ShapeDtypeStruct((n, w), F32) for w in widths],
        compiler_params=_params("arbitrary"),
        name="mix_sample_in",
    )(*args)


def _gla_step_kernel(q_ref, k_ref, v_ref, la_ref, s0_ref, s_ref, o_ref):
    nb = q_ref.shape[0]
    a = jnp.exp(la_ref[...])
    kb = k_ref[...].astype(BF16)
    qb = q_ref[...].astype(BF16)
    vf = v_ref[...].astype(BF16).astype(F32)
    rows = lax.broadcasted_iota(jnp.int32, (nb, 1), 0)
    spread = (lax.broadcasted_iota(jnp.int32, (nb, nb * GLA_DV), 0)
              == lax.broadcasted_iota(jnp.int32, (nb, nb * GLA_DV), 1) // GLA_DV)
    spread_b = spread.astype(BF16)
    for h in range(GLA_HEADS):
        ks = slice(h * GLA_DK, (h + 1) * GLA_DK)
        a_cols = _dot_tn(a[:, ks], spread.astype(F32), HIGHEST)
        k_cols = _dot_tn(kb[:, ks], spread_b)
        q_cols = _dot_tn(qb[:, ks], spread_b)
        o_h = jnp.zeros((nb, GLA_DV), F32)
        for n in range(nb):
            blk = slice(n * GLA_DV, (n + 1) * GLA_DV)
            v_row = vf[n:n + 1, h * GLA_DV:(h + 1) * GLA_DV]
            s_new = a_cols[:, blk] * s0_ref[0, n, h] + k_cols[:, blk] * v_row
            s_ref[n, h] = s_new
            o_h = jnp.where(rows == n, jnp.sum(q_cols[:, blk] * s_new, axis=0, keepdims=True), o_h)
        o_ref[:, h * GLA_DV:(h + 1) * GLA_DV] = o_h


def _gla_step(q, k, v, la, state, layer, nb):
    n = q.shape[0]
    sshape = (GLA_HEADS, GLA_DK, GLA_DV)
    return pl.pallas_call(
        _gla_step_kernel,
        grid=(n // nb,),
        in_specs=[
            pl.BlockSpec((nb, GLA_K), lambda i: (i, 0)),
            pl.BlockSpec((nb, GLA_K), lambda i: (i, 0)),
            pl.BlockSpec((nb, GLA_V), lambda i: (i, 0)),
            pl.BlockSpec((nb, GLA_K), lambda i: (i, 0)),
            pl.BlockSpec((1, nb) + sshape, lambda i: (layer, i, 0, 0, 0)),
        ],
        out_specs=[
            pl.BlockSpec((nb,) + sshape, lambda i: (i, 0, 0, 0)),
            pl.BlockSpec((nb, GLA_V), lambda i: (i, 0)),
        ],
        out_shape=[jax.ShapeDtypeStruct((n,) + sshape, F32), jax.ShapeDtypeStruct((n, GLA_V), F32)],
        compiler_params=_params("arbitrary"),
        name="gla_step",
    )(q, k, v, la, state)


def _mix_sample_out_kernel(x_ref, yc_ref, o_ref, gate_ref, gg_ref, wout_ref, gm_ref, wq_ref, x1_ref, q_ref):
    yg = _head_norm_gate(o_ref[...], gate_ref[...], gg_ref[...])
    y = _dot(jnp.concatenate([yc_ref[...].astype(BF16), yg.astype(BF16)], axis=1), wout_ref[...])
    x1 = x_ref[...] + y
    x1_ref[...] = x1
    q_ref[...] = _dot(_rms(x1, gm_ref[...]).astype(BF16), wq_ref[...])


def _mix_sample_out(x, yc, o, gate, gg, wout, gm, wq):
    args = (x, yc, o, gate, gg, wout, gm, wq)
    return pl.pallas_call(
        _mix_sample_out_kernel,
        grid=(1,),
        in_specs=[_full(a.shape) for a in args],
        out_specs=[_full(x.shape)] * 2,
        out_shape=[jax.ShapeDtypeStruct(x.shape, F32)] * 2,
        compiler_params=_params("arbitrary"),
        name="mix_sample_out",
    )(*args)


ATT_ROWS = 2 * MEM_HEADS


def _class_allreduce(x, op):
    n = x.shape[-1]
    shift = ATT_ROWS
    while shift < n:
        x = op(x, pltpu.roll(x, shift, axis=1))
        shift *= 2
    return x


def _att_sample_kernel(q_ref, k_ref, v_ref, o_ref):
    nb = q_ref.shape[0]
    ncol = k_ref.shape[2]
    diag = (lax.broadcasted_iota(jnp.int32, (ATT_ROWS, ncol), 0)
            == (lax.broadcasted_iota(jnp.int32, (ATT_ROWS, ncol), 1) & (ATT_ROWS - 1)))
    rows = lax.broadcasted_iota(jnp.int32, (nb, 1), 0)
    t = jnp.zeros((nb, ncol), F32)
    for n in range(nb):
        sc = _dot_nt(q_ref[n].astype(BF16), k_ref[0, n].astype(BF16))
        t = t + jnp.where(rows == n, jnp.sum(jnp.where(diag, sc, 0.0), axis=0, keepdims=True), 0.0)
    valid = (lax.broadcasted_iota(jnp.int32, (nb, ncol), 1) & (ATT_ROWS - 1)) < MEM_HEADS
    s = jnp.where(valid, (t + pltpu.roll(t, ncol - MEM_HEADS, axis=1)) * (MEM_DH ** -0.5), 0.0)
    e = jnp.where(valid, jnp.exp(s - _class_allreduce(s, jnp.maximum)), 0.0)
    den = jnp.where(valid, _class_allreduce(e, jnp.add), 1.0)
    p = e / den
    p = p + pltpu.roll(p, MEM_HEADS, axis=1)
    for n in range(nb):
        p_n = jnp.where(diag, jnp.broadcast_to(p[n:n + 1, :], (ATT_ROWS, ncol)), 0.0)
        o_ref[n] = _dot(p_n.astype(BF16), v_ref[0, n].astype(BF16))


def _att_sample(q, ck, cv, layer, nb):
    n = q.shape[0]
    ncol = ck.shape[2]
    return pl.pallas_call(
        _att_sample_kernel,
        grid=(n // nb,),
        in_specs=[
            pl.BlockSpec((nb, ATT_ROWS, LANES), lambda i: (i, 0, 0)),
            pl.BlockSpec((1, nb, ncol, LANES), lambda i: (layer, i, 0, 0)),
            pl.BlockSpec((1, nb, ncol, LANES), lambda i: (layer, i, 0, 0)),
        ],
        out_specs=pl.BlockSpec((nb, ATT_ROWS, LANES), lambda i: (i, 0, 0)),
        out_shape=jax.ShapeDtypeStruct((n, ATT_ROWS, LANES), F32),
        compiler_params=_params("arbitrary"),
        name="att_sample",
    )(q, ck, cv)


def _oproj_kernel(x_ref, o_ref, wo_ref, out_ref):
    out_ref[...] = x_ref[...] + _dot(o_ref[...].astype(BF16), wo_ref[...])


def _oproj(x, o, wo):
    return pl.pallas_call(
        _oproj_kernel,
        grid=(1,),
        in_specs=[_full(x.shape), _full(o.shape), _full(wo.shape)],
        out_specs=_full(x.shape),
        out_shape=jax.ShapeDtypeStruct(x.shape, F32),
        compiler_params=_params("arbitrary"),
        name="oproj",
    )(x, o, wo)


def kernel(x_prompt, x_sample, state_conv, state_gla, cache_mem_k, cache_mem_v, mem_prompt, norm_mix, w_in, conv_w, gla_gate_up, gla_gate_b, gla_out_norm, w_out, norm_mem, w_q, w_k, w_v, w_o, norm_ffn, router_group, router_group_b, router_expert, router_expert_b, w_gate, w_up, w_down, norm_final):
    depth = w_in.shape[0]
    bsz, t, d = x_prompt.shape
    ns = x_sample.shape[0]
    nm = mem_prompt.shape[1]
    n_tok = bsz * t

    tt = min(1024, t)
    tq = min(1024, t)
    tm_moe = min(1024, n_tok)
    tm_kv = min(512, bsz * nm)
    tm_sorted = 256
    rows_perm = (min(2048, n_tok), min(2048, n_tok))
    nb_gla = min(8, ns)
    nb_att = min(8, ns)

    row = lambda a: a.reshape(1, -1)
    mem2 = mem_prompt.reshape(bsz * nm, d)

    def tile_rows(c):
        c = c.reshape(depth, ns, nm, MEM_HEADS, 2, LANES).transpose(0, 1, 2, 4, 3, 5)
        return c.reshape(depth, ns, nm * ATT_ROWS, LANES)

    ck, cv = tile_rows(cache_mem_k), tile_rows(cache_mem_v)

    def untile_rows(c):
        c = c.reshape(depth, bsz, nm, 2, MEM_HEADS, LANES).transpose(0, 1, 2, 4, 3, 5)
        return c.reshape(depth, bsz, nm, MEM_HEADS, MEM_DH)

    win_all = w_in.astype(BF16)
    mk_all, mv_all = _mem_kv(mem2, w_k.astype(BF16), w_v.astype(BF16), tm_kv)

    xp = x_prompt
    xs = x_sample.reshape(ns, d)
    conv_p, gla_p, conv_s, gla_s = [], [], [], []
    for l in range(depth):
        wlr = jnp.pad(w_in[l, :, C_LR:], ((0, 0), (0, LANES - GLA_RANK))).astype(BF16)
        gup = jnp.pad(gla_gate_up[l], ((0, LANES - GLA_RANK), (0, 0))).astype(BF16)
        gb = row(gla_gate_b[l])
        gg = row(gla_out_norm[l])
        wout = w_out[l].astype(BF16)
        wq, wo = w_q[l].astype(BF16), w_o[l].astype(BF16)
        rw = jnp.concatenate([router_group[l], router_expert[l].transpose(1, 0, 2).reshape(d, N_EXPERTS)], axis=1)
        rw = jnp.pad(rw, ((0, 0), (0, LANES - rw.shape[1])))
        rb = jnp.concatenate([router_group_b[l], router_expert_b[l].reshape(-1)])
        rb = row(jnp.pad(rb, (0, LANES - rb.shape[0])))
        moe_w = (row(norm_ffn[l]), rw, rb, w_gate, w_up, w_down, l)

        xp, nbuf, ns_p = _mix_prompt(xp, (bsz, t, d), row(norm_mix[l]), win_all, l, wlr, gup, gb, conv_w[l], gg, wout, tt)
        conv_p.append(nbuf)
        gla_p.append(ns_p)
        xp, cls, rank, counts = _att_prompt(xp, row(norm_mem[l]), wq, wo, mk_all, mv_all, l,
                                            row(norm_ffn[l]), rw, rb, tq)
        xp = _moe_sparse(xp, cls, rank, counts, *moe_w, tm_sorted, rows_perm,
                         norm_g=row(norm_final) if l == depth - 1 else None)

        yc, u, q, k, v, gate, la = _mix_sample_in(
            xs, row(norm_mix[l]), win_all, l, wlr, gup, gb, conv_w[l], state_conv[l, :, 0], state_conv[l, :, 1])
        conv_s.append(jnp.stack([state_conv[l, :, 1], u], axis=1))
        s_new, o = _gla_step(q, k, v, la, state_gla, l, nb_gla)
        gla_s.append(s_new)
        wq_s = wq.reshape(d, MEM_HEADS, 2, LANES).transpose(0, 2, 1, 3).reshape(d, d)
        wo_s = wo.reshape(MEM_HEADS, 2, LANES, d).transpose(1, 0, 2, 3).reshape(d, d)
        xs, qa = _mix_sample_out(xs, yc, o, gate, gg, wout, row(norm_mem[l]), wq_s)
        oa = _att_sample(qa.reshape(ns, ATT_ROWS, LANES), ck, cv, l, nb_att)
        xs = _oproj(xs, oa.reshape(ns, d), wo_s)
        xs = _moe(xs, *moe_w, min(tm_moe, ns))

    y_prompt = xp.reshape(bsz, t, d)
    y_sample = _final_norm(xs, row(norm_final), ns).reshape(ns, 1, d)
    return (y_prompt, y_sample, jnp.stack(conv_p), jnp.stack(gla_p), untile_rows(mk_all), untile_rows(mv_all),
            jnp.stack(conv_s), jnp.stack(gla_s))
```

```python
import functools

import jax
import jax.numpy as jnp
from jax import lax
from jax.experimental import pallas as pl
from jax.experimental.pallas import tpu as pltpu

F32 = jnp.float32
BF16 = jnp.bfloat16
HIGHEST = lax.Precision.HIGHEST

EPS = 1e-6
CONV_W = 512
GLA_HEADS = 4
GLA_DK = 64
GLA_DV = 128
GLA_K = GLA_HEADS * GLA_DK
GLA_V = GLA_HEADS * GLA_DV
GLA_RANK = 16
GLA_TAU = 16.0
GLA_CHUNK = 64
MEM_HEADS = 4
MEM_DH = 256
N_GROUPS = 4
EXP_PER_GROUP = 4
N_EXPERTS = 16
LANES = 128
C_CB, C_CC, C_CH, C_Q, C_K, C_V, C_G, C_LR = 0, 512, 1024, 1536, 1792, 2048, 2560, 3072
VMEM_LIMIT = 52 * 1024 * 1024


def _params(*sem):
    return pltpu.CompilerParams(dimension_semantics=sem, vmem_limit_bytes=VMEM_LIMIT)


def _rms(x, g):
    return x * lax.rsqrt(jnp.mean(x * x, axis=-1, keepdims=True) + EPS) * g


def _dot(a, b, precision=None):
    return jnp.dot(a, b, precision=precision, preferred_element_type=F32)


def _dot_nt(a, b):
    return lax.dot_general(a, b, (((1,), (1,)), ((), ())), preferred_element_type=F32)


def _dot_tn(a, b, precision=None):
    return lax.dot_general(a, b, (((0,), (0,)), ((), ())), precision=precision,
                           preferred_element_type=F32)


def _dot_split(dot, mask, x):
    hi = x.astype(BF16)
    r1 = x - hi.astype(F32)
    mid = r1.astype(BF16)
    lo = (r1 - mid.astype(F32)).astype(BF16)
    return dot(mask, hi) + dot(mask, mid) + dot(mask, lo)


def _silu(x):
    return x / (1.0 + jnp.exp(-x))


def _log_sigmoid(x):
    return jnp.minimum(x, 0.0) - jnp.log1p(jnp.exp(-jnp.abs(x)))


def _head_norm_gate(o, g, gg):
    parts = []
    for h in range(GLA_HEADS):
        sl = slice(h * GLA_DV, (h + 1) * GLA_DV)
        parts.append(_rms(o[:, sl], gg[:, sl]))
    return jnp.concatenate(parts, axis=-1) * _silu(g)


def _full(shape):
    nd = len(shape)
    return pl.BlockSpec(shape, lambda *_: (0,) * nd)


def _mix_prompt_kernel(x_ref, g_ref, win_ref, wlr_ref, gup_ref, gb_ref, cw_ref, gg_ref, wout_ref,
                       x1_ref, conv_ref, s_ref, ubuf, s_scr, o_scr, *, tok_in):
    t = pl.program_id(1)
    tt = x1_ref.shape[1]

    @pl.when(t == 0)
    def _():
        ubuf[0:8, :] = jnp.zeros((8, CONV_W), F32)
        s_scr[...] = jnp.zeros(s_scr.shape, F32)

    x = _load_tok(x_ref, tt) if tok_in else x_ref[0]
    hb = _rms(x, g_ref[...]).astype(BF16)

    def proj(a, b):
        return _dot(hb, win_ref[0, :, a:b])

    u = proj(C_CC, C_CH) * proj(C_CH, C_Q)
    ubuf[8:8 + tt, :] = u
    cw = cw_ref[...]
    yc = proj(C_CB, C_CC) * (ubuf[6:6 + tt, :] * cw[0:1] + ubuf[7:7 + tt, :] * cw[1:2] + u * cw[2:3])
    ubuf[6:8, :] = u[tt - 2:tt, :]

    qs = proj(C_Q, C_K) * (GLA_DK ** -0.5)
    k = proj(C_K, C_V)
    v = proj(C_V, C_G)
    lr = _dot(hb, wlr_ref[...])
    gate = _dot(lr.astype(BF16), gup_ref[...]) + gb_ref[...]
    la = _log_sigmoid(gate) * (1.0 / GLA_TAU)

    c = GLA_CHUNK
    nc = tt // c
    iota = lambda shape, dim: lax.broadcasted_iota(jnp.int32, shape, dim)
    ltri = (iota((c, c), 0) >= iota((c, c), 1)).astype(BF16)
    b_wide = _dot_split(_dot, ltri, jnp.concatenate([la[j * c:(j + 1) * c] for j in range(nc)], axis=1))
    b_ends = jnp.concatenate([b_wide[c - 1:c, j * GLA_K:(j + 1) * GLA_K] for j in range(nc)], axis=0)
    spread = (iota((nc, LANES), 0) == iota((nc, LANES), 1)).astype(BF16)
    log_dec = _dot_split(lambda m, x: _dot_tn(x, m), spread, b_ends)
    head_feat = iota((GLA_K, GLA_K), 0) // c == iota((GLA_K, GLA_K), 1) // GLA_DK
    head_blk = iota((GLA_K, GLA_V), 0) // GLA_DK == iota((GLA_K, GLA_V), 1) // GLA_DV
    causal = iota((c, GLA_K), 0) >= iota((c, GLA_K), 1) % c
    for j in range(nc):
        r = slice(j * c, (j + 1) * c)
        b_c, q_c, k_c, v_c = b_wide[:, j * GLA_K:(j + 1) * GLA_K], qs[r], k[r], v[r]
        b_mid = b_c[c // 2:c // 2 + 1, :]
        b_last = b_c[c - 1:c, :]
        q_i = (q_c * jnp.exp(b_c - b_mid)).astype(BF16)
        k_i = k_c * jnp.exp(b_mid - b_c)
        k_dec = (k_c * jnp.exp(b_last - b_c)).astype(BF16)
        q_b = (q_c * jnp.exp(b_c)).astype(BF16)
        k_rows = jnp.where(head_feat, jnp.concatenate([k_i] * GLA_HEADS, axis=0), 0.0).astype(BF16)
        a = jnp.where(causal, _dot_nt(q_i, k_rows), 0.0).astype(BF16)
        v_blk = jnp.where(head_blk, jnp.concatenate([v_c] * GLA_HEADS, axis=0), 0.0).astype(BF16)
        s_prev = s_scr[...]
        o_scr[r, :] = _dot(jnp.concatenate([a, q_b], axis=1),
                           jnp.concatenate([v_blk, s_prev.astype(BF16)], axis=0))
        dec = jnp.exp(jnp.broadcast_to(log_dec[:, j:j + 1], (GLA_K, GLA_DV)))
        s_scr[...] = (jnp.concatenate([dec] * GLA_HEADS, axis=1) * s_prev
                      + jnp.where(head_blk, _dot_tn(k_dec, v_c.astype(BF16)), 0.0))

    yg = _head_norm_gate(o_scr[...], proj(C_G, C_LR), gg_ref[...])
    y = _dot(jnp.concatenate([yc.astype(BF16), yg.astype(BF16)], axis=1), wout_ref[...])
    x1_ref[0] = x + y

    @pl.when(t == pl.num_programs(1) - 1)
    def _():
        conv_ref[0] = u[tt - 2:tt, :]
        for h in range(GLA_HEADS):
            s_ref[0, h] = s_scr[h * GLA_DK:(h + 1) * GLA_DK, h * GLA_DV:(h + 1) * GLA_DV]


def _mix_prompt(x, shape, g, win, layer, wlr, gup, gb, cw, gg, wout, tt):
    bsz, t, d = shape
    tok_in = x.ndim == 2
    nt = t // tt
    x_spec = (pl.BlockSpec((tt * TOK_ROWS, LANES), lambda b, i: (b * nt + i, 0)) if tok_in
              else pl.BlockSpec((1, tt, d), lambda b, i: (b, i, 0)))
    return pl.pallas_call(
        functools.partial(_mix_prompt_kernel, tok_in=tok_in),
        grid=(bsz, nt),
        in_specs=[
            x_spec,
            _full(g.shape), pl.BlockSpec((1,) + win.shape[1:], lambda b, i: (layer, 0, 0)), _full(wlr.shape),
            _full(gup.shape), _full(gb.shape), _full(cw.shape), _full(gg.shape), _full(wout.shape),
        ],
        out_specs=[
            pl.BlockSpec((1, tt, d), lambda b, i: (b, i, 0)),
            pl.BlockSpec((1, 2, CONV_W), lambda b, i: (b, 0, 0)),
            pl.BlockSpec((1, GLA_HEADS, GLA_DK, GLA_DV), lambda b, i: (b, 0, 0, 0)),
        ],
        out_shape=[
            jax.ShapeDtypeStruct((bsz, t, d), F32),
            jax.ShapeDtypeStruct((bsz, 2, CONV_W), F32),
            jax.ShapeDtypeStruct((bsz, GLA_HEADS, GLA_DK, GLA_DV), F32),
        ],
        scratch_shapes=[
            pltpu.VMEM((8 + tt, CONV_W), F32),
            pltpu.VMEM((GLA_K, GLA_V), F32),
            pltpu.VMEM((tt, GLA_V), F32),
        ],
        compiler_params=_params("arbitrary", "arbitrary"),
        name="mix_prompt",
    )(x, g, win, wlr, gup, gb, cw, gg, wout)


def _kv_kernel(m_ref, wk_ref, wv_ref, k_ref, v_ref):
    tm = m_ref.shape[0]
    mb = m_ref[...].astype(BF16)
    for w_ref, o_ref in ((wk_ref, k_ref), (wv_ref, v_ref)):
        y = _dot(mb, w_ref[0])
        for h in range(MEM_HEADS):
            for dt in range(MEM_DH // LANES):
                col = h * MEM_DH + dt * LANES
                o_ref[0, pl.ds(dt * MEM_HEADS + h, tm, stride=ATT_ROWS), :] = y[:, col:col + LANES]


def _mem_kv(mem, wk, wv, tm):
    n, d = mem.shape
    depth = wk.shape[0]
    w_spec = pl.BlockSpec((1, d, d), lambda l, i: (l, 0, 0))
    o_spec = pl.BlockSpec((1, tm * ATT_ROWS, LANES), lambda l, i: (l, i, 0))
    return pl.pallas_call(
        _kv_kernel,
        grid=(depth, n // tm),
        in_specs=[pl.BlockSpec((tm, d), lambda l, i: (i, 0)), w_spec, w_spec],
        out_specs=[o_spec] * 2,
        out_shape=[jax.ShapeDtypeStruct((depth, n * ATT_ROWS, LANES), F32)] * 2,
        compiler_params=_params("arbitrary", "arbitrary"),
        name="mem_kv",
    )(mem, wk, wv)


def _att_prompt_kernel(x_ref, g_ref, wq_ref, wo_ref, mk_ref, mv_ref, gf_ref, rw_ref, rb_ref, ltri_ref,
                       out_ref, cls_ref, rank_ref, cnt_ref, carry):
    @pl.when((pl.program_id(0) == 0) & (pl.program_id(1) == 0))
    def _():
        carry[...] = jnp.zeros(carry.shape, F32)

    x = x_ref[0]
    xb = _rms(x, g_ref[...]).astype(BF16)
    q = _dot(xb, wq_ref[...])
    nm = mk_ref.shape[1] // ATT_ROWS

    def head_rows(ref, h):
        return jnp.concatenate([ref[0, pl.ds(dt * MEM_HEADS + h, nm, stride=ATT_ROWS), :]
                                for dt in range(MEM_DH // LANES)], axis=-1).astype(BF16)

    outs = []
    for h in range(MEM_HEADS):
        sl = slice(h * MEM_DH, (h + 1) * MEM_DH)
        s = _dot_nt(q[:, sl].astype(BF16), head_rows(mk_ref, h)) * (MEM_DH ** -0.5)
        e = jnp.exp(s - jnp.max(s, axis=-1, keepdims=True))
        p = e / jnp.sum(e, axis=-1, keepdims=True)
        outs.append(_dot(p.astype(BF16), head_rows(mv_ref, h)))
    o = jnp.concatenate(outs, axis=-1)
    x2 = x + _dot(o.astype(BF16), wo_ref[...])
    _store_tok(out_ref, x2)
    _route_meta(x2, gf_ref, rw_ref, rb_ref, ltri_ref, carry, cls_ref, rank_ref, cnt_ref)


def _att_prompt(x, g, wq, wo, mk, mv, layer, gf, rw, rb, tq):
    bsz, t, d = x.shape
    assert d == TOK_ROWS * LANES
    nq = t // tq
    rows = mk.shape[1] // bsz
    kv_spec = pl.BlockSpec((1, rows, LANES), lambda b, i: (layer, b, 0))
    meta_spec = pl.BlockSpec((tq // LANES, LANES), lambda b, i: (b * nq + i, 0))
    rw_hi = rw.astype(BF16)
    rw_split = jnp.concatenate([rw_hi, (rw - rw_hi.astype(F32)).astype(BF16)], axis=1)
    ltri = jnp.tril(jnp.ones((LANES, LANES), BF16))
    n_tok = bsz * t
    return pl.pallas_call(
        _att_prompt_kernel,
        grid=(bsz, nq),
        in_specs=[
            pl.BlockSpec((1, tq, d), lambda b, i: (b, i, 0)),
            _full(g.shape), _full(wq.shape), _full(wo.shape),
            kv_spec, kv_spec,
            _full(gf.shape), _full(rw_split.shape), _full(rb.shape), _full(ltri.shape),
        ],
        out_specs=[
            pl.BlockSpec((tq * TOK_ROWS, LANES), lambda b, i: (b * nq + i, 0)),
            meta_spec, meta_spec, _full((1, LANES)),
        ],
        out_shape=[
            jax.ShapeDtypeStruct((n_tok * TOK_ROWS, LANES), F32),
            jax.ShapeDtypeStruct((n_tok // LANES, LANES), jnp.int32),
            jax.ShapeDtypeStruct((n_tok // LANES, LANES), jnp.int32),
            jax.ShapeDtypeStruct((1, LANES), jnp.int32),
        ],
        scratch_shapes=[pltpu.VMEM((1, LANES), F32)],
        compiler_params=_params("arbitrary", "arbitrary"),
        name="att_prompt",
    )(x, g, wq, wo, mk, mv, gf, rw_split, rb, ltri)


def _route_top2(logits):
    lane = lax.broadcasted_iota(jnp.int32, logits.shape, 1)
    lanef = lane.astype(F32)
    ninf = -jnp.inf
    big = 1e9
    gl = jnp.where(lane < N_GROUPS, logits, ninf)
    gmax = jnp.max(gl, axis=-1, keepdims=True)
    g_idx = jnp.min(jnp.where(gl == gmax, lanef, big), axis=-1, keepdims=True)
    g_w = 1.0 / jnp.sum(jnp.exp(gl - gmax), axis=-1, keepdims=True)
    grp = ((lane - N_GROUPS) >> 2).astype(F32)
    emask = (lane >= N_GROUPS) & (lane < N_GROUPS + N_EXPERTS) & (grp == g_idx)
    el = jnp.where(emask, logits, ninf)
    m1 = jnp.max(el, axis=-1, keepdims=True)
    i1 = jnp.min(jnp.where(el == m1, lanef, big), axis=-1, keepdims=True)
    el2 = jnp.where(lanef == i1, ninf, el)
    m2 = jnp.max(el2, axis=-1, keepdims=True)
    i2 = jnp.min(jnp.where(el2 == m2, lanef, big), axis=-1, keepdims=True)
    tail = jnp.exp(m2 - m1)
    w1 = g_w / (1.0 + tail)
    w2 = g_w * tail / (1.0 + tail)
    return g_idx, i1, i2, w1, w2


def _route(logits):
    _, i1, i2, w1, w2 = _route_top2(logits)
    lanef = lax.broadcasted_iota(jnp.int32, logits.shape, 1).astype(F32)
    return jnp.where(lanef == i1, w1, 0.0) + jnp.where(lanef == i2, w2, 0.0)


N_PAIRS = 6
N_CLASSES = N_GROUPS * N_PAIRS
TOK_ROWS = 8


def _load_tok(ref, n):
    return jnp.concatenate([ref[pl.ds(j, n, stride=TOK_ROWS), :] for j in range(TOK_ROWS)], axis=-1)


def _store_tok(ref, val):
    n = val.shape[0]
    for j in range(TOK_ROWS):
        ref[pl.ds(j, n, stride=TOK_ROWS), :] = val[:, j * LANES:(j + 1) * LANES]


def _lane_dense(col):
    eye = lax.broadcasted_iota(jnp.int32, (LANES, LANES), 0) == lax.broadcasted_iota(jnp.int32, (LANES, LANES), 1)
    rows = [jnp.sum(jnp.where(eye, col[b * LANES:(b + 1) * LANES], 0.0), axis=0, keepdims=True)
            for b in range(col.shape[0] // LANES)]
    return jnp.concatenate(rows, axis=0)


def _route_meta(x, g_ref, rw_ref, rb_ref, ltri_ref, carry, cls_ref, rank_ref, cnt_ref):
    n = x.shape[0]
    xn = _rms(x, g_ref[...])
    x_hi = xn.astype(BF16)
    x_lo = (xn - x_hi.astype(F32)).astype(BF16)
    both = _dot(x_hi, rw_ref[...])
    logits = (both[:, :LANES] + (_dot(x_lo, rw_ref[:, :LANES]) + both[:, LANES:])) + rb_ref[...]
    g_idx, i1, i2, _, _ = _route_top2(logits)
    lo = jnp.minimum(i1, i2) - N_GROUPS - EXP_PER_GROUP * g_idx
    hi = jnp.maximum(i1, i2) - N_GROUPS - EXP_PER_GROUP * g_idx
    cls = g_idx * N_PAIRS + lo * (2 * EXP_PER_GROUP - 1.0 - lo) * 0.5 + hi - lo - 1.0

    lane = lax.broadcasted_iota(jnp.int32, (n, LANES), 1)
    onehot = lane.astype(F32) == cls
    onehot_b = onehot.astype(BF16)
    blocks = []
    run = carry[...]
    for b in range(n // LANES):
        p_b = _dot(ltri_ref[...], onehot_b[b * LANES:(b + 1) * LANES]) + run
        blocks.append(p_b)
        run = p_b[LANES - 1:LANES, :]
    prefix = jnp.concatenate(blocks, axis=0)
    rank = jnp.sum(jnp.where(onehot, prefix, 0.0), axis=-1, keepdims=True) - 1.0
    carry[...] = prefix[n - 1:n, :]
    cls_ref[...] = _lane_dense(cls).astype(jnp.int32)
    rank_ref[...] = _lane_dense(rank).astype(jnp.int32)
    cnt_ref[...] = prefix[n - 1:n, :].astype(jnp.int32)


DMA_UNROLL = 8


def _tok_rows(t):
    return pl.ds(pl.multiple_of(t * TOK_ROWS, TOK_ROWS), TOK_ROWS)


def _tok_scatter_kernel(pos, tail, x_ref, dst_ref, stage, zeros, sems, *, toks, steps):
    i = pl.program_id(0)
    slot = i % 2
    base = i * toks
    rows = toks * TOK_ROWS
    tile_rows = zeros.shape[0]

    def wait_slot(s):
        pltpu.make_async_copy(stage.at[s], dst_ref.at[pl.ds(0, rows)], sems.at[s]).wait()

    @pl.when(i == 0)
    def _():
        zeros[...] = jnp.zeros(zeros.shape, F32)

        def fill(c):
            return pltpu.make_async_copy(
                zeros, dst_ref.at[pl.ds(pl.multiple_of(tail[c] * tile_rows, tile_rows), tile_rows)], sems.at[2])

        for c in range(2 * N_CLASSES):
            @pl.when(tail[c] >= 0)
            def _():
                fill(c).start()
        for c in range(2 * N_CLASSES):
            @pl.when(tail[c] >= 0)
            def _():
                fill(c).wait()

    @pl.when(i >= 2)
    def _():
        wait_slot(slot)

    stage[slot] = x_ref[...]

    def start(r8, c):
        for u in range(DMA_UNROLL):
            r = r8 * DMA_UNROLL + u
            pltpu.make_async_copy(stage.at[slot, _tok_rows(r)], dst_ref.at[_tok_rows(pos[base + r])],
                                  sems.at[slot]).start(priority=u % 2)
        return c

    lax.fori_loop(0, toks // DMA_UNROLL, start, 0)

    @pl.when(i == steps - 1)
    def _():
        wait_slot(slot)
        if steps >= 2:
            wait_slot(1 - slot)


def _tok_scatter(x, n_out, pos, tail, tile, toks):
    steps = x.shape[0] // (toks * TOK_ROWS)
    return pl.pallas_call(
        functools.partial(_tok_scatter_kernel, toks=toks, steps=steps),
        grid_spec=pltpu.PrefetchScalarGridSpec(
            num_scalar_prefetch=2,
            grid=(steps,),
            in_specs=[pl.BlockSpec((toks * TOK_ROWS, LANES), lambda i, *_: (i, 0))],
            out_specs=pl.BlockSpec(memory_space=pl.ANY),
            scratch_shapes=[pltpu.VMEM((2, toks * TOK_ROWS, LANES), F32), pltpu.VMEM((tile * TOK_ROWS, LANES), F32),
                            pltpu.SemaphoreType.DMA((3,))],
        ),
        out_shape=jax.ShapeDtypeStruct((n_out * TOK_ROWS, LANES), F32),
        compiler_params=_params("arbitrary"),
        name="tok_scatter",
    )(pos, tail, x)


def _tok_gather_kernel(pos, src_ref, *rest, toks, norm):
    if norm:
        g_ref, out_ref, stage, sem = rest
        dst = stage
    else:
        out_ref, sem = rest
        dst = out_ref
    base = pl.program_id(0) * toks

    def start(r8, c):
        for u in range(DMA_UNROLL):
            r = r8 * DMA_UNROLL + u
            pltpu.make_async_copy(src_ref.at[_tok_rows(pos[base + r])], dst.at[_tok_rows(r)],
                                  sem).start(priority=u % 2)
        return c

    lax.fori_loop(0, toks // DMA_UNROLL, start, 0)
    pltpu.make_async_copy(src_ref.at[pl.ds(0, toks * TOK_ROWS)], dst, sem).wait()
    if norm:
        out_ref[...] = _rms(_load_tok(stage, toks), g_ref[...])


def _tok_gather(src, pos, toks, norm_g=None):
    n = pos.shape[0]
    norm = norm_g is not None
    d = TOK_ROWS * LANES
    return pl.pallas_call(
        functools.partial(_tok_gather_kernel, toks=toks, norm=norm),
        grid_spec=pltpu.PrefetchScalarGridSpec(
            num_scalar_prefetch=1,
            grid=(n // toks,),
            in_specs=[pl.BlockSpec(memory_space=pl.ANY)]
            + ([pl.BlockSpec(norm_g.shape, lambda i, *_: (0, 0))] if norm else []),
            out_specs=(pl.BlockSpec((toks, d), lambda i, *_: (i, 0)) if norm
                       else pl.BlockSpec((toks * TOK_ROWS, LANES), lambda i, *_: (i, 0))),
            scratch_shapes=([pltpu.VMEM((toks * TOK_ROWS, LANES), F32)] if norm else [])
            + [pltpu.SemaphoreType.DMA(())],
        ),
        out_shape=jax.ShapeDtypeStruct((n, d) if norm else (n * TOK_ROWS, LANES), F32),
        compiler_params=_params("arbitrary"),
        name="tok_gather",
    )(pos, src, *([norm_g] if norm else []))


def _moe_sorted_kernel(e_lo, e_hi, valid, xs_ref, g_ref, rw_ref, rb_ref,
                       wg_lo, wu_lo, wd_lo, wg_hi, wu_hi, wd_hi, out_ref, wgu_scr, wd_scr):
    t = pl.program_id(0)
    tm = xs_ref.shape[0] // TOK_ROWS
    ff = wd_lo.shape[2]
    prev = jnp.maximum(t - 1, 0)

    for slot, ids, wg, wu, wd in ((0, e_lo, wg_lo, wu_lo, wd_lo), (1, e_hi, wg_hi, wu_hi, wd_hi)):
        @pl.when((t == 0) | (ids[t] != ids[prev]))
        def _():
            wgu_scr[:, (2 * slot) * ff:(2 * slot + 1) * ff] = wg[0, 0].astype(BF16)
            wgu_scr[:, (2 * slot + 1) * ff:(2 * slot + 2) * ff] = wu[0, 0].astype(BF16)
            wd_scr[slot * ff:(slot + 1) * ff, :] = wd[0, 0].astype(BF16)

    @pl.when(valid[t] > 0)
    def _():
        x = _load_tok(xs_ref, tm)
        xb = _rms(x, g_ref[...]).astype(BF16)
        logits = _dot(xb, rw_ref[...]) + rb_ref[...]
        lane = lax.broadcasted_iota(jnp.int32, logits.shape, 1)
        gl = jnp.where(lane < N_GROUPS, logits, -jnp.inf)
        g_w = 1.0 / jnp.sum(jnp.exp(gl - jnp.max(gl, axis=-1, keepdims=True)), axis=-1, keepdims=True)
        l_lo = jnp.sum(jnp.where(lane == e_lo[t] + N_GROUPS, logits, 0.0), axis=-1, keepdims=True)
        l_hi = jnp.sum(jnp.where(lane == e_hi[t] + N_GROUPS, logits, 0.0), axis=-1, keepdims=True)
        tail = jnp.exp(-jnp.abs(l_lo - l_hi))
        w_top = g_w / (1.0 + tail)
        w_oth = g_w * tail / (1.0 + tail)
        lo_top = l_lo >= l_hi
        gu = _dot(xb, wgu_scr[...])
        hid = [_silu(gu[:, 2 * s * ff:(2 * s + 1) * ff]) * gu[:, (2 * s + 1) * ff:(2 * s + 2) * ff] * w
               for s, w in ((0, jnp.where(lo_top, w_top, w_oth)), (1, jnp.where(lo_top, w_oth, w_top)))]
        _store_tok(out_ref, x + _dot(jnp.concatenate(hid, axis=1).astype(BF16), wd_scr[...]))

    @pl.when(valid[t] == 0)
    def _():
        out_ref[...] = jnp.zeros(out_ref.shape, F32)


def _moe_sorted(xs, g, rw, rb, wg, wu, wd, layer, e_lo, e_hi, valid, tm):
    npad = xs.shape[0] // TOK_ROWS
    _, _, d, ff = wg.shape
    const = lambda t, lo, hi, v: (0, 0)
    at_lo = lambda t, lo, hi, v: (layer, lo[t], 0, 0)
    at_hi = lambda t, lo, hi, v: (layer, hi[t], 0, 0)
    return pl.pallas_call(
        _moe_sorted_kernel,
        grid_spec=pltpu.PrefetchScalarGridSpec(
            num_scalar_prefetch=3,
            grid=(npad // tm,),
            in_specs=[
                pl.BlockSpec((tm * TOK_ROWS, LANES), lambda t, lo, hi, v: (jnp.where(v[t] > 0, t, 0), 0)),
                pl.BlockSpec(g.shape, const), pl.BlockSpec(rw.shape, const), pl.BlockSpec(rb.shape, const),
                pl.BlockSpec((1, 1, d, ff), at_lo), pl.BlockSpec((1, 1, d, ff), at_lo),
                pl.BlockSpec((1, 1, ff, d), at_lo),
                pl.BlockSpec((1, 1, d, ff), at_hi), pl.BlockSpec((1, 1, d, ff), at_hi),
                pl.BlockSpec((1, 1, ff, d), at_hi),
            ],
            out_specs=pl.BlockSpec((tm * TOK_ROWS, LANES), lambda t, lo, hi, v: (t, 0)),
            scratch_shapes=[pltpu.VMEM((d, 4 * ff), BF16), pltpu.VMEM((2 * ff, d), BF16)],
        ),
        out_shape=jax.ShapeDtypeStruct(xs.shape, F32),
        compiler_params=_params("arbitrary"),
        name="moe_sorted",
    )(e_lo, e_hi, valid, xs, g, rw, rb, wg, wu, wd, wg, wu, wd)


def _moe_sparse(x, cls, rank, counts, g, rw, rb, wg, wu, wd, layer, tm, toks, norm_g=None):
    n = x.shape[0] // TOK_ROWS
    cls, rank = cls.reshape(n), rank.reshape(n)
    cnt = counts[0, :N_CLASSES]
    padded = ((cnt + tm - 1) // tm) * tm
    ends = jnp.cumsum(padded)
    pos = ((ends - padded)[cls] + rank).astype(jnp.int32)
    n_tiles = n // tm + N_CLASSES
    tile_start = jnp.arange(n_tiles, dtype=jnp.int32) * tm
    n_valid = ends[-1] // tm
    unused = n_valid + jnp.arange(N_CLASSES, dtype=jnp.int32)
    tail = jnp.concatenate([jnp.where(cnt > 0, ends // tm - 1, -1),
                            jnp.where(unused < n_tiles, unused, -1)]).astype(jnp.int32)
    last_cls = jnp.sum((tile_start[jnp.maximum(n_valid - 1, 0)] >= ends).astype(jnp.int32))
    tile_cls = jnp.sum((tile_start[:, None] >= ends[None, :]).astype(jnp.int32), axis=1)
    valid = (tile_start < ends[-1]).astype(jnp.int32)
    tile_cls = jnp.where(valid > 0, tile_cls, last_cls)
    pair_lo = jnp.array([0, 0, 0, 1, 1, 2], jnp.int32)
    pair_hi = jnp.array([1, 2, 3, 2, 3, 3], jnp.int32)
    e_lo = (tile_cls // N_PAIRS) * EXP_PER_GROUP + pair_lo[tile_cls % N_PAIRS]
    e_hi = (tile_cls // N_PAIRS) * EXP_PER_GROUP + pair_hi[tile_cls % N_PAIRS]

    xs = _tok_scatter(x, n_tiles * tm, pos, tail, tm, toks[0])
    ys = _moe_sorted(xs, g, rw.astype(BF16), rb, wg, wu, wd, layer, e_lo, e_hi, valid, tm)
    return _tok_gather(ys, pos, toks[1], norm_g)


def _moe_kernel(x_ref, g_ref, rw_ref, rb_ref, wg_ref, wu_ref, wd_ref, out_ref, xn_scr, comb_scr, acc_scr):
    e = pl.program_id(1)

    @pl.when(e == 0)
    def _():
        x = x_ref[...]
        xn = _rms(x, g_ref[...])
        xn_scr[...] = xn.astype(BF16)
        comb_scr[...] = _route(_dot(xn, rw_ref[...], HIGHEST) + rb_ref[...])
        acc_scr[...] = x

    xb = xn_scr[...]
    lane = lax.broadcasted_iota(jnp.int32, comb_scr.shape, 1)
    cmb = jnp.sum(jnp.where(lane == e + N_GROUPS, comb_scr[...], 0.0), axis=-1, keepdims=True)
    hid = _silu(_dot(xb, wg_ref[0, 0].astype(BF16))) * _dot(xb, wu_ref[0, 0].astype(BF16)) * cmb
    acc_scr[...] += _dot(hid.astype(BF16), wd_ref[0, 0].astype(BF16))

    @pl.when(e == pl.num_programs(1) - 1)
    def _():
        out_ref[...] = acc_scr[...]


def _moe(x, g, rw, rb, wg, wu, wd, layer, tm):
    n, d = x.shape
    _, ne, _, ff = wg.shape
    return pl.pallas_call(
        _moe_kernel,
        grid=(n // tm, ne),
        in_specs=[
            pl.BlockSpec((tm, d), lambda i, e: (i, 0)),
            _full(g.shape), _full(rw.shape), _full(rb.shape),
            pl.BlockSpec((1, 1, d, ff), lambda i, e: (layer, e, 0, 0)),
            pl.BlockSpec((1, 1, d, ff), lambda i, e: (layer, e, 0, 0)),
            pl.BlockSpec((1, 1, ff, d), lambda i, e: (layer, e, 0, 0)),
        ],
        out_specs=pl.BlockSpec((tm, d), lambda i, e: (i, 0)),
        out_shape=jax.ShapeDtypeStruct((n, d), F32),
        scratch_shapes=[pltpu.VMEM((tm, d), BF16), pltpu.VMEM((tm, LANES), F32), pltpu.VMEM((tm, d), F32)],
        compiler_params=_params("arbitrary", "arbitrary"),
        name="moe",
    )(x, g, rw, rb, wg, wu, wd)


def _norm_kernel(x_ref, g_ref, o_ref):
    o_ref[...] = _rms(x_ref[...], g_ref[...])


def _final_norm(x, g, tm):
    n, d = x.shape
    return pl.pallas_call(
        _norm_kernel,
        grid=(n // tm,),
        in_specs=[pl.BlockSpec((tm, d), lambda i: (i, 0)), _full(g.shape)],
        out_specs=pl.BlockSpec((tm, d), lambda i: (i, 0)),
        out_shape=jax.ShapeDtypeStruct((n, d), F32),
        compiler_params=_params("arbitrary"),
        name="final_norm",
    )(x, g)


def _mix_sample_in_kernel(x_ref, g_ref, win_ref, wlr_ref, gup_ref, gb_ref, cw_ref, b0_ref, b1_ref,
                          yc_ref, u_ref, q_ref, k_ref, v_ref, gate_ref, la_ref):
    hb = _rms(x_ref[...], g_ref[...]).astype(BF16)

    def proj(a, b):
        return _dot(hb, win_ref[0, :, a:b])

    u = proj(C_CC, C_CH) * proj(C_CH, C_Q)
    cw = cw_ref[...]
    yc_ref[...] = proj(C_CB, C_CC) * (b0_ref[...] * cw[0:1] + b1_ref[...] * cw[1:2] + u * cw[2:3])
    u_ref[...] = u
    q_ref[...] = proj(C_Q, C_K) * (GLA_DK ** -0.5)
    k_ref[...] = proj(C_K, C_V)
    v_ref[...] = proj(C_V, C_G)
    gate_ref[...] = proj(C_G, C_LR)
    lr = _dot(hb, wlr_ref[...])
    gate = _dot(lr.astype(BF16), gup_ref[...]) + gb_ref[...]
    la_ref[...] = _log_sigmoid(gate) * (1.0 / GLA_TAU)


def _mix_sample_in(x, g, win, layer, wlr, gup, gb, cw, b0, b1):
    n = x.shape[0]
    args = (x, g, win, wlr, gup, gb, cw, b0, b1)
    specs = [_full(a.shape) for a in args]
    specs[2] = pl.BlockSpec((1,) + win.shape[1:], lambda i: (layer, 0, 0))
    widths = (CONV_W, CONV_W, GLA_K, GLA_K, GLA_V, GLA_V, GLA_K)
    return pl.pallas_call(
        _mix_sample_in_kernel,
        grid=(1,),
        in_specs=specs,
        out_specs=[_full((n, w)) for w in widths],
        out_shape=[jax.ShapeDtypeStruct((n, w), F32) for w in widths],
        compiler_params=_params("arbitrary"),
        name="mix_sample_in",
    )(*args)


def _gla_step_kernel(q_ref, k_ref, v_ref, la_ref, s0_ref, s_ref, o_ref):
    nb = q_ref.shape[0]
    a = jnp.exp(la_ref[...])
    kb = k_ref[...].astype(BF16)
    qb = q_ref[...].astype(BF16)
    vf = v_ref[...].astype(BF16).astype(F32)
    rows = lax.broadcasted_iota(jnp.int32, (nb, 1), 0)
    spread = (lax.broadcasted_iota(jnp.int32, (nb, nb * GLA_DV), 0)
              == lax.broadcasted_iota(jnp.int32, (nb, nb * GLA_DV), 1) // GLA_DV)
    spread_b = spread.astype(BF16)
    for h in range(GLA_HEADS):
        ks = slice(h * GLA_DK, (h + 1) * GLA_DK)
        a_cols = _dot_tn(a[:, ks], spread.astype(F32), HIGHEST)
        k_cols = _dot_tn(kb[:, ks], spread_b)
        q_cols = _dot_tn(qb[:, ks], spread_b)
        o_h = jnp.zeros((nb, GLA_DV), F32)
        for n in range(nb):
            blk = slice(n * GLA_DV, (n + 1) * GLA_DV)
            v_row = vf[n:n + 1, h * GLA_DV:(h + 1) * GLA_DV]
            s_new = a_cols[:, blk] * s0_ref[0, n, h] + k_cols[:, blk] * v_row
            s_ref[n, h] = s_new
            o_h = jnp.where(rows == n, jnp.sum(q_cols[:, blk] * s_new, axis=0, keepdims=True), o_h)
        o_ref[:, h * GLA_DV:(h + 1) * GLA_DV] = o_h


def _gla_step(q, k, v, la, state, layer, nb):
    n = q.shape[0]
    sshape = (GLA_HEADS, GLA_DK, GLA_DV)
    return pl.pallas_call(
        _gla_step_kernel,
        grid=(n // nb,),
        in_specs=[
            pl.BlockSpec((nb, GLA_K), lambda i: (i, 0)),
            pl.BlockSpec((nb, GLA_K), lambda i: (i, 0)),
            pl.BlockSpec((nb, GLA_V), lambda i: (i, 0)),
            pl.BlockSpec((nb, GLA_K), lambda i: (i, 0)),
            pl.BlockSpec((1, nb) + sshape, lambda i: (layer, i, 0, 0, 0)),
        ],
        out_specs=[
            pl.BlockSpec((nb,) + sshape, lambda i: (i, 0, 0, 0)),
            pl.BlockSpec((nb, GLA_V), lambda i: (i, 0)),
        ],
        out_shape=[jax.ShapeDtypeStruct((n,) + sshape, F32), jax.ShapeDtypeStruct((n, GLA_V), F32)],
        compiler_params=_params("arbitrary"),
        name="gla_step",
    )(q, k, v, la, state)


def _mix_sample_out_kernel(x_ref, yc_ref, o_ref, gate_ref, gg_ref, wout_ref, gm_ref, wq_ref, x1_ref, q_ref):
    yg = _head_norm_gate(o_ref[...], gate_ref[...], gg_ref[...])
    y = _dot(jnp.concatenate([yc_ref[...].astype(BF16), yg.astype(BF16)], axis=1), wout_ref[...])
    x1 = x_ref[...] + y
    x1_ref[...] = x1
    q_ref[...] = _dot(_rms(x1, gm_ref[...]).astype(BF16), wq_ref[...])


def _mix_sample_out(x, yc, o, gate, gg, wout, gm, wq):
    args = (x, yc, o, gate, gg, wout, gm, wq)
    return pl.pallas_call(
        _mix_sample_out_kernel,
        grid=(1,),
        in_specs=[_full(a.shape) for a in args],
        out_specs=[_full(x.shape)] * 2,
        out_shape=[jax.ShapeDtypeStruct(x.shape, F32)] * 2,
        compiler_params=_params("arbitrary"),
        name="mix_sample_out",
    )(*args)


ATT_ROWS = 2 * MEM_HEADS


def _class_allreduce(x, op):
    n = x.shape[-1]
    shift = ATT_ROWS
    while shift < n:
        x = op(x, pltpu.roll(x, shift, axis=1))
        shift *= 2
    return x


def _att_sample_kernel(q_ref, k_ref, v_ref, o_ref):
    nb = q_ref.shape[0]
    ncol = k_ref.shape[2]
    diag = (lax.broadcasted_iota(jnp.int32, (ATT_ROWS, ncol), 0)
            == (lax.broadcasted_iota(jnp.int32, (ATT_ROWS, ncol), 1) & (ATT_ROWS - 1)))
    rows = lax.broadcasted_iota(jnp.int32, (nb, 1), 0)
    t = jnp.zeros((nb, ncol), F32)
    for n in range(nb):
        sc = _dot_nt(q_ref[n].astype(BF16), k_ref[0, n].astype(BF16))
        t = t + jnp.where(rows == n, jnp.sum(jnp.where(diag, sc, 0.0), axis=0, keepdims=True), 0.0)
    valid = (lax.broadcasted_iota(jnp.int32, (nb, ncol), 1) & (ATT_ROWS - 1)) < MEM_HEADS
    s = jnp.where(valid, (t + pltpu.roll(t, ncol - MEM_HEADS, axis=1)) * (MEM_DH ** -0.5), 0.0)
    e = jnp.where(valid, jnp.exp(s - _class_allreduce(s, jnp.maximum)), 0.0)
    den = jnp.where(valid, _class_allreduce(e, jnp.add), 1.0)
    p = e / den
    p = p + pltpu.roll(p, MEM_HEADS, axis=1)
    for n in range(nb):
        p_n = jnp.where(diag, jnp.broadcast_to(p[n:n + 1, :], (ATT_ROWS, ncol)), 0.0)
        o_ref[n] = _dot(p_n.astype(BF16), v_ref[0, n].astype(BF16))


def _att_sample(q, ck, cv, layer, nb):
    n = q.shape[0]
    ncol = ck.shape[2]
    return pl.pallas_call(
        _att_sample_kernel,
        grid=(n // nb,),
        in_specs=[
            pl.BlockSpec((nb, ATT_ROWS, LANES), lambda i: (i, 0, 0)),
            pl.BlockSpec((1, nb, ncol, LANES), lambda i: (layer, i, 0, 0)),
            pl.BlockSpec((1, nb, ncol, LANES), lambda i: (layer, i, 0, 0)),
        ],
        out_specs=pl.BlockSpec((nb, ATT_ROWS, LANES), lambda i: (i, 0, 0)),
        out_shape=jax.ShapeDtypeStruct((n, ATT_ROWS, LANES), F32),
        compiler_params=_params("arbitrary"),
        name="att_sample",
    )(q, ck, cv)


def _oproj_kernel(x_ref, o_ref, wo_ref, out_ref):
    out_ref[...] = x_ref[...] + _dot(o_ref[...].astype(BF16), wo_ref[...])


def _oproj(x, o, wo):
    return pl.pallas_call(
        _oproj_kernel,
        grid=(1,),
        in_specs=[_full(x.shape), _full(o.shape), _full(wo.shape)],
        out_specs=_full(x.shape),
        out_shape=jax.ShapeDtypeStruct(x.shape, F32),
        compiler_params=_params("arbitrary"),
        name="oproj",
    )(x, o, wo)


def kernel(x_prompt, x_sample, state_conv, state_gla, cache_mem_k, cache_mem_v, mem_prompt, norm_mix, w_in, conv_w, gla_gate_up, gla_gate_b, gla_out_norm, w_out, norm_mem, w_q, w_k, w_v, w_o, norm_ffn, router_group, router_group_b, router_expert, router_expert_b, w_gate, w_up, w_down, norm_final):
    depth = w_in.shape[0]
    bsz, t, d = x_prompt.shape
    ns = x_sample.shape[0]
    nm = mem_prompt.shape[1]
    n_tok = bsz * t

    tt = min(1024, t)
    tq = min(1024, t)
    tm_moe = min(1024, n_tok)
    tm_kv = min(512, bsz * nm)
    tm_sorted = 256
    rows_perm = (min(2048, n_tok), min(2048, n_tok))
    nb_gla = min(8, ns)
    nb_att = min(8, ns)

    row = lambda a: a.reshape(1, -1)
    mem2 = mem_prompt.reshape(bsz * nm, d)

    def tile_rows(c):
        c = c.reshape(depth, ns, nm, MEM_HEADS, 2, LANES).transpose(0, 1, 2, 4, 3, 5)
        return c.reshape(depth, ns, nm * ATT_ROWS, LANES)

    ck, cv = tile_rows(cache_mem_k), tile_rows(cache_mem_v)

    def untile_rows(c):
        c = c.reshape(depth, bsz, nm, 2, MEM_HEADS, LANES).transpose(0, 1, 2, 4, 3, 5)
        return c.reshape(depth, bsz, nm, MEM_HEADS, MEM_DH)

    win_all = w_in.astype(BF16)
    mk_all, mv_all = _mem_kv(mem2, w_k.astype(BF16), w_v.astype(BF16), tm_kv)

    xp = x_prompt
    xs = x_sample.reshape(ns, d)
    conv_p, gla_p, conv_s, gla_s = [], [], [], []
    for l in range(depth):
        wlr = jnp.pad(w_in[l, :, C_LR:], ((0, 0), (0, LANES - GLA_RANK))).astype(BF16)
        gup = jnp.pad(gla_gate_up[l], ((0, LANES - GLA_RANK), (0, 0))).astype(BF16)
        gb = row(gla_gate_b[l])
        gg = row(gla_out_norm[l])
        wout = w_out[l].astype(BF16)
        wq, wo = w_q[l].astype(BF16), w_o[l].astype(BF16)
        rw = jnp.concatenate([router_group[l], router_expert[l].transpose(1, 0, 2).reshape(d, N_EXPERTS)], axis=1)
        rw = jnp.pad(rw, ((0, 0), (0, LANES - rw.shape[1])))
        rb = jnp.concatenate([router_group_b[l], router_expert_b[l].reshape(-1)])
        rb = row(jnp.pad(rb, (0, LANES - rb.shape[0])))
        moe_w = (row(norm_ffn[l]), rw, rb, w_gate, w_up, w_down, l)

        xp, nbuf, ns_p = _mix_prompt(xp, (bsz, t, d), row(norm_mix[l]), win_all, l, wlr, gup, gb, conv_w[l], gg, wout, tt)
        conv_p.append(nbuf)
        gla_p.append(ns_p)
        xp, cls, rank, counts = _att_prompt(xp, row(norm_mem[l]), wq, wo, mk_all, mv_all, l,
                                            row(norm_ffn[l]), rw, rb, tq)
        xp = _moe_sparse(xp, cls, rank, counts, *moe_w, tm_sorted, rows_perm,
                         norm_g=row(norm_final) if l == depth - 1 else None)

        yc, u, q, k, v, gate, la = _mix_sample_in(
            xs, row(norm_mix[l]), win_all, l, wlr, gup, gb, conv_w[l], state_conv[l, :, 0], state_conv[l, :, 1])
        conv_s.append(jnp.stack([state_conv[l, :, 1], u], axis=1))
        s_new, o = _gla_step(q, k, v, la, state_gla, l, nb_gla)
        gla_s.append(s_new)
        wq_s = wq.reshape(d, MEM_HEADS, 2, LANES).transpose(0, 2, 1, 3).reshape(d, d)
        wo_s = wo.reshape(MEM_HEADS, 2, LANES, d).transpose(1, 0, 2, 3).reshape(d, d)
        xs, qa = _mix_sample_out(xs, yc, o, gate, gg, wout, row(norm_mem[l]), wq_s)
        oa = _att_sample(qa.reshape(ns, ATT_ROWS, LANES), ck, cv, l, nb_att)
        xs = _oproj(xs, oa.reshape(ns, d), wo_s)
        xs = _moe(xs, *moe_w, min(tm_moe, ns))

    y_prompt = xp.reshape(bsz, t, d)
    y_sample = _final_norm(xs, row(norm_final), ns).reshape(ns, 1, d)
    return (y_prompt, y_sample, jnp.stack(conv_p), jnp.stack(gla_p), untile_rows(mk_all), untile_rows(mv_all),
            jnp.stack(conv_s), jnp.stack(gla_s))
```

```python
import functools

import jax
import jax.numpy as jnp
from jax import lax
from jax.experimental import pallas as pl
from jax.experimental.pallas import tpu as pltpu

F32 = jnp.float32
BF16 = jnp.bfloat16
HIGHEST = lax.Precision.HIGHEST

EPS = 1e-6
CONV_W = 512
GLA_HEADS = 4
GLA_DK = 64
GLA_DV = 128
GLA_K = GLA_HEADS * GLA_DK
GLA_V = GLA_HEADS * GLA_DV
GLA_RANK = 16
GLA_TAU = 16.0
GLA_CHUNK = 64
MEM_HEADS = 4
MEM_DH = 256
N_GROUPS = 4
EXP_PER_GROUP = 4
N_EXPERTS = 16
LANES = 128
C_CB, C_CC, C_CH, C_Q, C_K, C_V, C_G, C_LR = 0, 512, 1024, 1536, 1792, 2048, 2560, 3072
VMEM_LIMIT = 52 * 1024 * 1024


def _params(*sem):
    return pltpu.CompilerParams(dimension_semantics=sem, vmem_limit_bytes=VMEM_LIMIT)


def _rms(x, g):
    return x * lax.rsqrt(jnp.mean(x * x, axis=-1, keepdims=True) + EPS) * g


def _dot(a, b, precision=None):
    return jnp.dot(a, b, precision=precision, preferred_element_type=F32)


def _dot_nt(a, b):
    return lax.dot_general(a, b, (((1,), (1,)), ((), ())), preferred_element_type=F32)


def _dot_tn(a, b, precision=None):
    return lax.dot_general(a, b, (((0,), (0,)), ((), ())), precision=precision,
                           preferred_element_type=F32)


def _dot_split(dot, mask, x):
    hi = x.astype(BF16)
    r1 = x - hi.astype(F32)
    mid = r1.astype(BF16)
    lo = (r1 - mid.astype(F32)).astype(BF16)
    return dot(mask, hi) + dot(mask, mid) + dot(mask, lo)


def _silu(x):
    return x / (1.0 + jnp.exp(-x))


def _log_sigmoid(x):
    return jnp.minimum(x, 0.0) - jnp.log1p(jnp.exp(-jnp.abs(x)))


def _head_norm_gate(o, g, gg):
    parts = []
    for h in range(GLA_HEADS):
        sl = slice(h * GLA_DV, (h + 1) * GLA_DV)
        parts.append(_rms(o[:, sl], gg[:, sl]))
    return jnp.concatenate(parts, axis=-1) * _silu(g)


def _full(shape):
    nd = len(shape)
    return pl.BlockSpec(shape, lambda *_: (0,) * nd)


def _mix_prompt_kernel(x_ref, g_ref, win_ref, wlr_ref, gup_ref, gb_ref, cw_ref, gg_ref, wout_ref,
                       x1_ref, conv_ref, s_ref, ubuf, s_scr, o_scr, *, tok_in):
    t = pl.program_id(1)
    tt = x1_ref.shape[1]

    @pl.when(t == 0)
    def _():
        ubuf[0:8, :] = jnp.zeros((8, CONV_W), F32)
        s_scr[...] = jnp.zeros(s_scr.shape, F32)

    x = _load_tok(x_ref, tt) if tok_in else x_ref[0]
    hb = _rms(x, g_ref[...]).astype(BF16)

    def proj(a, b):
        return _dot(hb, win_ref[0, :, a:b])

    u = proj(C_CC, C_CH) * proj(C_CH, C_Q)
    ubuf[8:8 + tt, :] = u
    cw = cw_ref[...]
    yc = proj(C_CB, C_CC) * (ubuf[6:6 + tt, :] * cw[0:1] + ubuf[7:7 + tt, :] * cw[1:2] + u * cw[2:3])
    ubuf[6:8, :] = u[tt - 2:tt, :]

    qs = proj(C_Q, C_K) * (GLA_DK ** -0.5)
    k = proj(C_K, C_V)
    v = proj(C_V, C_G)
    lr = _dot(hb, wlr_ref[...])
    gate = _dot(lr.astype(BF16), gup_ref[...]) + gb_ref[...]
    la = _log_sigmoid(gate) * (1.0 / GLA_TAU)

    c = GLA_CHUNK
    nc = tt // c
    iota = lambda shape, dim: lax.broadcasted_iota(jnp.int32, shape, dim)
    ltri = (iota((c, c), 0) >= iota((c, c), 1)).astype(BF16)
    b_wide = _dot_split(_dot, ltri, jnp.concatenate([la[j * c:(j + 1) * c] for j in range(nc)], axis=1))
    b_ends = jnp.concatenate([b_wide[c - 1:c, j * GLA_K:(j + 1) * GLA_K] for j in range(nc)], axis=0)
    spread = (iota((nc, LANES), 0) == iota((nc, LANES), 1)).astype(BF16)
    log_dec = _dot_split(lambda m, x: _dot_tn(x, m), spread, b_ends)
    head_feat = iota((GLA_K, GLA_K), 0) // c == iota((GLA_K, GLA_K), 1) // GLA_DK
    head_blk = iota((GLA_K, GLA_V), 0) // GLA_DK == iota((GLA_K, GLA_V), 1) // GLA_DV
    causal = iota((c, GLA_K), 0) >= iota((c, GLA_K), 1) % c
    for j in range(nc):
        r = slice(j * c, (j + 1) * c)
        b_c, q_c, k_c, v_c = b_wide[:, j * GLA_K:(j + 1) * GLA_K], qs[r], k[r], v[r]
        b_mid = b_c[c // 2:c // 2 + 1, :]
        b_last = b_c[c - 1:c, :]
        q_i = (q_c * jnp.exp(b_c - b_mid)).astype(BF16)
        k_i = k_c * jnp.exp(b_mid - b_c)
        k_dec = (k_c * jnp.exp(b_last - b_c)).astype(BF16)
        q_b = (q_c * jnp.exp(b_c)).astype(BF16)
        k_rows = jnp.where(head_feat, jnp.concatenate([k_i] * GLA_HEADS, axis=0), 0.0).astype(BF16)
        a = jnp.where(causal, _dot_nt(q_i, k_rows), 0.0).astype(BF16)
        v_blk = jnp.where(head_blk, jnp.concatenate([v_c] * GLA_HEADS, axis=0), 0.0).astype(BF16)
        s_prev = s_scr[...]
        o_scr[r, :] = _dot(jnp.concatenate([a, q_b], axis=1),
                           jnp.concatenate([v_blk, s_prev.astype(BF16)], axis=0))
        dec = jnp.exp(jnp.broadcast_to(log_dec[:, j:j + 1], (GLA_K, GLA_DV)))
        s_scr[...] = (jnp.concatenate([dec] * GLA_HEADS, axis=1) * s_prev
                      + jnp.where(head_blk, _dot_tn(k_dec, v_c.astype(BF16)), 0.0))

    yg = _head_norm_gate(o_scr[...], proj(C_G, C_LR), gg_ref[...])
    y = _dot(jnp.concatenate([yc.astype(BF16), yg.astype(BF16)], axis=1), wout_ref[...])
    x1_ref[0] = x + y

    @pl.when(t == pl.num_programs(1) - 1)
    def _():
        conv_ref[0] = u[tt - 2:tt, :]
        for h in range(GLA_HEADS):
            s_ref[0, h] = s_scr[h * GLA_DK:(h + 1) * GLA_DK, h * GLA_DV:(h + 1) * GLA_DV]


def _mix_prompt(x, shape, g, win, layer, wlr, gup, gb, cw, gg, wout, tt):
    bsz, t, d = shape
    tok_in = x.ndim == 2
    nt = t // tt
    x_spec = (pl.BlockSpec((tt * TOK_ROWS, LANES), lambda b, i: (b * nt + i, 0)) if tok_in
              else pl.BlockSpec((1, tt, d), lambda b, i: (b, i, 0)))
    return pl.pallas_call(
        functools.partial(_mix_prompt_kernel, tok_in=tok_in),
        grid=(bsz, nt),
        in_specs=[
            x_spec,
            _full(g.shape), pl.BlockSpec((1,) + win.shape[1:], lambda b, i: (layer, 0, 0)), _full(wlr.shape),
            _full(gup.shape), _full(gb.shape), _full(cw.shape), _full(gg.shape), _full(wout.shape),
        ],
        out_specs=[
            pl.BlockSpec((1, tt, d), lambda b, i: (b, i, 0)),
            pl.BlockSpec((1, 2, CONV_W), lambda b, i: (b, 0, 0)),
            pl.BlockSpec((1, GLA_HEADS, GLA_DK, GLA_DV), lambda b, i: (b, 0, 0, 0)),
        ],
        out_shape=[
            jax.ShapeDtypeStruct((bsz, t, d), F32),
            jax.ShapeDtypeStruct((bsz, 2, CONV_W), F32),
            jax.ShapeDtypeStruct((bsz, GLA_HEADS, GLA_DK, GLA_DV), F32),
        ],
        scratch_shapes=[
            pltpu.VMEM((8 + tt, CONV_W), F32),
            pltpu.VMEM((GLA_K, GLA_V), F32),
            pltpu.VMEM((tt, GLA_V), F32),
        ],
        compiler_params=_params("arbitrary", "arbitrary"),
        name="mix_prompt",
    )(x, g, win, wlr, gup, gb, cw, gg, wout)


def _kv_kernel(m_ref, wk_ref, wv_ref, k_ref, v_ref):
    tm = m_ref.shape[0]
    mb = m_ref[...].astype(BF16)
    for w_ref, o_ref in ((wk_ref, k_ref), (wv_ref, v_ref)):
        y = _dot(mb, w_ref[0])
        for h in range(MEM_HEADS):
            for dt in range(MEM_DH // LANES):
                col = h * MEM_DH + dt * LANES
                o_ref[0, pl.ds(dt * MEM_HEADS + h, tm, stride=ATT_ROWS), :] = y[:, col:col + LANES]


def _mem_kv(mem, wk, wv, tm):
    n, d = mem.shape
    depth = wk.shape[0]
    w_spec = pl.BlockSpec((1, d, d), lambda l, i: (l, 0, 0))
    o_spec = pl.BlockSpec((1, tm * ATT_ROWS, LANES), lambda l, i: (l, i, 0))
    return pl.pallas_call(
        _kv_kernel,
        grid=(depth, n // tm),
        in_specs=[pl.BlockSpec((tm, d), lambda l, i: (i, 0)), w_spec, w_spec],
        out_specs=[o_spec] * 2,
        out_shape=[jax.ShapeDtypeStruct((depth, n * ATT_ROWS, LANES), F32)] * 2,
        compiler_params=_params("arbitrary", "arbitrary"),
        name="mem_kv",
    )(mem, wk, wv)


def _att_prompt_kernel(x_ref, g_ref, wq_ref, wo_ref, mk_ref, mv_ref, gf_ref, rw_ref, rb_ref, ltri_ref,
                       out_ref, meta_ref, cnt_ref, carry):
    @pl.when((pl.program_id(0) == 0) & (pl.program_id(1) == 0))
    def _():
        carry[...] = jnp.zeros(carry.shape, F32)

    x = x_ref[0]
    xb = _rms(x, g_ref[...]).astype(BF16)
    q = _dot(xb, wq_ref[...])
    nm = mk_ref.shape[1] // ATT_ROWS

    def head_rows(ref, h):
        return jnp.concatenate([ref[0, pl.ds(dt * MEM_HEADS + h, nm, stride=ATT_ROWS), :]
                                for dt in range(MEM_DH // LANES)], axis=-1).astype(BF16)

    outs = []
    for h in range(MEM_HEADS):
        sl = slice(h * MEM_DH, (h + 1) * MEM_DH)
        s = _dot_nt(q[:, sl].astype(BF16), head_rows(mk_ref, h)) * (MEM_DH ** -0.5)
        e = jnp.exp(s - jnp.max(s, axis=-1, keepdims=True))
        p = e / jnp.sum(e, axis=-1, keepdims=True)
        outs.append(_dot(p.astype(BF16), head_rows(mv_ref, h)))
    o = jnp.concatenate(outs, axis=-1)
    x2 = x + _dot(o.astype(BF16), wo_ref[...])
    _store_tok(out_ref, x2)
    _route_meta(x2, gf_ref, rw_ref, rb_ref, ltri_ref, carry, meta_ref, cnt_ref)


def _att_prompt(x, g, wq, wo, mk, mv, layer, gf, rw, rb, tq):
    bsz, t, d = x.shape
    assert d == TOK_ROWS * LANES
    nq = t // tq
    rows = mk.shape[1] // bsz
    kv_spec = pl.BlockSpec((1, rows, LANES), lambda b, i: (layer, b, 0))
    meta_spec = pl.BlockSpec((tq // LANES, LANES), lambda b, i: (b * nq + i, 0))
    rw_hi = rw.astype(BF16)
    rw_split = jnp.concatenate([rw_hi, (rw - rw_hi.astype(F32)).astype(BF16)], axis=1)
    ltri = jnp.tril(jnp.ones((LANES, LANES), BF16))
    n_tok = bsz * t
    assert n_tok <= RANK_RANGE
    return pl.pallas_call(
        _att_prompt_kernel,
        grid=(bsz, nq),
        in_specs=[
            pl.BlockSpec((1, tq, d), lambda b, i: (b, i, 0)),
            _full(g.shape), _full(wq.shape), _full(wo.shape),
            kv_spec, kv_spec,
            _full(gf.shape), _full(rw_split.shape), _full(rb.shape), _full(ltri.shape),
        ],
        out_specs=[
            pl.BlockSpec((tq * TOK_ROWS, LANES), lambda b, i: (b * nq + i, 0)),
            meta_spec, _full((1, LANES)),
        ],
        out_shape=[
            jax.ShapeDtypeStruct((n_tok * TOK_ROWS, LANES), F32),
            jax.ShapeDtypeStruct((n_tok // LANES, LANES), jnp.int32),
            jax.ShapeDtypeStruct((1, LANES), jnp.int32),
        ],
        scratch_shapes=[pltpu.VMEM((1, LANES), F32)],
        compiler_params=_params("arbitrary", "arbitrary"),
        name="att_prompt",
    )(x, g, wq, wo, mk, mv, gf, rw_split, rb, ltri)


def _route_top2(logits):
    lane = lax.broadcasted_iota(jnp.int32, logits.shape, 1)
    lanef = lane.astype(F32)
    ninf = -jnp.inf
    big = 1e9
    gl = jnp.where(lane < N_GROUPS, logits, ninf)
    gmax = jnp.max(gl, axis=-1, keepdims=True)
    g_idx = jnp.min(jnp.where(gl == gmax, lanef, big), axis=-1, keepdims=True)
    g_w = 1.0 / jnp.sum(jnp.exp(gl - gmax), axis=-1, keepdims=True)
    grp = ((lane - N_GROUPS) >> 2).astype(F32)
    emask = (lane >= N_GROUPS) & (lane < N_GROUPS + N_EXPERTS) & (grp == g_idx)
    el = jnp.where(emask, logits, ninf)
    m1 = jnp.max(el, axis=-1, keepdims=True)
    i1 = jnp.min(jnp.where(el == m1, lanef, big), axis=-1, keepdims=True)
    el2 = jnp.where(lanef == i1, ninf, el)
    m2 = jnp.max(el2, axis=-1, keepdims=True)
    i2 = jnp.min(jnp.where(el2 == m2, lanef, big), axis=-1, keepdims=True)
    tail = jnp.exp(m2 - m1)
    w1 = g_w / (1.0 + tail)
    w2 = g_w * tail / (1.0 + tail)
    return g_idx, i1, i2, w1, w2


def _route(logits):
    _, i1, i2, w1, w2 = _route_top2(logits)
    lanef = lax.broadcasted_iota(jnp.int32, logits.shape, 1).astype(F32)
    return jnp.where(lanef == i1, w1, 0.0) + jnp.where(lanef == i2, w2, 0.0)


N_PAIRS = 6
N_CLASSES = N_GROUPS * N_PAIRS
TOK_ROWS = 8
RANK_RANGE = 1 << 16


def _load_tok(ref, n):
    return jnp.concatenate([ref[pl.ds(j, n, stride=TOK_ROWS), :] for j in range(TOK_ROWS)], axis=-1)


def _store_tok(ref, val):
    n = val.shape[0]
    for j in range(TOK_ROWS):
        ref[pl.ds(j, n, stride=TOK_ROWS), :] = val[:, j * LANES:(j + 1) * LANES]


def _lane_dense(col):
    eye = lax.broadcasted_iota(jnp.int32, (LANES, LANES), 0) == lax.broadcasted_iota(jnp.int32, (LANES, LANES), 1)
    rows = [jnp.sum(jnp.where(eye, col[b * LANES:(b + 1) * LANES], 0.0), axis=0, keepdims=True)
            for b in range(col.shape[0] // LANES)]
    return jnp.concatenate(rows, axis=0)


def _route_meta(x, g_ref, rw_ref, rb_ref, ltri_ref, carry, meta_ref, cnt_ref):
    n = x.shape[0]
    xn = _rms(x, g_ref[...])
    x_hi = xn.astype(BF16)
    x_lo = (xn - x_hi.astype(F32)).astype(BF16)
    both = _dot(x_hi, rw_ref[...])
    logits = (both[:, :LANES] + (_dot(x_lo, rw_ref[:, :LANES]) + both[:, LANES:])) + rb_ref[...]
    g_idx, i1, i2, _, _ = _route_top2(logits)
    lo = jnp.minimum(i1, i2) - N_GROUPS - EXP_PER_GROUP * g_idx
    hi = jnp.maximum(i1, i2) - N_GROUPS - EXP_PER_GROUP * g_idx
    cls = g_idx * N_PAIRS + lo * (2 * EXP_PER_GROUP - 1.0 - lo) * 0.5 + hi - lo - 1.0

    lane = lax.broadcasted_iota(jnp.int32, (n, LANES), 1)
    onehot = lane.astype(F32) == cls
    onehot_b = onehot.astype(BF16)
    blocks = []
    run = carry[...]
    for b in range(n // LANES):
        p_b = _dot(ltri_ref[...], onehot_b[b * LANES:(b + 1) * LANES]) + run
        blocks.append(p_b)
        run = p_b[LANES - 1:LANES, :]
    prefix = jnp.concatenate(blocks, axis=0)
    rank = jnp.sum(jnp.where(onehot, prefix, 0.0), axis=-1, keepdims=True) - 1.0
    carry[...] = prefix[n - 1:n, :]
    meta_ref[...] = _lane_dense(cls * float(RANK_RANGE) + rank).astype(jnp.int32)
    cnt_ref[...] = prefix[n - 1:n, :].astype(jnp.int32)


DMA_UNROLL = 8


def _tok_rows(t):
    return pl.ds(pl.multiple_of(t * TOK_ROWS, TOK_ROWS), TOK_ROWS)


def _tok_scatter_kernel(pos, tail, x_ref, dst_ref, stage, zeros, sems, *, toks, steps):
    i = pl.program_id(0)
    slot = i % 2
    base = i * toks
    rows = toks * TOK_ROWS
    tile_rows = zeros.shape[0]

    def wait_slot(s):
        pltpu.make_async_copy(stage.at[s], dst_ref.at[pl.ds(0, rows)], sems.at[s]).wait()

    @pl.when(i == 0)
    def _():
        zeros[...] = jnp.zeros(zeros.shape, F32)

        def fill(c):
            return pltpu.make_async_copy(
                zeros, dst_ref.at[pl.ds(pl.multiple_of(tail[c] * tile_rows, tile_rows), tile_rows)], sems.at[2])

        for c in range(2 * N_CLASSES):
            @pl.when(tail[c] >= 0)
            def _():
                fill(c).start()
        for c in range(2 * N_CLASSES):
            @pl.when(tail[c] >= 0)
            def _():
                fill(c).wait()

    @pl.when(i >= 2)
    def _():
        wait_slot(slot)

    stage[slot] = x_ref[...]

    def start(r8, c):
        for u in range(DMA_UNROLL):
            r = r8 * DMA_UNROLL + u
            pltpu.make_async_copy(stage.at[slot, _tok_rows(r)], dst_ref.at[_tok_rows(pos[base + r])],
                                  sems.at[slot]).start(priority=u % 2)
        return c

    lax.fori_loop(0, toks // DMA_UNROLL, start, 0)

    @pl.when(i == steps - 1)
    def _():
        wait_slot(slot)
        if steps >= 2:
            wait_slot(1 - slot)


def _tok_scatter(x, n_out, pos, tail, tile, toks):
    steps = x.shape[0] // (toks * TOK_ROWS)
    return pl.pallas_call(
        functools.partial(_tok_scatter_kernel, toks=toks, steps=steps),
        grid_spec=pltpu.PrefetchScalarGridSpec(
            num_scalar_prefetch=2,
            grid=(steps,),
            in_specs=[pl.BlockSpec((toks * TOK_ROWS, LANES), lambda i, *_: (i, 0))],
            out_specs=pl.BlockSpec(memory_space=pl.ANY),
            scratch_shapes=[pltpu.VMEM((2, toks * TOK_ROWS, LANES), F32), pltpu.VMEM((tile * TOK_ROWS, LANES), F32),
                            pltpu.SemaphoreType.DMA((3,))],
        ),
        out_shape=jax.ShapeDtypeStruct((n_out * TOK_ROWS, LANES), F32),
        compiler_params=_params("arbitrary"),
        name="tok_scatter",
    )(pos, tail, x)


def _tok_gather_kernel(pos, src_ref, *rest, toks, norm):
    if norm:
        g_ref, out_ref, stage, sem = rest
        dst = stage
    else:
        out_ref, sem = rest
        dst = out_ref
    base = pl.program_id(0) * toks

    def start(r8, c):
        for u in range(DMA_UNROLL):
            r = r8 * DMA_UNROLL + u
            pltpu.make_async_copy(src_ref.at[_tok_rows(pos[base + r])], dst.at[_tok_rows(r)],
                                  sem).start(priority=u % 2)
        return c

    lax.fori_loop(0, toks // DMA_UNROLL, start, 0)
    pltpu.make_async_copy(src_ref.at[pl.ds(0, toks * TOK_ROWS)], dst, sem).wait()
    if norm:
        out_ref[...] = _rms(_load_tok(stage, toks), g_ref[...])


def _tok_gather(src, pos, toks, norm_g=None):
    n = pos.shape[0]
    norm = norm_g is not None
    d = TOK_ROWS * LANES
    return pl.pallas_call(
        functools.partial(_tok_gather_kernel, toks=toks, norm=norm),
        grid_spec=pltpu.PrefetchScalarGridSpec(
            num_scalar_prefetch=1,
            grid=(n // toks,),
            in_specs=[pl.BlockSpec(memory_space=pl.ANY)]
            + ([pl.BlockSpec(norm_g.shape, lambda i, *_: (0, 0))] if norm else []),
            out_specs=(pl.BlockSpec((toks, d), lambda i, *_: (i, 0)) if norm
                       else pl.BlockSpec((toks * TOK_ROWS, LANES), lambda i, *_: (i, 0))),
            scratch_shapes=([pltpu.VMEM((toks * TOK_ROWS, LANES), F32)] if norm else [])
            + [pltpu.SemaphoreType.DMA(())],
        ),
        out_shape=jax.ShapeDtypeStruct((n, d) if norm else (n * TOK_ROWS, LANES), F32),
        compiler_params=_params("arbitrary"),
        name="tok_gather",
    )(pos, src, *([norm_g] if norm else []))


def _moe_sorted_kernel(e_lo, e_hi, valid, par, n_lo, n_hi, xs_ref, g_ref, rw_ref, rb_ref, wg_hbm, wu_hbm, wd_hbm,
                       out_ref, wgu_scr, wd_scr, stg_g, stg_u, stg_d, sems, *, layer):
    t = pl.program_id(0)
    tm = xs_ref.shape[0] // TOK_ROWS
    ff = wd_scr.shape[0] // 2
    prev = jnp.maximum(t - 1, 0)

    def fetch(buf, lo, hi):
        return [pltpu.make_async_copy(w.at[layer, e], stg.at[buf, s], sems.at[buf])
                for w, stg in ((wg_hbm, stg_g), (wu_hbm, stg_u), (wd_hbm, stg_d)) for s, e in ((0, lo), (1, hi))]

    @pl.when(t == 0)
    def _():
        for c in fetch(par[0], e_lo[0], e_hi[0]):
            c.start()

    @pl.when((t == 0) | (e_lo[t] != e_lo[prev]) | (e_hi[t] != e_hi[prev]))
    def _():
        buf = par[t]
        for c in fetch(buf, e_lo[t], e_hi[t]):
            c.wait()
        for s in range(2):
            wgu_scr[:, (2 * s) * ff:(2 * s + 1) * ff] = stg_g[buf, s].astype(BF16)
            wgu_scr[:, (2 * s + 1) * ff:(2 * s + 2) * ff] = stg_u[buf, s].astype(BF16)
            wd_scr[s * ff:(s + 1) * ff, :] = stg_d[buf, s].astype(BF16)

        @pl.when(n_lo[t] >= 0)
        def _():
            for c in fetch(1 - buf, n_lo[t], n_hi[t]):
                c.start()

    @pl.when(valid[t] > 0)
    def _():
        x = _load_tok(xs_ref, tm)
        xb = _rms(x, g_ref[...]).astype(BF16)
        logits = _dot(xb, rw_ref[...]) + rb_ref[...]
        lane = lax.broadcasted_iota(jnp.int32, logits.shape, 1)
        gl = jnp.where(lane < N_GROUPS, logits, -jnp.inf)
        g_w = 1.0 / jnp.sum(jnp.exp(gl - jnp.max(gl, axis=-1, keepdims=True)), axis=-1, keepdims=True)
        l_lo = jnp.sum(jnp.where(lane == e_lo[t] + N_GROUPS, logits, 0.0), axis=-1, keepdims=True)
        l_hi = jnp.sum(jnp.where(lane == e_hi[t] + N_GROUPS, logits, 0.0), axis=-1, keepdims=True)
        tail = jnp.exp(-jnp.abs(l_lo - l_hi))
        w_top = g_w / (1.0 + tail)
        w_oth = g_w * tail / (1.0 + tail)
        lo_top = l_lo >= l_hi
        gu = _dot(xb, wgu_scr[...])
        hid = [_silu(gu[:, 2 * s * ff:(2 * s + 1) * ff]) * gu[:, (2 * s + 1) * ff:(2 * s + 2) * ff] * w
               for s, w in ((0, jnp.where(lo_top, w_top, w_oth)), (1, jnp.where(lo_top, w_oth, w_top)))]
        _store_tok(out_ref, x + _dot(jnp.concatenate(hid, axis=1).astype(BF16), wd_scr[...]))

    @pl.when(valid[t] == 0)
    def _():
        out_ref[...] = jnp.zeros(out_ref.shape, F32)


def _moe_sorted(xs, g, rw, rb, wg, wu, wd, layer, e_lo, e_hi, valid, par, n_lo, n_hi, tm):
    npad = xs.shape[0] // TOK_ROWS
    _, _, d, ff = wg.shape
    const = lambda t, *_: (0, 0)
    hbm = pl.BlockSpec(memory_space=pl.ANY)
    return pl.pallas_call(
        functools.partial(_moe_sorted_kernel, layer=layer),
        grid_spec=pltpu.PrefetchScalarGridSpec(
            num_scalar_prefetch=6,
            grid=(npad // tm,),
            in_specs=[
                pl.BlockSpec((tm * TOK_ROWS, LANES), lambda t, lo, hi, v, *_: (jnp.where(v[t] > 0, t, 0), 0)),
                pl.BlockSpec(g.shape, const), pl.BlockSpec(rw.shape, const), pl.BlockSpec(rb.shape, const),
                hbm, hbm, hbm,
            ],
            out_specs=pl.BlockSpec((tm * TOK_ROWS, LANES), lambda t, *_: (t, 0)),
            scratch_shapes=[pltpu.VMEM((d, 4 * ff), BF16), pltpu.VMEM((2 * ff, d), BF16),
                            pltpu.VMEM((2, 2, d, ff), F32), pltpu.VMEM((2, 2, d, ff), F32),
                            pltpu.VMEM((2, 2, ff, d), F32), pltpu.SemaphoreType.DMA((2,))],
        ),
        out_shape=jax.ShapeDtypeStruct(xs.shape, F32),
        compiler_params=_params("arbitrary"),
        name="moe_sorted",
    )(e_lo, e_hi, valid, par, n_lo, n_hi, xs, g, rw, rb, wg, wu, wd)


def _moe_sparse(x, meta, counts, g, rw, rb, wg, wu, wd, layer, tm, toks, norm_g=None):
    n = x.shape[0] // TOK_ROWS
    meta = meta.reshape(n)
    cls, rank = meta // RANK_RANGE, meta % RANK_RANGE
    cnt = counts[0, :N_CLASSES]
    padded = ((cnt + tm - 1) // tm) * tm
    ends = jnp.cumsum(padded)
    pos = ((ends - padded)[cls] + rank).astype(jnp.int32)
    n_tiles = n // tm + N_CLASSES
    tile_start = jnp.arange(n_tiles, dtype=jnp.int32) * tm
    n_valid = ends[-1] // tm
    unused = n_valid + jnp.arange(N_CLASSES, dtype=jnp.int32)
    tail = jnp.concatenate([jnp.where(cnt > 0, ends // tm - 1, -1),
                            jnp.where(unused < n_tiles, unused, -1)]).astype(jnp.int32)
    last_cls = jnp.sum((tile_start[jnp.maximum(n_valid - 1, 0)] >= ends).astype(jnp.int32))
    tile_cls = jnp.sum((tile_start[:, None] >= ends[None, :]).astype(jnp.int32), axis=1)
    valid = (tile_start < ends[-1]).astype(jnp.int32)
    tile_cls = jnp.where(valid > 0, tile_cls, last_cls)
    pair_lo = jnp.array([0, 0, 0, 1, 1, 2], jnp.int32)
    pair_hi = jnp.array([1, 2, 3, 2, 3, 3], jnp.int32)
    experts = lambda c: ((c // N_PAIRS) * EXP_PER_GROUP + pair_lo[c % N_PAIRS],
                         (c // N_PAIRS) * EXP_PER_GROUP + pair_hi[c % N_PAIRS])
    e_lo, e_hi = experts(tile_cls)
    nonempty = cnt > 0
    cidx = jnp.arange(N_CLASSES, dtype=jnp.int32)
    later = nonempty[None, :] & (cidx[None, :] > cidx[:, None])
    nxt = jnp.min(jnp.where(later, cidx[None, :], N_CLASSES), axis=1)[tile_cls]
    par = ((jnp.cumsum(nonempty.astype(jnp.int32)) - 1)[tile_cls] % 2).astype(jnp.int32)
    nx_lo, nx_hi = experts(jnp.minimum(nxt, N_CLASSES - 1))
    n_lo = jnp.where(nxt < N_CLASSES, nx_lo, -1).astype(jnp.int32)
    n_hi = jnp.where(nxt < N_CLASSES, nx_hi, -1).astype(jnp.int32)

    xs = _tok_scatter(x, n_tiles * tm, pos, tail, tm, toks[0])
    ys = _moe_sorted(xs, g, rw.astype(BF16), rb, wg, wu, wd, layer, e_lo, e_hi, valid, par, n_lo, n_hi, tm)
    return _tok_gather(ys, pos, toks[1], norm_g)


def _moe_kernel(x_ref, g_ref, rw_ref, rb_ref, wg_ref, wu_ref, wd_ref, out_ref, xn_scr, comb_scr, acc_scr):
    e = pl.program_id(1)

    @pl.when(e == 0)
    def _():
        x = x_ref[...]
        xn = _rms(x, g_ref[...])
        xn_scr[...] = xn.astype(BF16)
        comb_scr[...] = _route(_dot(xn, rw_ref[...], HIGHEST) + rb_ref[...])
        acc_scr[...] = x

    xb = xn_scr[...]
    lane = lax.broadcasted_iota(jnp.int32, comb_scr.shape, 1)
    cmb = jnp.sum(jnp.where(lane == e + N_GROUPS, comb_scr[...], 0.0), axis=-1, keepdims=True)
    hid = _silu(_dot(xb, wg_ref[0, 0].astype(BF16))) * _dot(xb, wu_ref[0, 0].astype(BF16)) * cmb
    acc_scr[...] += _dot(hid.astype(BF16), wd_ref[0, 0].astype(BF16))

    @pl.when(e == pl.num_programs(1) - 1)
    def _():
        out_ref[...] = acc_scr[...]


def _moe(x, g, rw, rb, wg, wu, wd, layer, tm):
    n, d = x.shape
    _, ne, _, ff = wg.shape
    return pl.pallas_call(
        _moe_kernel,
        grid=(n // tm, ne),
        in_specs=[
            pl.BlockSpec((tm, d), lambda i, e: (i, 0)),
            _full(g.shape), _full(rw.shape), _full(rb.shape),
            pl.BlockSpec((1, 1, d, ff), lambda i, e: (layer, e, 0, 0)),
            pl.BlockSpec((1, 1, d, ff), lambda i, e: (layer, e, 0, 0)),
            pl.BlockSpec((1, 1, ff, d), lambda i, e: (layer, e, 0, 0)),
        ],
        out_specs=pl.BlockSpec((tm, d), lambda i, e: (i, 0)),
        out_shape=jax.ShapeDtypeStruct((n, d), F32),
        scratch_shapes=[pltpu.VMEM((tm, d), BF16), pltpu.VMEM((tm, LANES), F32), pltpu.VMEM((tm, d), F32)],
        compiler_params=_params("arbitrary", "arbitrary"),
        name="moe",
    )(x, g, rw, rb, wg, wu, wd)


def _norm_kernel(x_ref, g_ref, o_ref):
    o_ref[...] = _rms(x_ref[...], g_ref[...])


def _final_norm(x, g, tm):
    n, d = x.shape
    return pl.pallas_call(
        _norm_kernel,
        grid=(n // tm,),
        in_specs=[pl.BlockSpec((tm, d), lambda i: (i, 0)), _full(g.shape)],
        out_specs=pl.BlockSpec((tm, d), lambda i: (i, 0)),
        out_shape=jax.ShapeDtypeStruct((n, d), F32),
        compiler_params=_params("arbitrary"),
        name="final_norm",
    )(x, g)


def _mix_sample_in_kernel(x_ref, g_ref, win_ref, wlr_ref, gup_ref, gb_ref, cw_ref, b0_ref, b1_ref,
                          yc_ref, u_ref, q_ref, k_ref, v_ref, gate_ref, la_ref):
    hb = _rms(x_ref[...], g_ref[...]).astype(BF16)

    def proj(a, b):
        return _dot(hb, win_ref[0, :, a:b])

    u = proj(C_CC, C_CH) * proj(C_CH, C_Q)
    cw = cw_ref[...]
    yc_ref[...] = proj(C_CB, C_CC) * (b0_ref[...] * cw[0:1] + b1_ref[...] * cw[1:2] + u * cw[2:3])
    u_ref[...] = u
    q_ref[...] = proj(C_Q, C_K) * (GLA_DK ** -0.5)
    k_ref[...] = proj(C_K, C_V)
    v_ref[...] = proj(C_V, C_G)
    gate_ref[...] = proj(C_G, C_LR)
    lr = _dot(hb, wlr_ref[...])
    gate = _dot(lr.astype(BF16), gup_ref[...]) + gb_ref[...]
    la_ref[...] = _log_sigmoid(gate) * (1.0 / GLA_TAU)


def _mix_sample_in(x, g, win, layer, wlr, gup, gb, cw, b0, b1):
    n = x.shape[0]
    args = (x, g, win, wlr, gup, gb, cw, b0, b1)
    specs = [_full(a.shape) for a in args]
    specs[2] = pl.BlockSpec((1,) + win.shape[1:], lambda i: (layer, 0, 0))
    widths = (CONV_W, CONV_W, GLA_K, GLA_K, GLA_V, GLA_V, GLA_K)
    return pl.pallas_call(
        _mix_sample_in_kernel,
        grid=(1,),
        in_specs=specs,
        out_specs=[_full((n, w)) for w in widths],
        out_shape=[jax.ShapeDtypeStruct((n, w), F32) for w in widths],
        compiler_params=_params("arbitrary"),
        name="mix_sample_in",
    )(*args)


def _gla_step_kernel(q_ref, k_ref, v_ref, la_ref, s0_ref, s_ref, o_ref):
    nb = q_ref.shape[0]
    a = jnp.exp(la_ref[...])
    kb = k_ref[...].astype(BF16)
    qb = q_ref[...].astype(BF16)
    vf = v_ref[...].astype(BF16).astype(F32)
    rows = lax.broadcasted_iota(jnp.int32, (nb, 1), 0)
    spread = (lax.broadcasted_iota(jnp.int32, (nb, nb * GLA_DV), 0)
              == lax.broadcasted_iota(jnp.int32, (nb, nb * GLA_DV), 1) // GLA_DV)
    spread_b = spread.astype(BF16)
    for h in range(GLA_HEADS):
        ks = slice(h * GLA_DK, (h + 1) * GLA_DK)
        a_cols = _dot_tn(a[:, ks], spread.astype(F32), HIGHEST)
        k_cols = _dot_tn(kb[:, ks], spread_b)
        q_cols = _dot_tn(qb[:, ks], spread_b)
        o_h = jnp.zeros((nb, GLA_DV), F32)
        for n in range(nb):
            blk = slice(n * GLA_DV, (n + 1) * GLA_DV)
            v_row = vf[n:n + 1, h * GLA_DV:(h + 1) * GLA_DV]
            s_new = a_cols[:, blk] * s0_ref[0, n, h] + k_cols[:, blk] * v_row
            s_ref[n, h] = s_new
            o_h = jnp.where(rows == n, jnp.sum(q_cols[:, blk] * s_new, axis=0, keepdims=True), o_h)
        o_ref[:, h * GLA_DV:(h + 1) * GLA_DV] = o_h


def _gla_step(q, k, v, la, state, layer, nb):
    n = q.shape[0]
    sshape = (GLA_HEADS, GLA_DK, GLA_DV)
    return pl.pallas_call(
        _gla_step_kernel,
        grid=(n // nb,),
        in_specs=[
            pl.BlockSpec((nb, GLA_K), lambda i: (i, 0)),
            pl.BlockSpec((nb, GLA_K), lambda i: (i, 0)),
            pl.BlockSpec((nb, GLA_V), lambda i: (i, 0)),
            pl.BlockSpec((nb, GLA_K), lambda i: (i, 0)),
            pl.BlockSpec((1, nb) + sshape, lambda i: (layer, i, 0, 0, 0)),
        ],
        out_specs=[
            pl.BlockSpec((nb,) + sshape, lambda i: (i, 0, 0, 0)),
            pl.BlockSpec((nb, GLA_V), lambda i: (i, 0)),
        ],
        out_shape=[jax.ShapeDtypeStruct((n,) + sshape, F32), jax.ShapeDtypeStruct((n, GLA_V), F32)],
        compiler_params=_params("arbitrary"),
        name="gla_step",
    )(q, k, v, la, state)


def _mix_sample_out_kernel(x_ref, yc_ref, o_ref, gate_ref, gg_ref, wout_ref, gm_ref, wq_ref, x1_ref, q_ref):
    yg = _head_norm_gate(o_ref[...], gate_ref[...], gg_ref[...])
    y = _dot(jnp.concatenate([yc_ref[...].astype(BF16), yg.astype(BF16)], axis=1), wout_ref[...])
    x1 = x_ref[...] + y
    x1_ref[...] = x1
    q_ref[...] = _dot(_rms(x1, gm_ref[...]).astype(BF16), wq_ref[...])


def _mix_sample_out(x, yc, o, gate, gg, wout, gm, wq):
    args = (x, yc, o, gate, gg, wout, gm, wq)
    return pl.pallas_call(
        _mix_sample_out_kernel,
        grid=(1,),
        in_specs=[_full(a.shape) for a in args],
        out_specs=[_full(x.shape)] * 2,
        out_shape=[jax.ShapeDtypeStruct(x.shape, F32)] * 2,
        compiler_params=_params("arbitrary"),
        name="mix_sample_out",
    )(*args)


ATT_ROWS = 2 * MEM_HEADS


def _class_allreduce(x, op):
    n = x.shape[-1]
    shift = ATT_ROWS
    while shift < n:
        x = op(x, pltpu.roll(x, shift, axis=1))
        shift *= 2
    return x


def _att_sample_kernel(q_ref, k_ref, v_ref, o_ref):
    nb = q_ref.shape[0]
    ncol = k_ref.shape[2]
    diag = (lax.broadcasted_iota(jnp.int32, (ATT_ROWS, ncol), 0)
            == (lax.broadcasted_iota(jnp.int32, (ATT_ROWS, ncol), 1) & (ATT_ROWS - 1)))
    rows = lax.broadcasted_iota(jnp.int32, (nb, 1), 0)
    t = jnp.zeros((nb, ncol), F32)
    for n in range(nb):
        sc = _dot_nt(q_ref[n].astype(BF16), k_ref[0, n].astype(BF16))
        t = t + jnp.where(rows == n, jnp.sum(jnp.where(diag, sc, 0.0), axis=0, keepdims=True), 0.0)
    valid = (lax.broadcasted_iota(jnp.int32, (nb, ncol), 1) & (ATT_ROWS - 1)) < MEM_HEADS
    s = jnp.where(valid, (t + pltpu.roll(t, ncol - MEM_HEADS, axis=1)) * (MEM_DH ** -0.5), 0.0)
    e = jnp.where(valid, jnp.exp(s - _class_allreduce(s, jnp.maximum)), 0.0)
    den = jnp.where(valid, _class_allreduce(e, jnp.add), 1.0)
    p = e / den
    p = p + pltpu.roll(p, MEM_HEADS, axis=1)
    for n in range(nb):
        p_n = jnp.where(diag, jnp.broadcast_to(p[n:n + 1, :], (ATT_ROWS, ncol)), 0.0)
        o_ref[n] = _dot(p_n.astype(BF16), v_ref[0, n].astype(BF16))


def _att_sample(q, ck, cv, layer, nb):
    n = q.shape[0]
    ncol = ck.shape[2]
    return pl.pallas_call(
        _att_sample_kernel,
        grid=(n // nb,),
        in_specs=[
            pl.BlockSpec((nb, ATT_ROWS, LANES), lambda i: (i, 0, 0)),
            pl.BlockSpec((1, nb, ncol, LANES), lambda i: (layer, i, 0, 0)),
            pl.BlockSpec((1, nb, ncol, LANES), lambda i: (layer, i, 0, 0)),
        ],
        out_specs=pl.BlockSpec((nb, ATT_ROWS, LANES), lambda i: (i, 0, 0)),
        out_shape=jax.ShapeDtypeStruct((n, ATT_ROWS, LANES), F32),
        compiler_params=_params("arbitrary"),
        name="att_sample",
    )(q, ck, cv)


def _oproj_kernel(x_ref, o_ref, wo_ref, out_ref):
    out_ref[...] = x_ref[...] + _dot(o_ref[...].astype(BF16), wo_ref[...])


def _oproj(x, o, wo):
    return pl.pallas_call(
        _oproj_kernel,
        grid=(1,),
        in_specs=[_full(x.shape), _full(o.shape), _full(wo.shape)],
        out_specs=_full(x.shape),
        out_shape=jax.ShapeDtypeStruct(x.shape, F32),
        compiler_params=_params("arbitrary"),
        name="oproj",
    )(x, o, wo)


def kernel(x_prompt, x_sample, state_conv, state_gla, cache_mem_k, cache_mem_v, mem_prompt, norm_mix, w_in, conv_w, gla_gate_up, gla_gate_b, gla_out_norm, w_out, norm_mem, w_q, w_k, w_v, w_o, norm_ffn, router_group, router_group_b, router_expert, router_expert_b, w_gate, w_up, w_down, norm_final):
    depth = w_in.shape[0]
    bsz, t, d = x_prompt.shape
    ns = x_sample.shape[0]
    nm = mem_prompt.shape[1]
    n_tok = bsz * t

    tt = min(1024, t)
    tq = min(1024, t)
    tm_moe = min(1024, n_tok)
    tm_kv = min(512, bsz * nm)
    tm_sorted = 256
    rows_perm = (min(2048, n_tok), min(2048, n_tok))
    nb_gla = min(8, ns)
    nb_att = min(8, ns)

    row = lambda a: a.reshape(1, -1)
    mem2 = mem_prompt.reshape(bsz * nm, d)

    def tile_rows(c):
        c = c.reshape(depth, ns, nm, MEM_HEADS, 2, LANES).transpose(0, 1, 2, 4, 3, 5)
        return c.reshape(depth, ns, nm * ATT_ROWS, LANES)

    ck, cv = tile_rows(cache_mem_k), tile_rows(cache_mem_v)

    def untile_rows(c):
        c = c.reshape(depth, bsz, nm, 2, MEM_HEADS, LANES).transpose(0, 1, 2, 4, 3, 5)
        return c.reshape(depth, bsz, nm, MEM_HEADS, MEM_DH)

    win_all = w_in.astype(BF16)
    mk_all, mv_all = _mem_kv(mem2, w_k.astype(BF16), w_v.astype(BF16), tm_kv)

    xp = x_prompt
    xs = x_sample.reshape(ns, d)
    conv_p, gla_p, conv_s, gla_s = [], [], [], []
    for l in range(depth):
        wlr = jnp.pad(w_in[l, :, C_LR:], ((0, 0), (0, LANES - GLA_RANK))).astype(BF16)
        gup = jnp.pad(gla_gate_up[l], ((0, LANES - GLA_RANK), (0, 0))).astype(BF16)
        gb = row(gla_gate_b[l])
        gg = row(gla_out_norm[l])
        wout = w_out[l].astype(BF16)
        wq, wo = w_q[l].astype(BF16), w_o[l].astype(BF16)
        rw = jnp.concatenate([router_group[l], router_expert[l].transpose(1, 0, 2).reshape(d, N_EXPERTS)], axis=1)
        rw = jnp.pad(rw, ((0, 0), (0, LANES - rw.shape[1])))
        rb = jnp.concatenate([router_group_b[l], router_expert_b[l].reshape(-1)])
        rb = row(jnp.pad(rb, (0, LANES - rb.shape[0])))
        moe_w = (row(norm_ffn[l]), rw, rb, w_gate, w_up, w_down, l)

        xp, nbuf, ns_p = _mix_prompt(xp, (bsz, t, d), row(norm_mix[l]), win_all, l, wlr, gup, gb, conv_w[l], gg, wout, tt)
        conv_p.append(nbuf)
        gla_p.append(ns_p)
        xp, meta, counts = _att_prompt(xp, row(norm_mem[l]), wq, wo, mk_all, mv_all, l,
                                            row(norm_ffn[l]), rw, rb, tq)
        xp = _moe_sparse(xp, meta, counts, *moe_w, tm_sorted, rows_perm,
                         norm_g=row(norm_final) if l == depth - 1 else None)

        yc, u, q, k, v, gate, la = _mix_sample_in(
            xs, row(norm_mix[l]), win_all, l, wlr, gup, gb, conv_w[l], state_conv[l, :, 0], state_conv[l, :, 1])
        conv_s.append(jnp.stack([state_conv[l, :, 1], u], axis=1))
        s_new, o = _gla_step(q, k, v, la, state_gla, l, nb_gla)
        gla_s.append(s_new)
        wq_s = wq.reshape(d, MEM_HEADS, 2, LANES).transpose(0, 2, 1, 3).reshape(d, d)
        wo_s = wo.reshape(MEM_HEADS, 2, LANES, d).transpose(1, 0, 2, 3).reshape(d, d)
        xs, qa = _mix_sample_out(xs, yc, o, gate, gg, wout, row(norm_mem[l]), wq_s)
        oa = _att_sample(qa.reshape(ns, ATT_ROWS, LANES), ck, cv, l, nb_att)
        xs = _oproj(xs, oa.reshape(ns, d), wo_s)
        xs = _moe(xs, *moe_w, min(tm_moe, ns))

    y_prompt = xp.reshape(bsz, t, d)
    y_sample = _final_norm(xs, row(norm_final), ns).reshape(ns, 1, d)
    return (y_prompt, y_sample, jnp.stack(conv_p), jnp.stack(gla_p), untile_rows(mk_all), untile_rows(mv_all),
            jnp.stack(conv_s), jnp.stack(gla_s))
```

```python
import functools

import jax
import jax.numpy as jnp
from jax import lax
from jax.experimental import pallas as pl
from jax.experimental.pallas import tpu as pltpu

F32 = jnp.float32
BF16 = jnp.bfloat16
HIGHEST = lax.Precision.HIGHEST

EPS = 1e-6
CONV_W = 512
GLA_HEADS = 4
GLA_DK = 64
GLA_DV = 128
GLA_K = GLA_HEADS * GLA_DK
GLA_V = GLA_HEADS * GLA_DV
GLA_RANK = 16
GLA_TAU = 16.0
GLA_CHUNK = 64
MEM_HEADS = 4
MEM_DH = 256
N_GROUPS = 4
EXP_PER_GROUP = 4
N_EXPERTS = 16
LANES = 128
C_CB, C_CC, C_CH, C_Q, C_K, C_V, C_G, C_LR = 0, 512, 1024, 1536, 1792, 2048, 2560, 3072
VMEM_LIMIT = 52 * 1024 * 1024


def _params(*sem):
    return pltpu.CompilerParams(dimension_semantics=sem, vmem_limit_bytes=VMEM_LIMIT)


def _rms(x, g):
    return x * lax.rsqrt(jnp.mean(x * x, axis=-1, keepdims=True) + EPS) * g


def _dot(a, b, precision=None):
    return jnp.dot(a, b, precision=precision, preferred_element_type=F32)


def _dot_nt(a, b):
    return lax.dot_general(a, b, (((1,), (1,)), ((), ())), preferred_element_type=F32)


def _dot_tn(a, b, precision=None):
    return lax.dot_general(a, b, (((0,), (0,)), ((), ())), precision=precision,
                           preferred_element_type=F32)


def _dot_split(dot, mask, x):
    hi = x.astype(BF16)
    r1 = x - hi.astype(F32)
    mid = r1.astype(BF16)
    lo = (r1 - mid.astype(F32)).astype(BF16)
    return dot(mask, hi) + dot(mask, mid) + dot(mask, lo)


def _silu(x):
    return x / (1.0 + jnp.exp(-x))


def _log_sigmoid(x):
    return jnp.minimum(x, 0.0) - jnp.log1p(jnp.exp(-jnp.abs(x)))


def _head_norm_gate(o, g, gg):
    parts = []
    for h in range(GLA_HEADS):
        sl = slice(h * GLA_DV, (h + 1) * GLA_DV)
        parts.append(_rms(o[:, sl], gg[:, sl]))
    return jnp.concatenate(parts, axis=-1) * _silu(g)


def _full(shape):
    nd = len(shape)
    return pl.BlockSpec(shape, lambda *_: (0,) * nd)


def _mix_prompt_kernel(x_ref, g_ref, win_ref, wlr_ref, gup_ref, gb_ref, cw_ref, gg_ref, wout_ref,
                       x1_ref, conv_ref, s_ref, ubuf, s_scr, o_scr, *, tok_in):
    t = pl.program_id(1)
    tt = x1_ref.shape[1]

    @pl.when(t == 0)
    def _():
        ubuf[0:8, :] = jnp.zeros((8, CONV_W), F32)
        s_scr[...] = jnp.zeros(s_scr.shape, F32)

    x = _load_tok(x_ref, tt) if tok_in else x_ref[0]
    hb = _rms(x, g_ref[...]).astype(BF16)

    def proj(a, b):
        return _dot(hb, win_ref[0, :, a:b])

    u = proj(C_CC, C_CH) * proj(C_CH, C_Q)
    ubuf[8:8 + tt, :] = u
    cw = cw_ref[...]
    yc = proj(C_CB, C_CC) * (ubuf[6:6 + tt, :] * cw[0:1] + ubuf[7:7 + tt, :] * cw[1:2] + u * cw[2:3])
    ubuf[6:8, :] = u[tt - 2:tt, :]

    qs = proj(C_Q, C_K) * (GLA_DK ** -0.5)
    k = proj(C_K, C_V)
    v = proj(C_V, C_G)
    lr = _dot(hb, wlr_ref[...])
    gate = _dot(lr.astype(BF16), gup_ref[...]) + gb_ref[...]
    la = _log_sigmoid(gate) * (1.0 / GLA_TAU)

    c = GLA_CHUNK
    nc = tt // c
    iota = lambda shape, dim: lax.broadcasted_iota(jnp.int32, shape, dim)
    ltri = (iota((c, c), 0) >= iota((c, c), 1)).astype(BF16)
    b_wide = _dot_split(_dot, ltri, jnp.concatenate([la[j * c:(j + 1) * c] for j in range(nc)], axis=1))
    b_ends = jnp.concatenate([b_wide[c - 1:c, j * GLA_K:(j + 1) * GLA_K] for j in range(nc)], axis=0)
    spread = (iota((nc, LANES), 0) == iota((nc, LANES), 1)).astype(BF16)
    log_dec = _dot_split(lambda m, x: _dot_tn(x, m), spread, b_ends)
    head_feat = iota((GLA_K, GLA_K), 0) // c == iota((GLA_K, GLA_K), 1) // GLA_DK
    head_blk = iota((GLA_K, GLA_V), 0) // GLA_DK == iota((GLA_K, GLA_V), 1) // GLA_DV
    causal = iota((c, GLA_K), 0) >= iota((c, GLA_K), 1) % c
    for j in range(nc):
        r = slice(j * c, (j + 1) * c)
        b_c, q_c, k_c, v_c = b_wide[:, j * GLA_K:(j + 1) * GLA_K], qs[r], k[r], v[r]
        b_mid = b_c[c // 2:c // 2 + 1, :]
        b_last = b_c[c - 1:c, :]
        q_i = (q_c * jnp.exp(b_c - b_mid)).astype(BF16)
        k_i = k_c * jnp.exp(b_mid - b_c)
        k_dec = (k_c * jnp.exp(b_last - b_c)).astype(BF16)
        q_b = (q_c * jnp.exp(b_c)).astype(BF16)
        k_rows = jnp.where(head_feat, jnp.concatenate([k_i] * GLA_HEADS, axis=0), 0.0).astype(BF16)
        a = jnp.where(causal, _dot_nt(q_i, k_rows), 0.0).astype(BF16)
        v_blk = jnp.where(head_blk, jnp.concatenate([v_c] * GLA_HEADS, axis=0), 0.0).astype(BF16)
        s_prev = s_scr[...]
        o_scr[r, :] = _dot(jnp.concatenate([a, q_b], axis=1),
                           jnp.concatenate([v_blk, s_prev.astype(BF16)], axis=0))
        dec = jnp.exp(jnp.broadcast_to(log_dec[:, j:j + 1], (GLA_K, GLA_DV)))
        s_scr[...] = (jnp.concatenate([dec] * GLA_HEADS, axis=1) * s_prev
                      + jnp.where(head_blk, _dot_tn(k_dec, v_c.astype(BF16)), 0.0))

    yg = _head_norm_gate(o_scr[...], proj(C_G, C_LR), gg_ref[...])
    y = _dot(jnp.concatenate([yc.astype(BF16), yg.astype(BF16)], axis=1), wout_ref[...])
    x1_ref[0] = x + y

    @pl.when(t == pl.num_programs(1) - 1)
    def _():
        conv_ref[0] = u[tt - 2:tt, :]
        for h in range(GLA_HEADS):
            s_ref[0, h] = s_scr[h * GLA_DK:(h + 1) * GLA_DK, h * GLA_DV:(h + 1) * GLA_DV]


def _mix_prompt(x, shape, g, win, layer, wlr, gup, gb, cw, gg, wout, tt):
    bsz, t, d = shape
    tok_in = x.ndim == 2
    nt = t // tt
    x_spec = (pl.BlockSpec((tt * TOK_ROWS, LANES), lambda b, i: (b * nt + i, 0)) if tok_in
              else pl.BlockSpec((1, tt, d), lambda b, i: (b, i, 0)))
    return pl.pallas_call(
        functools.partial(_mix_prompt_kernel, tok_in=tok_in),
        grid=(bsz, nt),
        in_specs=[
            x_spec,
            _full(g.shape), pl.BlockSpec((1,) + win.shape[1:], lambda b, i: (layer, 0, 0)), _full(wlr.shape),
            _full(gup.shape), _full(gb.shape), _full(cw.shape), _full(gg.shape), _full(wout.shape),
        ],
        out_specs=[
            pl.BlockSpec((1, tt, d), lambda b, i: (b, i, 0)),
            pl.BlockSpec((1, 2, CONV_W), lambda b, i: (b, 0, 0)),
            pl.BlockSpec((1, GLA_HEADS, GLA_DK, GLA_DV), lambda b, i: (b, 0, 0, 0)),
        ],
        out_shape=[
            jax.ShapeDtypeStruct((bsz, t, d), F32),
            jax.ShapeDtypeStruct((bsz, 2, CONV_W), F32),
            jax.ShapeDtypeStruct((bsz, GLA_HEADS, GLA_DK, GLA_DV), F32),
        ],
        scratch_shapes=[
            pltpu.VMEM((8 + tt, CONV_W), F32),
            pltpu.VMEM((GLA_K, GLA_V), F32),
            pltpu.VMEM((tt, GLA_V), F32),
        ],
        compiler_params=_params("arbitrary", "arbitrary"),
        name="mix_prompt",
    )(x, g, win, wlr, gup, gb, cw, gg, wout)


def _kv_kernel(m_ref, wk_ref, wv_ref, k_ref, v_ref):
    tm = m_ref.shape[0]
    mb = m_ref[...].astype(BF16)
    for w_ref, o_ref in ((wk_ref, k_ref), (wv_ref, v_ref)):
        y = _dot(mb, w_ref[0])
        for h in range(MEM_HEADS):
            for dt in range(MEM_DH // LANES):
                col = h * MEM_DH + dt * LANES
                o_ref[0, pl.ds(dt * MEM_HEADS + h, tm, stride=ATT_ROWS), :] = y[:, col:col + LANES]


def _mem_kv(mem, wk, wv, tm):
    n, d = mem.shape
    depth = wk.shape[0]
    w_spec = pl.BlockSpec((1, d, d), lambda l, i: (l, 0, 0))
    o_spec = pl.BlockSpec((1, tm * ATT_ROWS, LANES), lambda l, i: (l, i, 0))
    return pl.pallas_call(
        _kv_kernel,
        grid=(depth, n // tm),
        in_specs=[pl.BlockSpec((tm, d), lambda l, i: (i, 0)), w_spec, w_spec],
        out_specs=[o_spec] * 2,
        out_shape=[jax.ShapeDtypeStruct((depth, n * ATT_ROWS, LANES), F32)] * 2,
        compiler_params=_params("arbitrary", "arbitrary"),
        name="mem_kv",
    )(mem, wk, wv)


def _att_prompt_kernel(x_ref, g_ref, wq_ref, wo_ref, mk_ref, mv_ref, gf_ref, rw_ref, rb_ref, ltri_ref,
                       out_ref, meta_ref, cnt_ref, carry):
    @pl.when((pl.program_id(0) == 0) & (pl.program_id(1) == 0))
    def _():
        carry[...] = jnp.zeros(carry.shape, F32)

    x = x_ref[0]
    xb = _rms(x, g_ref[...]).astype(BF16)
    q = _dot(xb, wq_ref[...])
    nm = mk_ref.shape[1] // ATT_ROWS

    def head_rows(ref, h):
        return jnp.concatenate([ref[0, pl.ds(dt * MEM_HEADS + h, nm, stride=ATT_ROWS), :]
                                for dt in range(MEM_DH // LANES)], axis=-1).astype(BF16)

    outs = []
    for h in range(MEM_HEADS):
        sl = slice(h * MEM_DH, (h + 1) * MEM_DH)
        s = _dot_nt(q[:, sl].astype(BF16), head_rows(mk_ref, h)) * (MEM_DH ** -0.5)
        e = jnp.exp(s - jnp.max(s, axis=-1, keepdims=True))
        p = e / jnp.sum(e, axis=-1, keepdims=True)
        outs.append(_dot(p.astype(BF16), head_rows(mv_ref, h)))
    o = jnp.concatenate(outs, axis=-1)
    x2 = x + _dot(o.astype(BF16), wo_ref[...])
    _store_tok(out_ref, x2)
    _route_meta(x2, gf_ref, rw_ref, rb_ref, ltri_ref, carry, meta_ref, cnt_ref)


def _att_prompt(x, g, wq, wo, mk, mv, layer, gf, rw, rb, tq):
    bsz, t, d = x.shape
    assert d == TOK_ROWS * LANES
    nq = t // tq
    rows = mk.shape[1] // bsz
    kv_spec = pl.BlockSpec((1, rows, LANES), lambda b, i: (layer, b, 0))
    meta_spec = pl.BlockSpec((tq // LANES, LANES), lambda b, i: (b * nq + i, 0))
    rw_hi = rw.astype(BF16)
    rw_split = jnp.concatenate([rw_hi, (rw - rw_hi.astype(F32)).astype(BF16)], axis=1)
    ltri = jnp.tril(jnp.ones((LANES, LANES), BF16))
    n_tok = bsz * t
    assert n_tok <= RANK_RANGE
    return pl.pallas_call(
        _att_prompt_kernel,
        grid=(bsz, nq),
        in_specs=[
            pl.BlockSpec((1, tq, d), lambda b, i: (b, i, 0)),
            _full(g.shape), _full(wq.shape), _full(wo.shape),
            kv_spec, kv_spec,
            _full(gf.shape), _full(rw_split.shape), _full(rb.shape), _full(ltri.shape),
        ],
        out_specs=[
            pl.BlockSpec((tq * TOK_ROWS, LANES), lambda b, i: (b * nq + i, 0)),
            meta_spec, _full((1, LANES)),
        ],
        out_shape=[
            jax.ShapeDtypeStruct((n_tok * TOK_ROWS, LANES), F32),
            jax.ShapeDtypeStruct((n_tok // LANES, LANES), jnp.int32),
            jax.ShapeDtypeStruct((1, LANES), jnp.int32),
        ],
        scratch_shapes=[pltpu.VMEM((1, LANES), F32)],
        compiler_params=_params("arbitrary", "arbitrary"),
        name="att_prompt",
    )(x, g, wq, wo, mk, mv, gf, rw_split, rb, ltri)


def _route_top2(logits):
    lane = lax.broadcasted_iota(jnp.int32, logits.shape, 1)
    lanef = lane.astype(F32)
    ninf = -jnp.inf
    big = 1e9
    gl = jnp.where(lane < N_GROUPS, logits, ninf)
    gmax = jnp.max(gl, axis=-1, keepdims=True)
    g_idx = jnp.min(jnp.where(gl == gmax, lanef, big), axis=-1, keepdims=True)
    g_w = 1.0 / jnp.sum(jnp.exp(gl - gmax), axis=-1, keepdims=True)
    grp = ((lane - N_GROUPS) >> 2).astype(F32)
    emask = (lane >= N_GROUPS) & (lane < N_GROUPS + N_EXPERTS) & (grp == g_idx)
    el = jnp.where(emask, logits, ninf)
    m1 = jnp.max(el, axis=-1, keepdims=True)
    i1 = jnp.min(jnp.where(el == m1, lanef, big), axis=-1, keepdims=True)
    el2 = jnp.where(lanef == i1, ninf, el)
    m2 = jnp.max(el2, axis=-1, keepdims=True)
    i2 = jnp.min(jnp.where(el2 == m2, lanef, big), axis=-1, keepdims=True)
    tail = jnp.exp(m2 - m1)
    w1 = g_w / (1.0 + tail)
    w2 = g_w * tail / (1.0 + tail)
    return g_idx, i1, i2, w1, w2


def _route(logits):
    _, i1, i2, w1, w2 = _route_top2(logits)
    lanef = lax.broadcasted_iota(jnp.int32, logits.shape, 1).astype(F32)
    return jnp.where(lanef == i1, w1, 0.0) + jnp.where(lanef == i2, w2, 0.0)


N_PAIRS = 6
N_CLASSES = N_GROUPS * N_PAIRS
TOK_ROWS = 8
RANK_RANGE = 1 << 16


def _load_tok(ref, n):
    return jnp.concatenate([ref[pl.ds(j, n, stride=TOK_ROWS), :] for j in range(TOK_ROWS)], axis=-1)


def _store_tok(ref, val):
    n = val.shape[0]
    for j in range(TOK_ROWS):
        ref[pl.ds(j, n, stride=TOK_ROWS), :] = val[:, j * LANES:(j + 1) * LANES]


def _lane_dense(col):
    eye = lax.broadcasted_iota(jnp.int32, (LANES, LANES), 0) == lax.broadcasted_iota(jnp.int32, (LANES, LANES), 1)
    rows = [jnp.sum(jnp.where(eye, col[b * LANES:(b + 1) * LANES], 0.0), axis=0, keepdims=True)
            for b in range(col.shape[0] // LANES)]
    return jnp.concatenate(rows, axis=0)


def _route_meta(x, g_ref, rw_ref, rb_ref, ltri_ref, carry, meta_ref, cnt_ref):
    n = x.shape[0]
    xn = _rms(x, g_ref[...])
    x_hi = xn.astype(BF16)
    x_lo = (xn - x_hi.astype(F32)).astype(BF16)
    both = _dot(x_hi, rw_ref[...])
    logits = (both[:, :LANES] + (_dot(x_lo, rw_ref[:, :LANES]) + both[:, LANES:])) + rb_ref[...]
    g_idx, i1, i2, _, _ = _route_top2(logits)
    lo = jnp.minimum(i1, i2) - N_GROUPS - EXP_PER_GROUP * g_idx
    hi = jnp.maximum(i1, i2) - N_GROUPS - EXP_PER_GROUP * g_idx
    cls = g_idx * N_PAIRS + lo * (2 * EXP_PER_GROUP - 1.0 - lo) * 0.5 + hi - lo - 1.0

    lane = lax.broadcasted_iota(jnp.int32, (n, LANES), 1)
    onehot = lane.astype(F32) == cls
    onehot_b = onehot.astype(BF16)
    blocks = []
    run = carry[...]
    for b in range(n // LANES):
        p_b = _dot(ltri_ref[...], onehot_b[b * LANES:(b + 1) * LANES]) + run
        blocks.append(p_b)
        run = p_b[LANES - 1:LANES, :]
    prefix = jnp.concatenate(blocks, axis=0)
    rank = jnp.sum(jnp.where(onehot, prefix, 0.0), axis=-1, keepdims=True) - 1.0
    carry[...] = prefix[n - 1:n, :]
    meta_ref[...] = _lane_dense(cls * float(RANK_RANGE) + rank).astype(jnp.int32)
    cnt_ref[...] = prefix[n - 1:n, :].astype(jnp.int32)


DMA_UNROLL = 8


def _tok_rows(t):
    return pl.ds(pl.multiple_of(t * TOK_ROWS, TOK_ROWS), TOK_ROWS)


def _tok_scatter_kernel(pos, tail, x_ref, dst_ref, stage, zeros, sems, *, toks, steps):
    i = pl.program_id(0)
    slot = i % 2
    base = i * toks
    rows = toks * TOK_ROWS
    tile_rows = zeros.shape[0]

    def wait_slot(s):
        pltpu.make_async_copy(stage.at[s], dst_ref.at[pl.ds(0, rows)], sems.at[s]).wait()

    @pl.when(i == 0)
    def _():
        zeros[...] = jnp.zeros(zeros.shape, F32)

        def fill(c):
            return pltpu.make_async_copy(
                zeros, dst_ref.at[pl.ds(pl.multiple_of(tail[c] * tile_rows, tile_rows), tile_rows)], sems.at[2])

        for c in range(2 * N_CLASSES):
            @pl.when(tail[c] >= 0)
            def _():
                fill(c).start()
        for c in range(2 * N_CLASSES):
            @pl.when(tail[c] >= 0)
            def _():
                fill(c).wait()

    @pl.when(i >= 2)
    def _():
        wait_slot(slot)

    stage[slot] = x_ref[...]

    def start(r8, c):
        for u in range(DMA_UNROLL):
            r = r8 * DMA_UNROLL + u
            pltpu.make_async_copy(stage.at[slot, _tok_rows(r)], dst_ref.at[_tok_rows(pos[base + r])],
                                  sems.at[slot]).start(priority=u % 2)
        return c

    lax.fori_loop(0, toks // DMA_UNROLL, start, 0)

    @pl.when(i == steps - 1)
    def _():
        wait_slot(slot)
        if steps >= 2:
            wait_slot(1 - slot)


def _tok_scatter(x, n_out, pos, tail, tile, toks):
    steps = x.shape[0] // (toks * TOK_ROWS)
    return pl.pallas_call(
        functools.partial(_tok_scatter_kernel, toks=toks, steps=steps),
        grid_spec=pltpu.PrefetchScalarGridSpec(
            num_scalar_prefetch=2,
            grid=(steps,),
            in_specs=[pl.BlockSpec((toks * TOK_ROWS, LANES), lambda i, *_: (i, 0))],
            out_specs=pl.BlockSpec(memory_space=pl.ANY),
            scratch_shapes=[pltpu.VMEM((2, toks * TOK_ROWS, LANES), F32), pltpu.VMEM((tile * TOK_ROWS, LANES), F32),
                            pltpu.SemaphoreType.DMA((3,))],
        ),
        out_shape=jax.ShapeDtypeStruct((n_out * TOK_ROWS, LANES), F32),
        compiler_params=_params("arbitrary"),
        name="tok_scatter",
    )(pos, tail, x)


def _tok_gather_kernel(pos, src_ref, *rest, toks, norm):
    if norm:
        g_ref, out_ref, stage, sem = rest
        dst = stage
    else:
        out_ref, sem = rest
        dst = out_ref
    base = pl.program_id(0) * toks

    def start(r8, c):
        for u in range(DMA_UNROLL):
            r = r8 * DMA_UNROLL + u
            pltpu.make_async_copy(src_ref.at[_tok_rows(pos[base + r])], dst.at[_tok_rows(r)],
                                  sem).start(priority=u % 2)
        return c

    lax.fori_loop(0, toks // DMA_UNROLL, start, 0)
    pltpu.make_async_copy(src_ref.at[pl.ds(0, toks * TOK_ROWS)], dst, sem).wait()
    if norm:
        out_ref[...] = _rms(_load_tok(stage, toks), g_ref[...])


def _tok_gather(src, pos, toks, norm_g=None):
    n = pos.shape[0]
    norm = norm_g is not None
    d = TOK_ROWS * LANES
    return pl.pallas_call(
        functools.partial(_tok_gather_kernel, toks=toks, norm=norm),
        grid_spec=pltpu.PrefetchScalarGridSpec(
            num_scalar_prefetch=1,
            grid=(n // toks,),
            in_specs=[pl.BlockSpec(memory_space=pl.ANY)]
            + ([pl.BlockSpec(norm_g.shape, lambda i, *_: (0, 0))] if norm else []),
            out_specs=(pl.BlockSpec((toks, d), lambda i, *_: (i, 0)) if norm
                       else pl.BlockSpec((toks * TOK_ROWS, LANES), lambda i, *_: (i, 0))),
            scratch_shapes=([pltpu.VMEM((toks * TOK_ROWS, LANES), F32)] if norm else [])
            + [pltpu.SemaphoreType.DMA(())],
        ),
        out_shape=jax.ShapeDtypeStruct((n, d) if norm else (n * TOK_ROWS, LANES), F32),
        compiler_params=_params("arbitrary"),
        name="tok_gather",
    )(pos, src, *([norm_g] if norm else []))


def _moe_sorted_kernel(e_lo, e_hi, valid, par, n_lo, n_hi, xs_ref, g_ref, rw_ref, rb_ref, wg_hbm, wu_hbm, wd_hbm,
                       out_ref, wgu_scr, wd_scr, stg_g, stg_u, stg_d, sems, *, layer):
    t = pl.program_id(0)
    tm = xs_ref.shape[0] // TOK_ROWS
    ff = wd_scr.shape[0] // 2
    prev = jnp.maximum(t - 1, 0)

    def fetch(buf, lo, hi):
        return [pltpu.make_async_copy(w.at[layer, e], stg.at[buf, s], sems.at[buf])
                for w, stg in ((wg_hbm, stg_g), (wu_hbm, stg_u), (wd_hbm, stg_d)) for s, e in ((0, lo), (1, hi))]

    @pl.when(t == 0)
    def _():
        for c in fetch(par[0], e_lo[0], e_hi[0]):
            c.start()

    @pl.when((t == 0) | (e_lo[t] != e_lo[prev]) | (e_hi[t] != e_hi[prev]))
    def _():
        buf = par[t]
        for c in fetch(buf, e_lo[t], e_hi[t]):
            c.wait()
        for s in range(2):
            wgu_scr[:, (2 * s) * ff:(2 * s + 1) * ff] = stg_g[buf, s].astype(BF16)
            wgu_scr[:, (2 * s + 1) * ff:(2 * s + 2) * ff] = stg_u[buf, s].astype(BF16)
            wd_scr[s * ff:(s + 1) * ff, :] = stg_d[buf, s].astype(BF16)

        @pl.when(n_lo[t] >= 0)
        def _():
            for c in fetch(1 - buf, n_lo[t], n_hi[t]):
                c.start()

    @pl.when(valid[t] > 0)
    def _():
        x = _load_tok(xs_ref, tm)
        xb = _rms(x, g_ref[...]).astype(BF16)
        logits = _dot(xb, rw_ref[...]) + rb_ref[...]
        lane = lax.broadcasted_iota(jnp.int32, logits.shape, 1)
        gl = jnp.where(lane < N_GROUPS, logits, -jnp.inf)
        g_w = 1.0 / jnp.sum(jnp.exp(gl - jnp.max(gl, axis=-1, keepdims=True)), axis=-1, keepdims=True)
        l_lo = jnp.sum(jnp.where(lane == e_lo[t] + N_GROUPS, logits, 0.0), axis=-1, keepdims=True)
        l_hi = jnp.sum(jnp.where(lane == e_hi[t] + N_GROUPS, logits, 0.0), axis=-1, keepdims=True)
        tail = jnp.exp(-jnp.abs(l_lo - l_hi))
        w_top = g_w / (1.0 + tail)
        w_oth = g_w * tail / (1.0 + tail)
        lo_top = l_lo >= l_hi
        gu = _dot(xb, wgu_scr[...])
        hid = [_silu(gu[:, 2 * s * ff:(2 * s + 1) * ff]) * gu[:, (2 * s + 1) * ff:(2 * s + 2) * ff] * w
               for s, w in ((0, jnp.where(lo_top, w_top, w_oth)), (1, jnp.where(lo_top, w_oth, w_top)))]
        _store_tok(out_ref, x + _dot(jnp.concatenate(hid, axis=1).astype(BF16), wd_scr[...]))

    @pl.when(valid[t] == 0)
    def _():
        out_ref[...] = jnp.zeros(out_ref.shape, F32)


def _moe_sorted(xs, g, rw, rb, wg, wu, wd, layer, e_lo, e_hi, valid, par, n_lo, n_hi, tm):
    npad = xs.shape[0] // TOK_ROWS
    _, _, d, ff = wg.shape
    const = lambda t, *_: (0, 0)
    hbm = pl.BlockSpec(memory_space=pl.ANY)
    return pl.pallas_call(
        functools.partial(_moe_sorted_kernel, layer=layer),
        grid_spec=pltpu.PrefetchScalarGridSpec(
            num_scalar_prefetch=6,
            grid=(npad // tm,),
            in_specs=[
                pl.BlockSpec((tm * TOK_ROWS, LANES), lambda t, lo, hi, v, *_: (jnp.where(v[t] > 0, t, 0), 0)),
                pl.BlockSpec(g.shape, const), pl.BlockSpec(rw.shape, const), pl.BlockSpec(rb.shape, const),
                hbm, hbm, hbm,
            ],
            out_specs=pl.BlockSpec((tm * TOK_ROWS, LANES), lambda t, *_: (t, 0)),
            scratch_shapes=[pltpu.VMEM((d, 4 * ff), BF16), pltpu.VMEM((2 * ff, d), BF16),
                            pltpu.VMEM((2, 2, d, ff), F32), pltpu.VMEM((2, 2, d, ff), F32),
                            pltpu.VMEM((2, 2, ff, d), F32), pltpu.SemaphoreType.DMA((2,))],
        ),
        out_shape=jax.ShapeDtypeStruct(xs.shape, F32),
        compiler_params=_params("arbitrary"),
        name="moe_sorted",
    )(e_lo, e_hi, valid, par, n_lo, n_hi, xs, g, rw, rb, wg, wu, wd)


def _moe_sparse(x, meta, counts, g, rw, rb, wg, wu, wd, layer, tm, toks, norm_g=None):
    n = x.shape[0] // TOK_ROWS
    meta = meta.reshape(n)
    cls, rank = meta // RANK_RANGE, meta % RANK_RANGE
    cnt = counts[0, :N_CLASSES]
    padded = ((cnt + tm - 1) // tm) * tm
    ends = jnp.cumsum(padded)
    pos = ((ends - padded)[cls] + rank).astype(jnp.int32)
    n_tiles = n // tm + N_CLASSES
    tile_start = jnp.arange(n_tiles, dtype=jnp.int32) * tm
    n_valid = ends[-1] // tm
    unused = n_valid + jnp.arange(N_CLASSES, dtype=jnp.int32)
    tail = jnp.concatenate([jnp.where(cnt > 0, ends // tm - 1, -1),
                            jnp.where(unused < n_tiles, unused, -1)]).astype(jnp.int32)
    last_cls = jnp.sum((tile_start[jnp.maximum(n_valid - 1, 0)] >= ends).astype(jnp.int32))
    tile_cls = jnp.sum((tile_start[:, None] >= ends[None, :]).astype(jnp.int32), axis=1)
    valid = (tile_start < ends[-1]).astype(jnp.int32)
    tile_cls = jnp.where(valid > 0, tile_cls, last_cls)
    pair_lo = jnp.array([0, 0, 0, 1, 1, 2], jnp.int32)
    pair_hi = jnp.array([1, 2, 3, 2, 3, 3], jnp.int32)
    experts = lambda c: ((c // N_PAIRS) * EXP_PER_GROUP + pair_lo[c % N_PAIRS],
                         (c // N_PAIRS) * EXP_PER_GROUP + pair_hi[c % N_PAIRS])
    e_lo, e_hi = experts(tile_cls)
    nonempty = cnt > 0
    cidx = jnp.arange(N_CLASSES, dtype=jnp.int32)
    later = nonempty[None, :] & (cidx[None, :] > cidx[:, None])
    nxt = jnp.min(jnp.where(later, cidx[None, :], N_CLASSES), axis=1)[tile_cls]
    par = ((jnp.cumsum(nonempty.astype(jnp.int32)) - 1)[tile_cls] % 2).astype(jnp.int32)
    nx_lo, nx_hi = experts(jnp.minimum(nxt, N_CLASSES - 1))
    n_lo = jnp.where(nxt < N_CLASSES, nx_lo, -1).astype(jnp.int32)
    n_hi = jnp.where(nxt < N_CLASSES, nx_hi, -1).astype(jnp.int32)

    xs = _tok_scatter(x, n_tiles * tm, pos, tail, tm, toks[0])
    ys = _moe_sorted(xs, g, rw.astype(BF16), rb, wg, wu, wd, layer, e_lo, e_hi, valid, par, n_lo, n_hi, tm)
    return _tok_gather(ys, pos, toks[1], norm_g)


def _moe_kernel(x_ref, g_ref, rw_ref, rb_ref, wg_ref, wu_ref, wd_ref, out_ref, xn_scr, comb_scr, acc_scr):
    e = pl.program_id(1)

    @pl.when(e == 0)
    def _():
        x = x_ref[...]
        xn = _rms(x, g_ref[...])
        xn_scr[...] = xn.astype(BF16)
        comb_scr[...] = _route(_dot(xn, rw_ref[...], HIGHEST) + rb_ref[...])
        acc_scr[...] = x

    xb = xn_scr[...]
    lane = lax.broadcasted_iota(jnp.int32, comb_scr.shape, 1)
    cmb = jnp.sum(jnp.where(lane == e + N_GROUPS, comb_scr[...], 0.0), axis=-1, keepdims=True)
    hid = _silu(_dot(xb, wg_ref[0, 0].astype(BF16))) * _dot(xb, wu_ref[0, 0].astype(BF16)) * cmb
    acc_scr[...] += _dot(hid.astype(BF16), wd_ref[0, 0].astype(BF16))

    @pl.when(e == pl.num_programs(1) - 1)
    def _():
        out_ref[...] = acc_scr[...]


def _moe(x, g, rw, rb, wg, wu, wd, layer, tm):
    n, d = x.shape
    _, ne, _, ff = wg.shape
    return pl.pallas_call(
        _moe_kernel,
        grid=(n // tm, ne),
        in_specs=[
            pl.BlockSpec((tm, d), lambda i, e: (i, 0)),
            _full(g.shape), _full(rw.shape), _full(rb.shape),
            pl.BlockSpec((1, 1, d, ff), lambda i, e: (layer, e, 0, 0)),
            pl.BlockSpec((1, 1, d, ff), lambda i, e: (layer, e, 0, 0)),
            pl.BlockSpec((1, 1, ff, d), lambda i, e: (layer, e, 0, 0)),
        ],
        out_specs=pl.BlockSpec((tm, d), lambda i, e: (i, 0)),
        out_shape=jax.ShapeDtypeStruct((n, d), F32),
        scratch_shapes=[pltpu.VMEM((tm, d), BF16), pltpu.VMEM((tm, LANES), F32), pltpu.VMEM((tm, d), F32)],
        compiler_params=_params("arbitrary", "arbitrary"),
        name="moe",
    )(x, g, rw, rb, wg, wu, wd)


def _norm_kernel(x_ref, g_ref, o_ref):
    o_ref[...] = _rms(x_ref[...], g_ref[...])


def _final_norm(x, g, tm):
    n, d = x.shape
    return pl.pallas_call(
        _norm_kernel,
        grid=(n // tm,),
        in_specs=[pl.BlockSpec((tm, d), lambda i: (i, 0)), _full(g.shape)],
        out_specs=pl.BlockSpec((tm, d), lambda i: (i, 0)),
        out_shape=jax.ShapeDtypeStruct((n, d), F32),
        compiler_params=_params("arbitrary"),
        name="final_norm",
    )(x, g)


def _mix_sample_in_kernel(x_ref, g_ref, win_ref, wlr_ref, gup_ref, gb_ref, cw_ref, b0_ref, b1_ref,
                          yc_ref, u_ref, q_ref, k_ref, v_ref, gate_ref, la_ref):
    hb = _rms(x_ref[...], g_ref[...]).astype(BF16)

    def proj(a, b):
        return _dot(hb, win_ref[0, :, a:b])

    u = proj(C_CC, C_CH) * proj(C_CH, C_Q)
    cw = cw_ref[...]
    yc_ref[...] = proj(C_CB, C_CC) * (b0_ref[...] * cw[0:1] + b1_ref[...] * cw[1:2] + u * cw[2:3])
    u_ref[...] = u
    q_ref[...] = proj(C_Q, C_K) * (GLA_DK ** -0.5)
    k_ref[...] = proj(C_K, C_V)
    v_ref[...] = proj(C_V, C_G)
    gate_ref[...] = proj(C_G, C_LR)
    lr = _dot(hb, wlr_ref[...])
    gate = _dot(lr.astype(BF16), gup_ref[...]) + gb_ref[...]
    la_ref[...] = _log_sigmoid(gate) * (1.0 / GLA_TAU)


def _mix_sample_in(x, g, win, layer, wlr, gup, gb, cw, b0, b1):
    n = x.shape[0]
    args = (x, g, win, wlr, gup, gb, cw, b0, b1)
    specs = [_full(a.shape) for a in args]
    specs[2] = pl.BlockSpec((1,) + win.shape[1:], lambda i: (layer, 0, 0))
    widths = (CONV_W, CONV_W, GLA_K, GLA_K, GLA_V, GLA_V, GLA_K)
    return pl.pallas_call(
        _mix_sample_in_kernel,
        grid=(1,),
        in_specs=specs,
        out_specs=[_full((n, w)) for w in widths],
        out_shape=[jax.ShapeDtypeStruct((n, w), F32) for w in widths],
        compiler_params=_params("arbitrary"),
        name="mix_sample_in",
    )(*args)


def _gla_step_kernel(q_ref, k_ref, v_ref, la_ref, s0_ref, s_ref, o_ref):
    nb = q_ref.shape[0]
    a = jnp.exp(la_ref[...])
    kb = k_ref[...].astype(BF16)
    qb = q_ref[...].astype(BF16)
    vf = v_ref[...].astype(BF16).astype(F32)
    rows = lax.broadcasted_iota(jnp.int32, (nb, 1), 0)
    spread = (lax.broadcasted_iota(jnp.int32, (nb, nb * GLA_DV), 0)
              == lax.broadcasted_iota(jnp.int32, (nb, nb * GLA_DV), 1) // GLA_DV)
    spread_b = spread.astype(BF16)
    for h in range(GLA_HEADS):
        ks = slice(h * GLA_DK, (h + 1) * GLA_DK)
        a_cols = _dot_tn(a[:, ks], spread.astype(F32), HIGHEST)
        k_cols = _dot_tn(kb[:, ks], spread_b)
        q_cols = _dot_tn(qb[:, ks], spread_b)
        o_h = jnp.zeros((nb, GLA_DV), F32)
        for n in range(nb):
            blk = slice(n * GLA_DV, (n + 1) * GLA_DV)
            v_row = vf[n:n + 1, h * GLA_DV:(h + 1) * GLA_DV]
            s_new = a_cols[:, blk] * s0_ref[0, n, h] + k_cols[:, blk] * v_row
            s_ref[n, h] = s_new
            o_h = jnp.where(rows == n, jnp.sum(q_cols[:, blk] * s_new, axis=0, keepdims=True), o_h)
        o_ref[:, h * GLA_DV:(h + 1) * GLA_DV] = o_h


def _gla_step(q, k, v, la, state, layer, nb):
    n = q.shape[0]
    sshape = (GLA_HEADS, GLA_DK, GLA_DV)
    return pl.pallas_call(
        _gla_step_kernel,
        grid=(n // nb,),
        in_specs=[
            pl.BlockSpec((nb, GLA_K), lambda i: (i, 0)),
            pl.BlockSpec((nb, GLA_K), lambda i: (i, 0)),
            pl.BlockSpec((nb, GLA_V), lambda i: (i, 0)),
            pl.BlockSpec((nb, GLA_K), lambda i: (i, 0)),
            pl.BlockSpec((1, nb) + sshape, lambda i: (layer, i, 0, 0, 0)),
        ],
        out_specs=[
            pl.BlockSpec((nb,) + sshape, lambda i: (i, 0, 0, 0)),
            pl.BlockSpec((nb, GLA_V), lambda i: (i, 0)),
        ],
        out_shape=[jax.ShapeDtypeStruct((n,) + sshape, F32), jax.ShapeDtypeStruct((n, GLA_V), F32)],
        compiler_params=_params("arbitrary"),
        name="gla_step",
    )(q, k, v, la, state)


def _mix_sample_out_kernel(x_ref, yc_ref, o_ref, gate_ref, gg_ref, wout_ref, gm_ref, wq_ref, x1_ref, q_ref):
    yg = _head_norm_gate(o_ref[...], gate_ref[...], gg_ref[...])
    y = _dot(jnp.concatenate([yc_ref[...].astype(BF16), yg.astype(BF16)], axis=1), wout_ref[...])
    x1 = x_ref[...] + y
    x1_ref[...] = x1
    q_ref[...] = _dot(_rms(x1, gm_ref[...]).astype(BF16), wq_ref[...])


def _mix_sample_out(x, yc, o, gate, gg, wout, gm, wq):
    args = (x, yc, o, gate, gg, wout, gm, wq)
    return pl.pallas_call(
        _mix_sample_out_kernel,
        grid=(1,),
        in_specs=[_full(a.shape) for a in args],
        out_specs=[_full(x.shape)] * 2,
        out_shape=[jax.ShapeDtypeStruct(x.shape, F32)] * 2,
        compiler_params=_params("arbitrary"),
        name="mix_sample_out",
    )(*args)


ATT_ROWS = 2 * MEM_HEADS


def _class_allreduce(x, op):
    n = x.shape[-1]
    shift = ATT_ROWS
    while shift < n:
        x = op(x, pltpu.roll(x, shift, axis=1))
        shift *= 2
    return x


def _att_sample_kernel(q_ref, k_ref, v_ref, o_ref):
    nb = q_ref.shape[0]
    ncol = k_ref.shape[2]
    diag = (lax.broadcasted_iota(jnp.int32, (ATT_ROWS, ncol), 0)
            == (lax.broadcasted_iota(jnp.int32, (ATT_ROWS, ncol), 1) & (ATT_ROWS - 1)))
    rows = lax.broadcasted_iota(jnp.int32, (nb, 1), 0)
    t = jnp.zeros((nb, ncol), F32)
    for n in range(nb):
        sc = _dot_nt(q_ref[n].astype(BF16), k_ref[0, n].astype(BF16))
        t = t + jnp.where(rows == n, jnp.sum(jnp.where(diag, sc, 0.0), axis=0, keepdims=True), 0.0)
    valid = (lax.broadcasted_iota(jnp.int32, (nb, ncol), 1) & (ATT_ROWS - 1)) < MEM_HEADS
    s = jnp.where(valid, (t + pltpu.roll(t, ncol - MEM_HEADS, axis=1)) * (MEM_DH ** -0.5), 0.0)
    e = jnp.where(valid, jnp.exp(s - _class_allreduce(s, jnp.maximum)), 0.0)
    den = jnp.where(valid, _class_allreduce(e, jnp.add), 1.0)
    p = e / den
    p = p + pltpu.roll(p, MEM_HEADS, axis=1)
    for n in range(nb):
        p_n = jnp.where(diag, jnp.broadcast_to(p[n:n + 1, :], (ATT_ROWS, ncol)), 0.0)
        o_ref[n] = _dot(p_n.astype(BF16), v_ref[0, n].astype(BF16))


def _att_sample(q, ck, cv, layer, nb):
    n = q.shape[0]
    ncol = ck.shape[2]
    return pl.pallas_call(
        _att_sample_kernel,
        grid=(n // nb,),
        in_specs=[
            pl.BlockSpec((nb, ATT_ROWS, LANES), lambda i: (i, 0, 0)),
            pl.BlockSpec((1, nb, ncol, LANES), lambda i: (layer, i, 0, 0)),
            pl.BlockSpec((1, nb, ncol, LANES), lambda i: (layer, i, 0, 0)),
        ],
        out_specs=pl.BlockSpec((nb, ATT_ROWS, LANES), lambda i: (i, 0, 0)),
        out_shape=jax.ShapeDtypeStruct((n, ATT_ROWS, LANES), F32),
        compiler_params=_params("arbitrary"),
        name="att_sample",
    )(q, ck, cv)


def _oproj_kernel(x_ref, o_ref, wo_ref, out_ref):
    out_ref[...] = x_ref[...] + _dot(o_ref[...].astype(BF16), wo_ref[...])


def _oproj(x, o, wo):
    return pl.pallas_call(
        _oproj_kernel,
        grid=(1,),
        in_specs=[_full(x.shape), _full(o.shape), _full(wo.shape)],
        out_specs=_full(x.shape),
        out_shape=jax.ShapeDtypeStruct(x.shape, F32),
        compiler_params=_params("arbitrary"),
        name="oproj",
    )(x, o, wo)


def kernel(x_prompt, x_sample, state_conv, state_gla, cache_mem_k, cache_mem_v, mem_prompt, norm_mix, w_in, conv_w, gla_gate_up, gla_gate_b, gla_out_norm, w_out, norm_mem, w_q, w_k, w_v, w_o, norm_ffn, router_group, router_group_b, router_expert, router_expert_b, w_gate, w_up, w_down, norm_final):
    depth = w_in.shape[0]
    bsz, t, d = x_prompt.shape
    ns = x_sample.shape[0]
    nm = mem_prompt.shape[1]
    n_tok = bsz * t

    tt = min(1024, t)
    tq = min(1024, t)
    tm_moe = min(1024, n_tok)
    tm_kv = min(1024, bsz * nm)
    tm_sorted = 256
    rows_perm = (min(2048, n_tok), min(2048, n_tok))
    nb_gla = min(8, ns)
    nb_att = min(8, ns)

    row = lambda a: a.reshape(1, -1)
    mem2 = mem_prompt.reshape(bsz * nm, d)

    def tile_rows(c):
        c = c.reshape(depth, ns, nm, MEM_HEADS, 2, LANES).transpose(0, 1, 2, 4, 3, 5)
        return c.reshape(depth, ns, nm * ATT_ROWS, LANES)

    ck, cv = tile_rows(cache_mem_k), tile_rows(cache_mem_v)

    def untile_rows(c):
        c = c.reshape(depth, bsz, nm, 2, MEM_HEADS, LANES).transpose(0, 1, 2, 4, 3, 5)
        return c.reshape(depth, bsz, nm, MEM_HEADS, MEM_DH)

    win_all = w_in.astype(BF16)
    mk_all, mv_all = _mem_kv(mem2, w_k.astype(BF16), w_v.astype(BF16), tm_kv)

    xp = x_prompt
    xs = x_sample.reshape(ns, d)
    conv_p, gla_p, conv_s, gla_s = [], [], [], []
    for l in range(depth):
        wlr = jnp.pad(w_in[l, :, C_LR:], ((0, 0), (0, LANES - GLA_RANK))).astype(BF16)
        gup = jnp.pad(gla_gate_up[l], ((0, LANES - GLA_RANK), (0, 0))).astype(BF16)
        gb = row(gla_gate_b[l])
        gg = row(gla_out_norm[l])
        wout = w_out[l].astype(BF16)
        wq, wo = w_q[l].astype(BF16), w_o[l].astype(BF16)
        rw = jnp.concatenate([router_group[l], router_expert[l].transpose(1, 0, 2).reshape(d, N_EXPERTS)], axis=1)
        rw = jnp.pad(rw, ((0, 0), (0, LANES - rw.shape[1])))
        rb = jnp.concatenate([router_group_b[l], router_expert_b[l].reshape(-1)])
        rb = row(jnp.pad(rb, (0, LANES - rb.shape[0])))
        moe_w = (row(norm_ffn[l]), rw, rb, w_gate, w_up, w_down, l)

        xp, nbuf, ns_p = _mix_prompt(xp, (bsz, t, d), row(norm_mix[l]), win_all, l, wlr, gup, gb, conv_w[l], gg, wout, tt)
        conv_p.append(nbuf)
        gla_p.append(ns_p)
        xp, meta, counts = _att_prompt(xp, row(norm_mem[l]), wq, wo, mk_all, mv_all, l,
                                            row(norm_ffn[l]), rw, rb, tq)
        xp = _moe_sparse(xp, meta, counts, *moe_w, tm_sorted, rows_perm,
                         norm_g=row(norm_final) if l == depth - 1 else None)

        yc, u, q, k, v, gate, la = _mix_sample_in(
            xs, row(norm_mix[l]), win_all, l, wlr, gup, gb, conv_w[l], state_conv[l, :, 0], state_conv[l, :, 1])
        conv_s.append(jnp.stack([state_conv[l, :, 1], u], axis=1))
        s_new, o = _gla_step(q, k, v, la, state_gla, l, nb_gla)
        gla_s.append(s_new)
        wq_s = wq.reshape(d, MEM_HEADS, 2, LANES).transpose(0, 2, 1, 3).reshape(d, d)
        wo_s = wo.reshape(MEM_HEADS, 2, LANES, d).transpose(1, 0, 2, 3).reshape(d, d)
        xs, qa = _mix_sample_out(xs, yc, o, gate, gg, wout, row(norm_mem[l]), wq_s)
        oa = _att_sample(qa.reshape(ns, ATT_ROWS, LANES), ck, cv, l, nb_att)
        xs = _oproj(xs, oa.reshape(ns, d), wo_s)
        xs = _moe(xs, *moe_w, min(tm_moe, ns))

    y_prompt = xp.reshape(bsz, t, d)
    y_sample = _final_norm(xs, row(norm_final), ns).reshape(ns, 1, d)
    return (y_prompt, y_sample, jnp.stack(conv_p), jnp.stack(gla_p), untile_rows(mk_all), untile_rows(mv_all),
            jnp.stack(conv_s), jnp.stack(gla_s))
```
